```python
import math
import jax
import jax.numpy as jnp
from jax import lax
import numpy as np

D_MODEL = 2048
BATCH = 4
SEQ = 4096
DEPTH = 2

GRID_W = 64
CTX_LEN = 256
F32 = jnp.float32

N_BRANCH = 4
BRANCH_W = D_MODEL // N_BRANCH
Q_BLOCK = 128
ROPE_THETA = 10000.0
LN_EPS = 1e-6
RMS_EPS = 1e-6
DEEPNORM_ALPHA = (2 * DEPTH) ** 0.25
DEEPNORM_BETA = (8 * DEPTH) ** -0.25

MLA_HEADS = 4
MLA_Q_RANK = 448
MLA_KV_RANK = 128
MLA_NOPE = 128
MLA_ROPE = 64
MLA_V = BRANCH_W // MLA_HEADS
MLA_COLS = MLA_Q_RANK + MLA_KV_RANK + MLA_ROPE

GQA_HEADS = 4
GQA_KV_HEADS = 2
GQA_HD = BRANCH_W // GQA_HEADS
GQA_COLS = (GQA_HEADS + 2 * GQA_KV_HEADS) * GQA_HD

HY_W = BRANCH_W
HY_SHORT = 3
HY_EMB = 33
HY_BANDS = (HY_EMB - 1) // 2
HY_FFN = 64
HY_MIN_DECAY = math.log(1e-2) / 1.5
HY_MAX_DECAY = math.log(1e-2) / 0.3
HY_COLS = 3 * HY_W

MB_INNER = BRANCH_W
MB_HEADDIM = 64
MB_HEADS = MB_INNER // MB_HEADDIM
MB_GROUPS = 2
MB_STATE = 128
MB_CONV = 3
MB_CHUNK = 128
MB_CONV_CH = MB_INNER + 2 * MB_GROUPS * MB_STATE
MB_COLS = MB_INNER + MB_CONV_CH + 2 * MB_HEADS

IN_COLS = MLA_COLS + GQA_COLS + HY_COLS + MB_COLS
IN_SPLITS = [MLA_COLS, MLA_COLS + GQA_COLS, MLA_COLS + GQA_COLS + HY_COLS]

FFN_HIDDEN = ((8 * D_MODEL // 3 + 255) // 256) * 256

kernel_name = "hybrid_mla_gqa_hyena_ssd_dit_block"


def standardize(x):
    xf = x.astype(F32)
    mu = jnp.mean(xf, -1, keepdims=True)
    var = jnp.mean(jnp.square(xf - mu), -1, keepdims=True)
    return (xf - mu) * lax.rsqrt(var + LN_EPS)


def layer_norm(x, g, b):
    return (standardize(x) * g.astype(F32) + b.astype(F32)).astype(x.dtype)


def modulate(x, shift, scale):
    return (standardize(x) * (1.0 + scale.astype(F32)) + shift.astype(F32)).astype(x.dtype)


def rms_norm(x, g):
    xf = x.astype(F32)
    y = xf * lax.rsqrt(jnp.mean(jnp.square(xf), -1, keepdims=True) + RMS_EPS)
    return (y * g.astype(F32)).astype(x.dtype)


def axial_rope(n_tokens, rot_dim):
    rows = n_tokens // GRID_W
    row = jnp.repeat(jnp.arange(rows, dtype=F32), GRID_W)
    col = jnp.tile(jnp.arange(GRID_W, dtype=F32), rows)
    n_freq = rot_dim // 4
    inv_freq = ROPE_THETA ** (-jnp.arange(n_freq, dtype=F32) / n_freq)
    ang = jnp.concatenate([row[:, None] * inv_freq, col[:, None] * inv_freq], axis=-1)
    return jnp.cos(ang), jnp.sin(ang)


def apply_rope(x, cos, sin):
    xf = x.astype(F32).reshape(x.shape[:-1] + (x.shape[-1] // 2, 2))
    x0, x1 = xf[..., 0], xf[..., 1]
    c = cos[None, :, None, :]
    s = sin[None, :, None, :]
    out = jnp.stack([x0 * c - x1 * s, x0 * s + x1 * c], axis=-1)
    return out.reshape(x.shape).astype(x.dtype)


def block_attention(q, k, v, scale):
    b, lq = q.shape[:2]
    nb = lq // Q_BLOCK
    qb = q.reshape((b, nb, Q_BLOCK) + q.shape[2:]).swapaxes(0, 1)

    def one_block(qblk):
        s = jnp.einsum('bqhgd,bkhd->bhgqk', qblk, k, preferred_element_type=F32) * scale
        p = jax.nn.softmax(s, axis=-1).astype(v.dtype)
        return jnp.einsum('bhgqk,bkhd->bqhgd', p, v)

    out = lax.map(one_block, qb)
    return out.swapaxes(0, 1).reshape((b, lq) + out.shape[3:])


def depthwise_conv(u, w, b):
    k, ch = w.shape
    pad = k // 2
    y = lax.conv_general_dilated(u, w[:, None, :].astype(u.dtype), window_strides=(1,),
                                 padding=[(pad, pad)], dimension_numbers=('NWC', 'WIO', 'NWC'),
                                 feature_group_count=ch)
    return y + b.astype(u.dtype)


def mla_qkv(p, q_norm_g, kv_norm_g, w_uq, w_ukv, rope):
    b, n = p.shape[:2]
    c_q, c_kv, k_rot = jnp.split(p, [MLA_Q_RANK, MLA_Q_RANK + MLA_KV_RANK], axis=-1)
    q = (rms_norm(c_q, q_norm_g) @ w_uq).reshape(b, n, MLA_HEADS, MLA_NOPE + MLA_ROPE)
    kv = (rms_norm(c_kv, kv_norm_g) @ w_ukv).reshape(b, n, MLA_HEADS, MLA_NOPE + MLA_V)
    q_nope, q_rot = jnp.split(q, [MLA_NOPE], axis=-1)
    k_nope, v = jnp.split(kv, [MLA_NOPE], axis=-1)
    k_rot = k_rot[:, :, None, :]
    if rope is not None:
        q_rot = apply_rope(q_rot, *rope)
        k_rot = apply_rope(k_rot, *rope)
    q = jnp.concatenate([q_nope, q_rot], axis=-1)
    k = jnp.concatenate([k_nope, jnp.broadcast_to(k_rot, (b, n, MLA_HEADS, MLA_ROPE))], axis=-1)
    return q, k, v


def mla_mixer(p_ctx, p_lat, q_norm_g, kv_norm_g, w_uq, w_ukv, need_ctx):
    scale = (MLA_NOPE + MLA_ROPE) ** -0.5
    b, n = p_lat.shape[:2]
    q_c, k_c, v_c = mla_qkv(p_ctx, q_norm_g, kv_norm_g, w_uq, w_ukv, None)
    q_l, k_l, v_l = mla_qkv(p_lat, q_norm_g, kv_norm_g, w_uq, w_ukv, axial_rope(n, MLA_ROPE))
    k_all = jnp.concatenate([k_c, k_l], axis=1)
    v_all = jnp.concatenate([v_c, v_l], axis=1)
    o_lat = block_attention(q_l[:, :, :, None, :], k_all, v_all, scale).reshape(b, n, MLA_HEADS * MLA_V)
    o_ctx = None
    if need_ctx:
        o_ctx = block_attention(q_c[:, :, :, None, :], k_c, v_c, scale).reshape(b, p_ctx.shape[1], MLA_HEADS * MLA_V)
    return o_ctx, o_lat


def gqa_qkv(p, q_norm_g, k_norm_g, rope):
    b, n = p.shape[:2]
    q, k, v = jnp.split(p, [GQA_HEADS * GQA_HD, (GQA_HEADS + GQA_KV_HEADS) * GQA_HD], axis=-1)
    q = rms_norm(q.reshape(b, n, GQA_HEADS, GQA_HD), q_norm_g)
    k = rms_norm(k.reshape(b, n, GQA_KV_HEADS, GQA_HD), k_norm_g)
    v = v.reshape(b, n, GQA_KV_HEADS, GQA_HD)
    if rope is not None:
        q = apply_rope(q, *rope)
        k = apply_rope(k, *rope)
    return q.reshape(b, n, GQA_KV_HEADS, GQA_HEADS // GQA_KV_HEADS, GQA_HD), k, v


def gqa_mixer(p_ctx, p_lat, q_norm_g, k_norm_g, need_ctx):
    scale = GQA_HD ** -0.5
    b, n = p_lat.shape[:2]
    q_c, k_c, v_c = gqa_qkv(p_ctx, q_norm_g, k_norm_g, None)
    q_l, k_l, v_l = gqa_qkv(p_lat, q_norm_g, k_norm_g, axial_rope(n, GQA_HD))
    k_all = jnp.concatenate([k_c, k_l], axis=1)
    v_all = jnp.concatenate([v_c, v_l], axis=1)
    o_lat = block_attention(q_l, k_all, v_all, scale).reshape(b, n, GQA_HEADS * GQA_HD)
    o_ctx = None
    if need_ctx:
        o_ctx = block_attention(q_c, k_c, v_c, scale).reshape(b, p_ctx.shape[1], GQA_HEADS * GQA_HD)
    return o_ctx, o_lat


def hyena_filters(n, w1, b1, w2, b2, w3, freq):
    t = jnp.linspace(0.0, 1.0, n, dtype=F32)[:, None]
    omega = 2.0 * math.pi * jnp.arange(n, dtype=F32) / n
    bands = jnp.linspace(1e-4, HY_BANDS - 1, HY_BANDS, dtype=F32)
    ang = omega[:, None] * bands[None, :]
    feats = jnp.concatenate([t, jnp.cos(ang), -jnp.sin(ang)], axis=-1)
    fr = freq.astype(F32)
    hdn = jnp.sin(fr * (feats @ w1.astype(F32) + b1.astype(F32)))
    hdn = jnp.sin(fr * (hdn @ w2.astype(F32) + b2.astype(F32)))
    filt = hdn @ w3.astype(F32)
    deltas = jnp.abs(jnp.linspace(HY_MIN_DECAY, HY_MAX_DECAY, HY_W, dtype=F32))
    window = jnp.exp(-t * deltas[None, :])
    h_fwd, h_bwd = jnp.split(filt, 2, axis=-1)
    return h_fwd * window, h_bwd * window


def bidir_long_conv(u, h_fwd, h_bwd):
    n, ch = h_fwd.shape
    k2 = jnp.concatenate([h_fwd, jnp.zeros((1, ch), F32), h_bwd[1:][::-1]], axis=0)
    u_f = jnp.fft.rfft(u.astype(F32), n=2 * n, axis=1)
    k_f = jnp.fft.rfft(k2, n=2 * n, axis=0)
    return jnp.fft.irfft(u_f * k_f[None], n=2 * n, axis=1)[:, :n]


def hyena_seq(p, conv_w, conv_b, w1, b1, w2, b2, w3, freq, skip):
    u = depthwise_conv(p, conv_w, conv_b)
    x0, x1, v = jnp.split(u, 3, axis=-1)
    v = (v * x1).astype(F32)
    h_fwd, h_bwd = hyena_filters(p.shape[1], w1, b1, w2, b2, w3, freq)
    v = bidir_long_conv(v, h_fwd, h_bwd) + v * skip.astype(F32)
    return (x0.astype(F32) * v).astype(p.dtype)


def hyena_mixer(p_ctx, p_lat, conv_w, conv_b, w1, b1, w2, b2, w3, freq, skip, need_ctx):
    o_lat = hyena_seq(p_lat, conv_w, conv_b, w1, b1, w2, b2, w3, freq, skip)
    o_ctx = hyena_seq(p_ctx, conv_w, conv_b, w1, b1, w2, b2, w3, freq, skip) if need_ctx else None
    return o_ctx, o_lat


def ssd_scan(x, dt, a, bm, cm, init_state):
    b, n, h, p = x.shape
    nc = n // MB_CHUNK

    def chunk(t):
        return t.reshape((b, nc, MB_CHUNK) + t.shape[2:])

    xdt = chunk(x.astype(F32) * dt[..., None])
    da = chunk(dt * a)
    bc = chunk(bm.astype(F32))
    cc = chunk(cm.astype(F32))
    cum = jnp.cumsum(da, axis=2)
    lower = jnp.tril(jnp.ones((MB_CHUNK, MB_CHUNK), bool))
    seg = cum[:, :, :, None, :] - cum[:, :, None, :, :]
    decay_ls = jnp.exp(jnp.where(lower[None, None, :, :, None], seg, -jnp.inf))
    scores = jnp.einsum('bclhn,bcshn->bclsh', cc, bc) * decay_ls
    y_diag = jnp.einsum('bclsh,bcshp->bclhp', scores, xdt)
    decay_to_end = jnp.exp(cum[:, :, -1:, :] - cum)
    chunk_states = jnp.einsum('bclhn,bclh,bclhp->bchpn', bc, decay_to_end, xdt)
    chunk_decay = jnp.exp(cum[:, :, -1, :])

    def step(state, inp):
        st, dec = inp
        return state * dec[:, :, None, None] + st, state

    final, s_in = lax.scan(step, init_state.astype(F32),
                           (chunk_states.swapaxes(0, 1), chunk_decay.swapaxes(0, 1)))
    s_in = s_in.swapaxes(0, 1)
    y_off = jnp.einsum('bclhn,bchpn,bclh->bclhp', cc, s_in, jnp.exp(cum))
    return (y_diag + y_off).reshape(b, n, h, p), final


def mamba_prep(p, conv_w, conv_b, dt_bias):
    b, n = p.shape[:2]
    z, xbc, dt = jnp.split(p, [MB_INNER, MB_INNER + MB_CONV_CH], axis=-1)
    xbc = jax.nn.silu(depthwise_conv(xbc, conv_w, conv_b))
    xs, bm, cm = jnp.split(xbc, [MB_INNER, MB_INNER + MB_GROUPS * MB_STATE], axis=-1)
    rep = MB_HEADS // MB_GROUPS
    xs = xs.reshape(b, n, MB_HEADS, MB_HEADDIM)
    bm = jnp.repeat(bm.reshape(b, n, MB_GROUPS, MB_STATE), rep, axis=2)
    cm = jnp.repeat(cm.reshape(b, n, MB_GROUPS, MB_STATE), rep, axis=2)
    dt = jax.nn.softplus(dt.astype(F32).reshape(b, n, 2, MB_HEADS) + dt_bias.astype(F32))
    return z, xs, bm, cm, dt


def mamba_mixer(p_ctx, p_lat, conv_w, conv_b, a_log, dt_bias, d_skip, norm_g, need_ctx):
    a = -jnp.exp(a_log.astype(F32))
    zc, xc, bc, cc, dtc = mamba_prep(p_ctx, conv_w, conv_b, dt_bias)
    zl, xl, bl, cl, dtl = mamba_prep(p_lat, conv_w, conv_b, dt_bias)
    zero = jnp.zeros((p_lat.shape[0], MB_HEADS, MB_HEADDIM, MB_STATE), F32)

    def flip(t):
        return jnp.flip(t, axis=1)

    yc_f, sc_f = ssd_scan(xc, dtc[:, :, 0], a[0], bc, cc, zero)
    yl_f, _ = ssd_scan(xl, dtl[:, :, 0], a[0], bl, cl, sc_f)
    yc_b, sc_b = ssd_scan(flip(xc), flip(dtc[:, :, 1]), a[1], flip(bc), flip(cc), zero)
    yl_b, _ = ssd_scan(flip(xl), flip(dtl[:, :, 1]), a[1], flip(bl), flip(cl), sc_b)

    def finish(yf, yb, xs, z):
        y = yf + flip(yb) + xs.astype(F32) * d_skip.astype(F32)[:, None]
        y = y.reshape(z.shape).astype(z.dtype)
        return rms_norm(y * jax.nn.silu(z), norm_g)

    o_lat = finish(yl_f, yl_b, xl, zl)
    o_ctx = finish(yc_f, yc_b, xc, zc) if need_ctx else None
    return o_ctx, o_lat


def merge_branches(h, outs, w_gate, b_gate, w_branch, w_out):
    acc = None
    for i, o in enumerate(outs):
        term = jax.nn.sigmoid(h @ w_gate[i] + b_gate[i]) * (o @ w_branch[i])
        acc = term if acc is None else acc + term
    return acc @ w_out


def swiglu(h, w_in, w_out):
    up, gate = jnp.split(h @ w_in, 2, axis=-1)
    return (jax.nn.silu(gate) * up) @ w_out


def setup_inputs(seed: int = 0) -> dict:
    key = jax.random.key(seed)
    ks = iter(jax.random.split(key, 64))
    L, D = DEPTH, D_MODEL

    def nrm(shape, scale):
        return jax.random.normal(next(ks), shape, F32) * scale

    def gain(shape):
        return 1.0 + nrm(shape, 0.02)

    x = nrm((BATCH, SEQ, D), 1.0)
    c = nrm((BATCH, D), 1.0)
    ctx = nrm((BATCH, CTX_LEN, D), 1.0)
    c_ctx = nrm((D,), 1.0)
    w_ada = nrm((L, D, 6 * D), 0.5 * D ** -0.5)
    b_ada = nrm((L, 6 * D), 0.01)
    w_in = nrm((L, D, IN_COLS), D ** -0.5)
    mla_q_norm = gain((L, MLA_Q_RANK))
    mla_kv_norm = gain((L, MLA_KV_RANK))
    mla_w_uq = nrm((L, MLA_Q_RANK, MLA_HEADS * (MLA_NOPE + MLA_ROPE)), MLA_Q_RANK ** -0.5)
    mla_w_ukv = nrm((L, MLA_KV_RANK, MLA_HEADS * (MLA_NOPE + MLA_V)), MLA_KV_RANK ** -0.5)
    gqa_q_norm = gain((L, GQA_HD))
    gqa_k_norm = gain((L, GQA_HD))
    hy_conv_w = nrm((L, HY_SHORT, HY_COLS), HY_SHORT ** -0.5)
    hy_conv_b = nrm((L, HY_COLS), 0.02)
    hy_w1 = nrm((L, HY_EMB, HY_FFN), HY_EMB ** -0.5)
    hy_b1 = nrm((L, HY_FFN), 0.02)
    hy_w2 = nrm((L, HY_FFN, HY_FFN), HY_FFN ** -0.5)
    hy_b2 = nrm((L, HY_FFN), 0.02)
    hy_w3 = nrm((L, HY_FFN, 2 * HY_W), 0.02)
    hy_freq = gain((L, HY_FFN))
    hy_skip = nrm((L, HY_W), 1.0)
    mb_conv_w = nrm((L, MB_CONV, MB_CONV_CH), MB_CONV ** -0.5)
    mb_conv_b = nrm((L, MB_CONV_CH), 0.02)
    mb_a_log = jnp.log(jax.random.uniform(next(ks), (L, 2, MB_HEADS), F32, 1.0, 16.0))
    dt0 = jnp.exp(jax.random.uniform(next(ks), (L, 2, MB_HEADS), F32, math.log(1e-3), math.log(1e-1)))
    mb_dt_bias = dt0 + jnp.log(-jnp.expm1(-dt0))
    mb_d = gain((L, MB_HEADS))
    mb_norm = gain((L, MB_INNER))
    w_mgate = nrm((L, N_BRANCH, D, D), D ** -0.5)
    b_mgate = nrm((L, N_BRANCH, D), 0.02)
    w_branch = nrm((L, N_BRANCH, BRANCH_W, D), BRANCH_W ** -0.5)
    w_out = nrm((L, D, D), DEEPNORM_BETA * D ** -0.5)
    ln1_g = gain((L, D))
    ln1_b = nrm((L, D), 0.02)
    w_ffn_in = nrm((L, D, 2 * FFN_HIDDEN), D ** -0.5)
    w_ffn_out = nrm((L, FFN_HIDDEN, D), DEEPNORM_BETA * FFN_HIDDEN ** -0.5)
    ln2_g = gain((L, D))
    ln2_b = nrm((L, D), 0.02)
    return {"x": x, "c": c, "ctx": ctx, "c_ctx": c_ctx, "w_ada": w_ada, "b_ada": b_ada, "w_in": w_in,
            "mla_q_norm": mla_q_norm, "mla_kv_norm": mla_kv_norm, "mla_w_uq": mla_w_uq, "mla_w_ukv": mla_w_ukv,
            "gqa_q_norm": gqa_q_norm, "gqa_k_norm": gqa_k_norm,
            "hy_conv_w": hy_conv_w, "hy_conv_b": hy_conv_b, "hy_w1": hy_w1, "hy_b1": hy_b1, "hy_w2": hy_w2,
            "hy_b2": hy_b2, "hy_w3": hy_w3, "hy_freq": hy_freq, "hy_skip": hy_skip,
            "mb_conv_w": mb_conv_w, "mb_conv_b": mb_conv_b, "mb_a_log": mb_a_log, "mb_dt_bias": mb_dt_bias,
            "mb_d": mb_d, "mb_norm": mb_norm,
            "w_mgate": w_mgate, "b_mgate": b_mgate, "w_branch": w_branch, "w_out": w_out,
            "ln1_g": ln1_g, "ln1_b": ln1_b, "w_ffn_in": w_ffn_in, "w_ffn_out": w_ffn_out,
            "ln2_g": ln2_g, "ln2_b": ln2_b}


def reference(x, c, ctx, c_ctx, w_ada, b_ada, w_in, mla_q_norm, mla_kv_norm, mla_w_uq, mla_w_ukv,
              gqa_q_norm, gqa_k_norm, hy_conv_w, hy_conv_b, hy_w1, hy_b1, hy_w2, hy_b2, hy_w3, hy_freq,
              hy_skip, mb_conv_w, mb_conv_b, mb_a_log, mb_dt_bias, mb_d, mb_norm, w_mgate, b_mgate,
              w_branch, w_out, ln1_g, ln1_b, w_ffn_in, w_ffn_out, ln2_g, ln2_b):
    z = ctx
    for l in range(DEPTH):
        need_ctx = l < DEPTH - 1
        mod = jax.nn.silu(c) @ w_ada[l] + b_ada[l]
        mod_c = jax.nn.silu(c_ctx) @ w_ada[l] + b_ada[l]
        sh1, sc1, g1, sh2, sc2, g2 = jnp.split(mod[:, None, :], 6, axis=-1)
        csh1, csc1, cg1, csh2, csc2, cg2 = jnp.split(mod_c, 6, axis=-1)

        h = modulate(x, sh1, sc1)
        hc = modulate(z, csh1, csc1)
        pa, pb, pcv, pd = jnp.split(h @ w_in[l], IN_SPLITS, axis=-1)
        pa_c, pb_c, pcv_c, pd_c = jnp.split(hc @ w_in[l], IN_SPLITS, axis=-1)

        oa_c, oa = mla_mixer(pa_c, pa, mla_q_norm[l], mla_kv_norm[l], mla_w_uq[l], mla_w_ukv[l], need_ctx)
        ob_c, ob = gqa_mixer(pb_c, pb, gqa_q_norm[l], gqa_k_norm[l], need_ctx)
        oc_c, oc = hyena_mixer(pcv_c, pcv, hy_conv_w[l], hy_conv_b[l], hy_w1[l], hy_b1[l], hy_w2[l],
                               hy_b2[l], hy_w3[l], hy_freq[l], hy_skip[l], need_ctx)
        od_c, od = mamba_mixer(pd_c, pd, mb_conv_w[l], mb_conv_b[l], mb_a_log[l], mb_dt_bias[l],
                               mb_d[l], mb_norm[l], need_ctx)

        y = merge_branches(h, (oa, ob, oc, od), w_mgate[l], b_mgate[l], w_branch[l], w_out[l])
        x_next = layer_norm(DEEPNORM_ALPHA * x + g1 * y, ln1_g[l], ln1_b[l])
        f = swiglu(modulate(x_next, sh2, sc2), w_ffn_in[l], w_ffn_out[l])
        x_next = layer_norm(DEEPNORM_ALPHA * x_next + g2 * f, ln2_g[l], ln2_b[l])

        if need_ctx:
            yc = merge_branches(hc, (oa_c, ob_c, oc_c, od_c), w_mgate[l], b_mgate[l], w_branch[l], w_out[l])
            z = layer_norm(DEEPNORM_ALPHA * z + cg1 * yc, ln1_g[l], ln1_b[l])
            fc = swiglu(modulate(z, csh2, csc2), w_ffn_in[l], w_ffn_out[l])
            z = layer_norm(DEEPNORM_ALPHA * z + cg2 * fc, ln2_g[l], ln2_b[l])
        x = x_next
    return x
```

```python
import functools
import math

import jax
import jax.numpy as jnp
import numpy as np
from jax import lax
from jax.experimental import pallas as pl
from jax.experimental.pallas import tpu as pltpu

F32 = jnp.float32
MXU_DT = jnp.bfloat16

D_MODEL = 2048
DEPTH = 2
GRID_W = 64
N_BRANCH = 4
BRANCH_W = D_MODEL // N_BRANCH
ROPE_THETA = 10000.0
LN_EPS = 1e-6
RMS_EPS = 1e-6
DEEPNORM_ALPHA = (2 * DEPTH) ** 0.25

MLA_HEADS = 4
MLA_Q_RANK = 448
MLA_KV_RANK = 128
MLA_NOPE = 128
MLA_ROPE = 64
MLA_V = 128
MLA_COLS = MLA_Q_RANK + MLA_KV_RANK + MLA_ROPE

GQA_HEADS = 4
GQA_KV_HEADS = 2
GQA_HD = 128
GQA_COLS = (GQA_HEADS + 2 * GQA_KV_HEADS) * GQA_HD

HY_W = BRANCH_W
HY_EMB = 33
HY_BANDS = (HY_EMB - 1) // 2
HY_FFN = 64
HY_MIN_DECAY = math.log(1e-2) / 1.5
HY_MAX_DECAY = math.log(1e-2) / 0.3
HY_COLS = 3 * HY_W

MB_INNER = BRANCH_W
MB_HEADDIM = 64
MB_HEADS = 8
MB_GROUPS = 2
MB_STATE = 128
MB_CHUNK = 128
MB_CONV_CH = MB_INNER + 2 * MB_GROUPS * MB_STATE
MB_COLS = MB_INNER + MB_CONV_CH + 2 * MB_HEADS

FFN_HIDDEN = 5632

LANE = 128
VMEM_LIMIT = 56 * 1024 * 1024

P_MLA = 0
P_MLA_W = 896
P_DT = 896
P_GQA = 1024
P_Z = 2048
P_HY = 2560
P_XBC = 4096
P_COLS = 5120

_HI = lax.Precision.HIGHEST


def _dot(a, b):
    return jnp.dot(a, b, preferred_element_type=F32)


def _dot_hi(a, b):
    return jnp.dot(a, b, precision=_HI, preferred_element_type=F32)


def _dot_nt(a, b):
    return lax.dot_general(a, b, (((1,), (1,)), ((), ())), preferred_element_type=F32)


def _sigmoid(x):
    return 1.0 / (1.0 + jnp.exp(-x))


def _silu(x):
    return x * _sigmoid(x)


def _params(*sem):
    return pltpu.CompilerParams(dimension_semantics=sem, vmem_limit_bytes=VMEM_LIMIT)


def _pick(n, prefs):
    for p in prefs:
        if n % p == 0:
            return p
    raise ValueError(f"no tile for {n} in {prefs}")


def _standardize(x):
    mu = jnp.mean(x, axis=-1, keepdims=True)
    xc = x - mu
    var = jnp.mean(xc * xc, axis=-1, keepdims=True)
    return xc * lax.rsqrt(var + LN_EPS)


def _is_ctx_rows(tile_idx, tm, n_lat):
    row = tile_idx * tm + lax.broadcasted_iota(jnp.int32, (tm, 1), 0)
    return row >= n_lat


def _ada_kernel(c_ref, w_ref, b_ref, o_ref):
    cs = _silu(c_ref[...])
    o_ref[...] = _dot(cs.astype(MXU_DT), w_ref[...].astype(MXU_DT)) + b_ref[...]


def ada_mod(c8, w_ada, b_ada3, layer):
    d = c8.shape[1]
    n = w_ada.shape[2]
    tn = 1024
    return pl.pallas_call(
        _ada_kernel,
        grid=(n // tn,),
        in_specs=[pl.BlockSpec((8, d), lambda j: (0, 0)),
                  pl.BlockSpec((None, d, tn), lambda j: (layer, 0, j)),
                  pl.BlockSpec((None, 1, tn), lambda j: (layer, 0, j))],
        out_specs=pl.BlockSpec((8, tn), lambda j: (0, j)),
        out_shape=jax.ShapeDtypeStruct((8, n), F32),
        compiler_params=_params("arbitrary"),
        name="ada_mod",
    )(c8, w_ada, b_ada3)


def _modulated(x_ref, sh_ref, sc_ref, shc_ref, scc_ref, tile_idx, tm, n_lat):
    xn = _standardize(x_ref[...])
    is_ctx = _is_ctx_rows(tile_idx, tm, n_lat)
    scale = jnp.where(is_ctx, scc_ref[...], sc_ref[...])
    shift = jnp.where(is_ctx, shc_ref[...], sh_ref[...])
    return (xn * (1.0 + scale) + shift).astype(MXU_DT)


def _inproj_kernel(x_ref, sh_ref, sc_ref, shc_ref, scc_ref, w_ref, p_ref, h_ref, *, tm, n_lat):
    @pl.when(pl.program_id(2) == 0)
    def _():
        h_ref[...] = _modulated(x_ref, sh_ref, sc_ref, shc_ref, scc_ref, pl.program_id(1), tm, n_lat)

    p_ref[...] = _dot(h_ref[...], w_ref[...])


def _ffn_in_kernel(x_ref, sh_ref, sc_ref, shc_ref, scc_ref, wu_ref, wg_ref, a_ref, h_ref, *, tm, n_lat):
    @pl.when(pl.program_id(2) == 0)
    def _():
        h_ref[...] = _modulated(x_ref, sh_ref, sc_ref, shc_ref, scc_ref, pl.program_id(1), tm, n_lat)

    h = h_ref[...]
    up = _dot(h, wu_ref[...])
    gate = _dot(h, wg_ref[...])
    a_ref[...] = (_silu(gate) * up).astype(a_ref.dtype)


def _mod_specs(k_shift, k_scale, n_batch):
    d = D_MODEL
    return [pl.BlockSpec((None, 1, d), lambda b, i, j: (b, 0, k_shift)),
            pl.BlockSpec((None, 1, d), lambda b, i, j: (b, 0, k_scale)),
            pl.BlockSpec((None, 1, d), lambda b, i, j: (n_batch, 0, k_shift)),
            pl.BlockSpec((None, 1, d), lambda b, i, j: (n_batch, 0, k_scale))]


def in_projection(xz, mod, w_in_p, layer, n_lat):
    bsz, l_all, d = xz.shape
    n = w_in_p.shape[2]
    tm = _pick(l_all, (1088, 544, 384, 128))
    tn = _pick(n, (1024, 512))
    kern = functools.partial(_inproj_kernel, tm=tm, n_lat=n_lat)
    return pl.pallas_call(
        kern,
        grid=(bsz, l_all // tm, n // tn),
        in_specs=[pl.BlockSpec((None, tm, d), lambda b, i, j: (b, i, 0))]
        + _mod_specs(0, 1, bsz)
        + [pl.BlockSpec((None, d, tn), lambda b, i, j: (layer, 0, j))],
        out_specs=[pl.BlockSpec((None, tm, tn), lambda b, i, j: (b, i, j)),
                   pl.BlockSpec((None, tm, d), lambda b, i, j: (b, i, 0))],
        out_shape=[jax.ShapeDtypeStruct((bsz, l_all, n), F32),
                   jax.ShapeDtypeStruct((bsz, l_all, d), MXU_DT)],
        compiler_params=_params("arbitrary", "arbitrary", "arbitrary"),
        name="in_projection",
    )(xz, mod, mod, mod, mod, w_in_p)


def ffn_in(xz, mod, w_ffn_in, layer, n_lat):
    bsz, l_all, d = xz.shape
    hid = w_ffn_in.shape[2] // 2
    tm = _pick(l_all, (1088, 544, 384, 128))
    tn = 512
    nj = hid // tn
    kern = functools.partial(_ffn_in_kernel, tm=tm, n_lat=n_lat)
    return pl.pallas_call(
        kern,
        grid=(bsz, l_all // tm, nj),
        in_specs=[pl.BlockSpec((None, tm, d), lambda b, i, j: (b, i, 0))]
        + _mod_specs(3, 4, bsz)
        + [pl.BlockSpec((None, d, tn), lambda b, i, j: (layer, 0, j)),
           pl.BlockSpec((None, d, tn), lambda b, i, j: (layer, 0, j + nj))],
        out_specs=pl.BlockSpec((None, tm, tn), lambda b, i, j: (b, i, j)),
        out_shape=jax.ShapeDtypeStruct((bsz, l_all, hid), MXU_DT),
        scratch_shapes=[pltpu.VMEM((tm, d), MXU_DT)],
        compiler_params=_params("arbitrary", "arbitrary", "arbitrary"),
        name="ffn_in",
    )(xz, mod, mod, mod, mod, w_ffn_in, w_ffn_in)


def _resln_kernel(a_ref, w_ref, res_ref, g_ref, gc_ref, lng_ref, lnb_ref, o_ref, acc_ref, *, tm, n_lat, nk):
    k = pl.program_id(2)

    @pl.when(k == 0)
    def _():
        acc_ref[...] = jnp.zeros_like(acc_ref)

    acc_ref[...] += _dot(a_ref[...], w_ref[...])

    @pl.when(k == nk - 1)
    def _():
        is_ctx = _is_ctx_rows(pl.program_id(1), tm, n_lat)
        gate = jnp.where(is_ctx, gc_ref[...], g_ref[...])
        y = DEEPNORM_ALPHA * res_ref[...] + gate * acc_ref[...]
        o_ref[...] = _standardize(y) * lng_ref[...] + lnb_ref[...]


def matmul_res_ln(a, w, res, mod, k_gate, ln_g, ln_b, layer, n_lat, out_rows):
    bsz, l_all, kdim = a.shape
    d = w.shape[2]
    tm = _pick(l_all, (544, 384, 128))
    tk = _pick(kdim, (512,))
    nk = kdim // tk
    kern = functools.partial(_resln_kernel, tm=tm, n_lat=n_lat, nk=nk)
    return pl.pallas_call(
        kern,
        grid=(bsz, pl.cdiv(out_rows, tm), nk),
        in_specs=[pl.BlockSpec((None, tm, tk), lambda b, i, k: (b, i, k)),
                  pl.BlockSpec((None, tk, d), lambda b, i, k: (layer, k, 0)),
                  pl.BlockSpec((None, tm, d), lambda b, i, k: (b, i, 0)),
                  pl.BlockSpec((None, 1, d), lambda b, i, k: (b, 0, k_gate)),
                  pl.BlockSpec((None, 1, d), lambda b, i, k: (bsz, 0, k_gate)),
                  pl.BlockSpec((None, 1, d), lambda b, i, k: (layer, 0, 0)),
                  pl.BlockSpec((None, 1, d), lambda b, i, k: (layer, 0, 0))],
        out_specs=pl.BlockSpec((None, tm, d), lambda b, i, k: (b, i, 0)),
        out_shape=jax.ShapeDtypeStruct((bsz, out_rows, d), F32),
        scratch_shapes=[pltpu.VMEM((tm, d), F32)],
        compiler_params=_params("arbitrary", "arbitrary", "arbitrary"),
        name="matmul_res_ln",
    )(a, w, res, mod, mod, ln_g, ln_b)


def _merge_kernel(h_ref, oa_ref, ob_ref, oc_ref, od_ref, wg_ref, bg_ref, wb_ref, o_ref):
    h = h_ref[...]
    acc = None
    for i, o_r in enumerate((oa_ref, ob_ref, oc_ref, od_ref)):
        g = _dot(h, wg_ref[i]) + bg_ref[i]
        t = _dot(o_r[...], wb_ref[i])
        term = _sigmoid(g) * t
        acc = term if acc is None else acc + term
    o_ref[...] = acc.astype(o_ref.dtype)


def merge_branches(h, outs, w_gate, b_gate4, w_branch, layer):
    bsz, l_all, d = h.shape
    bw = outs[0].shape[2]
    tm = _pick(l_all, (1088, 544, 384, 128))
    tn = 512
    o_spec = pl.BlockSpec((None, tm, bw), lambda b, i, j: (b, i, 0))
    return pl.pallas_call(
        _merge_kernel,
        grid=(bsz, l_all // tm, d // tn),
        in_specs=[pl.BlockSpec((None, tm, d), lambda b, i, j: (b, i, 0)), o_spec, o_spec, o_spec, o_spec,
                  pl.BlockSpec((None, N_BRANCH, d, tn), lambda b, i, j: (layer, 0, 0, j)),
                  pl.BlockSpec((None, N_BRANCH, 1, tn), lambda b, i, j: (layer, 0, 0, j)),
                  pl.BlockSpec((None, N_BRANCH, bw, tn), lambda b, i, j: (layer, 0, 0, j))],
        out_specs=pl.BlockSpec((None, tm, tn), lambda b, i, j: (b, i, j)),
        out_shape=jax.ShapeDtypeStruct((bsz, l_all, d), MXU_DT),
        compiler_params=_params("arbitrary", "arbitrary", "arbitrary"),
        name="merge_branches",
    )(h, *outs, w_gate, b_gate4, w_branch)


def _mla_prep_kernel(p_ref, gq_ref, gkv_ref, wuq_ref, wukv_ref, cos_ref, sin_ref, q_ref, k_ref, v_ref):
    scale = (MLA_NOPE + MLA_ROPE) ** -0.5
    p = p_ref[...]
    cq = p[:, 0:512]
    ckv = p[:, 512:640]
    ka = p[:, 640:768]
    kb = p[:, 768:896]
    rq = lax.rsqrt(jnp.sum(cq * cq, axis=-1, keepdims=True) * (1.0 / MLA_Q_RANK) + RMS_EPS)
    cqn = (cq * rq * gq_ref[...]).astype(MXU_DT)
    rkv = lax.rsqrt(jnp.mean(ckv * ckv, axis=-1, keepdims=True) + RMS_EPS)
    ckvn = (ckv * rkv * gkv_ref[...]).astype(MXU_DT)
    qf = _dot(cqn, wuq_ref[...])
    kvf = _dot(ckvn, wukv_ref[...])
    cos = cos_ref[...]
    sin = sin_ref[...]
    k_rope = (ka * cos + kb * sin).astype(k_ref.dtype)
    for hd in range(MLA_HEADS):
        qb = hd * 384
        q_ref[hd, :, 0:128] = (qf[:, qb:qb + 128] * scale).astype(q_ref.dtype)
        q_rope = qf[:, qb + 128:qb + 256] * cos + qf[:, qb + 256:qb + 384] * sin
        q_ref[hd, :, 128:256] = (q_rope * scale).astype(q_ref.dtype)
        k_ref[hd, :, 0:128] = kvf[:, hd * 256:hd * 256 + 128].astype(k_ref.dtype)
        k_ref[hd, :, 128:256] = k_rope
        v_ref[hd] = kvf[:, hd * 256 + 128:hd * 256 + 256].astype(v_ref.dtype)


def mla_prep(p, gq, gkv, wuq, wukv, cos, sin, layer):
    bsz, l_all, _ = p.shape
    tr = _pick(l_all, (544, 384, 128))
    h = MLA_HEADS
    return pl.pallas_call(
        _mla_prep_kernel,
        grid=(bsz, l_all // tr),
        in_specs=[pl.BlockSpec((None, tr, P_MLA_W), lambda b, i: (b, i, 0)),
                  pl.BlockSpec((None, 1, 512), lambda b, i: (layer, 0, 0)),
                  pl.BlockSpec((None, 1, 128), lambda b, i: (layer, 0, 0)),
                  pl.BlockSpec((None, 512, h * 384), lambda b, i: (layer, 0, 0)),
                  pl.BlockSpec((None, 128, h * 256), lambda b, i: (layer, 0, 0)),
                  pl.BlockSpec((tr, LANE), lambda b, i: (i, 0)),
                  pl.BlockSpec((tr, LANE), lambda b, i: (i, 0))],
        out_specs=[pl.BlockSpec((None, h, tr, 256), lambda b, i: (b, 0, i, 0)),
                   pl.BlockSpec((None, h, tr, 256), lambda b, i: (b, 0, i, 0)),
                   pl.BlockSpec((None, h, tr, 128), lambda b, i: (b, 0, i, 0))],
        out_shape=[jax.ShapeDtypeStruct((bsz, h, l_all, 256), MXU_DT),
                   jax.ShapeDtypeStruct((bsz, h, l_all, 256), MXU_DT),
                   jax.ShapeDtypeStruct((bsz, h, l_all, 128), MXU_DT)],
        compiler_params=_params("arbitrary", "arbitrary"),
        name="mla_prep",
    )(p, gq, gkv, wuq, wukv, cos, sin)


def _gqa_prep_kernel(p_ref, gq_ref, gk_ref, cos_ref, sin_ref, q_ref, k_ref, v_ref):
    scale = GQA_HD ** -0.5
    cos = cos_ref[...]
    sin = sin_ref[...]

    def norm_rope(x, g):
        r = lax.rsqrt(jnp.mean(x * x, axis=-1, keepdims=True) + RMS_EPS)
        xn = x * r * g
        return xn * cos + pltpu.roll(xn, GQA_HD // 2, 1) * sin

    for hd in range(GQA_HEADS):
        x = p_ref[:, hd * 128:(hd + 1) * 128]
        q_ref[hd] = (norm_rope(x, gq_ref[...]) * scale).astype(q_ref.dtype)
    for hd in range(GQA_KV_HEADS):
        x = p_ref[:, 512 + hd * 128:512 + (hd + 1) * 128]
        k_ref[hd] = norm_rope(x, gk_ref[...]).astype(k_ref.dtype)
        v_ref[hd] = p_ref[:, 768 + hd * 128:768 + (hd + 1) * 128].astype(v_ref.dtype)


def gqa_prep(p, gq, gk, cos, sin, layer):
    bsz, l_all, _ = p.shape
    tr = _pick(l_all, (544, 384, 128))
    return pl.pallas_call(
        _gqa_prep_kernel,
        grid=(bsz, l_all // tr),
        in_specs=[pl.BlockSpec((None, tr, GQA_COLS), lambda b, i: (b, i, P_GQA // GQA_COLS)),
                  pl.BlockSpec((None, 1, 128), lambda b, i: (layer, 0, 0)),
                  pl.BlockSpec((None, 1, 128), lambda b, i: (layer, 0, 0)),
                  pl.BlockSpec((tr, LANE), lambda b, i: (i, 0)),
                  pl.BlockSpec((tr, LANE), lambda b, i: (i, 0))],
        out_specs=[pl.BlockSpec((None, GQA_HEADS, tr, 128), lambda b, i: (b, 0, i, 0)),
                   pl.BlockSpec((None, GQA_KV_HEADS, tr, 128), lambda b, i: (b, 0, i, 0)),
                   pl.BlockSpec((None, GQA_KV_HEADS, tr, 128), lambda b, i: (b, 0, i, 0))],
        out_shape=[jax.ShapeDtypeStruct((bsz, GQA_HEADS, l_all, 128), MXU_DT),
                   jax.ShapeDtypeStruct((bsz, GQA_KV_HEADS, l_all, 128), MXU_DT),
                   jax.ShapeDtypeStruct((bsz, GQA_KV_HEADS, l_all, 128), MXU_DT)],
        compiler_params=_params("arbitrary", "arbitrary"),
        name="gqa_prep",
    )(p, gq, gk, cos, sin)


def _attn_kernel(q_ref, k_ref, v_ref, o_ref, *, n_lat, n_lat_tiles):
    i = pl.program_id(2)
    q = q_ref[...]

    def attend(k, v):
        s = _dot_nt(q, k)
        m = jnp.max(s, axis=-1, keepdims=True)
        e = jnp.exp(s - m)
        l = jnp.sum(e, axis=-1, keepdims=True)
        o = _dot(e.astype(v.dtype), v)
        return (o / l).astype(o_ref.dtype)

    @pl.when(i < n_lat_tiles)
    def _():
        o_ref[...] = attend(k_ref[...], v_ref[...])

    @pl.when(i >= n_lat_tiles)
    def _():
        o_ref[...] = attend(k_ref[n_lat:, :], v_ref[n_lat:, :])


def attention(q, k, v, n_lat):
    bsz, h, l_all, dk = q.shape
    hkv = k.shape[1]
    grp = h // hkv
    dv = v.shape[3]
    tq = min(256, l_all - n_lat)
    kern = functools.partial(_attn_kernel, n_lat=n_lat, n_lat_tiles=n_lat // tq)
    return pl.pallas_call(
        kern,
        grid=(bsz, h, l_all // tq),
        in_specs=[pl.BlockSpec((None, None, tq, dk), lambda b, hh, i: (b, hh, i, 0)),
                  pl.BlockSpec((None, None, l_all, dk), lambda b, hh, i: (b, hh // grp, 0, 0)),
                  pl.BlockSpec((None, None, l_all, dv), lambda b, hh, i: (b, hh // grp, 0, 0))],
        out_specs=pl.BlockSpec((None, tq, dv), lambda b, hh, i: (b, i, hh)),
        out_shape=jax.ShapeDtypeStruct((bsz, l_all, h * dv), MXU_DT),
        compiler_params=_params("arbitrary", "arbitrary", "arbitrary"),
        name="attention",
    )(q, k, v)


def _conv3(u, w, b, n_lat):
    n = u.shape[0]
    row = lax.broadcasted_iota(jnp.int32, (n, 1), 0)
    prev = jnp.where((row == 0) | (row == n_lat), 0.0, pltpu.roll(u, 1, 0))
    nxt = jnp.where((row == n_lat - 1) | (row == n - 1), 0.0, pltpu.roll(u, n - 1, 0))
    return w[0:1, :] * prev + w[1:2, :] * u + w[2:3, :] * nxt + b


def _hy_conv_kernel(p0_ref, p1_ref, pv_ref, w0_ref, w1_ref, wv_ref, b0_ref, b1_ref, bv_ref,
                    x0_ref, vx_ref, *, n_lat):
    x0_ref[...] = _conv3(p0_ref[...], w0_ref[...], b0_ref[...], n_lat)
    x1 = _conv3(p1_ref[...], w1_ref[...], b1_ref[...], n_lat)
    v = _conv3(pv_ref[...], wv_ref[...], bv_ref[...], n_lat)
    vx_ref[...] = v * x1


def hyena_conv(p, conv_w, conv_b3, layer, n_lat):
    bsz, l_all, _ = p.shape
    nb = HY_W // LANE
    base = P_HY // LANE

    def pspec(off):
        return pl.BlockSpec((None, l_all, LANE), lambda b, c: (b, 0, base + off + c))

    def wspec(off):
        return pl.BlockSpec((None, 3, LANE), lambda b, c: (layer, 0, off + c))

    def bspec(off):
        return pl.BlockSpec((None, 1, LANE), lambda b, c: (layer, 0, off + c))

    o_spec = pl.BlockSpec((None, l_all, LANE), lambda b, c: (b, 0, c))
    return pl.pallas_call(
        functools.partial(_hy_conv_kernel, n_lat=n_lat),
        grid=(bsz, nb),
        in_specs=[pspec(0), pspec(nb), pspec(2 * nb), wspec(0), wspec(nb), wspec(2 * nb),
                  bspec(0), bspec(nb), bspec(2 * nb)],
        out_specs=[o_spec, o_spec],
        out_shape=[jax.ShapeDtypeStruct((bsz, l_all, HY_W), F32)] * 2,
        compiler_params=_params("arbitrary", "arbitrary"),
        name="hyena_conv",
    )(p, p, p, conv_w, conv_w, conv_w, conv_b3, conv_b3, conv_b3)


def _mb_conv_kernel(p_ref, w_ref, b_ref, o_ref, *, n_lat):
    o_ref[...] = _silu(_conv3(p_ref[...], w_ref[...], b_ref[...], n_lat))


def mamba_conv(p, conv_w, conv_b3, layer, n_lat):
    bsz, l_all, _ = p.shape
    nb = MB_CONV_CH // LANE
    base = P_XBC // LANE
    return pl.pallas_call(
        functools.partial(_mb_conv_kernel, n_lat=n_lat),
        grid=(bsz, nb),
        in_specs=[pl.BlockSpec((None, l_all, LANE), lambda b, c: (b, 0, base + c)),
                  pl.BlockSpec((None, 3, LANE), lambda b, c: (layer, 0, c)),
                  pl.BlockSpec((None, 1, LANE), lambda b, c: (layer, 0, c))],
        out_specs=pl.BlockSpec((None, l_all, LANE), lambda b, c: (b, 0, c)),
        out_shape=jax.ShapeDtypeStruct((bsz, l_all, MB_CONV_CH), F32),
        compiler_params=_params("arbitrary", "arbitrary"),
        name="mamba_conv",
    )(p, conv_w, conv_b3)


def _softplus(x):
    return jnp.maximum(x, 0.0) + jnp.log(1.0 + jnp.exp(-jnp.abs(x)))


def _mb_dt_kernel(p_ref, bias_ref, o_ref):
    dt = _softplus(p_ref[...] + bias_ref[...])
    o_ref[0] = dt
    o_ref[1] = pltpu.roll(dt, LANE - MB_HEADS, 1)


def mamba_dt(p, dt_bias_row, layer):
    bsz, l_all, _ = p.shape
    return pl.pallas_call(
        _mb_dt_kernel,
        grid=(bsz,),
        in_specs=[pl.BlockSpec((None, l_all, LANE), lambda b: (b, 0, P_DT // LANE)),
                  pl.BlockSpec((None, 1, LANE), lambda b: (layer, 0, 0))],
        out_specs=pl.BlockSpec((2, None, l_all, LANE), lambda b: (0, b, 0, 0)),
        out_shape=jax.ShapeDtypeStruct((2, bsz, l_all, LANE), F32),
        compiler_params=_params("arbitrary"),
        name="mamba_dt",
    )(p, dt_bias_row)


def _hy_filter_kernel(f_ref, aux_ref, w1_ref, b1_ref, w2_ref, b2_ref, w3_ref, fr_ref, dl_ref, o_ref):
    fr = fr_ref[...]
    hdn = jnp.sin(fr * (_dot_hi(f_ref[...], w1_ref[...]) + b1_ref[...]))
    hdn = jnp.sin(fr * (_dot_hi(hdn, w2_ref[...]) + b2_ref[...]))
    filt = _dot_hi(hdn, w3_ref[...])
    aux = aux_ref[...]
    t = aux[:, 0:1]
    is_fwd = aux[:, 1:2] > 0.5
    valid = aux[:, 2:3]
    window = jnp.exp(-t * dl_ref[...]) * valid
    o_ref[...] = jnp.where(is_fwd, filt[:, :HY_W], filt[:, HY_W:]) * window


def hyena_filter(feats2, aux, w1p, b1p, w2p, b2p, w3p, frp, deltas, layer):
    rows = feats2.shape[0]
    tr = _pick(rows, (512, 256))

    def lspec(shape):
        return pl.BlockSpec((None,) + shape, lambda i: (layer, 0, 0))

    return pl.pallas_call(
        _hy_filter_kernel,
        grid=(rows // tr,),
        in_specs=[pl.BlockSpec((tr, LANE), lambda i: (i, 0)),
                  pl.BlockSpec((tr, LANE), lambda i: (i, 0)),
                  lspec((LANE, LANE)), lspec((1, LANE)), lspec((LANE, LANE)), lspec((1, LANE)),
                  lspec((LANE, 2 * HY_W)), lspec((1, LANE)),
                  pl.BlockSpec((1, HY_W), lambda i: (0, 0))],
        out_specs=pl.BlockSpec((tr, HY_W), lambda i: (i, 0)),
        out_shape=jax.ShapeDtypeStruct((rows, HY_W), F32),
        compiler_params=_params("arbitrary"),
        name="hyena_filter",
    )(feats2, aux, w1p, b1p, w2p, b2p, w3p, frp, deltas)


def _left_mm_kernel(f_ref, x_ref, o_ref):
    o_ref[...] = _dot_hi(f_ref[...], x_ref[...])


def left_matmul(fmat, x, k_rows):
    bsz, _, n = x.shape
    m = fmat.shape[0]
    tn = _pick(n, (8192, 4096, 2048))
    return pl.pallas_call(
        _left_mm_kernel,
        grid=(bsz, n // tn),
        in_specs=[pl.BlockSpec((m, k_rows), lambda b, j: (0, 0)),
                  pl.BlockSpec((None, k_rows, tn), lambda b, j: (b, 0, j))],
        out_specs=pl.BlockSpec((None, m, tn), lambda b, j: (b, 0, j)),
        out_shape=jax.ShapeDtypeStruct((bsz, m, n), F32),
        compiler_params=_params("arbitrary", "arbitrary"),
        name="dft_outer_stage",
    )(fmat, x)


def _dft_inner_kernel(a_ref, c_ref, s_ref, mf_ref, o_ref, *, k1b):
    def body(t, carry):
        are = a_ref[0, t]
        aim = a_ref[1, t]
        c = c_ref[t]
        s = s_ref[t]
        z = jnp.concatenate([are * c + aim * s, aim * c - are * s], axis=0)
        x = _dot_hi(mf_ref[...], z)
        o_ref[0, t] = x[:LANE]
        o_ref[1, t] = x[LANE:]
        return carry

    lax.fori_loop(0, k1b, body, 0)


def _dft_conv_inner_kernel(a_ref, h_ref, c_ref, s_ref, mf_ref, mi_ref, o_ref, *, k1b):
    def body(t, carry):
        are = a_ref[0, t]
        aim = a_ref[1, t]
        c = c_ref[t]
        s = s_ref[t]
        z = jnp.concatenate([are * c + aim * s, aim * c - are * s], axis=0)
        x = _dot_hi(mf_ref[...], z)
        xre = x[:LANE]
        xim = x[LANE:]
        hre = h_ref[0, t]
        him = h_ref[1, t]
        y = jnp.concatenate([xre * hre - xim * him, xre * him + xim * hre], axis=0)
        g = _dot_hi(mi_ref[...], y)
        gre = g[:LANE]
        gim = g[LANE:]
        o_ref[0, t] = gre * c - gim * s
        o_ref[1, t] = gim * c + gre * s
        return carry

    lax.fori_loop(0, k1b, body, 0)


def dft_inner(a5, tw_c, tw_s, mf):
    bsz, _, n1, _, ch = a5.shape
    k1b = _pick(n1, (8, 4))
    blk = pl.BlockSpec((None, 2, k1b, LANE, LANE), lambda b, c, k: (b, 0, k, 0, c))
    tw = pl.BlockSpec((k1b, LANE, LANE), lambda b, c, k: (k, 0, 0))
    mat = pl.BlockSpec((2 * LANE, 2 * LANE), lambda b, c, k: (0, 0))
    return pl.pallas_call(
        functools.partial(_dft_inner_kernel, k1b=k1b),
        grid=(bsz, ch // LANE, n1 // k1b),
        in_specs=[blk, tw, tw, mat],
        out_specs=blk,
        out_shape=jax.ShapeDtypeStruct(a5.shape, F32),
        compiler_params=_params("arbitrary", "arbitrary", "arbitrary"),
        name="dft_inner",
    )(a5, tw_c, tw_s, mf)


def dft_conv_inner(a5, h4, tw_c, tw_s, mf, mi):
    bsz, _, n1, _, ch = a5.shape
    k1b = _pick(n1, (8, 4))
    blk = pl.BlockSpec((None, 2, k1b, LANE, LANE), lambda b, c, k: (b, 0, k, 0, c))
    hblk = pl.BlockSpec((None, 2, k1b, LANE, LANE), lambda b, c, k: (0, 0, k, 0, c))
    tw = pl.BlockSpec((k1b, LANE, LANE), lambda b, c, k: (k, 0, 0))
    mat = pl.BlockSpec((2 * LANE, 2 * LANE), lambda b, c, k: (0, 0))
    return pl.pallas_call(
        functools.partial(_dft_conv_inner_kernel, k1b=k1b),
        grid=(bsz, ch // LANE, n1 // k1b),
        in_specs=[blk, hblk, tw, tw, mat, mat],
        out_specs=blk,
        out_shape=jax.ShapeDtypeStruct(a5.shape, F32),
        compiler_params=_params("arbitrary", "arbitrary", "arbitrary"),
        name="dft_conv_inner",
    )(a5, h4, tw_c, tw_s, mf, mi)


def _dft_tables(n_seq):
    n = 2 * n_seq
    n1 = n // LANE
    k1 = np.arange(n1)[:, None]
    ang1 = 2.0 * np.pi * k1 * np.arange(n1)[None, :] / n1
    f_outer = np.concatenate([np.cos(ang1), -np.sin(ang1)], axis=0)
    f_outer_inv = np.concatenate([np.cos(ang1), -np.sin(ang1)], axis=1)[: n1 // 2] / n
    ang_t = 2.0 * np.pi * k1 * np.arange(LANE)[None, :] / n
    tw_c = np.repeat(np.cos(ang_t)[:, :, None], LANE, axis=2)
    tw_s = np.repeat(np.sin(ang_t)[:, :, None], LANE, axis=2)
    ang2 = 2.0 * np.pi * np.arange(LANE)[:, None] * np.arange(LANE)[None, :] / LANE
    cf, sf = np.cos(ang2), np.sin(ang2)
    mf = np.block([[cf, sf], [-sf, cf]])
    mi = np.block([[cf, -sf], [sf, cf]])
    as32 = lambda a: jnp.asarray(a, dtype=F32)
    return dict(n1=n1, f_outer=as32(f_outer), f_outer_half=as32(f_outer[:, : n1 // 2]),
                f_outer_inv=as32(f_outer_inv), tw_c=as32(tw_c), tw_s=as32(tw_s), mf=as32(mf), mi=as32(mi))


def long_conv_latent(vx, k2, tb, n_lat):
    bsz, l_all, ch = vx.shape
    n1 = tb["n1"]
    hk = left_matmul(tb["f_outer"], k2.reshape(1, n1, LANE * ch), n1)
    h4 = dft_inner(hk.reshape(1, 2, n1, LANE, ch), tb["tw_c"], tb["tw_s"], tb["mf"])
    a = left_matmul(tb["f_outer_half"], vx.reshape(bsz, l_all // LANE, LANE * ch), n1 // 2)
    g = dft_conv_inner(a.reshape(bsz, 2, n1, LANE, ch), h4, tb["tw_c"], tb["tw_s"], tb["mf"], tb["mi"])
    y = left_matmul(tb["f_outer_inv"], g.reshape(bsz, 2 * n1, LANE * ch), 2 * n1)
    return y.reshape(bsz, n_lat, ch)


def _ctx_conv_kernel(v_ref, k_ref, fd_ref, fk_ref, gi_ref, o_ref, *, kp):
    x = _dot_hi(fd_ref[...], v_ref[...])
    h = _dot_hi(fk_ref[...], k_ref[...])
    xre, xim = x[:kp], x[kp:]
    hre, him = h[:kp], h[kp:]
    y = jnp.concatenate([xre * hre - xim * him, xre * him + xim * hre], axis=0)
    o_ref[...] = _dot_hi(gi_ref[...], y)


def _ctx_tables(n_ctx):
    n = 2 * n_ctx
    kh = n_ctx + 1
    kp = ((kh + 7) // 8) * 8
    k = np.arange(kp)[:, None]
    live = (k < kh).astype(np.float64)
    ang = 2.0 * np.pi * k * np.arange(n)[None, :] / n
    ck, sk = np.cos(ang) * live, np.sin(ang) * live
    fk = np.concatenate([ck, -sk], axis=0)
    fd = fk[:, :n_ctx]
    wk = np.where((k == 0) | (k == n_ctx), 1.0, 2.0) * live / n
    gi = np.concatenate([(ck * wk).T, (-sk * wk).T], axis=1)[:n_ctx]
    as32 = lambda a: jnp.asarray(a, dtype=F32)
    return dict(kp=kp, fd=as32(fd), fk=as32(fk), gi=as32(gi))


def long_conv_ctx(vx, k2c, tb, n_lat, n_ctx):
    bsz, _, ch = vx.shape
    kp = tb["kp"]
    full = lambda a: pl.BlockSpec(a.shape, lambda b, c: (0, 0))
    return pl.pallas_call(
        functools.partial(_ctx_conv_kernel, kp=kp),
        grid=(bsz, ch // LANE),
        in_specs=[pl.BlockSpec((None, n_ctx, LANE), lambda b, c: (b, n_lat // n_ctx, c)),
                  pl.BlockSpec((2 * n_ctx, LANE), lambda b, c: (0, c)),
                  full(tb["fd"]), full(tb["fk"]), full(tb["gi"])],
        out_specs=pl.BlockSpec((None, n_ctx, LANE), lambda b, c: (b, 0, c)),
        out_shape=jax.ShapeDtypeStruct((bsz, n_ctx, ch), F32),
        compiler_params=_params("arbitrary", "arbitrary"),
        name="ctx_long_conv",
    )(vx, k2c, tb["fd"], tb["fk"], tb["gi"])


def _hy_finish_kernel(x0_ref, vx_ref, yl_ref, yc_ref, skip_ref, o_ref, *, n_lat_tiles):
    y = jnp.where(pl.program_id(1) < n_lat_tiles, yl_ref[...], yc_ref[...])
    o_ref[...] = (x0_ref[...] * (y + vx_ref[...] * skip_ref[...])).astype(o_ref.dtype)


def hyena_finish(x0, vx, y_lat, y_ctx, skip3, layer, n_lat):
    bsz, l_all, ch = x0.shape
    tr = l_all - n_lat
    nlt = n_lat // tr
    row = pl.BlockSpec((None, tr, ch), lambda b, i: (b, i, 0))
    return pl.pallas_call(
        functools.partial(_hy_finish_kernel, n_lat_tiles=nlt),
        grid=(bsz, l_all // tr),
        in_specs=[row, row,
                  pl.BlockSpec((None, tr, ch), lambda b, i: (b, jnp.minimum(i, nlt - 1), 0)),
                  pl.BlockSpec((None, tr, ch), lambda b, i: (b, 0, 0)),
                  pl.BlockSpec((None, 1, ch), lambda b, i: (layer, 0, 0))],
        out_specs=row,
        out_shape=jax.ShapeDtypeStruct((bsz, l_all, ch), MXU_DT),
        compiler_params=_params("arbitrary", "arbitrary"),
        name="hyena_finish",
    )(x0, vx, y_lat, y_ctx, skip3)


def _ssd_kernel(xs_ref, b_ref, c_ref, dt_ref, alog_ref, ex_ref, y_ref, state_ref):
    d = pl.program_id(1)
    n = MB_CHUNK

    @pl.when(pl.program_id(2) == 0)
    def _():
        state_ref[...] = jnp.zeros_like(state_ref)

    li = lax.broadcasted_iota(jnp.int32, (n, n), 0)
    si = lax.broadcasted_iota(jnp.int32, (n, n), 1)
    mask = (li - si) * (1 - 2 * d) >= 0
    tri = mask.astype(F32)
    lane = lax.broadcasted_iota(jnp.int32, (n, LANE), 1)

    dt = dt_ref[...]
    da = dt * (-jnp.exp(alog_ref[...]))
    cum = _dot_hi(tri, da)
    cum_t = cum.T
    ex = ex_ref[...]
    cum_e = _dot_hi(cum, ex)
    dt_e = _dot_hi(dt, ex)
    total = jnp.sum(da, axis=0, keepdims=True)
    total_e = _dot_hi(jnp.broadcast_to(total, (8, LANE)), ex)[0:1]

    xdt = xs_ref[...] * dt_e
    xdt_b = xdt.astype(MXU_DT)
    w_b = (jnp.exp(total_e - cum_e) * xdt).astype(MXU_DT)
    st = state_ref[...]
    y_parts = []
    s_parts = []
    gw = MB_STATE
    hw = (MB_HEADS // MB_GROUPS) * MB_HEADDIM
    for g in range(MB_GROUPS):
        cg = c_ref[:, g * gw:(g + 1) * gw].astype(MXU_DT)
        bg = b_ref[:, g * gw:(g + 1) * gw]
        cb = _dot_nt(cg, bg.astype(MXU_DT))
        y_off = _dot(cg, st[:, g * hw:(g + 1) * hw].astype(MXU_DT))
        s_parts.append(_dot(bg.T.astype(MXU_DT), w_b[:, g * hw:(g + 1) * hw]))
        diag = []
        for j in range(hw // LANE):
            lo = g * hw + j * LANE
            xp = xdt_b[:, lo:lo + LANE]
            pair = []
            for e in range(2):
                hd = (lo // MB_HEADDIM) + e
                seg = cum[:, hd:hd + 1] - cum_t[hd:hd + 1, :]
                decay = jnp.exp(jnp.where(mask, seg, -jnp.inf))
                pair.append(_dot((cb * decay).astype(MXU_DT), xp))
            diag.append(jnp.where(lane < MB_HEADDIM, pair[0], pair[1]))
        y_parts.append(jnp.concatenate(diag, axis=1) + y_off * jnp.exp(cum_e[:, g * hw:(g + 1) * hw]))
    y_ref[...] = jnp.concatenate(y_parts, axis=1)
    state_ref[...] = st * jnp.exp(total_e) + jnp.concatenate(s_parts, axis=1)


def ssd_scan(xbc, dt2, a_log_rows, expand, layer, n_lat):
    bsz, l_all, _ = xbc.shape
    nc = l_all // MB_CHUNK
    ncl = n_lat // MB_CHUNK

    def blk(d, c):
        return jnp.where(d == 0, (c + ncl) % nc, nc - 1 - c)

    return pl.pallas_call(
        _ssd_kernel,
        grid=(bsz, 2, nc),
        in_specs=[pl.BlockSpec((None, MB_CHUNK, MB_INNER), lambda b, d, c: (b, blk(d, c), 0)),
                  pl.BlockSpec((None, MB_CHUNK, 256), lambda b, d, c: (b, blk(d, c), 2)),
                  pl.BlockSpec((None, MB_CHUNK, 256), lambda b, d, c: (b, blk(d, c), 3)),
                  pl.BlockSpec((None, None, MB_CHUNK, LANE), lambda b, d, c: (d, b, blk(d, c), 0)),
                  pl.BlockSpec((None, None, 1, LANE), lambda b, d, c: (layer, d, 0, 0)),
                  pl.BlockSpec((LANE, MB_INNER), lambda b, d, c: (0, 0))],
        out_specs=pl.BlockSpec((None, None, MB_CHUNK, MB_INNER), lambda b, d, c: (d, b, blk(d, c), 0)),
        out_shape=jax.ShapeDtypeStruct((2, bsz, l_all, MB_INNER), F32),
        scratch_shapes=[pltpu.VMEM((MB_STATE, MB_INNER), F32)],
        compiler_params=_params("arbitrary", "arbitrary", "arbitrary"),
        name="ssd_scan",
    )(xbc, xbc, xbc, dt2, a_log_rows, expand)


def _mb_finish_kernel(yf_ref, yb_ref, xs_ref, z_ref, dsk_ref, g_ref, o_ref):
    y = yf_ref[...] + yb_ref[...] + xs_ref[...] * dsk_ref[...]
    y = y * _silu(z_ref[...])
    r = lax.rsqrt(jnp.mean(y * y, axis=-1, keepdims=True) + RMS_EPS)
    o_ref[...] = (y * r * g_ref[...]).astype(o_ref.dtype)


def mamba_finish(y2, xbc, p, d_skip_e, norm_g3, layer):
    bsz, l_all, _ = xbc.shape
    tr = _pick(l_all, (544, 384, 128))
    w = MB_INNER
    return pl.pallas_call(
        _mb_finish_kernel,
        grid=(bsz, l_all // tr),
        in_specs=[pl.BlockSpec((None, None, tr, w), lambda b, i: (0, b, i, 0)),
                  pl.BlockSpec((None, None, tr, w), lambda b, i: (1, b, i, 0)),
                  pl.BlockSpec((None, tr, w), lambda b, i: (b, i, 0)),
                  pl.BlockSpec((None, tr, w), lambda b, i: (b, i, P_Z // w)),
                  pl.BlockSpec((None, 1, w), lambda b, i: (layer, 0, 0)),
                  pl.BlockSpec((None, 1, w), lambda b, i: (layer, 0, 0))],
        out_specs=pl.BlockSpec((None, tr, w), lambda b, i: (b, i, 0)),
        out_shape=jax.ShapeDtypeStruct((bsz, l_all, w), MXU_DT),
        compiler_params=_params("arbitrary", "arbitrary"),
        name="mamba_finish",
    )(y2, y2, xbc, p, d_skip_e, norm_g3)


def _deinterleave(w, heads):
    lead = w.shape[:-1]
    w = w.reshape(lead + (heads, GQA_HD // 2, 2))
    return jnp.concatenate([w[..., 0], w[..., 1]], axis=-1).reshape(lead + (heads * GQA_HD,))


def _prep_w_in(w_in):
    nl, d, _ = w_in.shape
    z64 = jnp.zeros((nl, d, 64), w_in.dtype)
    mla = w_in[..., :MLA_COLS]
    cq, ckv, kr = mla[..., :448], mla[..., 448:576], mla[..., 576:640]
    ev, od = kr[..., 0::2], kr[..., 1::2]
    mla_blk = jnp.concatenate([cq, z64, ckv, ev, od, z64, -od, ev, z64], axis=-1)
    gqa = w_in[..., MLA_COLS:MLA_COLS + GQA_COLS]
    gqa_blk = jnp.concatenate([_deinterleave(gqa[..., :512], GQA_HEADS),
                               _deinterleave(gqa[..., 512:768], GQA_KV_HEADS), gqa[..., 768:]], axis=-1)
    hy = w_in[..., MLA_COLS + GQA_COLS:MLA_COLS + GQA_COLS + HY_COLS]
    mb = w_in[..., MLA_COLS + GQA_COLS + HY_COLS:]
    z, xbc, dt = mb[..., :MB_INNER], mb[..., MB_INNER:MB_INNER + MB_CONV_CH], mb[..., MB_INNER + MB_CONV_CH:]
    dt_blk = jnp.concatenate([dt, jnp.zeros((nl, d, LANE - 2 * MB_HEADS), w_in.dtype)], axis=-1)
    out = jnp.concatenate([mla_blk, dt_blk, gqa_blk, z, hy, xbc], axis=-1)
    assert out.shape[-1] == P_COLS
    return out.astype(MXU_DT)


def _prep_w_uq(w_uq):
    nl = w_uq.shape[0]
    w = w_uq.reshape(nl, MLA_Q_RANK, MLA_HEADS, MLA_NOPE + MLA_ROPE)
    nope, rot = w[..., :MLA_NOPE], w[..., MLA_NOPE:]
    ev, od = rot[..., 0::2], rot[..., 1::2]
    z64 = jnp.zeros_like(rot)
    per_head = jnp.concatenate([nope, ev, od, z64, -od, ev, z64], axis=-1)
    w = per_head.reshape(nl, MLA_Q_RANK, MLA_HEADS * 384)
    w = jnp.concatenate([w, jnp.zeros((nl, 512 - MLA_Q_RANK, MLA_HEADS * 384), w.dtype)], axis=1)
    return w.astype(MXU_DT)


def _rope_tables(n_lat, n_ctx, rot_dim, sign_folded):
    rows = n_lat // GRID_W
    row = jnp.repeat(jnp.arange(rows, dtype=F32), GRID_W)
    col = jnp.tile(jnp.arange(GRID_W, dtype=F32), rows)
    n_freq = rot_dim // 4
    inv_freq = ROPE_THETA ** (-jnp.arange(n_freq, dtype=F32) / n_freq)
    ang = jnp.concatenate([row[:, None] * inv_freq, col[:, None] * inv_freq], axis=-1)
    half = rot_dim // 2
    cos = jnp.concatenate([jnp.cos(ang), jnp.ones((n_ctx, half), F32)], axis=0)
    sin = jnp.concatenate([jnp.sin(ang), jnp.zeros((n_ctx, half), F32)], axis=0)
    pad = jnp.zeros((n_lat + n_ctx, LANE - rot_dim), F32)
    cos_t = jnp.concatenate([cos, cos, pad], axis=-1)
    sin_t = jnp.concatenate([sin if sign_folded else -sin, sin, pad], axis=-1)
    return cos_t, sin_t


def _filter_inputs(n):
    t = jnp.linspace(0.0, 1.0, n, dtype=F32)[:, None]
    omega = 2.0 * math.pi * jnp.arange(n, dtype=F32) / n
    bands = jnp.linspace(1e-4, HY_BANDS - 1, HY_BANDS, dtype=F32)
    ang = omega[:, None] * bands[None, :]
    feats = jnp.concatenate([t, jnp.cos(ang), -jnp.sin(ang)], axis=-1)
    zero = jnp.zeros((1, HY_EMB), F32)
    feats2 = jnp.concatenate([feats, zero, feats[1:][::-1]], axis=0)
    feats2 = jnp.concatenate([feats2, jnp.zeros((2 * n, LANE - HY_EMB), F32)], axis=-1)
    t2 = jnp.concatenate([t, jnp.zeros((1, 1), F32), t[1:][::-1]], axis=0)
    idx = jnp.arange(2 * n)[:, None]
    aux = jnp.concatenate([t2, (idx < n).astype(F32), (idx != n).astype(F32),
                           jnp.zeros((2 * n, LANE - 3), F32)], axis=-1)
    return feats2, aux


def _pad_to(a, shape):
    pads = [(0, s - d) for d, s in zip(a.shape, shape)]
    return jnp.pad(a, pads)


def kernel(x, c, ctx, c_ctx, w_ada, b_ada, w_in, mla_q_norm, mla_kv_norm, mla_w_uq, mla_w_ukv, gqa_q_norm, gqa_k_norm, hy_conv_w, hy_conv_b, hy_w1, hy_b1, hy_w2, hy_b2, hy_w3, hy_freq, hy_skip, mb_conv_w, mb_conv_b, mb_a_log, mb_dt_bias, mb_d, mb_norm, w_mgate, b_mgate, w_branch, w_out, ln1_g, ln1_b, w_ffn_in, w_ffn_out, ln2_g, ln2_b):
    bsz, n_lat, d = x.shape
    n_ctx = ctx.shape[1]
    nl = w_in.shape[0]
    assert d == D_MODEL and bsz < 8 and n_lat % n_ctx == 0 and n_ctx % MB_CHUNK == 0

    w_in_p = _prep_w_in(w_in)
    w_uq_p = _prep_w_uq(mla_w_uq)
    w_ukv_p = mla_w_ukv.astype(MXU_DT)
    gq_mla = _pad_to(mla_q_norm, (nl, 512))[:, None, :]
    gkv_mla = mla_kv_norm[:, None, :]
    gq_gqa = _deinterleave(gqa_q_norm, 1)[:, None, :]
    gk_gqa = _deinterleave(gqa_k_norm, 1)[:, None, :]
    w_gate_b = w_mgate.astype(MXU_DT)
    w_branch_b = w_branch.astype(MXU_DT)
    w_out_b = w_out.astype(MXU_DT)
    w_ffn_in_b = w_ffn_in.astype(MXU_DT)
    w_ffn_out_b = w_ffn_out.astype(MXU_DT)
    b_gate4 = b_mgate[:, :, None, :]
    b_ada3 = b_ada[:, None, :]
    hy_conv_b3 = hy_conv_b[:, None, :]
    mb_conv_b3 = mb_conv_b[:, None, :]
    hy_skip3 = hy_skip[:, None, :]
    ln1_g3, ln1_b3, ln2_g3, ln2_b3 = (a[:, None, :] for a in (ln1_g, ln1_b, ln2_g, ln2_b))
    hy_w1p = _pad_to(hy_w1, (nl, LANE, LANE))
    hy_b1p = _pad_to(hy_b1, (nl, LANE))[:, None, :]
    hy_w2p = _pad_to(hy_w2, (nl, LANE, LANE))
    hy_b2p = _pad_to(hy_b2, (nl, LANE))[:, None, :]
    hy_w3p = _pad_to(hy_w3, (nl, LANE, 2 * HY_W))
    hy_frp = _pad_to(hy_freq, (nl, LANE))[:, None, :]
    deltas = jnp.abs(jnp.linspace(HY_MIN_DECAY, HY_MAX_DECAY, HY_W, dtype=F32))[None, :]
    dt_bias_row = _pad_to(mb_dt_bias.reshape(nl, 2 * MB_HEADS), (nl, LANE))[:, None, :]
    a_log_rows = _pad_to(mb_a_log, (nl, 2, LANE))[:, :, None, :]
    d_skip_e = jnp.repeat(mb_d, MB_HEADDIM, axis=-1)[:, None, :]
    mb_norm3 = mb_norm[:, None, :]
    expand = jnp.asarray(np.kron(np.eye(LANE, MB_HEADS), np.ones((1, MB_HEADDIM))), dtype=F32)

    cos_m, sin_m = _rope_tables(n_lat, n_ctx, MLA_ROPE, True)
    cos_g, sin_g = _rope_tables(n_lat, n_ctx, GQA_HD, False)
    feats_lat, aux_lat = _filter_inputs(n_lat)
    feats_ctx, aux_ctx = _filter_inputs(n_ctx)
    tb_lat = _dft_tables(n_lat)
    tb_ctx = _ctx_tables(n_ctx)

    c8 = jnp.concatenate([c, c_ctx[None, :], jnp.zeros((8 - bsz - 1, d), F32)], axis=0)
    xz = jnp.concatenate([x, ctx], axis=1)

    for l in range(nl):
        last = l == nl - 1
        mod = ada_mod(c8, w_ada, b_ada3, l).reshape(8, 1, 6 * d)
        p, h = in_projection(xz, mod, w_in_p, l, n_lat)

        q, k, v = mla_prep(p, gq_mla, gkv_mla, w_uq_p, w_ukv_p, cos_m, sin_m, l)
        oa = attention(q, k, v, n_lat)
        q, k, v = gqa_prep(p, gq_gqa, gk_gqa, cos_g, sin_g, l)
        ob = attention(q, k, v, n_lat)
        x0, vx = hyena_conv(p, hy_conv_w, hy_conv_b3, l, n_lat)
        k2 = hyena_filter(feats_lat, aux_lat, hy_w1p, hy_b1p, hy_w2p, hy_b2p, hy_w3p, hy_frp, deltas, l)
        y_lat = long_conv_latent(vx, k2, tb_lat, n_lat)
        if last:
            y_ctx = jnp.zeros((bsz, n_ctx, HY_W), F32)
        else:
            k2c = hyena_filter(feats_ctx, aux_ctx, hy_w1p, hy_b1p, hy_w2p, hy_b2p, hy_w3p, hy_frp, deltas, l)
            y_ctx = long_conv_ctx(vx, k2c, tb_ctx, n_lat, n_ctx)
        oc = hyena_finish(x0, vx, y_lat, y_ctx, hy_skip3, l, n_lat)
        xbc = mamba_conv(p, mb_conv_w, mb_conv_b3, l, n_lat)
        dt2 = mamba_dt(p, dt_bias_row, l)
        y2 = ssd_scan(xbc, dt2, a_log_rows, expand, l, n_lat)
        od = mamba_finish(y2, xbc, p, d_skip_e, mb_norm3, l)

        acc = merge_branches(h, (oa, ob, oc, od), w_gate_b, b_gate4, w_branch_b, l)
        x1 = matmul_res_ln(acc, w_out_b, xz, mod, 2, ln1_g3, ln1_b3, l, n_lat, n_lat + n_ctx)
        act = ffn_in(x1, mod, w_ffn_in_b, l, n_lat)
        out_rows = n_lat if last else n_lat + n_ctx
        xz = matmul_res_ln(act, w_ffn_out_b, x1, mod, 5, ln2_g3, ln2_b3, l, n_lat, out_rows)
    return xz
```

```python
import functools
import math

import jax
import jax.numpy as jnp
import numpy as np
from jax import lax
from jax.experimental import pallas as pl
from jax.experimental.pallas import tpu as pltpu

F32 = jnp.float32
MXU_DT = jnp.bfloat16

D_MODEL = 2048
DEPTH = 2
GRID_W = 64
N_BRANCH = 4
BRANCH_W = D_MODEL // N_BRANCH
ROPE_THETA = 10000.0
LN_EPS = 1e-6
RMS_EPS = 1e-6
DEEPNORM_ALPHA = (2 * DEPTH) ** 0.25

MLA_HEADS = 4
MLA_Q_RANK = 448
MLA_KV_RANK = 128
MLA_NOPE = 128
MLA_ROPE = 64
MLA_V = 128
MLA_COLS = MLA_Q_RANK + MLA_KV_RANK + MLA_ROPE

GQA_HEADS = 4
GQA_KV_HEADS = 2
GQA_HD = 128
GQA_COLS = (GQA_HEADS + 2 * GQA_KV_HEADS) * GQA_HD

HY_W = BRANCH_W
HY_EMB = 33
HY_BANDS = (HY_EMB - 1) // 2
HY_FFN = 64
HY_MIN_DECAY = math.log(1e-2) / 1.5
HY_MAX_DECAY = math.log(1e-2) / 0.3
HY_COLS = 3 * HY_W

MB_INNER = BRANCH_W
MB_HEADDIM = 64
MB_HEADS = 8
MB_GROUPS = 2
MB_STATE = 128
MB_CHUNK = 128
MB_CONV_CH = MB_INNER + 2 * MB_GROUPS * MB_STATE
MB_COLS = MB_INNER + MB_CONV_CH + 2 * MB_HEADS

FFN_HIDDEN = 5632

LANE = 128
VMEM_LIMIT = 56 * 1024 * 1024

P_MLA = 0
P_MLA_W = 896
P_DT = 896
P_GQA = 1024
P_Z = 2048
P_HY = 2560
P_XBC = 4096
P_COLS = 5120

LOG2_E = math.log2(math.e)
_HI = lax.Precision.HIGHEST


def _dot(a, b):
    return jnp.dot(a, b, preferred_element_type=F32)


def _dot_hi(a, b):
    return jnp.dot(a, b, precision=_HI, preferred_element_type=F32)


def _dot_nt(a, b):
    return lax.dot_general(a, b, (((1,), (1,)), ((), ())), preferred_element_type=F32)


def _sigmoid(x):
    return 1.0 / (1.0 + jnp.exp(-x))


def _silu(x):
    return x * _sigmoid(x)


def _params(*sem):
    return pltpu.CompilerParams(dimension_semantics=sem, vmem_limit_bytes=VMEM_LIMIT)


def _pick(n, prefs):
    for p in prefs:
        if n % p == 0:
            return p
    raise ValueError(f"no tile for {n} in {prefs}")


def _standardize(x):
    mu = jnp.mean(x, axis=-1, keepdims=True)
    xc = x - mu
    var = jnp.mean(xc * xc, axis=-1, keepdims=True)
    return xc * lax.rsqrt(var + LN_EPS)


def _is_ctx_rows(tile_idx, tm, n_lat):
    row = tile_idx * tm + lax.broadcasted_iota(jnp.int32, (tm, 1), 0)
    return row >= n_lat


def _ada_kernel(c_ref, w_ref, b_ref, o_ref):
    cs = _silu(c_ref[...])
    o_ref[...] = _dot(cs.astype(MXU_DT), w_ref[...].astype(MXU_DT)) + b_ref[...]


def ada_mod(c8, w_ada, b_ada3, layer):
    d = c8.shape[1]
    n = w_ada.shape[2]
    tn = 1024
    return pl.pallas_call(
        _ada_kernel,
        grid=(n // tn,),
        in_specs=[pl.BlockSpec((8, d), lambda j: (0, 0)),
                  pl.BlockSpec((None, d, tn), lambda j: (layer, 0, j)),
                  pl.BlockSpec((None, 1, tn), lambda j: (layer, 0, j))],
        out_specs=pl.BlockSpec((8, tn), lambda j: (0, j)),
        out_shape=jax.ShapeDtypeStruct((8, n), F32),
        compiler_params=_params("arbitrary"),
        name="ada_mod",
    )(c8, w_ada, b_ada3)


def _modulated(x_ref, sh_ref, sc_ref, shc_ref, scc_ref, tile_idx, tm, n_lat):
    xn = _standardize(x_ref[...])
    is_ctx = _is_ctx_rows(tile_idx, tm, n_lat)
    scale = jnp.where(is_ctx, scc_ref[...], sc_ref[...])
    shift = jnp.where(is_ctx, shc_ref[...], sh_ref[...])
    return (xn * (1.0 + scale) + shift).astype(MXU_DT)


def _inproj_kernel(x_ref, sh_ref, sc_ref, shc_ref, scc_ref, w_ref, p_ref, h_ref, *, tm, n_lat):
    @pl.when(pl.program_id(2) == 0)
    def _():
        h_ref[...] = _modulated(x_ref, sh_ref, sc_ref, shc_ref, scc_ref, pl.program_id(1), tm, n_lat)

    p_ref[...] = _dot(h_ref[...], w_ref[...])


def _ffn_in_kernel(x_ref, sh_ref, sc_ref, shc_ref, scc_ref, wu_ref, wg_ref, a_ref, h_ref, *, tm, n_lat):
    @pl.when(pl.program_id(2) == 0)
    def _():
        h_ref[...] = _modulated(x_ref, sh_ref, sc_ref, shc_ref, scc_ref, pl.program_id(1), tm, n_lat)

    h = h_ref[...]
    up = _dot(h, wu_ref[...])
    gate = _dot(h, wg_ref[...])
    a_ref[...] = (_silu(gate) * up).astype(a_ref.dtype)


def _mod_specs(k_shift, k_scale, n_batch):
    d = D_MODEL
    return [pl.BlockSpec((None, 1, d), lambda b, i, j: (b, 0, k_shift)),
            pl.BlockSpec((None, 1, d), lambda b, i, j: (b, 0, k_scale)),
            pl.BlockSpec((None, 1, d), lambda b, i, j: (n_batch, 0, k_shift)),
            pl.BlockSpec((None, 1, d), lambda b, i, j: (n_batch, 0, k_scale))]


def in_projection(xz, mod, w_in_p, layer, n_lat):
    bsz, l_all, d = xz.shape
    n = w_in_p.shape[2]
    tm = _pick(l_all, (1088, 544, 384, 128))
    tn = _pick(n, (1024, 512))
    kern = functools.partial(_inproj_kernel, tm=tm, n_lat=n_lat)
    return pl.pallas_call(
        kern,
        grid=(bsz, l_all // tm, n // tn),
        in_specs=[pl.BlockSpec((None, tm, d), lambda b, i, j: (b, i, 0))]
        + _mod_specs(0, 1, bsz)
        + [pl.BlockSpec((None, d, tn), lambda b, i, j: (layer, 0, j))],
        out_specs=[pl.BlockSpec((None, tm, tn), lambda b, i, j: (b, i, j)),
                   pl.BlockSpec((None, tm, d), lambda b, i, j: (b, i, 0))],
        out_shape=[jax.ShapeDtypeStruct((bsz, l_all, n), F32),
                   jax.ShapeDtypeStruct((bsz, l_all, d), MXU_DT)],
        compiler_params=_params("arbitrary", "arbitrary", "arbitrary"),
        name="in_projection",
    )(xz, mod, mod, mod, mod, w_in_p)


def ffn_in(xz, mod, w_ffn_in, layer, n_lat):
    bsz, l_all, d = xz.shape
    hid = w_ffn_in.shape[2] // 2
    tm = _pick(l_all, (1088, 544, 384, 128))
    tn = 512
    nj = hid // tn
    kern = functools.partial(_ffn_in_kernel, tm=tm, n_lat=n_lat)
    return pl.pallas_call(
        kern,
        grid=(bsz, l_all // tm, nj),
        in_specs=[pl.BlockSpec((None, tm, d), lambda b, i, j: (b, i, 0))]
        + _mod_specs(3, 4, bsz)
        + [pl.BlockSpec((None, d, tn), lambda b, i, j: (layer, 0, j)),
           pl.BlockSpec((None, d, tn), lambda b, i, j: (layer, 0, j + nj))],
        out_specs=pl.BlockSpec((None, tm, tn), lambda b, i, j: (b, i, j)),
        out_shape=jax.ShapeDtypeStruct((bsz, l_all, hid), MXU_DT),
        scratch_shapes=[pltpu.VMEM((tm, d), MXU_DT)],
        compiler_params=_params("arbitrary", "arbitrary", "arbitrary"),
        name="ffn_in",
    )(xz, mod, mod, mod, mod, w_ffn_in, w_ffn_in)


def _resln_kernel(a_ref, w_ref, res_ref, g_ref, gc_ref, lng_ref, lnb_ref, o_ref, *, tm, n_lat, nk, rc):
    k = pl.program_id(2)
    i = pl.program_id(1)

    @pl.when(k == 0)
    def _():
        o_ref[...] = _dot(a_ref[...], w_ref[...])

    @pl.when(k > 0)
    def _():
        o_ref[...] += _dot(a_ref[...], w_ref[...])

    @pl.when(k == nk - 1)
    def _():
        def body(r, carry):
            rows = pl.ds(pl.multiple_of(r * rc, 8), rc)
            row = i * tm + r * rc + lax.broadcasted_iota(jnp.int32, (rc, 1), 0)
            gate = jnp.where(row >= n_lat, gc_ref[...], g_ref[...])
            y = DEEPNORM_ALPHA * res_ref[rows, :] + gate * o_ref[rows, :]
            o_ref[rows, :] = _standardize(y) * lng_ref[...] + lnb_ref[...]
            return carry

        lax.fori_loop(0, tm // rc, body, 0)


def matmul_res_ln(a, w, res, mod, k_gate, ln_g, ln_b, layer, n_lat, out_rows, tm_prefs, tk):
    bsz, l_all, kdim = a.shape
    d = w.shape[2]
    tm = _pick(l_all, tm_prefs)
    nk = kdim // tk
    rc = _pick(tm, (136, 128))
    kern = functools.partial(_resln_kernel, tm=tm, n_lat=n_lat, nk=nk, rc=rc)
    return pl.pallas_call(
        kern,
        grid=(bsz, pl.cdiv(out_rows, tm), nk),
        in_specs=[pl.BlockSpec((None, tm, tk), lambda b, i, k: (b, i, k)),
                  pl.BlockSpec((None, tk, d), lambda b, i, k: (layer, k, 0)),
                  pl.BlockSpec((None, tm, d), lambda b, i, k: (b, i, 0)),
                  pl.BlockSpec((None, 1, d), lambda b, i, k: (b, 0, k_gate)),
                  pl.BlockSpec((None, 1, d), lambda b, i, k: (bsz, 0, k_gate)),
                  pl.BlockSpec((None, 1, d), lambda b, i, k: (layer, 0, 0)),
                  pl.BlockSpec((None, 1, d), lambda b, i, k: (layer, 0, 0))],
        out_specs=pl.BlockSpec((None, tm, d), lambda b, i, k: (b, i, 0)),
        out_shape=jax.ShapeDtypeStruct((bsz, out_rows, d), F32),
        compiler_params=_params("arbitrary", "arbitrary", "arbitrary"),
        name="matmul_res_ln",
    )(a, w, res, mod, mod, ln_g, ln_b)


def _merge_kernel(h_ref, oa_ref, ob_ref, oc_ref, od_ref, wg_ref, bg_ref, wb_ref, o_ref):
    h = h_ref[...]
    acc = None
    for i, o_r in enumerate((oa_ref, ob_ref, oc_ref, od_ref)):
        g = _dot(h, wg_ref[i]) + bg_ref[i]
        t = _dot(o_r[...], wb_ref[i])
        term = _sigmoid(g) * t
        acc = term if acc is None else acc + term
    o_ref[...] = acc.astype(o_ref.dtype)


def merge_branches(h, outs, w_gate, b_gate4, w_branch, layer):
    bsz, l_all, d = h.shape
    bw = outs[0].shape[2]
    tm = _pick(l_all, (1088, 544, 384, 128))
    tn = 512
    o_spec = pl.BlockSpec((None, tm, bw), lambda b, i, j: (b, i, 0))
    return pl.pallas_call(
        _merge_kernel,
        grid=(bsz, l_all // tm, d // tn),
        in_specs=[pl.BlockSpec((None, tm, d), lambda b, i, j: (b, i, 0)), o_spec, o_spec, o_spec, o_spec,
                  pl.BlockSpec((None, N_BRANCH, d, tn), lambda b, i, j: (layer, 0, 0, j)),
                  pl.BlockSpec((None, N_BRANCH, 1, tn), lambda b, i, j: (layer, 0, 0, j)),
                  pl.BlockSpec((None, N_BRANCH, bw, tn), lambda b, i, j: (layer, 0, 0, j))],
        out_specs=pl.BlockSpec((None, tm, tn), lambda b, i, j: (b, i, j)),
        out_shape=jax.ShapeDtypeStruct((bsz, l_all, d), MXU_DT),
        compiler_params=_params("arbitrary", "arbitrary", "arbitrary"),
        name="merge_branches",
    )(h, *outs, w_gate, b_gate4, w_branch)


def _mla_prep_kernel(p_ref, gq_ref, gkv_ref, wuq_ref, wukv_ref, cos_ref, sin_ref, q_ref, k_ref, v_ref):
    scale = (MLA_NOPE + MLA_ROPE) ** -0.5 * LOG2_E
    p = p_ref[...]
    cq = p[:, 0:512]
    ckv = p[:, 512:640]
    ka = p[:, 640:768]
    kb = p[:, 768:896]
    rq = lax.rsqrt(jnp.sum(cq * cq, axis=-1, keepdims=True) * (1.0 / MLA_Q_RANK) + RMS_EPS)
    cqn = (cq * rq * gq_ref[...]).astype(MXU_DT)
    rkv = lax.rsqrt(jnp.mean(ckv * ckv, axis=-1, keepdims=True) + RMS_EPS)
    ckvn = (ckv * rkv * gkv_ref[...]).astype(MXU_DT)
    qf = _dot(cqn, wuq_ref[...])
    kvf = _dot(ckvn, wukv_ref[...])
    cos = cos_ref[...]
    sin = sin_ref[...]
    k_rope = (ka * cos + kb * sin).astype(k_ref.dtype)
    for hd in range(MLA_HEADS):
        qb = hd * 384
        q_ref[hd, :, 0:128] = (qf[:, qb:qb + 128] * scale).astype(q_ref.dtype)
        q_rope = qf[:, qb + 128:qb + 256] * cos + qf[:, qb + 256:qb + 384] * sin
        q_ref[hd, :, 128:256] = (q_rope * scale).astype(q_ref.dtype)
        k_ref[hd, :, 0:128] = kvf[:, hd * 256:hd * 256 + 128].astype(k_ref.dtype)
        k_ref[hd, :, 128:256] = k_rope
        v_ref[hd] = kvf[:, hd * 256 + 128:hd * 256 + 256].astype(v_ref.dtype)


def mla_prep(p, gq, gkv, wuq, wukv, cos, sin, layer):
    bsz, l_all, _ = p.shape
    tr = _pick(l_all, (544, 384, 128))
    h = MLA_HEADS
    return pl.pallas_call(
        _mla_prep_kernel,
        grid=(bsz, l_all // tr),
        in_specs=[pl.BlockSpec((None, tr, P_MLA_W), lambda b, i: (b, i, 0)),
                  pl.BlockSpec((None, 1, 512), lambda b, i: (layer, 0, 0)),
                  pl.BlockSpec((None, 1, 128), lambda b, i: (layer, 0, 0)),
                  pl.BlockSpec((None, 512, h * 384), lambda b, i: (layer, 0, 0)),
                  pl.BlockSpec((None, 128, h * 256), lambda b, i: (layer, 0, 0)),
                  pl.BlockSpec((tr, LANE), lambda b, i: (i, 0)),
                  pl.BlockSpec((tr, LANE), lambda b, i: (i, 0))],
        out_specs=[pl.BlockSpec((None, h, tr, 256), lambda b, i: (b, 0, i, 0)),
                   pl.BlockSpec((None, h, tr, 256), lambda b, i: (b, 0, i, 0)),
                   pl.BlockSpec((None, h, tr, 128), lambda b, i: (b, 0, i, 0))],
        out_shape=[jax.ShapeDtypeStruct((bsz, h, l_all, 256), MXU_DT),
                   jax.ShapeDtypeStruct((bsz, h, l_all, 256), MXU_DT),
                   jax.ShapeDtypeStruct((bsz, h, l_all, 128), MXU_DT)],
        compiler_params=_params("arbitrary", "arbitrary"),
        name="mla_prep",
    )(p, gq, gkv, wuq, wukv, cos, sin)


def _gqa_prep_kernel(p_ref, gq_ref, gk_ref, cos_ref, sin_ref, q_ref, k_ref, v_ref):
    scale = GQA_HD ** -0.5 * LOG2_E
    cos = cos_ref[...]
    sin = sin_ref[...]

    def norm_rope(x, g):
        r = lax.rsqrt(jnp.mean(x * x, axis=-1, keepdims=True) + RMS_EPS)
        xn = x * r * g
        return xn * cos + pltpu.roll(xn, GQA_HD // 2, 1) * sin

    for hd in range(GQA_HEADS):
        x = p_ref[:, hd * 128:(hd + 1) * 128]
        q_ref[hd] = (norm_rope(x, gq_ref[...]) * scale).astype(q_ref.dtype)
    for hd in range(GQA_KV_HEADS):
        x = p_ref[:, 512 + hd * 128:512 + (hd + 1) * 128]
        k_ref[hd] = norm_rope(x, gk_ref[...]).astype(k_ref.dtype)
        v_ref[hd] = p_ref[:, 768 + hd * 128:768 + (hd + 1) * 128].astype(v_ref.dtype)


def gqa_prep(p, gq, gk, cos, sin, layer):
    bsz, l_all, _ = p.shape
    tr = _pick(l_all, (544, 384, 128))
    return pl.pallas_call(
        _gqa_prep_kernel,
        grid=(bsz, l_all // tr),
        in_specs=[pl.BlockSpec((None, tr, GQA_COLS), lambda b, i: (b, i, P_GQA // GQA_COLS)),
                  pl.BlockSpec((None, 1, 128), lambda b, i: (layer, 0, 0)),
                  pl.BlockSpec((None, 1, 128), lambda b, i: (layer, 0, 0)),
                  pl.BlockSpec((tr, LANE), lambda b, i: (i, 0)),
                  pl.BlockSpec((tr, LANE), lambda b, i: (i, 0))],
        out_specs=[pl.BlockSpec((None, GQA_HEADS, tr, 128), lambda b, i: (b, 0, i, 0)),
                   pl.BlockSpec((None, GQA_KV_HEADS, tr, 128), lambda b, i: (b, 0, i, 0)),
                   pl.BlockSpec((None, GQA_KV_HEADS, tr, 128), lambda b, i: (b, 0, i, 0))],
        out_shape=[jax.ShapeDtypeStruct((bsz, GQA_HEADS, l_all, 128), MXU_DT),
                   jax.ShapeDtypeStruct((bsz, GQA_KV_HEADS, l_all, 128), MXU_DT),
                   jax.ShapeDtypeStruct((bsz, GQA_KV_HEADS, l_all, 128), MXU_DT)],
        compiler_params=_params("arbitrary", "arbitrary"),
        name="gqa_prep",
    )(p, gq, gk, cos, sin)


def _attn_kernel(q_ref, k_ref, v_ref, o_ref, *, n_lat, n_lat_tiles, sub):
    i = pl.program_id(2)

    def attend(q, k, v):
        s = _dot_nt(q, k)
        m = jnp.max(s, axis=-1, keepdims=True)
        e = jnp.exp2(s - m)
        l = jnp.sum(e, axis=-1, keepdims=True)
        o = _dot(e.astype(v.dtype), v)
        return (o / l).astype(o_ref.dtype)

    @pl.when(i < n_lat_tiles)
    def _():
        for r in range(0, q_ref.shape[0], sub):
            o_ref[r:r + sub, :] = attend(q_ref[r:r + sub, :], k_ref[...], v_ref[...])

    @pl.when(i >= n_lat_tiles)
    def _():
        for r in range(0, q_ref.shape[0], sub):
            o_ref[r:r + sub, :] = attend(q_ref[r:r + sub, :], k_ref[n_lat:, :], v_ref[n_lat:, :])


def attention(q, k, v, n_lat):
    bsz, h, l_all, dk = q.shape
    hkv = k.shape[1]
    grp = h // hkv
    dv = v.shape[3]
    tq = _pick(n_lat, (512, 256, 128))
    assert l_all - n_lat <= tq
    kern = functools.partial(_attn_kernel, n_lat=n_lat, n_lat_tiles=n_lat // tq, sub=min(256, tq))
    return pl.pallas_call(
        kern,
        grid=(bsz, h, pl.cdiv(l_all, tq)),
        in_specs=[pl.BlockSpec((None, None, tq, dk), lambda b, hh, i: (b, hh, i, 0)),
                  pl.BlockSpec((None, None, l_all, dk), lambda b, hh, i: (b, hh // grp, 0, 0)),
                  pl.BlockSpec((None, None, l_all, dv), lambda b, hh, i: (b, hh // grp, 0, 0))],
        out_specs=pl.BlockSpec((None, tq, dv), lambda b, hh, i: (b, i, hh)),
        out_shape=jax.ShapeDtypeStruct((bsz, l_all, h * dv), MXU_DT),
        compiler_params=_params("arbitrary", "arbitrary", "arbitrary"),
        name="attention",
    )(q, k, v)


def _conv3(u, w, b, n_lat):
    n = u.shape[0]
    row = lax.broadcasted_iota(jnp.int32, (n, 1), 0)
    prev = jnp.where((row == 0) | (row == n_lat), 0.0, pltpu.roll(u, 1, 0))
    nxt = jnp.where((row == n_lat - 1) | (row == n - 1), 0.0, pltpu.roll(u, n - 1, 0))
    return w[0:1, :] * prev + w[1:2, :] * u + w[2:3, :] * nxt + b


def _hy_conv_kernel(p0_ref, p1_ref, pv_ref, w0_ref, w1_ref, wv_ref, b0_ref, b1_ref, bv_ref,
                    x0_ref, vxt_ref, vxc_ref, *, n_lat):
    x0_ref[...] = _conv3(p0_ref[...], w0_ref[...], b0_ref[...], n_lat)
    x1 = _conv3(p1_ref[...], w1_ref[...], b1_ref[...], n_lat)
    v = _conv3(pv_ref[...], wv_ref[...], bv_ref[...], n_lat)
    vx = v * x1
    vxt_ref[...] = vx[:n_lat, :].T
    vxc_ref[...] = vx[n_lat:, :]


def hyena_conv(p, conv_w, conv_b3, layer, n_lat):
    bsz, l_all, _ = p.shape
    nb = HY_W // LANE
    base = P_HY // LANE

    def pspec(off):
        return pl.BlockSpec((None, l_all, LANE), lambda b, c: (b, 0, base + off + c))

    def wspec(off):
        return pl.BlockSpec((None, 3, LANE), lambda b, c: (layer, 0, off + c))

    def bspec(off):
        return pl.BlockSpec((None, 1, LANE), lambda b, c: (layer, 0, off + c))

    o_spec = pl.BlockSpec((None, l_all, LANE), lambda b, c: (b, 0, c))
    return pl.pallas_call(
        functools.partial(_hy_conv_kernel, n_lat=n_lat),
        grid=(bsz, nb),
        in_specs=[pspec(0), pspec(nb), pspec(2 * nb), wspec(0), wspec(nb), wspec(2 * nb),
                  bspec(0), bspec(nb), bspec(2 * nb)],
        out_specs=[o_spec,
                   pl.BlockSpec((None, LANE, n_lat), lambda b, c: (b, c, 0)),
                   pl.BlockSpec((None, l_all - n_lat, LANE), lambda b, c: (b, 0, c))],
        out_shape=[jax.ShapeDtypeStruct((bsz, l_all, HY_W), F32),
                   jax.ShapeDtypeStruct((bsz, HY_W, n_lat), F32),
                   jax.ShapeDtypeStruct((bsz, l_all - n_lat, HY_W), F32)],
        compiler_params=_params("arbitrary", "arbitrary"),
        name="hyena_conv",
    )(p, p, p, conv_w, conv_w, conv_w, conv_b3, conv_b3, conv_b3)


def _mb_conv_kernel(p_ref, w_ref, b_ref, o_ref, *, n_lat):
    o_ref[...] = _silu(_conv3(p_ref[...], w_ref[...], b_ref[...], n_lat))


def mamba_conv(p, conv_w, conv_b3, layer, n_lat):
    bsz, l_all, _ = p.shape
    nb = MB_CONV_CH // LANE
    base = P_XBC // LANE
    return pl.pallas_call(
        functools.partial(_mb_conv_kernel, n_lat=n_lat),
        grid=(bsz, nb),
        in_specs=[pl.BlockSpec((None, l_all, LANE), lambda b, c: (b, 0, base + c)),
                  pl.BlockSpec((None, 3, LANE), lambda b, c: (layer, 0, c)),
                  pl.BlockSpec((None, 1, LANE), lambda b, c: (layer, 0, c))],
        out_specs=pl.BlockSpec((None, l_all, LANE), lambda b, c: (b, 0, c)),
        out_shape=jax.ShapeDtypeStruct((bsz, l_all, MB_CONV_CH), F32),
        compiler_params=_params("arbitrary", "arbitrary"),
        name="mamba_conv",
    )(p, conv_w, conv_b3)


def _softplus(x):
    return jnp.maximum(x, 0.0) + jnp.log(1.0 + jnp.exp(-jnp.abs(x)))


def _mb_dt_kernel(p_ref, bias_ref, o_ref):
    dt = _softplus(p_ref[...] + bias_ref[...])
    o_ref[0] = dt
    o_ref[1] = pltpu.roll(dt, LANE - MB_HEADS, 1)


def mamba_dt(p, dt_bias_row, layer):
    bsz, l_all, _ = p.shape
    return pl.pallas_call(
        _mb_dt_kernel,
        grid=(bsz,),
        in_specs=[pl.BlockSpec((None, l_all, LANE), lambda b: (b, 0, P_DT // LANE)),
                  pl.BlockSpec((None, 1, LANE), lambda b: (layer, 0, 0))],
        out_specs=pl.BlockSpec((2, None, l_all, LANE), lambda b: (0, b, 0, 0)),
        out_shape=jax.ShapeDtypeStruct((2, bsz, l_all, LANE), F32),
        compiler_params=_params("arbitrary"),
        name="mamba_dt",
    )(p, dt_bias_row)


def _hy_filter_kernel(f_ref, aux_ref, w1_ref, b1_ref, w2_ref, b2_ref, w3_ref, fr_ref, dl_ref, o_ref, *,
                      channel_major):
    fr = fr_ref[...]
    hdn = jnp.sin(fr * (_dot_hi(f_ref[...], w1_ref[...]) + b1_ref[...]))
    hdn = jnp.sin(fr * (_dot_hi(hdn, w2_ref[...]) + b2_ref[...]))
    filt = _dot_hi(hdn, w3_ref[...])
    aux = aux_ref[...]
    t = aux[:, 0:1]
    is_fwd = aux[:, 1:2] > 0.5
    valid = aux[:, 2:3]
    window = jnp.exp(-t * dl_ref[...]) * valid
    k2 = jnp.where(is_fwd, filt[:, :HY_W], filt[:, HY_W:]) * window
    o_ref[...] = k2.T if channel_major else k2


def hyena_filter(feats2, aux, w1p, b1p, w2p, b2p, w3p, frp, deltas, layer, channel_major):
    rows = feats2.shape[0]
    tr = _pick(rows, (512, 256))

    def lspec(shape):
        return pl.BlockSpec((None,) + shape, lambda i: (layer, 0, 0))

    if channel_major:
        out_spec = pl.BlockSpec((HY_W, tr), lambda i: (0, i))
        out_shape = jax.ShapeDtypeStruct((HY_W, rows), F32)
    else:
        out_spec = pl.BlockSpec((tr, HY_W), lambda i: (i, 0))
        out_shape = jax.ShapeDtypeStruct((rows, HY_W), F32)
    return pl.pallas_call(
        functools.partial(_hy_filter_kernel, channel_major=channel_major),
        grid=(rows // tr,),
        in_specs=[pl.BlockSpec((tr, LANE), lambda i: (i, 0)),
                  pl.BlockSpec((tr, LANE), lambda i: (i, 0)),
                  lspec((LANE, LANE)), lspec((1, LANE)), lspec((LANE, LANE)), lspec((1, LANE)),
                  lspec((LANE, 2 * HY_W)), lspec((1, LANE)),
                  pl.BlockSpec((1, HY_W), lambda i: (0, 0))],
        out_specs=out_spec,
        out_shape=out_shape,
        compiler_params=_params("arbitrary"),
        name="hyena_filter",
    )(feats2, aux, w1p, b1p, w2p, b2p, w3p, frp, deltas)


DFT_MINOR = 256


def _snap(c):
    for v in (0.0, 1.0, -1.0):
        if abs(c - v) < 1e-12:
            return v
    return float(c)


def _lincomb(terms):
    acc = None
    for cf, tile in terms:
        if cf == 0.0:
            continue
        v = tile()
        if acc is None:
            acc = v if cf == 1.0 else (-v if cf == -1.0 else cf * v)
        elif cf == 1.0:
            acc = acc + v
        elif cf == -1.0:
            acc = acc - v
        else:
            acc = acc + cf * v
    return acc


def _dft_consts(n_seq):
    n = 2 * n_seq
    n2 = DFT_MINOR
    n1 = n // n2
    nkj = n1 // 2 + 1
    ang1 = 2.0 * np.pi * np.outer(np.arange(nkj), np.arange(n1)) / n1
    cos1 = [[_snap(v) for v in r] for r in np.cos(ang1)]
    sin1 = [[_snap(v) for v in r] for r in np.sin(ang1)]
    ang_t = 2.0 * np.pi * np.outer(np.arange(nkj), np.arange(n2)) / n
    rows = ((nkj + 7) // 8) * 8
    tw_c = np.zeros((rows, n2))
    tw_s = np.zeros((rows, n2))
    tw_c[:nkj] = np.cos(ang_t)
    tw_s[:nkj] = np.sin(ang_t)
    ang2 = 2.0 * np.pi * np.outer(np.arange(n2), np.arange(n2)) / n2
    c2, s2 = np.cos(ang2), np.sin(ang2)
    m_fwd = np.block([[c2, -s2], [s2, c2]])
    m_inv = np.block([[c2, s2], [-s2, c2]])
    as32 = lambda a: jnp.asarray(a, dtype=F32)

    def split(m):
        m32 = as32(m)
        hi = m32.astype(MXU_DT)
        lo = (m32 - hi.astype(F32)).astype(MXU_DT)
        return jnp.stack([hi, lo])

    return dict(n=n, n1=n1, nkj=nkj, cos1=cos1, sin1=sin1, tw_c=as32(tw_c), tw_s=as32(tw_s),
                m_fwd=split(m_fwd), m_inv=split(m_inv))


def _dot_split(a, m_ref):
    a_hi = a.astype(MXU_DT)
    a_lo = (a - a_hi.astype(F32)).astype(MXU_DT)
    return _dot(a_hi, m_ref[0]) + _dot(a_lo, m_ref[0]) + _dot(a_hi, m_ref[1])


def _outer_fwd(src_ref, z_ref, ct, n_in, cst):
    n2 = DFT_MINOR
    nkj, cos1, sin1 = cst["nkj"], cst["cos1"], cst["sin1"]

    def body(cb, carry):
        r0 = pl.multiple_of(cb * 8, 8)
        tile = lambda j: (lambda: src_ref[pl.ds(r0, 8), j * n2:(j + 1) * n2])
        for kj in range(nkj):
            re = _lincomb([(cos1[kj][j], tile(j)) for j in range(n_in)])
            im = _lincomb([(-sin1[kj][j], tile(j)) for j in range(n_in)])
            z_ref[pl.ds(kj * ct + r0, 8), 0:n2] = re
            z_ref[pl.ds(kj * ct + r0, 8), n2:2 * n2] = jnp.zeros_like(re) if im is None else im
        return carry

    lax.fori_loop(0, ct // 8, body, 0)


def _twiddle(z_ref, twc_ref, tws_ref, ct, nkj, inverse):
    n2 = DFT_MINOR
    for kj in range(1, nkj):
        re = z_ref[kj * ct:(kj + 1) * ct, 0:n2]
        im = z_ref[kj * ct:(kj + 1) * ct, n2:2 * n2]
        c = twc_ref[kj:kj + 1, :]
        s = tws_ref[kj:kj + 1, :]
        if inverse:
            s = -s
        z_ref[kj * ct:(kj + 1) * ct, 0:n2] = re * c + im * s
        z_ref[kj * ct:(kj + 1) * ct, n2:2 * n2] = im * c - re * s


def _spectrum_kernel(k_ref, twc_ref, tws_ref, mf_ref, h_ref, z_ref, *, ct, cst, n_chunks):
    nkj = cst["nkj"]
    _outer_fwd(k_ref, z_ref, ct, cst["n1"], cst)
    _twiddle(z_ref, twc_ref, tws_ref, ct, nkj, False)
    rc = nkj * ct // n_chunks

    def chunk(r, carry):
        rows = pl.ds(pl.multiple_of(r * rc, 8), rc)
        h_ref[rows, :] = _dot_split(z_ref[rows, :], mf_ref)
        return carry

    lax.fori_loop(0, n_chunks, chunk, 0)


def _long_conv_kernel(x_ref, h_ref, twc_ref, tws_ref, mf_ref, mi_ref, x0_ref, vxc_ref, yc_ref, skip_ref,
                      skip_row_ref, o_ref, z_ref, yt_ref, *, ct, cst, n_chunks, n_lat):
    n2 = DFT_MINOR
    nkj, n1, cos1, sin1 = cst["nkj"], cst["n1"], cst["cos1"], cst["sin1"]
    _outer_fwd(x_ref, z_ref, ct, n1 // 2, cst)
    _twiddle(z_ref, twc_ref, tws_ref, ct, nkj, False)
    rc = nkj * ct // n_chunks

    def chunk(r, carry):
        rows = pl.ds(pl.multiple_of(r * rc, 8), rc)
        x = _dot_split(z_ref[rows, :], mf_ref)
        xre, xim = x[:, :n2], x[:, n2:]
        hre, him = h_ref[rows, 0:n2], h_ref[rows, n2:2 * n2]
        y = jnp.concatenate([xre * hre - xim * him, xre * him + xim * hre], axis=1)
        z_ref[rows, :] = _dot_split(y, mi_ref)
        return carry

    lax.fori_loop(0, n_chunks, chunk, 0)
    _twiddle(z_ref, twc_ref, tws_ref, ct, nkj, True)

    inv_n = 1.0 / cst["n"]

    def body(cb, carry):
        r0 = pl.multiple_of(cb * 8, 8)
        g_re = lambda kj: (lambda: z_ref[pl.ds(kj * ct + r0, 8), 0:n2])
        g_im = lambda kj: (lambda: z_ref[pl.ds(kj * ct + r0, 8), n2:2 * n2])
        for j in range(n1 // 2):
            terms = []
            for kj in range(nkj):
                wgt = inv_n if kj in (0, nkj - 1) else 2.0 * inv_n
                terms.append((wgt * cos1[kj][j], g_re(kj)))
                terms.append((-wgt * sin1[kj][j], g_im(kj)))
            yt_ref[pl.ds(r0, 8), j * n2:(j + 1) * n2] = _lincomb(terms)
        return carry

    lax.fori_loop(0, ct // 8, body, 0)

    y = (yt_ref[...] + x_ref[...] * skip_ref[...]).T
    o_ref[0:n_lat, :] = (x0_ref[0:n_lat, :] * y).astype(o_ref.dtype)
    yc = yc_ref[...] + vxc_ref[...] * skip_row_ref[...]
    o_ref[n_lat:, :] = (x0_ref[n_lat:, :] * yc).astype(o_ref.dtype)


def filter_spectrum(k2t, cst):
    ch, n = k2t.shape
    ct = LANE
    nkj = cst["nkj"]
    n_chunks = _pick(nkj * ct // 8, (4, 5, 1))
    full = lambda a: pl.BlockSpec(a.shape, lambda c: (0,) * a.ndim)
    return pl.pallas_call(
        functools.partial(_spectrum_kernel, ct=ct, cst=cst, n_chunks=n_chunks),
        grid=(ch // ct,),
        in_specs=[pl.BlockSpec((ct, n), lambda c: (c, 0)),
                  full(cst["tw_c"]), full(cst["tw_s"]), full(cst["m_fwd"])],
        out_specs=pl.BlockSpec((None, nkj * ct, 2 * DFT_MINOR), lambda c: (c, 0, 0)),
        out_shape=jax.ShapeDtypeStruct((ch // ct, nkj * ct, 2 * DFT_MINOR), F32),
        scratch_shapes=[pltpu.VMEM((nkj * ct, 2 * DFT_MINOR), F32)],
        compiler_params=_params("arbitrary"),
        name="filter_spectrum",
    )(k2t, cst["tw_c"], cst["tw_s"], cst["m_fwd"])


def hyena_long_conv(vxt, h_spec, x0, vx_ctx, y_ctx, skip_col, skip_row, cst, layer, n_lat):
    bsz, ch, _ = vxt.shape
    l_all = x0.shape[1]
    n_ctx = l_all - n_lat
    ct = LANE
    nkj = cst["nkj"]
    n_chunks = _pick(nkj * ct // 8, (4, 5, 1))
    full = lambda a: pl.BlockSpec(a.shape, lambda b, c: (0,) * a.ndim)
    return pl.pallas_call(
        functools.partial(_long_conv_kernel, ct=ct, cst=cst, n_chunks=n_chunks, n_lat=n_lat),
        grid=(bsz, ch // ct),
        in_specs=[pl.BlockSpec((None, ct, n_lat), lambda b, c: (b, c, 0)),
                  pl.BlockSpec((None, nkj * ct, 2 * DFT_MINOR), lambda b, c: (c, 0, 0)),
                  full(cst["tw_c"]), full(cst["tw_s"]), full(cst["m_fwd"]), full(cst["m_inv"]),
                  pl.BlockSpec((None, l_all, ct), lambda b, c: (b, 0, c)),
                  pl.BlockSpec((None, n_ctx, ct), lambda b, c: (b, 0, c)),
                  pl.BlockSpec((None, n_ctx, ct), lambda b, c: (b, 0, c)),
                  pl.BlockSpec((None, ct, 1), lambda b, c: (layer, c, 0)),
                  pl.BlockSpec((None, 1, ct), lambda b, c: (layer, 0, c))],
        out_specs=pl.BlockSpec((None, l_all, ct), lambda b, c: (b, 0, c)),
        out_shape=jax.ShapeDtypeStruct((bsz, l_all, ch), MXU_DT),
        scratch_shapes=[pltpu.VMEM((nkj * ct, 2 * DFT_MINOR), F32), pltpu.VMEM((ct, n_lat), F32)],
        compiler_params=_params("arbitrary", "arbitrary"),
        name="hyena_long_conv",
    )(vxt, h_spec, cst["tw_c"], cst["tw_s"], cst["m_fwd"], cst["m_inv"], x0, vx_ctx, y_ctx, skip_col, skip_row)


def _ctx_conv_kernel(v_ref, k_ref, fd_ref, fk_ref, gi_ref, o_ref, *, kp):
    x = _dot_hi(fd_ref[...], v_ref[...])
    h = _dot_hi(fk_ref[...], k_ref[...])
    xre, xim = x[:kp], x[kp:]
    hre, him = h[:kp], h[kp:]
    y = jnp.concatenate([xre * hre - xim * him, xre * him + xim * hre], axis=0)
    o_ref[...] = _dot_hi(gi_ref[...], y)


def _ctx_tables(n_ctx):
    n = 2 * n_ctx
    kh = n_ctx + 1
    kp = ((kh + 7) // 8) * 8
    k = np.arange(kp)[:, None]
    live = (k < kh).astype(np.float64)
    ang = 2.0 * np.pi * k * np.arange(n)[None, :] / n
    ck, sk = np.cos(ang) * live, np.sin(ang) * live
    fk = np.concatenate([ck, -sk], axis=0)
    fd = fk[:, :n_ctx]
    wk = np.where((k == 0) | (k == n_ctx), 1.0, 2.0) * live / n
    gi = np.concatenate([(ck * wk).T, (-sk * wk).T], axis=1)[:n_ctx]
    as32 = lambda a: jnp.asarray(a, dtype=F32)
    return dict(kp=kp, fd=as32(fd), fk=as32(fk), gi=as32(gi))


def long_conv_ctx(vx_ctx, k2c, tb):
    bsz, n_ctx, ch = vx_ctx.shape
    kp = tb["kp"]
    full = lambda a: pl.BlockSpec(a.shape, lambda b, c: (0,) * a.ndim)
    return pl.pallas_call(
        functools.partial(_ctx_conv_kernel, kp=kp),
        grid=(bsz, ch // LANE),
        in_specs=[pl.BlockSpec((None, n_ctx, LANE), lambda b, c: (b, 0, c)),
                  pl.BlockSpec((2 * n_ctx, LANE), lambda b, c: (0, c)),
                  full(tb["fd"]), full(tb["fk"]), full(tb["gi"])],
        out_specs=pl.BlockSpec((None, n_ctx, LANE), lambda b, c: (b, 0, c)),
        out_shape=jax.ShapeDtypeStruct((bsz, n_ctx, ch), F32),
        compiler_params=_params("arbitrary", "arbitrary"),
        name="ctx_long_conv",
    )(vx_ctx, k2c, tb["fd"], tb["fk"], tb["gi"])


def _ssd_kernel(xs_ref, b_ref, c_ref, dt_ref, alog_ref, ex_ref, y_ref, state_ref):
    d = pl.program_id(1)
    n = MB_CHUNK

    @pl.when(pl.program_id(2) == 0)
    def _():
        state_ref[...] = jnp.zeros_like(state_ref)

    li = lax.broadcasted_iota(jnp.int32, (n, n), 0)
    si = lax.broadcasted_iota(jnp.int32, (n, n), 1)
    mask = (li - si) * (1 - 2 * d) >= 0
    tri = mask.astype(F32)
    lane = lax.broadcasted_iota(jnp.int32, (n, LANE), 1)

    dt = dt_ref[...]
    da = dt * (-jnp.exp(alog_ref[...]))
    cum = _dot_hi(tri, da)
    cum_t = cum.T
    ex = ex_ref[...]
    cum_e = _dot_hi(cum, ex)
    dt_e = _dot_hi(dt, ex)
    total = jnp.sum(da, axis=0, keepdims=True)
    total_e = _dot_hi(jnp.broadcast_to(total, (8, LANE)), ex)[0:1]

    xdt = xs_ref[...] * dt_e
    xdt_b = xdt.astype(MXU_DT)
    w_b = (jnp.exp(total_e - cum_e) * xdt).astype(MXU_DT)
    st = state_ref[...]
    y_parts = []
    s_parts = []
    gw = MB_STATE
    hw = (MB_HEADS // MB_GROUPS) * MB_HEADDIM
    for g in range(MB_GROUPS):
        cg = c_ref[:, g * gw:(g + 1) * gw].astype(MXU_DT)
        bg = b_ref[:, g * gw:(g + 1) * gw]
        cb = _dot_nt(cg, bg.astype(MXU_DT))
        y_off = _dot(cg, st[:, g * hw:(g + 1) * hw].astype(MXU_DT))
        s_parts.append(_dot(bg.T.astype(MXU_DT), w_b[:, g * hw:(g + 1) * hw]))
        diag = []
        for j in range(hw // LANE):
            lo = g * hw + j * LANE
            xp = xdt_b[:, lo:lo + LANE]
            pair = []
            for e in range(2):
                hd = (lo // MB_HEADDIM) + e
                seg = cum[:, hd:hd + 1] - cum_t[hd:hd + 1, :]
                decay = jnp.exp(jnp.where(mask, seg, -jnp.inf))
                pair.append(_dot((cb * decay).astype(MXU_DT), xp))
            diag.append(jnp.where(lane < MB_HEADDIM, pair[0], pair[1]))
        y_parts.append(jnp.concatenate(diag, axis=1) + y_off * jnp.exp(cum_e[:, g * hw:(g + 1) * hw]))
    y_ref[...] = jnp.concatenate(y_parts, axis=1)
    state_ref[...] = st * jnp.exp(total_e) + jnp.concatenate(s_parts, axis=1)


def ssd_scan(xbc, dt2, a_log_rows, expand, layer, n_lat):
    bsz, l_all, _ = xbc.shape
    nc = l_all // MB_CHUNK
    ncl = n_lat // MB_CHUNK

    def blk(d, c):
        return jnp.where(d == 0, (c + ncl) % nc, nc - 1 - c)

    return pl.pallas_call(
        _ssd_kernel,
        grid=(bsz, 2, nc),
        in_specs=[pl.BlockSpec((None, MB_CHUNK, MB_INNER), lambda b, d, c: (b, blk(d, c), 0)),
                  pl.BlockSpec((None, MB_CHUNK, 256), lambda b, d, c: (b, blk(d, c), 2)),
                  pl.BlockSpec((None, MB_CHUNK, 256), lambda b, d, c: (b, blk(d, c), 3)),
                  pl.BlockSpec((None, None, MB_CHUNK, LANE), lambda b, d, c: (d, b, blk(d, c), 0)),
                  pl.BlockSpec((None, None, 1, LANE), lambda b, d, c: (layer, d, 0, 0)),
                  pl.BlockSpec((LANE, MB_INNER), lambda b, d, c: (0, 0))],
        out_specs=pl.BlockSpec((None, None, MB_CHUNK, MB_INNER), lambda b, d, c: (d, b, blk(d, c), 0)),
        out_shape=jax.ShapeDtypeStruct((2, bsz, l_all, MB_INNER), F32),
        scratch_shapes=[pltpu.VMEM((MB_STATE, MB_INNER), F32)],
        compiler_params=_params("arbitrary", "arbitrary", "arbitrary"),
        name="ssd_scan",
    )(xbc, xbc, xbc, dt2, a_log_rows, expand)


def _mb_finish_kernel(yf_ref, yb_ref, xs_ref, z_ref, dsk_ref, g_ref, o_ref):
    y = yf_ref[...] + yb_ref[...] + xs_ref[...] * dsk_ref[...]
    y = y * _silu(z_ref[...])
    r = lax.rsqrt(jnp.mean(y * y, axis=-1, keepdims=True) + RMS_EPS)
    o_ref[...] = (y * r * g_ref[...]).astype(o_ref.dtype)


def mamba_finish(y2, xbc, p, d_skip_e, norm_g3, layer):
    bsz, l_all, _ = xbc.shape
    tr = _pick(l_all, (544, 384, 128))
    w = MB_INNER
    return pl.pallas_call(
        _mb_finish_kernel,
        grid=(bsz, l_all // tr),
        in_specs=[pl.BlockSpec((None, None, tr, w), lambda b, i: (0, b, i, 0)),
                  pl.BlockSpec((None, None, tr, w), lambda b, i: (1, b, i, 0)),
                  pl.BlockSpec((None, tr, w), lambda b, i: (b, i, 0)),
                  pl.BlockSpec((None, tr, w), lambda b, i: (b, i, P_Z // w)),
                  pl.BlockSpec((None, 1, w), lambda b, i: (layer, 0, 0)),
                  pl.BlockSpec((None, 1, w), lambda b, i: (layer, 0, 0))],
        out_specs=pl.BlockSpec((None, tr, w), lambda b, i: (b, i, 0)),
        out_shape=jax.ShapeDtypeStruct((bsz, l_all, w), MXU_DT),
        compiler_params=_params("arbitrary", "arbitrary"),
        name="mamba_finish",
    )(y2, y2, xbc, p, d_skip_e, norm_g3)


def _deinterleave(w, heads):
    lead = w.shape[:-1]
    w = w.reshape(lead + (heads, GQA_HD // 2, 2))
    return jnp.concatenate([w[..., 0], w[..., 1]], axis=-1).reshape(lead + (heads * GQA_HD,))


def _prep_w_in(w_in):
    nl, d, _ = w_in.shape
    z64 = jnp.zeros((nl, d, 64), w_in.dtype)
    mla = w_in[..., :MLA_COLS]
    cq, ckv, kr = mla[..., :448], mla[..., 448:576], mla[..., 576:640]
    ev, od = kr[..., 0::2], kr[..., 1::2]
    mla_blk = jnp.concatenate([cq, z64, ckv, ev, od, z64, -od, ev, z64], axis=-1)
    gqa = w_in[..., MLA_COLS:MLA_COLS + GQA_COLS]
    gqa_blk = jnp.concatenate([_deinterleave(gqa[..., :512], GQA_HEADS),
                               _deinterleave(gqa[..., 512:768], GQA_KV_HEADS), gqa[..., 768:]], axis=-1)
    hy = w_in[..., MLA_COLS + GQA_COLS:MLA_COLS + GQA_COLS + HY_COLS]
    mb = w_in[..., MLA_COLS + GQA_COLS + HY_COLS:]
    z, xbc, dt = mb[..., :MB_INNER], mb[..., MB_INNER:MB_INNER + MB_CONV_CH], mb[..., MB_INNER + MB_CONV_CH:]
    dt_blk = jnp.concatenate([dt, jnp.zeros((nl, d, LANE - 2 * MB_HEADS), w_in.dtype)], axis=-1)
    out = jnp.concatenate([mla_blk, dt_blk, gqa_blk, z, hy, xbc], axis=-1)
    assert out.shape[-1] == P_COLS
    return out.astype(MXU_DT)


def _prep_w_uq(w_uq):
    nl = w_uq.shape[0]
    w = w_uq.reshape(nl, MLA_Q_RANK, MLA_HEADS, MLA_NOPE + MLA_ROPE)
    nope, rot = w[..., :MLA_NOPE], w[..., MLA_NOPE:]
    ev, od = rot[..., 0::2], rot[..., 1::2]
    z64 = jnp.zeros_like(rot)
    per_head = jnp.concatenate([nope, ev, od, z64, -od, ev, z64], axis=-1)
    w = per_head.reshape(nl, MLA_Q_RANK, MLA_HEADS * 384)
    w = jnp.concatenate([w, jnp.zeros((nl, 512 - MLA_Q_RANK, MLA_HEADS * 384), w.dtype)], axis=1)
    return w.astype(MXU_DT)


def _rope_tables(n_lat, n_ctx, rot_dim, sign_folded):
    rows = n_lat // GRID_W
    row = jnp.repeat(jnp.arange(rows, dtype=F32), GRID_W)
    col = jnp.tile(jnp.arange(GRID_W, dtype=F32), rows)
    n_freq = rot_dim // 4
    inv_freq = ROPE_THETA ** (-jnp.arange(n_freq, dtype=F32) / n_freq)
    ang = jnp.concatenate([row[:, None] * inv_freq, col[:, None] * inv_freq], axis=-1)
    half = rot_dim // 2
    cos = jnp.concatenate([jnp.cos(ang), jnp.ones((n_ctx, half), F32)], axis=0)
    sin = jnp.concatenate([jnp.sin(ang), jnp.zeros((n_ctx, half), F32)], axis=0)
    pad = jnp.zeros((n_lat + n_ctx, LANE - rot_dim), F32)
    cos_t = jnp.concatenate([cos, cos, pad], axis=-1)
    sin_t = jnp.concatenate([sin if sign_folded else -sin, sin, pad], axis=-1)
    return cos_t, sin_t


def _filter_inputs(n):
    t = jnp.linspace(0.0, 1.0, n, dtype=F32)[:, None]
    omega = 2.0 * math.pi * jnp.arange(n, dtype=F32) / n
    bands = jnp.linspace(1e-4, HY_BANDS - 1, HY_BANDS, dtype=F32)
    ang = omega[:, None] * bands[None, :]
    feats = jnp.concatenate([t, jnp.cos(ang), -jnp.sin(ang)], axis=-1)
    zero = jnp.zeros((1, HY_EMB), F32)
    feats2 = jnp.concatenate([feats, zero, feats[1:][::-1]], axis=0)
    feats2 = jnp.concatenate([feats2, jnp.zeros((2 * n, LANE - HY_EMB), F32)], axis=-1)
    t2 = jnp.concatenate([t, jnp.zeros((1, 1), F32), t[1:][::-1]], axis=0)
    idx = jnp.arange(2 * n)[:, None]
    aux = jnp.concatenate([t2, (idx < n).astype(F32), (idx != n).astype(F32),
                           jnp.zeros((2 * n, LANE - 3), F32)], axis=-1)
    return feats2, aux


def _pad_to(a, shape):
    pads = [(0, s - d) for d, s in zip(a.shape, shape)]
    return jnp.pad(a, pads)


def kernel(x, c, ctx, c_ctx, w_ada, b_ada, w_in, mla_q_norm, mla_kv_norm, mla_w_uq, mla_w_ukv, gqa_q_norm, gqa_k_norm, hy_conv_w, hy_conv_b, hy_w1, hy_b1, hy_w2, hy_b2, hy_w3, hy_freq, hy_skip, mb_conv_w, mb_conv_b, mb_a_log, mb_dt_bias, mb_d, mb_norm, w_mgate, b_mgate, w_branch, w_out, ln1_g, ln1_b, w_ffn_in, w_ffn_out, ln2_g, ln2_b):
    bsz, n_lat, d = x.shape
    n_ctx = ctx.shape[1]
    nl = w_in.shape[0]
    assert d == D_MODEL and bsz < 8 and n_lat % n_ctx == 0 and n_ctx % MB_CHUNK == 0

    w_in_p = _prep_w_in(w_in)
    w_uq_p = _prep_w_uq(mla_w_uq)
    w_ukv_p = mla_w_ukv.astype(MXU_DT)
    gq_mla = _pad_to(mla_q_norm, (nl, 512))[:, None, :]
    gkv_mla = mla_kv_norm[:, None, :]
    gq_gqa = _deinterleave(gqa_q_norm, 1)[:, None, :]
    gk_gqa = _deinterleave(gqa_k_norm, 1)[:, None, :]
    w_gate_b = w_mgate.astype(MXU_DT)
    w_branch_b = w_branch.astype(MXU_DT)
    w_out_b = w_out.astype(MXU_DT)
    w_ffn_in_b = w_ffn_in.astype(MXU_DT)
    w_ffn_out_b = w_ffn_out.astype(MXU_DT)
    b_gate4 = b_mgate[:, :, None, :]
    b_ada3 = b_ada[:, None, :]
    hy_conv_b3 = hy_conv_b[:, None, :]
    mb_conv_b3 = mb_conv_b[:, None, :]
    hy_skip3 = hy_skip[:, None, :]
    hy_skip_col = hy_skip[:, :, None]
    ln1_g3, ln1_b3, ln2_g3, ln2_b3 = (a[:, None, :] for a in (ln1_g, ln1_b, ln2_g, ln2_b))
    hy_w1p = _pad_to(hy_w1, (nl, LANE, LANE))
    hy_b1p = _pad_to(hy_b1, (nl, LANE))[:, None, :]
    hy_w2p = _pad_to(hy_w2, (nl, LANE, LANE))
    hy_b2p = _pad_to(hy_b2, (nl, LANE))[:, None, :]
    hy_w3p = _pad_to(hy_w3, (nl, LANE, 2 * HY_W))
    hy_frp = _pad_to(hy_freq, (nl, LANE))[:, None, :]
    deltas = jnp.abs(jnp.linspace(HY_MIN_DECAY, HY_MAX_DECAY, HY_W, dtype=F32))[None, :]
    dt_bias_row = _pad_to(mb_dt_bias.reshape(nl, 2 * MB_HEADS), (nl, LANE))[:, None, :]
    a_log_rows = _pad_to(mb_a_log, (nl, 2, LANE))[:, :, None, :]
    d_skip_e = jnp.repeat(mb_d, MB_HEADDIM, axis=-1)[:, None, :]
    mb_norm3 = mb_norm[:, None, :]
    expand = jnp.asarray(np.kron(np.eye(LANE, MB_HEADS), np.ones((1, MB_HEADDIM))), dtype=F32)

    cos_m, sin_m = _rope_tables(n_lat, n_ctx, MLA_ROPE, True)
    cos_g, sin_g = _rope_tables(n_lat, n_ctx, GQA_HD, False)
    feats_lat, aux_lat = _filter_inputs(n_lat)
    feats_ctx, aux_ctx = _filter_inputs(n_ctx)
    dft_lat = _dft_consts(n_lat)
    tb_ctx = _ctx_tables(n_ctx)

    c8 = jnp.concatenate([c, c_ctx[None, :], jnp.zeros((8 - bsz - 1, d), F32)], axis=0)
    xz = jnp.concatenate([x, ctx], axis=1)

    for l in range(nl):
        last = l == nl - 1
        mod = ada_mod(c8, w_ada, b_ada3, l).reshape(8, 1, 6 * d)
        p, h = in_projection(xz, mod, w_in_p, l, n_lat)

        q, k, v = mla_prep(p, gq_mla, gkv_mla, w_uq_p, w_ukv_p, cos_m, sin_m, l)
        oa = attention(q, k, v, n_lat)
        q, k, v = gqa_prep(p, gq_gqa, gk_gqa, cos_g, sin_g, l)
        ob = attention(q, k, v, n_lat)
        x0, vxt, vx_ctx = hyena_conv(p, hy_conv_w, hy_conv_b3, l, n_lat)
        k2t = hyena_filter(feats_lat, aux_lat, hy_w1p, hy_b1p, hy_w2p, hy_b2p, hy_w3p, hy_frp, deltas, l, True)
        h_spec = filter_spectrum(k2t, dft_lat)
        if last:
            y_ctx = jnp.zeros((bsz, n_ctx, HY_W), F32)
        else:
            k2c = hyena_filter(feats_ctx, aux_ctx, hy_w1p, hy_b1p, hy_w2p, hy_b2p, hy_w3p, hy_frp, deltas, l,
                               False)
            y_ctx = long_conv_ctx(vx_ctx, k2c, tb_ctx)
        oc = hyena_long_conv(vxt, h_spec, x0, vx_ctx, y_ctx, hy_skip_col, hy_skip3, dft_lat, l, n_lat)
        xbc = mamba_conv(p, mb_conv_w, mb_conv_b3, l, n_lat)
        dt2 = mamba_dt(p, dt_bias_row, l)
        y2 = ssd_scan(xbc, dt2, a_log_rows, expand, l, n_lat)
        od = mamba_finish(y2, xbc, p, d_skip_e, mb_norm3, l)

        acc = merge_branches(h, (oa, ob, oc, od), w_gate_b, b_gate4, w_branch_b, l)
        x1 = matmul_res_ln(acc, w_out_b, xz, mod, 2, ln1_g3, ln1_b3, l, n_lat, n_lat + n_ctx,
                           (544, 384, 128), d)
        act = ffn_in(x1, mod, w_ffn_in_b, l, n_lat)
        out_rows = n_lat if last else n_lat + n_ctx
        xz = matmul_res_ln(act, w_ffn_out_b, x1, mod, 5, ln2_g3, ln2_b3, l, n_lat, out_rows,
                           (1088, 384, 128), 512)
    return xz
```

```python
import functools
import math

import jax
import jax.numpy as jnp
import numpy as np
from jax import lax
from jax.experimental import pallas as pl
from jax.experimental.pallas import tpu as pltpu

F32 = jnp.float32
MXU_DT = jnp.bfloat16

D_MODEL = 2048
DEPTH = 2
GRID_W = 64
N_BRANCH = 4
BRANCH_W = D_MODEL // N_BRANCH
ROPE_THETA = 10000.0
LN_EPS = 1e-6
RMS_EPS = 1e-6
DEEPNORM_ALPHA = (2 * DEPTH) ** 0.25

MLA_HEADS = 4
MLA_Q_RANK = 448
MLA_KV_RANK = 128
MLA_NOPE = 128
MLA_ROPE = 64
MLA_V = 128
MLA_COLS = MLA_Q_RANK + MLA_KV_RANK + MLA_ROPE

GQA_HEADS = 4
GQA_KV_HEADS = 2
GQA_HD = 128
GQA_COLS = (GQA_HEADS + 2 * GQA_KV_HEADS) * GQA_HD

HY_W = BRANCH_W
HY_EMB = 33
HY_BANDS = (HY_EMB - 1) // 2
HY_FFN = 64
HY_MIN_DECAY = math.log(1e-2) / 1.5
HY_MAX_DECAY = math.log(1e-2) / 0.3
HY_COLS = 3 * HY_W

MB_INNER = BRANCH_W
MB_HEADDIM = 64
MB_HEADS = 8
MB_GROUPS = 2
MB_STATE = 128
MB_CHUNK = 128
MB_CONV_CH = MB_INNER + 2 * MB_GROUPS * MB_STATE
MB_COLS = MB_INNER + MB_CONV_CH + 2 * MB_HEADS

FFN_HIDDEN = 5632

LANE = 128
VMEM_LIMIT = 56 * 1024 * 1024

P_MLA = 0
P_MLA_W = 896
P_DT = 896
P_GQA = 1024
P_Z = 2048
P_HY = 2560
P_XBC = 4096
P_COLS = 5120

LOG2_E = math.log2(math.e)
_HI = lax.Precision.HIGHEST


def _dot(a, b):
    return jnp.dot(a, b, preferred_element_type=F32)


def _dot_hi(a, b):
    return jnp.dot(a, b, precision=_HI, preferred_element_type=F32)


def _dot_nt(a, b):
    return lax.dot_general(a, b, (((1,), (1,)), ((), ())), preferred_element_type=F32)


def _sigmoid(x):
    return 1.0 / (1.0 + jnp.exp(-x))


def _silu(x):
    return x * _sigmoid(x)


def _params(*sem):
    return pltpu.CompilerParams(dimension_semantics=sem, vmem_limit_bytes=VMEM_LIMIT)


def _pick(n, prefs):
    for p in prefs:
        if n % p == 0:
            return p
    raise ValueError(f"no tile for {n} in {prefs}")


def _standardize(x):
    mu = jnp.mean(x, axis=-1, keepdims=True)
    xc = x - mu
    var = jnp.mean(xc * xc, axis=-1, keepdims=True)
    return xc * lax.rsqrt(var + LN_EPS)


def _is_ctx_rows(tile_idx, tm, n_lat):
    row = tile_idx * tm + lax.broadcasted_iota(jnp.int32, (tm, 1), 0)
    return row >= n_lat


def _ada_kernel(c_ref, w_ref, b_ref, o_ref):
    cs = _silu(c_ref[...])
    o_ref[...] = _dot(cs.astype(MXU_DT), w_ref[...].astype(MXU_DT)) + b_ref[...]


def ada_mod(c8, w_ada, b_ada3, layer):
    d = c8.shape[1]
    n = w_ada.shape[2]
    tn = 1024
    return pl.pallas_call(
        _ada_kernel,
        grid=(n // tn,),
        in_specs=[pl.BlockSpec((8, d), lambda j: (0, 0)),
                  pl.BlockSpec((None, d, tn), lambda j: (layer, 0, j)),
                  pl.BlockSpec((None, 1, tn), lambda j: (layer, 0, j))],
        out_specs=pl.BlockSpec((8, tn), lambda j: (0, j)),
        out_shape=jax.ShapeDtypeStruct((8, n), F32),
        compiler_params=_params("arbitrary"),
        name="ada_mod",
    )(c8, w_ada, b_ada3)


ROW_CHUNKS = 4


def _modulated(x, sh_ref, sc_ref, shc_ref, scc_ref, row0, n_lat):
    xn = _standardize(x)
    row = row0 + lax.broadcasted_iota(jnp.int32, (x.shape[0], 1), 0)
    is_ctx = row >= n_lat
    scale = jnp.where(is_ctx, scc_ref[...], sc_ref[...])
    shift = jnp.where(is_ctx, shc_ref[...], sh_ref[...])
    return (xn * (1.0 + scale) + shift).astype(MXU_DT)


def _inproj_kernel(x_ref, sh_ref, sc_ref, shc_ref, scc_ref, w_ref, p_ref, pdt_ref, h_ref, *, tm, n_lat):
    j = pl.program_id(2)
    rc = tm // ROW_CHUNKS

    @pl.when(j == 0)
    def _():
        for r in range(ROW_CHUNKS):
            rows = slice(r * rc, (r + 1) * rc)
            hb = _modulated(x_ref[rows, :], sh_ref, sc_ref, shc_ref, scc_ref,
                            pl.program_id(1) * tm + r * rc, n_lat)
            h_ref[rows, :] = hb
            acc = _dot(hb, w_ref[...])
            p_ref[rows, :] = acc.astype(p_ref.dtype)
            pdt_ref[rows, :] = acc[:, P_DT:P_DT + LANE]

    @pl.when(j > 0)
    def _():
        p_ref[...] = _dot(h_ref[...], w_ref[...]).astype(p_ref.dtype)


def _ffn_in_kernel(x_ref, sh_ref, sc_ref, shc_ref, scc_ref, wu_ref, wg_ref, a_ref, h_ref, *, tm, n_lat):
    j = pl.program_id(2)
    rc = tm // ROW_CHUNKS

    def swiglu(h):
        up = _dot(h, wu_ref[...])
        gate = _dot(h, wg_ref[...])
        return (_silu(gate) * up).astype(a_ref.dtype)

    @pl.when(j == 0)
    def _():
        for r in range(ROW_CHUNKS):
            rows = slice(r * rc, (r + 1) * rc)
            hb = _modulated(x_ref[rows, :], sh_ref, sc_ref, shc_ref, scc_ref,
                            pl.program_id(1) * tm + r * rc, n_lat)
            h_ref[rows, :] = hb
            a_ref[rows, :] = swiglu(hb)

    @pl.when(j > 0)
    def _():
        a_ref[...] = swiglu(h_ref[...])


def _mod_specs(k_shift, k_scale, n_batch):
    d = D_MODEL
    return [pl.BlockSpec((None, 1, d), lambda b, i, j: (b, 0, k_shift)),
            pl.BlockSpec((None, 1, d), lambda b, i, j: (b, 0, k_scale)),
            pl.BlockSpec((None, 1, d), lambda b, i, j: (n_batch, 0, k_shift)),
            pl.BlockSpec((None, 1, d), lambda b, i, j: (n_batch, 0, k_scale))]


def in_projection(xz, mod, w_in_p, layer, n_lat):
    bsz, l_all, d = xz.shape
    n = w_in_p.shape[2]
    tm = _pick(l_all, (1088, 544, 384, 128))
    tn = 1024
    assert n % tn == 0 and P_DT + LANE <= tn
    kern = functools.partial(_inproj_kernel, tm=tm, n_lat=n_lat)
    return pl.pallas_call(
        kern,
        grid=(bsz, l_all // tm, n // tn),
        in_specs=[pl.BlockSpec((None, tm, d), lambda b, i, j: (b, i, 0))]
        + _mod_specs(0, 1, bsz)
        + [pl.BlockSpec((None, d, tn), lambda b, i, j: (layer, 0, j))],
        out_specs=[pl.BlockSpec((None, tm, tn), lambda b, i, j: (b, i, j)),
                   pl.BlockSpec((None, tm, LANE), lambda b, i, j: (b, i, 0)),
                   pl.BlockSpec((None, tm, d), lambda b, i, j: (b, i, 0))],
        out_shape=[jax.ShapeDtypeStruct((bsz, l_all, n), MXU_DT),
                   jax.ShapeDtypeStruct((bsz, l_all, LANE), F32),
                   jax.ShapeDtypeStruct((bsz, l_all, d), MXU_DT)],
        compiler_params=_params("arbitrary", "arbitrary", "arbitrary"),
        name="in_projection",
    )(xz, mod, mod, mod, mod, w_in_p)


def ffn_in(xz, mod, w_ffn_in, layer, n_lat):
    bsz, l_all, d = xz.shape
    hid = w_ffn_in.shape[2] // 2
    tm = _pick(l_all, (1088, 544, 384, 128))
    tn = 512
    nj = hid // tn
    kern = functools.partial(_ffn_in_kernel, tm=tm, n_lat=n_lat)
    return pl.pallas_call(
        kern,
        grid=(bsz, l_all // tm, nj),
        in_specs=[pl.BlockSpec((None, tm, d), lambda b, i, j: (b, i, 0))]
        + _mod_specs(3, 4, bsz)
        + [pl.BlockSpec((None, d, tn), lambda b, i, j: (layer, 0, j)),
           pl.BlockSpec((None, d, tn), lambda b, i, j: (layer, 0, j + nj))],
        out_specs=pl.BlockSpec((None, tm, tn), lambda b, i, j: (b, i, j)),
        out_shape=jax.ShapeDtypeStruct((bsz, l_all, hid), MXU_DT),
        scratch_shapes=[pltpu.VMEM((tm, d), MXU_DT)],
        compiler_params=_params("arbitrary", "arbitrary", "arbitrary"),
        name="ffn_in",
    )(xz, mod, mod, mod, mod, w_ffn_in, w_ffn_in)


def _resln_kernel(a_ref, w_ref, res_ref, g_ref, gc_ref, lng_ref, lnb_ref, o_ref, *, tm, n_lat, nk, rc):
    k = pl.program_id(2)
    i = pl.program_id(1)

    @pl.when(k == 0)
    def _():
        o_ref[...] = _dot(a_ref[...], w_ref[...])

    @pl.when(k > 0)
    def _():
        o_ref[...] += _dot(a_ref[...], w_ref[...])

    @pl.when(k == nk - 1)
    def _():
        def body(r, carry):
            rows = pl.ds(pl.multiple_of(r * rc, 8), rc)
            row = i * tm + r * rc + lax.broadcasted_iota(jnp.int32, (rc, 1), 0)
            gate = jnp.where(row >= n_lat, gc_ref[...], g_ref[...])
            y = DEEPNORM_ALPHA * res_ref[rows, :] + gate * o_ref[rows, :]
            o_ref[rows, :] = _standardize(y) * lng_ref[...] + lnb_ref[...]
            return carry

        lax.fori_loop(0, tm // rc, body, 0)


def matmul_res_ln(a, w, res, mod, k_gate, ln_g, ln_b, layer, n_lat, out_rows, tm_prefs, tk):
    bsz, l_all, kdim = a.shape
    d = w.shape[2]
    tm = _pick(l_all, tm_prefs)
    nk = kdim // tk
    rc = _pick(tm, (136, 128))
    kern = functools.partial(_resln_kernel, tm=tm, n_lat=n_lat, nk=nk, rc=rc)
    return pl.pallas_call(
        kern,
        grid=(bsz, pl.cdiv(out_rows, tm), nk),
        in_specs=[pl.BlockSpec((None, tm, tk), lambda b, i, k: (b, i, k)),
                  pl.BlockSpec((None, tk, d), lambda b, i, k: (layer, k, 0)),
                  pl.BlockSpec((None, tm, d), lambda b, i, k: (b, i, 0)),
                  pl.BlockSpec((None, 1, d), lambda b, i, k: (b, 0, k_gate)),
                  pl.BlockSpec((None, 1, d), lambda b, i, k: (bsz, 0, k_gate)),
                  pl.BlockSpec((None, 1, d), lambda b, i, k: (layer, 0, 0)),
                  pl.BlockSpec((None, 1, d), lambda b, i, k: (layer, 0, 0))],
        out_specs=pl.BlockSpec((None, tm, d), lambda b, i, k: (b, i, 0)),
        out_shape=jax.ShapeDtypeStruct((bsz, out_rows, d), F32),
        compiler_params=_params("arbitrary", "arbitrary", "arbitrary"),
        name="matmul_res_ln",
    )(a, w, res, mod, mod, ln_g, ln_b)


def _merge_kernel(h_ref, oa_ref, ob_ref, oc_ref, od_ref, wg_ref, bg_ref, wb_ref, o_ref):
    h = h_ref[...]
    acc = None
    for i, o_r in enumerate((oa_ref, ob_ref, oc_ref, od_ref)):
        g = _dot(h, wg_ref[i]) + bg_ref[i]
        t = _dot(o_r[...], wb_ref[i])
        term = _sigmoid(g) * t
        acc = term if acc is None else acc + term
    o_ref[...] = acc.astype(o_ref.dtype)


def merge_branches(h, outs, w_gate, b_gate4, w_branch, layer):
    bsz, l_all, d = h.shape
    bw = outs[0].shape[2]
    tm = _pick(l_all, (1088, 544, 384, 128))
    tn = 512
    o_spec = pl.BlockSpec((None, tm, bw), lambda b, i, j: (b, i, 0))
    return pl.pallas_call(
        _merge_kernel,
        grid=(bsz, l_all // tm, d // tn),
        in_specs=[pl.BlockSpec((None, tm, d), lambda b, i, j: (b, i, 0)), o_spec, o_spec, o_spec, o_spec,
                  pl.BlockSpec((None, N_BRANCH, d, tn), lambda b, i, j: (layer, 0, 0, j)),
                  pl.BlockSpec((None, N_BRANCH, 1, tn), lambda b, i, j: (layer, 0, 0, j)),
                  pl.BlockSpec((None, N_BRANCH, bw, tn), lambda b, i, j: (layer, 0, 0, j))],
        out_specs=pl.BlockSpec((None, tm, tn), lambda b, i, j: (b, i, j)),
        out_shape=jax.ShapeDtypeStruct((bsz, l_all, d), MXU_DT),
        compiler_params=_params("arbitrary", "arbitrary", "arbitrary"),
        name="merge_branches",
    )(h, *outs, w_gate, b_gate4, w_branch)


def _mla_prep_kernel(p_ref, gq_ref, gkv_ref, wuq_ref, wukv_ref, cos_ref, sin_ref, q_ref, k_ref, v_ref):
    scale = (MLA_NOPE + MLA_ROPE) ** -0.5 * LOG2_E
    p = p_ref[...].astype(F32)
    cq = p[:, 0:512]
    ckv = p[:, 512:640]
    ka = p[:, 640:768]
    kb = p[:, 768:896]
    rq = lax.rsqrt(jnp.sum(cq * cq, axis=-1, keepdims=True) * (1.0 / MLA_Q_RANK) + RMS_EPS)
    cqn = (cq * rq * gq_ref[...]).astype(MXU_DT)
    rkv = lax.rsqrt(jnp.mean(ckv * ckv, axis=-1, keepdims=True) + RMS_EPS)
    ckvn = (ckv * rkv * gkv_ref[...]).astype(MXU_DT)
    qf = _dot(cqn, wuq_ref[...])
    kvf = _dot(ckvn, wukv_ref[...])
    cos = cos_ref[...]
    sin = sin_ref[...]
    k_rope = (ka * cos + kb * sin).astype(k_ref.dtype)
    for hd in range(MLA_HEADS):
        qb = hd * 384
        q_ref[hd, :, 0:128] = (qf[:, qb:qb + 128] * scale).astype(q_ref.dtype)
        q_rope = qf[:, qb + 128:qb + 256] * cos + qf[:, qb + 256:qb + 384] * sin
        q_ref[hd, :, 128:256] = (q_rope * scale).astype(q_ref.dtype)
        k_ref[hd, :, 0:128] = kvf[:, hd * 256:hd * 256 + 128].astype(k_ref.dtype)
        k_ref[hd, :, 128:256] = k_rope
        v_ref[hd] = kvf[:, hd * 256 + 128:hd * 256 + 256].astype(v_ref.dtype)


def mla_prep(p, gq, gkv, wuq, wukv, cos, sin, layer):
    bsz, l_all, _ = p.shape
    tr = _pick(l_all, (544, 384, 128))
    h = MLA_HEADS
    return pl.pallas_call(
        _mla_prep_kernel,
        grid=(bsz, l_all // tr),
        in_specs=[pl.BlockSpec((None, tr, P_MLA_W), lambda b, i: (b, i, 0)),
                  pl.BlockSpec((None, 1, 512), lambda b, i: (layer, 0, 0)),
                  pl.BlockSpec((None, 1, 128), lambda b, i: (layer, 0, 0)),
                  pl.BlockSpec((None, 512, h * 384), lambda b, i: (layer, 0, 0)),
                  pl.BlockSpec((None, 128, h * 256), lambda b, i: (layer, 0, 0)),
                  pl.BlockSpec((tr, LANE), lambda b, i: (i, 0)),
                  pl.BlockSpec((tr, LANE), lambda b, i: (i, 0))],
        out_specs=[pl.BlockSpec((None, h, tr, 256), lambda b, i: (b, 0, i, 0)),
                   pl.BlockSpec((None, h, tr, 256), lambda b, i: (b, 0, i, 0)),
                   pl.BlockSpec((None, h, tr, 128), lambda b, i: (b, 0, i, 0))],
        out_shape=[jax.ShapeDtypeStruct((bsz, h, l_all, 256), MXU_DT),
                   jax.ShapeDtypeStruct((bsz, h, l_all, 256), MXU_DT),
                   jax.ShapeDtypeStruct((bsz, h, l_all, 128), MXU_DT)],
        compiler_params=_params("arbitrary", "arbitrary"),
        name="mla_prep",
    )(p, gq, gkv, wuq, wukv, cos, sin)


def _gqa_prep_kernel(p_ref, gq_ref, gk_ref, cos_ref, sin_ref, q_ref, k_ref, v_ref):
    scale = GQA_HD ** -0.5 * LOG2_E
    cos = cos_ref[...]
    sin = sin_ref[...]

    def norm_rope(x, g):
        r = lax.rsqrt(jnp.mean(x * x, axis=-1, keepdims=True) + RMS_EPS)
        xn = x * r * g
        return xn * cos + pltpu.roll(xn, GQA_HD // 2, 1) * sin

    for hd in range(GQA_HEADS):
        x = p_ref[:, hd * 128:(hd + 1) * 128].astype(F32)
        q_ref[hd] = (norm_rope(x, gq_ref[...]) * scale).astype(q_ref.dtype)
    for hd in range(GQA_KV_HEADS):
        x = p_ref[:, 512 + hd * 128:512 + (hd + 1) * 128].astype(F32)
        k_ref[hd] = norm_rope(x, gk_ref[...]).astype(k_ref.dtype)
        v_ref[hd] = p_ref[:, 768 + hd * 128:768 + (hd + 1) * 128].astype(v_ref.dtype)


def gqa_prep(p, gq, gk, cos, sin, layer):
    bsz, l_all, _ = p.shape
    tr = _pick(l_all, (544, 384, 128))
    return pl.pallas_call(
        _gqa_prep_kernel,
        grid=(bsz, l_all // tr),
        in_specs=[pl.BlockSpec((None, tr, GQA_COLS), lambda b, i: (b, i, P_GQA // GQA_COLS)),
                  pl.BlockSpec((None, 1, 128), lambda b, i: (layer, 0, 0)),
                  pl.BlockSpec((None, 1, 128), lambda b, i: (layer, 0, 0)),
                  pl.BlockSpec((tr, LANE), lambda b, i: (i, 0)),
                  pl.BlockSpec((tr, LANE), lambda b, i: (i, 0))],
        out_specs=[pl.BlockSpec((None, GQA_HEADS, tr, 128), lambda b, i: (b, 0, i, 0)),
                   pl.BlockSpec((None, GQA_KV_HEADS, tr, 128), lambda b, i: (b, 0, i, 0)),
                   pl.BlockSpec((None, GQA_KV_HEADS, tr, 128), lambda b, i: (b, 0, i, 0))],
        out_shape=[jax.ShapeDtypeStruct((bsz, GQA_HEADS, l_all, 128), MXU_DT),
                   jax.ShapeDtypeStruct((bsz, GQA_KV_HEADS, l_all, 128), MXU_DT),
                   jax.ShapeDtypeStruct((bsz, GQA_KV_HEADS, l_all, 128), MXU_DT)],
        compiler_params=_params("arbitrary", "arbitrary"),
        name="gqa_prep",
    )(p, gq, gk, cos, sin)


def _attn_kernel(q_ref, k_ref, v_ref, o_ref, *, n_lat, n_lat_tiles, sub):
    i = pl.program_id(2)

    def attend(q, k, v):
        s = _dot_nt(q, k)
        m = jnp.max(s, axis=-1, keepdims=True)
        e = jnp.exp2(s - m)
        l = jnp.sum(e, axis=-1, keepdims=True)
        o = _dot(e.astype(v.dtype), v)
        return (o / l).astype(o_ref.dtype)

    @pl.when(i < n_lat_tiles)
    def _():
        for r in range(0, q_ref.shape[0], sub):
            o_ref[r:r + sub, :] = attend(q_ref[r:r + sub, :], k_ref[...], v_ref[...])

    @pl.when(i >= n_lat_tiles)
    def _():
        for r in range(0, q_ref.shape[0], sub):
            o_ref[r:r + sub, :] = attend(q_ref[r:r + sub, :], k_ref[n_lat:, :], v_ref[n_lat:, :])


def attention(q, k, v, n_lat):
    bsz, h, l_all, dk = q.shape
    hkv = k.shape[1]
    grp = h // hkv
    dv = v.shape[3]
    tq = _pick(n_lat, (512, 256, 128))
    assert l_all - n_lat <= tq
    kern = functools.partial(_attn_kernel, n_lat=n_lat, n_lat_tiles=n_lat // tq, sub=min(256, tq))
    return pl.pallas_call(
        kern,
        grid=(bsz, h, pl.cdiv(l_all, tq)),
        in_specs=[pl.BlockSpec((None, None, tq, dk), lambda b, hh, i: (b, hh, i, 0)),
                  pl.BlockSpec((None, None, l_all, dk), lambda b, hh, i: (b, hh // grp, 0, 0)),
                  pl.BlockSpec((None, None, l_all, dv), lambda b, hh, i: (b, hh // grp, 0, 0))],
        out_specs=pl.BlockSpec((None, tq, dv), lambda b, hh, i: (b, i, hh)),
        out_shape=jax.ShapeDtypeStruct((bsz, l_all, h * dv), MXU_DT),
        compiler_params=_params("arbitrary", "arbitrary", "arbitrary"),
        name="attention",
    )(q, k, v)


def _conv3(u, w, b, n_lat):
    u = u.astype(F32)
    n = u.shape[0]
    row = lax.broadcasted_iota(jnp.int32, (n, 1), 0)
    prev = jnp.where((row == 0) | (row == n_lat), 0.0, pltpu.roll(u, 1, 0))
    nxt = jnp.where((row == n_lat - 1) | (row == n - 1), 0.0, pltpu.roll(u, n - 1, 0))
    return w[0:1, :] * prev + w[1:2, :] * u + w[2:3, :] * nxt + b


def _hy_conv_kernel(p0_ref, p1_ref, pv_ref, w0_ref, w1_ref, wv_ref, b0_ref, b1_ref, bv_ref,
                    x0_ref, vxt_ref, vxc_ref, *, n_lat):
    x0_ref[...] = _conv3(p0_ref[...], w0_ref[...], b0_ref[...], n_lat)
    x1 = _conv3(p1_ref[...], w1_ref[...], b1_ref[...], n_lat)
    v = _conv3(pv_ref[...], wv_ref[...], bv_ref[...], n_lat)
    vx = v * x1
    vxt_ref[...] = vx[:n_lat, :].T
    vxc_ref[...] = vx[n_lat:, :]


def hyena_conv(p, conv_w, conv_b3, layer, n_lat):
    bsz, l_all, _ = p.shape
    nb = HY_W // LANE
    base = P_HY // LANE

    def pspec(off):
        return pl.BlockSpec((None, l_all, LANE), lambda b, c: (b, 0, base + off + c))

    def wspec(off):
        return pl.BlockSpec((None, 3, LANE), lambda b, c: (layer, 0, off + c))

    def bspec(off):
        return pl.BlockSpec((None, 1, LANE), lambda b, c: (layer, 0, off + c))

    o_spec = pl.BlockSpec((None, l_all, LANE), lambda b, c: (b, 0, c))
    return pl.pallas_call(
        functools.partial(_hy_conv_kernel, n_lat=n_lat),
        grid=(bsz, nb),
        in_specs=[pspec(0), pspec(nb), pspec(2 * nb), wspec(0), wspec(nb), wspec(2 * nb),
                  bspec(0), bspec(nb), bspec(2 * nb)],
        out_specs=[o_spec,
                   pl.BlockSpec((None, LANE, n_lat), lambda b, c: (b, c, 0)),
                   pl.BlockSpec((None, l_all - n_lat, LANE), lambda b, c: (b, 0, c))],
        out_shape=[jax.ShapeDtypeStruct((bsz, l_all, HY_W), F32),
                   jax.ShapeDtypeStruct((bsz, HY_W, n_lat), F32),
                   jax.ShapeDtypeStruct((bsz, l_all - n_lat, HY_W), F32)],
        compiler_params=_params("arbitrary", "arbitrary"),
        name="hyena_conv",
    )(p, p, p, conv_w, conv_w, conv_w, conv_b3, conv_b3, conv_b3)


def _mb_conv_kernel(p_ref, w_ref, b_ref, o_ref, *, n_lat):
    o_ref[...] = _silu(_conv3(p_ref[...], w_ref[...], b_ref[...], n_lat))


def mamba_conv(p, conv_w, conv_b3, layer, n_lat):
    bsz, l_all, _ = p.shape
    nb = MB_CONV_CH // LANE
    base = P_XBC // LANE
    return pl.pallas_call(
        functools.partial(_mb_conv_kernel, n_lat=n_lat),
        grid=(bsz, nb),
        in_specs=[pl.BlockSpec((None, l_all, LANE), lambda b, c: (b, 0, base + c)),
                  pl.BlockSpec((None, 3, LANE), lambda b, c: (layer, 0, c)),
                  pl.BlockSpec((None, 1, LANE), lambda b, c: (layer, 0, c))],
        out_specs=pl.BlockSpec((None, l_all, LANE), lambda b, c: (b, 0, c)),
        out_shape=jax.ShapeDtypeStruct((bsz, l_all, MB_CONV_CH), F32),
        compiler_params=_params("arbitrary", "arbitrary"),
        name="mamba_conv",
    )(p, conv_w, conv_b3)


def _softplus(x):
    return jnp.maximum(x, 0.0) + jnp.log(1.0 + jnp.exp(-jnp.abs(x)))


def _mb_dt_kernel(p_ref, bias_ref, o_ref):
    dt = _softplus(p_ref[...] + bias_ref[...])
    o_ref[0] = dt
    o_ref[1] = pltpu.roll(dt, LANE - MB_HEADS, 1)


def mamba_dt(p, dt_bias_row, layer):
    bsz, l_all, _ = p.shape
    return pl.pallas_call(
        _mb_dt_kernel,
        grid=(bsz,),
        in_specs=[pl.BlockSpec((None, l_all, LANE), lambda b: (b, 0, 0)),
                  pl.BlockSpec((None, 1, LANE), lambda b: (layer, 0, 0))],
        out_specs=pl.BlockSpec((2, None, l_all, LANE), lambda b: (0, b, 0, 0)),
        out_shape=jax.ShapeDtypeStruct((2, bsz, l_all, LANE), F32),
        compiler_params=_params("arbitrary"),
        name="mamba_dt",
    )(p, dt_bias_row)


def _hy_filter_kernel(f_ref, aux_ref, w1_ref, b1_ref, w2_ref, b2_ref, w3_ref, fr_ref, dl_ref, o_ref, *,
                      channel_major):
    fr = fr_ref[...]
    hdn = jnp.sin(fr * (_dot_hi(f_ref[...], w1_ref[...]) + b1_ref[...]))
    hdn = jnp.sin(fr * (_dot_hi(hdn, w2_ref[...]) + b2_ref[...]))
    filt = _dot_hi(hdn, w3_ref[...])
    aux = aux_ref[...]
    t = aux[:, 0:1]
    is_fwd = aux[:, 1:2] > 0.5
    valid = aux[:, 2:3]
    window = jnp.exp(-t * dl_ref[...]) * valid
    k2 = jnp.where(is_fwd, filt[:, :HY_W], filt[:, HY_W:]) * window
    o_ref[...] = k2.T if channel_major else k2


def hyena_filter(feats2, aux, w1p, b1p, w2p, b2p, w3p, frp, deltas, layer, channel_major):
    rows = feats2.shape[0]
    tr = _pick(rows, (512, 256))

    def lspec(shape):
        return pl.BlockSpec((None,) + shape, lambda i: (layer, 0, 0))

    if channel_major:
        out_spec = pl.BlockSpec((HY_W, tr), lambda i: (0, i))
        out_shape = jax.ShapeDtypeStruct((HY_W, rows), F32)
    else:
        out_spec = pl.BlockSpec((tr, HY_W), lambda i: (i, 0))
        out_shape = jax.ShapeDtypeStruct((rows, HY_W), F32)
    return pl.pallas_call(
        functools.partial(_hy_filter_kernel, channel_major=channel_major),
        grid=(rows // tr,),
        in_specs=[pl.BlockSpec((tr, LANE), lambda i: (i, 0)),
                  pl.BlockSpec((tr, LANE), lambda i: (i, 0)),
                  lspec((LANE, LANE)), lspec((1, LANE)), lspec((LANE, LANE)), lspec((1, LANE)),
                  lspec((LANE, 2 * HY_W)), lspec((1, LANE)),
                  pl.BlockSpec((1, HY_W), lambda i: (0, 0))],
        out_specs=out_spec,
        out_shape=out_shape,
        compiler_params=_params("arbitrary"),
        name="hyena_filter",
    )(feats2, aux, w1p, b1p, w2p, b2p, w3p, frp, deltas)


DFT_MINOR = 256


def _snap(c):
    for v in (0.0, 1.0, -1.0):
        if abs(c - v) < 1e-12:
            return v
    return float(c)


def _lincomb(terms):
    acc = None
    for cf, tile in terms:
        if cf == 0.0:
            continue
        v = tile()
        if acc is None:
            acc = v if cf == 1.0 else (-v if cf == -1.0 else cf * v)
        elif cf == 1.0:
            acc = acc + v
        elif cf == -1.0:
            acc = acc - v
        else:
            acc = acc + cf * v
    return acc


def _dft_consts(n_seq):
    n = 2 * n_seq
    n2 = DFT_MINOR
    n1 = n // n2
    nkj = n1 // 2 + 1
    ang1 = 2.0 * np.pi * np.outer(np.arange(nkj), np.arange(n1)) / n1
    cos1 = [[_snap(v) for v in r] for r in np.cos(ang1)]
    sin1 = [[_snap(v) for v in r] for r in np.sin(ang1)]
    ang_t = 2.0 * np.pi * np.outer(np.arange(nkj), np.arange(n2)) / n
    rows = ((nkj + 7) // 8) * 8
    tw_c = np.zeros((rows, n2))
    tw_s = np.zeros((rows, n2))
    tw_c[:nkj] = np.cos(ang_t)
    tw_s[:nkj] = np.sin(ang_t)
    ang2 = 2.0 * np.pi * np.outer(np.arange(n2), np.arange(n2)) / n2
    c2, s2 = np.cos(ang2), np.sin(ang2)
    m_fwd = np.block([[c2, -s2], [s2, c2]])
    m_inv = np.block([[c2, s2], [-s2, c2]])
    as32 = lambda a: jnp.asarray(a, dtype=F32)

    def split(m):
        m32 = as32(m)
        hi = m32.astype(MXU_DT)
        lo = (m32 - hi.astype(F32)).astype(MXU_DT)
        return jnp.stack([hi, lo])

    return dict(n=n, n1=n1, nkj=nkj, cos1=cos1, sin1=sin1, tw_c=as32(tw_c), tw_s=as32(tw_s),
                m_fwd=split(m_fwd), m_inv=split(m_inv))


def _dot_split(a, m_ref):
    a_hi = a.astype(MXU_DT)
    a_lo = (a - a_hi.astype(F32)).astype(MXU_DT)
    return _dot(a_hi, m_ref[0]) + _dot(a_lo, m_ref[0]) + _dot(a_hi, m_ref[1])


def _outer_fwd(src_ref, z_ref, ct, n_in, cst):
    n2 = DFT_MINOR
    nkj, cos1, sin1 = cst["nkj"], cst["cos1"], cst["sin1"]

    def body(cb, carry):
        r0 = pl.multiple_of(cb * 8, 8)
        tile = lambda j: (lambda: src_ref[pl.ds(r0, 8), j * n2:(j + 1) * n2])
        for kj in range(nkj):
            re = _lincomb([(cos1[kj][j], tile(j)) for j in range(n_in)])
            im = _lincomb([(-sin1[kj][j], tile(j)) for j in range(n_in)])
            z_ref[pl.ds(kj * ct + r0, 8), 0:n2] = re
            z_ref[pl.ds(kj * ct + r0, 8), n2:2 * n2] = jnp.zeros_like(re) if im is None else im
        return carry

    lax.fori_loop(0, ct // 8, body, 0)


def _twiddle(z_ref, twc_ref, tws_ref, ct, nkj, inverse):
    n2 = DFT_MINOR
    for kj in range(1, nkj):
        re = z_ref[kj * ct:(kj + 1) * ct, 0:n2]
        im = z_ref[kj * ct:(kj + 1) * ct, n2:2 * n2]
        c = twc_ref[kj:kj + 1, :]
        s = tws_ref[kj:kj + 1, :]
        if inverse:
            s = -s
        z_ref[kj * ct:(kj + 1) * ct, 0:n2] = re * c + im * s
        z_ref[kj * ct:(kj + 1) * ct, n2:2 * n2] = im * c - re * s


def _spectrum_kernel(k_ref, twc_ref, tws_ref, mf_ref, h_ref, z_ref, *, ct, cst, n_chunks):
    nkj = cst["nkj"]
    _outer_fwd(k_ref, z_ref, ct, cst["n1"], cst)
    _twiddle(z_ref, twc_ref, tws_ref, ct, nkj, False)
    rc = nkj * ct // n_chunks

    def chunk(r, carry):
        rows = pl.ds(pl.multiple_of(r * rc, 8), rc)
        h_ref[rows, :] = _dot_split(z_ref[rows, :], mf_ref)
        return carry

    lax.fori_loop(0, n_chunks, chunk, 0)


def _long_conv_kernel(x_ref, h_ref, twc_ref, tws_ref, mf_ref, mi_ref, x0_ref, vxc_ref, yc_ref, skip_ref,
                      skip_row_ref, o_ref, z_ref, yt_ref, *, ct, cst, n_chunks, n_lat):
    n2 = DFT_MINOR
    nkj, n1, cos1, sin1 = cst["nkj"], cst["n1"], cst["cos1"], cst["sin1"]
    _outer_fwd(x_ref, z_ref, ct, n1 // 2, cst)
    _twiddle(z_ref, twc_ref, tws_ref, ct, nkj, False)
    rc = nkj * ct // n_chunks

    def chunk(r, carry):
        rows = pl.ds(pl.multiple_of(r * rc, 8), rc)
        x = _dot_split(z_ref[rows, :], mf_ref)
        xre, xim = x[:, :n2], x[:, n2:]
        hre, him = h_ref[rows, 0:n2], h_ref[rows, n2:2 * n2]
        y = jnp.concatenate([xre * hre - xim * him, xre * him + xim * hre], axis=1)
        z_ref[rows, :] = _dot_split(y, mi_ref)
        return carry

    lax.fori_loop(0, n_chunks, chunk, 0)
    _twiddle(z_ref, twc_ref, tws_ref, ct, nkj, True)

    inv_n = 1.0 / cst["n"]

    def body(cb, carry):
        r0 = pl.multiple_of(cb * 8, 8)
        g_re = lambda kj: (lambda: z_ref[pl.ds(kj * ct + r0, 8), 0:n2])
        g_im = lambda kj: (lambda: z_ref[pl.ds(kj * ct + r0, 8), n2:2 * n2])
        for j in range(n1 // 2):
            terms = []
            for kj in range(nkj):
                wgt = inv_n if kj in (0, nkj - 1) else 2.0 * inv_n
                terms.append((wgt * cos1[kj][j], g_re(kj)))
                terms.append((-wgt * sin1[kj][j], g_im(kj)))
            yt_ref[pl.ds(r0, 8), j * n2:(j + 1) * n2] = _lincomb(terms)
        return carry

    lax.fori_loop(0, ct // 8, body, 0)

    y = (yt_ref[...] + x_ref[...] * skip_ref[...]).T
    o_ref[0:n_lat, :] = (x0_ref[0:n_lat, :] * y).astype(o_ref.dtype)
    yc = yc_ref[...] + vxc_ref[...] * skip_row_ref[...]
    o_ref[n_lat:, :] = (x0_ref[n_lat:, :] * yc).astype(o_ref.dtype)


def filter_spectrum(k2t, cst):
    ch, n = k2t.shape
    ct = LANE
    nkj = cst["nkj"]
    n_chunks = _pick(nkj * ct // 8, (4, 5, 1))
    full = lambda a: pl.BlockSpec(a.shape, lambda c: (0,) * a.ndim)
    return pl.pallas_call(
        functools.partial(_spectrum_kernel, ct=ct, cst=cst, n_chunks=n_chunks),
        grid=(ch // ct,),
        in_specs=[pl.BlockSpec((ct, n), lambda c: (c, 0)),
                  full(cst["tw_c"]), full(cst["tw_s"]), full(cst["m_fwd"])],
        out_specs=pl.BlockSpec((None, nkj * ct, 2 * DFT_MINOR), lambda c: (c, 0, 0)),
        out_shape=jax.ShapeDtypeStruct((ch // ct, nkj * ct, 2 * DFT_MINOR), F32),
        scratch_shapes=[pltpu.VMEM((nkj * ct, 2 * DFT_MINOR), F32)],
        compiler_params=_params("arbitrary"),
        name="filter_spectrum",
    )(k2t, cst["tw_c"], cst["tw_s"], cst["m_fwd"])


def hyena_long_conv(vxt, h_spec, x0, vx_ctx, y_ctx, skip_col, skip_row, cst, layer, n_lat):
    bsz, ch, _ = vxt.shape
    l_all = x0.shape[1]
    n_ctx = l_all - n_lat
    ct = LANE
    nkj = cst["nkj"]
    n_chunks = _pick(nkj * ct // 8, (4, 5, 1))
    full = lambda a: pl.BlockSpec(a.shape, lambda b, c: (0,) * a.ndim)
    return pl.pallas_call(
        functools.partial(_long_conv_kernel, ct=ct, cst=cst, n_chunks=n_chunks, n_lat=n_lat),
        grid=(bsz, ch // ct),
        in_specs=[pl.BlockSpec((None, ct, n_lat), lambda b, c: (b, c, 0)),
                  pl.BlockSpec((None, nkj * ct, 2 * DFT_MINOR), lambda b, c: (c, 0, 0)),
                  full(cst["tw_c"]), full(cst["tw_s"]), full(cst["m_fwd"]), full(cst["m_inv"]),
                  pl.BlockSpec((None, l_all, ct), lambda b, c: (b, 0, c)),
                  pl.BlockSpec((None, n_ctx, ct), lambda b, c: (b, 0, c)),
                  pl.BlockSpec((None, n_ctx, ct), lambda b, c: (b, 0, c)),
                  pl.BlockSpec((None, ct, 1), lambda b, c: (layer, c, 0)),
                  pl.BlockSpec((None, 1, ct), lambda b, c: (layer, 0, c))],
        out_specs=pl.BlockSpec((None, l_all, ct), lambda b, c: (b, 0, c)),
        out_shape=jax.ShapeDtypeStruct((bsz, l_all, ch), MXU_DT),
        scratch_shapes=[pltpu.VMEM((nkj * ct, 2 * DFT_MINOR), F32), pltpu.VMEM((ct, n_lat), F32)],
        compiler_params=_params("arbitrary", "arbitrary"),
        name="hyena_long_conv",
    )(vxt, h_spec, cst["tw_c"], cst["tw_s"], cst["m_fwd"], cst["m_inv"], x0, vx_ctx, y_ctx, skip_col, skip_row)


def _ctx_conv_kernel(v_ref, k_ref, fd_ref, fk_ref, gi_ref, o_ref, *, kp):
    x = _dot_hi(fd_ref[...], v_ref[...])
    h = _dot_hi(fk_ref[...], k_ref[...])
    xre, xim = x[:kp], x[kp:]
    hre, him = h[:kp], h[kp:]
    y = jnp.concatenate([xre * hre - xim * him, xre * him + xim * hre], axis=0)
    o_ref[...] = _dot_hi(gi_ref[...], y)


def _ctx_tables(n_ctx):
    n = 2 * n_ctx
    kh = n_ctx + 1
    kp = ((kh + 7) // 8) * 8
    k = np.arange(kp)[:, None]
    live = (k < kh).astype(np.float64)
    ang = 2.0 * np.pi * k * np.arange(n)[None, :] / n
    ck, sk = np.cos(ang) * live, np.sin(ang) * live
    fk = np.concatenate([ck, -sk], axis=0)
    fd = fk[:, :n_ctx]
    wk = np.where((k == 0) | (k == n_ctx), 1.0, 2.0) * live / n
    gi = np.concatenate([(ck * wk).T, (-sk * wk).T], axis=1)[:n_ctx]
    as32 = lambda a: jnp.asarray(a, dtype=F32)
    return dict(kp=kp, fd=as32(fd), fk=as32(fk), gi=as32(gi))


def long_conv_ctx(vx_ctx, k2c, tb):
    bsz, n_ctx, ch = vx_ctx.shape
    kp = tb["kp"]
    full = lambda a: pl.BlockSpec(a.shape, lambda b, c: (0,) * a.ndim)
    return pl.pallas_call(
        functools.partial(_ctx_conv_kernel, kp=kp),
        grid=(bsz, ch // LANE),
        in_specs=[pl.BlockSpec((None, n_ctx, LANE), lambda b, c: (b, 0, c)),
                  pl.BlockSpec((2 * n_ctx, LANE), lambda b, c: (0, c)),
                  full(tb["fd"]), full(tb["fk"]), full(tb["gi"])],
        out_specs=pl.BlockSpec((None, n_ctx, LANE), lambda b, c: (b, 0, c)),
        out_shape=jax.ShapeDtypeStruct((bsz, n_ctx, ch), F32),
        compiler_params=_params("arbitrary", "arbitrary"),
        name="ctx_long_conv",
    )(vx_ctx, k2c, tb["fd"], tb["fk"], tb["gi"])


def _split3(a):
    a1 = a.astype(MXU_DT)
    r = a - a1.astype(F32)
    a2 = r.astype(MXU_DT)
    a3 = (r - a2.astype(F32)).astype(MXU_DT)
    return a1, a2, a3


def _ssd_chunk(backward, xs_ref, b_ref, c_ref, dt_ref, alog, ex, y_ref, state_ref):
    n = MB_CHUNK
    li = lax.broadcasted_iota(jnp.int32, (n, n), 0)
    si = lax.broadcasted_iota(jnp.int32, (n, n), 1)
    mask = (li <= si) if backward else (li >= si)
    tri = mask.astype(MXU_DT)
    lane = lax.broadcasted_iota(jnp.int32, (n, LANE), 1)

    dt = dt_ref[...]
    da = dt * (-jnp.exp(alog))
    cum = sum(_dot(tri, t) for t in _split3(da))
    cum_t = cum.T
    both = jnp.concatenate([cum, dt], axis=0)
    both_e = sum(_dot(t, ex) for t in _split3(both))
    cum_e = both_e[:n]
    dt_e = both_e[n:]
    total_e = cum_e[0:1] if backward else cum_e[n - 1:n]

    xdt = xs_ref[...] * dt_e
    xdt_b = xdt.astype(MXU_DT)
    w_b = (jnp.exp(total_e - cum_e) * xdt).astype(MXU_DT)
    st = state_ref[...]
    y_parts = []
    s_parts = []
    gw = MB_STATE
    hw = (MB_HEADS // MB_GROUPS) * MB_HEADDIM
    for g in range(MB_GROUPS):
        cg = c_ref[:, g * gw:(g + 1) * gw].astype(MXU_DT)
        bg = b_ref[:, g * gw:(g + 1) * gw]
        cb = _dot_nt(cg, bg.astype(MXU_DT))
        y_off = _dot(cg, st[:, g * hw:(g + 1) * hw].astype(MXU_DT))
        s_parts.append(_dot(bg.T.astype(MXU_DT), w_b[:, g * hw:(g + 1) * hw]))
        diag = []
        for j in range(hw // LANE):
            lo = g * hw + j * LANE
            xp = xdt_b[:, lo:lo + LANE]
            pair = []
            for e in range(2):
                hd = (lo // MB_HEADDIM) + e
                seg = cum[:, hd:hd + 1] - cum_t[hd:hd + 1, :]
                decay = jnp.exp(jnp.where(mask, seg, -jnp.inf))
                pair.append(_dot((cb * decay).astype(MXU_DT), xp))
            diag.append(jnp.where(lane < MB_HEADDIM, pair[0], pair[1]))
        y_parts.append(jnp.concatenate(diag, axis=1) + y_off * jnp.exp(cum_e[:, g * hw:(g + 1) * hw]))
    y_ref[...] = jnp.concatenate(y_parts, axis=1)
    state_ref[...] = st * jnp.exp(total_e) + jnp.concatenate(s_parts, axis=1)


def _ssd_kernel(xf_ref, bf_ref, cf_ref, dtf_ref, xb_ref, bb_ref, cb_ref, dtb_ref, alog_ref, ex_ref,
                yf_ref, yb_ref, state_ref):
    @pl.when(pl.program_id(1) == 0)
    def _():
        state_ref[...] = jnp.zeros_like(state_ref)

    ex = ex_ref[...]
    _ssd_chunk(False, xf_ref, bf_ref, cf_ref, dtf_ref, alog_ref[0], ex, yf_ref, state_ref.at[0])
    _ssd_chunk(True, xb_ref, bb_ref, cb_ref, dtb_ref, alog_ref[1], ex, yb_ref, state_ref.at[1])


def ssd_scan(xbc, dt2, a_log_rows, expand, layer, n_lat):
    bsz, l_all, _ = xbc.shape
    nc = l_all // MB_CHUNK
    ncl = n_lat // MB_CHUNK
    fwd = lambda c: (c + ncl) % nc
    bwd = lambda c: nc - 1 - c

    def specs(blk, direction):
        return [pl.BlockSpec((None, MB_CHUNK, MB_INNER), lambda b, c: (b, blk(c), 0)),
                pl.BlockSpec((None, MB_CHUNK, 256), lambda b, c: (b, blk(c), 2)),
                pl.BlockSpec((None, MB_CHUNK, 256), lambda b, c: (b, blk(c), 3)),
                pl.BlockSpec((None, None, MB_CHUNK, LANE), lambda b, c: (direction, b, blk(c), 0))]

    y_shape = jax.ShapeDtypeStruct((bsz, l_all, MB_INNER), F32)
    return pl.pallas_call(
        _ssd_kernel,
        grid=(bsz, nc),
        in_specs=specs(fwd, 0) + specs(bwd, 1)
        + [pl.BlockSpec((None, 2, 1, LANE), lambda b, c: (layer, 0, 0, 0)),
           pl.BlockSpec((LANE, MB_INNER), lambda b, c: (0, 0))],
        out_specs=[pl.BlockSpec((None, MB_CHUNK, MB_INNER), lambda b, c: (b, fwd(c), 0)),
                   pl.BlockSpec((None, MB_CHUNK, MB_INNER), lambda b, c: (b, bwd(c), 0))],
        out_shape=[y_shape, y_shape],
        scratch_shapes=[pltpu.VMEM((2, MB_STATE, MB_INNER), F32)],
        compiler_params=_params("arbitrary", "arbitrary"),
        name="ssd_scan",
    )(xbc, xbc, xbc, dt2, xbc, xbc, xbc, dt2, a_log_rows, expand)


def _mb_finish_kernel(yf_ref, yb_ref, xs_ref, z_ref, dsk_ref, g_ref, o_ref):
    y = yf_ref[...] + yb_ref[...] + xs_ref[...] * dsk_ref[...]
    y = y * _silu(z_ref[...].astype(F32))
    r = lax.rsqrt(jnp.mean(y * y, axis=-1, keepdims=True) + RMS_EPS)
    o_ref[...] = (y * r * g_ref[...]).astype(o_ref.dtype)


def mamba_finish(y_f, y_b, xbc, p, d_skip_e, norm_g3, layer):
    bsz, l_all, _ = xbc.shape
    tr = _pick(l_all, (544, 384, 128))
    w = MB_INNER
    return pl.pallas_call(
        _mb_finish_kernel,
        grid=(bsz, l_all // tr),
        in_specs=[pl.BlockSpec((None, tr, w), lambda b, i: (b, i, 0)),
                  pl.BlockSpec((None, tr, w), lambda b, i: (b, i, 0)),
                  pl.BlockSpec((None, tr, w), lambda b, i: (b, i, 0)),
                  pl.BlockSpec((None, tr, w), lambda b, i: (b, i, P_Z // w)),
                  pl.BlockSpec((None, 1, w), lambda b, i: (layer, 0, 0)),
                  pl.BlockSpec((None, 1, w), lambda b, i: (layer, 0, 0))],
        out_specs=pl.BlockSpec((None, tr, w), lambda b, i: (b, i, 0)),
        out_shape=jax.ShapeDtypeStruct((bsz, l_all, w), MXU_DT),
        compiler_params=_params("arbitrary", "arbitrary"),
        name="mamba_finish",
    )(y_f, y_b, xbc, p, d_skip_e, norm_g3)


def _deinterleave(w, heads):
    lead = w.shape[:-1]
    w = w.reshape(lead + (heads, GQA_HD // 2, 2))
    return jnp.concatenate([w[..., 0], w[..., 1]], axis=-1).reshape(lead + (heads * GQA_HD,))


def _prep_w_in(w_in):
    nl, d, _ = w_in.shape
    z64 = jnp.zeros((nl, d, 64), w_in.dtype)
    mla = w_in[..., :MLA_COLS]
    cq, ckv, kr = mla[..., :448], mla[..., 448:576], mla[..., 576:640]
    ev, od = kr[..., 0::2], kr[..., 1::2]
    mla_blk = jnp.concatenate([cq, z64, ckv, ev, od, z64, -od, ev, z64], axis=-1)
    gqa = w_in[..., MLA_COLS:MLA_COLS + GQA_COLS]
    gqa_blk = jnp.concatenate([_deinterleave(gqa[..., :512], GQA_HEADS),
                               _deinterleave(gqa[..., 512:768], GQA_KV_HEADS), gqa[..., 768:]], axis=-1)
    hy = w_in[..., MLA_COLS + GQA_COLS:MLA_COLS + GQA_COLS + HY_COLS]
    mb = w_in[..., MLA_COLS + GQA_COLS + HY_COLS:]
    z, xbc, dt = mb[..., :MB_INNER], mb[..., MB_INNER:MB_INNER + MB_CONV_CH], mb[..., MB_INNER + MB_CONV_CH:]
    dt_blk = jnp.concatenate([dt, jnp.zeros((nl, d, LANE - 2 * MB_HEADS), w_in.dtype)], axis=-1)
    out = jnp.concatenate([mla_blk, dt_blk, gqa_blk, z, hy, xbc], axis=-1)
    assert out.shape[-1] == P_COLS
    return out.astype(MXU_DT)


def _prep_w_uq(w_uq):
    nl = w_uq.shape[0]
    w = w_uq.reshape(nl, MLA_Q_RANK, MLA_HEADS, MLA_NOPE + MLA_ROPE)
    nope, rot = w[..., :MLA_NOPE], w[..., MLA_NOPE:]
    ev, od = rot[..., 0::2], rot[..., 1::2]
    z64 = jnp.zeros_like(rot)
    per_head = jnp.concatenate([nope, ev, od, z64, -od, ev, z64], axis=-1)
    w = per_head.reshape(nl, MLA_Q_RANK, MLA_HEADS * 384)
    w = jnp.concatenate([w, jnp.zeros((nl, 512 - MLA_Q_RANK, MLA_HEADS * 384), w.dtype)], axis=1)
    return w.astype(MXU_DT)


def _rope_tables(n_lat, n_ctx, rot_dim, sign_folded):
    rows = n_lat // GRID_W
    row = jnp.repeat(jnp.arange(rows, dtype=F32), GRID_W)
    col = jnp.tile(jnp.arange(GRID_W, dtype=F32), rows)
    n_freq = rot_dim // 4
    inv_freq = ROPE_THETA ** (-jnp.arange(n_freq, dtype=F32) / n_freq)
    ang = jnp.concatenate([row[:, None] * inv_freq, col[:, None] * inv_freq], axis=-1)
    half = rot_dim // 2
    cos = jnp.concatenate([jnp.cos(ang), jnp.ones((n_ctx, half), F32)], axis=0)
    sin = jnp.concatenate([jnp.sin(ang), jnp.zeros((n_ctx, half), F32)], axis=0)
    pad = jnp.zeros((n_lat + n_ctx, LANE - rot_dim), F32)
    cos_t = jnp.concatenate([cos, cos, pad], axis=-1)
    sin_t = jnp.concatenate([sin if sign_folded else -sin, sin, pad], axis=-1)
    return cos_t, sin_t


def _filter_inputs(n):
    t = jnp.linspace(0.0, 1.0, n, dtype=F32)[:, None]
    omega = 2.0 * math.pi * jnp.arange(n, dtype=F32) / n
    bands = jnp.linspace(1e-4, HY_BANDS - 1, HY_BANDS, dtype=F32)
    ang = omega[:, None] * bands[None, :]
    feats = jnp.concatenate([t, jnp.cos(ang), -jnp.sin(ang)], axis=-1)
    zero = jnp.zeros((1, HY_EMB), F32)
    feats2 = jnp.concatenate([feats, zero, feats[1:][::-1]], axis=0)
    feats2 = jnp.concatenate([feats2, jnp.zeros((2 * n, LANE - HY_EMB), F32)], axis=-1)
    t2 = jnp.concatenate([t, jnp.zeros((1, 1), F32), t[1:][::-1]], axis=0)
    idx = jnp.arange(2 * n)[:, None]
    aux = jnp.concatenate([t2, (idx < n).astype(F32), (idx != n).astype(F32),
                           jnp.zeros((2 * n, LANE - 3), F32)], axis=-1)
    return feats2, aux


def _pad_to(a, shape):
    pads = [(0, s - d) for d, s in zip(a.shape, shape)]
    return jnp.pad(a, pads)


def kernel(x, c, ctx, c_ctx, w_ada, b_ada, w_in, mla_q_norm, mla_kv_norm, mla_w_uq, mla_w_ukv, gqa_q_norm, gqa_k_norm, hy_conv_w, hy_conv_b, hy_w1, hy_b1, hy_w2, hy_b2, hy_w3, hy_freq, hy_skip, mb_conv_w, mb_conv_b, mb_a_log, mb_dt_bias, mb_d, mb_norm, w_mgate, b_mgate, w_branch, w_out, ln1_g, ln1_b, w_ffn_in, w_ffn_out, ln2_g, ln2_b):
    bsz, n_lat, d = x.shape
    n_ctx = ctx.shape[1]
    nl = w_in.shape[0]
    assert d == D_MODEL and bsz < 8 and n_lat % n_ctx == 0 and n_ctx % MB_CHUNK == 0

    w_in_p = _prep_w_in(w_in)
    w_uq_p = _prep_w_uq(mla_w_uq)
    w_ukv_p = mla_w_ukv.astype(MXU_DT)
    gq_mla = _pad_to(mla_q_norm, (nl, 512))[:, None, :]
    gkv_mla = mla_kv_norm[:, None, :]
    gq_gqa = _deinterleave(gqa_q_norm, 1)[:, None, :]
    gk_gqa = _deinterleave(gqa_k_norm, 1)[:, None, :]
    w_gate_b = w_mgate.astype(MXU_DT)
    w_branch_b = w_branch.astype(MXU_DT)
    w_out_b = w_out.astype(MXU_DT)
    w_ffn_in_b = w_ffn_in.astype(MXU_DT)
    w_ffn_out_b = w_ffn_out.astype(MXU_DT)
    b_gate4 = b_mgate[:, :, None, :]
    b_ada3 = b_ada[:, None, :]
    hy_conv_b3 = hy_conv_b[:, None, :]
    mb_conv_b3 = mb_conv_b[:, None, :]
    hy_skip3 = hy_skip[:, None, :]
    hy_skip_col = hy_skip[:, :, None]
    ln1_g3, ln1_b3, ln2_g3, ln2_b3 = (a[:, None, :] for a in (ln1_g, ln1_b, ln2_g, ln2_b))
    hy_w1p = _pad_to(hy_w1, (nl, LANE, LANE))
    hy_b1p = _pad_to(hy_b1, (nl, LANE))[:, None, :]
    hy_w2p = _pad_to(hy_w2, (nl, LANE, LANE))
    hy_b2p = _pad_to(hy_b2, (nl, LANE))[:, None, :]
    hy_w3p = _pad_to(hy_w3, (nl, LANE, 2 * HY_W))
    hy_frp = _pad_to(hy_freq, (nl, LANE))[:, None, :]
    deltas = jnp.abs(jnp.linspace(HY_MIN_DECAY, HY_MAX_DECAY, HY_W, dtype=F32))[None, :]
    dt_bias_row = _pad_to(mb_dt_bias.reshape(nl, 2 * MB_HEADS), (nl, LANE))[:, None, :]
    a_log_rows = _pad_to(mb_a_log, (nl, 2, LANE))[:, :, None, :]
    d_skip_e = jnp.repeat(mb_d, MB_HEADDIM, axis=-1)[:, None, :]
    mb_norm3 = mb_norm[:, None, :]
    expand = jnp.asarray(np.kron(np.eye(LANE, MB_HEADS), np.ones((1, MB_HEADDIM))), dtype=MXU_DT)

    cos_m, sin_m = _rope_tables(n_lat, n_ctx, MLA_ROPE, True)
    cos_g, sin_g = _rope_tables(n_lat, n_ctx, GQA_HD, False)
    feats_lat, aux_lat = _filter_inputs(n_lat)
    feats_ctx, aux_ctx = _filter_inputs(n_ctx)
    dft_lat = _dft_consts(n_lat)
    tb_ctx = _ctx_tables(n_ctx)

    c8 = jnp.concatenate([c, c_ctx[None, :], jnp.zeros((8 - bsz - 1, d), F32)], axis=0)
    xz = jnp.concatenate([x, ctx], axis=1)

    for l in range(nl):
        last = l == nl - 1
        mod = ada_mod(c8, w_ada, b_ada3, l).reshape(8, 1, 6 * d)
        p, p_dt, h = in_projection(xz, mod, w_in_p, l, n_lat)

        q, k, v = mla_prep(p, gq_mla, gkv_mla, w_uq_p, w_ukv_p, cos_m, sin_m, l)
        oa = attention(q, k, v, n_lat)
        q, k, v = gqa_prep(p, gq_gqa, gk_gqa, cos_g, sin_g, l)
        ob = attention(q, k, v, n_lat)
        x0, vxt, vx_ctx = hyena_conv(p, hy_conv_w, hy_conv_b3, l, n_lat)
        k2t = hyena_filter(feats_lat, aux_lat, hy_w1p, hy_b1p, hy_w2p, hy_b2p, hy_w3p, hy_frp, deltas, l, True)
        h_spec = filter_spectrum(k2t, dft_lat)
        if last:
            y_ctx = jnp.zeros((bsz, n_ctx, HY_W), F32)
        else:
            k2c = hyena_filter(feats_ctx, aux_ctx, hy_w1p, hy_b1p, hy_w2p, hy_b2p, hy_w3p, hy_frp, deltas, l,
                               False)
            y_ctx = long_conv_ctx(vx_ctx, k2c, tb_ctx)
        oc = hyena_long_conv(vxt, h_spec, x0, vx_ctx, y_ctx, hy_skip_col, hy_skip3, dft_lat, l, n_lat)
        xbc = mamba_conv(p, mb_conv_w, mb_conv_b3, l, n_lat)
        dt2 = mamba_dt(p_dt, dt_bias_row, l)
        y_f, y_b = ssd_scan(xbc, dt2, a_log_rows, expand, l, n_lat)
        od = mamba_finish(y_f, y_b, xbc, p, d_skip_e, mb_norm3, l)

        acc = merge_branches(h, (oa, ob, oc, od), w_gate_b, b_gate4, w_branch_b, l)
        x1 = matmul_res_ln(acc, w_out_b, xz, mod, 2, ln1_g3, ln1_b3, l, n_lat, n_lat + n_ctx,
                           (544, 384, 128), d)
        act = ffn_in(x1, mod, w_ffn_in_b, l, n_lat)
        out_rows = n_lat if last else n_lat + n_ctx
        xz = matmul_res_ln(act, w_ffn_out_b, x1, mod, 5, ln2_g3, ln2_b3, l, n_lat, out_rows,
                           (1088, 384, 128), 512)
    return xz
```

```python
import functools
import math

import jax
import jax.numpy as jnp
import numpy as np
from jax import lax
from jax.experimental import pallas as pl
from jax.experimental.pallas import tpu as pltpu

F32 = jnp.float32
MXU_DT = jnp.bfloat16

D_MODEL = 2048
DEPTH = 2
GRID_W = 64
N_BRANCH = 4
BRANCH_W = D_MODEL // N_BRANCH
ROPE_THETA = 10000.0
LN_EPS = 1e-6
RMS_EPS = 1e-6
DEEPNORM_ALPHA = (2 * DEPTH) ** 0.25

MLA_HEADS = 4
MLA_Q_RANK = 448
MLA_KV_RANK = 128
MLA_NOPE = 128
MLA_ROPE = 64
MLA_V = 128
MLA_COLS = MLA_Q_RANK + MLA_KV_RANK + MLA_ROPE

GQA_HEADS = 4
GQA_KV_HEADS = 2
GQA_HD = 128
GQA_COLS = (GQA_HEADS + 2 * GQA_KV_HEADS) * GQA_HD

HY_W = BRANCH_W
HY_EMB = 33
HY_BANDS = (HY_EMB - 1) // 2
HY_FFN = 64
HY_MIN_DECAY = math.log(1e-2) / 1.5
HY_MAX_DECAY = math.log(1e-2) / 0.3
HY_COLS = 3 * HY_W

MB_INNER = BRANCH_W
MB_HEADDIM = 64
MB_HEADS = 8
MB_GROUPS = 2
MB_STATE = 128
MB_CHUNK = 128
MB_CONV_CH = MB_INNER + 2 * MB_GROUPS * MB_STATE
MB_COLS = MB_INNER + MB_CONV_CH + 2 * MB_HEADS

FFN_HIDDEN = 5632

LANE = 128
VMEM_LIMIT = 56 * 1024 * 1024

P_MLA = 0
P_MLA_W = 896
P_DT = 896
P_GQA = 1024
P_Z = 2048
P_HY = 2560
P_XBC = 4096
P_COLS = 5120

LOG2_E = math.log2(math.e)
_HI = lax.Precision.HIGHEST


def _dot(a, b):
    return jnp.dot(a, b, preferred_element_type=F32)


def _dot_hi(a, b):
    return jnp.dot(a, b, precision=_HI, preferred_element_type=F32)


def _dot_nt(a, b):
    return lax.dot_general(a, b, (((1,), (1,)), ((), ())), preferred_element_type=F32)


def _sigmoid(x):
    return 1.0 / (1.0 + jnp.exp(-x))


def _silu(x):
    return x * _sigmoid(x)


def _params(*sem):
    return pltpu.CompilerParams(dimension_semantics=sem, vmem_limit_bytes=VMEM_LIMIT)


def _pick(n, prefs):
    for p in prefs:
        if n % p == 0:
            return p
    raise ValueError(f"no tile for {n} in {prefs}")


def _standardize(x):
    mu = jnp.mean(x, axis=-1, keepdims=True)
    xc = x - mu
    var = jnp.mean(xc * xc, axis=-1, keepdims=True)
    return xc * lax.rsqrt(var + LN_EPS)


def _is_ctx_rows(tile_idx, tm, n_lat):
    row = tile_idx * tm + lax.broadcasted_iota(jnp.int32, (tm, 1), 0)
    return row >= n_lat


def _ada_kernel(c_ref, w_ref, b_ref, o_ref):
    cs = _silu(c_ref[...])
    o_ref[...] = _dot(cs.astype(MXU_DT), w_ref[...].astype(MXU_DT)) + b_ref[...]


def ada_mod(c8, w_ada, b_ada3, layer):
    d = c8.shape[1]
    n = w_ada.shape[2]
    tn = 1024
    return pl.pallas_call(
        _ada_kernel,
        grid=(n // tn,),
        in_specs=[pl.BlockSpec((8, d), lambda j: (0, 0)),
                  pl.BlockSpec((None, d, tn), lambda j: (layer, 0, j)),
                  pl.BlockSpec((None, 1, tn), lambda j: (layer, 0, j))],
        out_specs=pl.BlockSpec((8, tn), lambda j: (0, j)),
        out_shape=jax.ShapeDtypeStruct((8, n), F32),
        compiler_params=_params("arbitrary"),
        name="ada_mod",
    )(c8, w_ada, b_ada3)


ROW_CHUNKS = 4


def _modulated(x, sh_ref, sc_ref, shc_ref, scc_ref, row0, n_lat):
    xn = _standardize(x)
    row = row0 + lax.broadcasted_iota(jnp.int32, (x.shape[0], 1), 0)
    is_ctx = row >= n_lat
    scale = jnp.where(is_ctx, scc_ref[...], sc_ref[...])
    shift = jnp.where(is_ctx, shc_ref[...], sh_ref[...])
    return (xn * (1.0 + scale) + shift).astype(MXU_DT)


def _inproj_kernel(x_ref, sh_ref, sc_ref, shc_ref, scc_ref, w_ref, p_ref, pdt_ref, h_ref, *, tm, n_lat):
    j = pl.program_id(2)
    rc = tm // ROW_CHUNKS

    @pl.when(j == 0)
    def _():
        for r in range(ROW_CHUNKS):
            rows = slice(r * rc, (r + 1) * rc)
            hb = _modulated(x_ref[rows, :], sh_ref, sc_ref, shc_ref, scc_ref,
                            pl.program_id(1) * tm + r * rc, n_lat)
            h_ref[rows, :] = hb
            acc = _dot(hb, w_ref[...])
            p_ref[rows, :] = acc.astype(p_ref.dtype)
            pdt_ref[rows, :] = acc[:, P_DT:P_DT + LANE]

    @pl.when(j > 0)
    def _():
        p_ref[...] = _dot(h_ref[...], w_ref[...]).astype(p_ref.dtype)


def _ffn_in_kernel(x_ref, sh_ref, sc_ref, shc_ref, scc_ref, wu_ref, wg_ref, a_ref, h_ref, *, tm, n_lat):
    j = pl.program_id(2)
    rc = tm // ROW_CHUNKS

    def swiglu(h):
        up = _dot(h, wu_ref[...])
        gate = _dot(h, wg_ref[...])
        return (_silu(gate) * up).astype(a_ref.dtype)

    @pl.when(j == 0)
    def _():
        for r in range(ROW_CHUNKS):
            rows = slice(r * rc, (r + 1) * rc)
            hb = _modulated(x_ref[rows, :], sh_ref, sc_ref, shc_ref, scc_ref,
                            pl.program_id(1) * tm + r * rc, n_lat)
            h_ref[rows, :] = hb
            a_ref[rows, :] = swiglu(hb)

    @pl.when(j > 0)
    def _():
        half = tm // 2
        for r in range(2):
            rows = slice(r * half, (r + 1) * half)
            a_ref[rows, :] = swiglu(h_ref[rows, :])


def _mod_specs(k_shift, k_scale, n_batch):
    d = D_MODEL
    return [pl.BlockSpec((None, 1, d), lambda b, i, j: (b, 0, k_shift)),
            pl.BlockSpec((None, 1, d), lambda b, i, j: (b, 0, k_scale)),
            pl.BlockSpec((None, 1, d), lambda b, i, j: (n_batch, 0, k_shift)),
            pl.BlockSpec((None, 1, d), lambda b, i, j: (n_batch, 0, k_scale))]


def in_projection(xz, mod, w_in_p, layer, n_lat):
    bsz, l_all, d = xz.shape
    n = w_in_p.shape[2]
    tm = _pick(l_all, (1088, 544, 384, 128))
    tn = 1024
    assert n % tn == 0 and P_DT + LANE <= tn
    kern = functools.partial(_inproj_kernel, tm=tm, n_lat=n_lat)
    return pl.pallas_call(
        kern,
        grid=(bsz, l_all // tm, n // tn),
        in_specs=[pl.BlockSpec((None, tm, d), lambda b, i, j: (b, i, 0))]
        + _mod_specs(0, 1, bsz)
        + [pl.BlockSpec((None, d, tn), lambda b, i, j: (layer, 0, j))],
        out_specs=[pl.BlockSpec((None, tm, tn), lambda b, i, j: (b, i, j)),
                   pl.BlockSpec((None, tm, LANE), lambda b, i, j: (b, i, 0)),
                   pl.BlockSpec((None, tm, d), lambda b, i, j: (b, i, 0))],
        out_shape=[jax.ShapeDtypeStruct((bsz, l_all, n), MXU_DT),
                   jax.ShapeDtypeStruct((bsz, l_all, LANE), F32),
                   jax.ShapeDtypeStruct((bsz, l_all, d), MXU_DT)],
        compiler_params=_params("arbitrary", "arbitrary", "arbitrary"),
        name="in_projection",
    )(xz, mod, mod, mod, mod, w_in_p)


def ffn_in(xz, mod, w_ffn_in, layer, n_lat):
    bsz, l_all, d = xz.shape
    hid = w_ffn_in.shape[2] // 2
    tm = _pick(l_all, (1088, 544, 384, 128))
    tn = 512
    nj = hid // tn
    kern = functools.partial(_ffn_in_kernel, tm=tm, n_lat=n_lat)
    return pl.pallas_call(
        kern,
        grid=(bsz, l_all // tm, nj),
        in_specs=[pl.BlockSpec((None, tm, d), lambda b, i, j: (b, i, 0))]
        + _mod_specs(3, 4, bsz)
        + [pl.BlockSpec((None, d, tn), lambda b, i, j: (layer, 0, j)),
           pl.BlockSpec((None, d, tn), lambda b, i, j: (layer, 0, j + nj))],
        out_specs=pl.BlockSpec((None, tm, tn), lambda b, i, j: (b, i, j)),
        out_shape=jax.ShapeDtypeStruct((bsz, l_all, hid), MXU_DT),
        scratch_shapes=[pltpu.VMEM((tm, d), MXU_DT)],
        compiler_params=_params("arbitrary", "arbitrary", "arbitrary"),
        name="ffn_in",
    )(xz, mod, mod, mod, mod, w_ffn_in, w_ffn_in)


def _resln_kernel(a_ref, w_ref, res_ref, g_ref, gc_ref, lng_ref, lnb_ref, o_ref, *, tm, n_lat, nk, rc):
    k = pl.program_id(2)
    i = pl.program_id(1)

    if nk > 1:
        @pl.when(k == 0)
        def _():
            o_ref[...] = _dot(a_ref[...], w_ref[...])

        @pl.when((k > 0) & (k < nk - 1))
        def _():
            o_ref[...] += _dot(a_ref[...], w_ref[...])

    @pl.when(k == nk - 1)
    def _():
        for r in range(tm // rc):
            rows = slice(r * rc, (r + 1) * rc)
            acc = _dot(a_ref[rows, :], w_ref[...])
            if nk > 1:
                acc = acc + o_ref[rows, :]
            row = i * tm + r * rc + lax.broadcasted_iota(jnp.int32, (rc, 1), 0)
            gate = jnp.where(row >= n_lat, gc_ref[...], g_ref[...])
            y = DEEPNORM_ALPHA * res_ref[rows, :] + gate * acc
            o_ref[rows, :] = _standardize(y) * lng_ref[...] + lnb_ref[...]


def matmul_res_ln(a, w, res, mod, k_gate, ln_g, ln_b, layer, n_lat, out_rows, tm_prefs, tk):
    bsz, l_all, kdim = a.shape
    d = w.shape[2]
    tm = _pick(l_all, tm_prefs)
    nk = kdim // tk
    rc = _pick(tm, (272, 192, 128))
    kern = functools.partial(_resln_kernel, tm=tm, n_lat=n_lat, nk=nk, rc=rc)
    return pl.pallas_call(
        kern,
        grid=(bsz, pl.cdiv(out_rows, tm), nk),
        in_specs=[pl.BlockSpec((None, tm, tk), lambda b, i, k: (b, i, k)),
                  pl.BlockSpec((None, tk, d), lambda b, i, k: (layer, k, 0)),
                  pl.BlockSpec((None, tm, d), lambda b, i, k: (b, i, 0)),
                  pl.BlockSpec((None, 1, d), lambda b, i, k: (b, 0, k_gate)),
                  pl.BlockSpec((None, 1, d), lambda b, i, k: (bsz, 0, k_gate)),
                  pl.BlockSpec((None, 1, d), lambda b, i, k: (layer, 0, 0)),
                  pl.BlockSpec((None, 1, d), lambda b, i, k: (layer, 0, 0))],
        out_specs=pl.BlockSpec((None, tm, d), lambda b, i, k: (b, i, 0)),
        out_shape=jax.ShapeDtypeStruct((bsz, out_rows, d), F32),
        compiler_params=_params("arbitrary", "arbitrary", "arbitrary"),
        name="matmul_res_ln",
    )(a, w, res, mod, mod, ln_g, ln_b)


def _merge_kernel(h_ref, oa_ref, ob_ref, oc_ref, od_ref, wg_ref, bg_ref, wb_ref, o_ref):
    h = h_ref[...]
    acc = None
    for i, o_r in enumerate((oa_ref, ob_ref, oc_ref, od_ref)):
        g = _dot(h, wg_ref[i]) + bg_ref[i]
        t = _dot(o_r[...], wb_ref[i])
        term = _sigmoid(g) * t
        acc = term if acc is None else acc + term
    o_ref[...] = acc.astype(o_ref.dtype)


def merge_branches(h, outs, w_gate, b_gate4, w_branch, layer):
    bsz, l_all, d = h.shape
    bw = outs[0].shape[2]
    tm = _pick(l_all, (1088, 544, 384, 128))
    tn = 512
    o_spec = pl.BlockSpec((None, tm, bw), lambda b, i, j: (b, i, 0))
    return pl.pallas_call(
        _merge_kernel,
        grid=(bsz, l_all // tm, d // tn),
        in_specs=[pl.BlockSpec((None, tm, d), lambda b, i, j: (b, i, 0)), o_spec, o_spec, o_spec, o_spec,
                  pl.BlockSpec((None, N_BRANCH, d, tn), lambda b, i, j: (layer, 0, 0, j)),
                  pl.BlockSpec((None, N_BRANCH, 1, tn), lambda b, i, j: (layer, 0, 0, j)),
                  pl.BlockSpec((None, N_BRANCH, bw, tn), lambda b, i, j: (layer, 0, 0, j))],
        out_specs=pl.BlockSpec((None, tm, tn), lambda b, i, j: (b, i, j)),
        out_shape=jax.ShapeDtypeStruct((bsz, l_all, d), MXU_DT),
        compiler_params=_params("arbitrary", "arbitrary", "arbitrary"),
        name="merge_branches",
    )(h, *outs, w_gate, b_gate4, w_branch)


def _mla_prep_kernel(p_ref, gq_ref, gkv_ref, wuq_ref, wukv_ref, cos_ref, sin_ref, q_ref, k_ref, v_ref):
    scale = (MLA_NOPE + MLA_ROPE) ** -0.5 * LOG2_E
    p = p_ref[...].astype(F32)
    cq = p[:, 0:512]
    ckv = p[:, 512:640]
    ka = p[:, 640:768]
    kb = p[:, 768:896]
    rq = lax.rsqrt(jnp.sum(cq * cq, axis=-1, keepdims=True) * (1.0 / MLA_Q_RANK) + RMS_EPS)
    cqn = (cq * rq * gq_ref[...]).astype(MXU_DT)
    rkv = lax.rsqrt(jnp.mean(ckv * ckv, axis=-1, keepdims=True) + RMS_EPS)
    ckvn = (ckv * rkv * gkv_ref[...]).astype(MXU_DT)
    qf = _dot(cqn, wuq_ref[...])
    kvf = _dot(ckvn, wukv_ref[...])
    cos = cos_ref[...]
    sin = sin_ref[...]
    k_rope = (ka * cos + kb * sin).astype(k_ref.dtype)
    for hd in range(MLA_HEADS):
        qb = hd * 384
        q_ref[hd, :, 0:128] = (qf[:, qb:qb + 128] * scale).astype(q_ref.dtype)
        q_rope = qf[:, qb + 128:qb + 256] * cos + qf[:, qb + 256:qb + 384] * sin
        q_ref[hd, :, 128:256] = (q_rope * scale).astype(q_ref.dtype)
        k_ref[hd, :, 0:128] = kvf[:, hd * 256:hd * 256 + 128].astype(k_ref.dtype)
        k_ref[hd, :, 128:256] = k_rope
        v_ref[hd] = kvf[:, hd * 256 + 128:hd * 256 + 256].T.astype(v_ref.dtype)


def mla_prep(p, gq, gkv, wuq, wukv, cos, sin, layer):
    bsz, l_all, _ = p.shape
    tr = _pick(l_all, (256, 128))
    h = MLA_HEADS
    return pl.pallas_call(
        _mla_prep_kernel,
        grid=(bsz, l_all // tr),
        in_specs=[pl.BlockSpec((None, tr, P_MLA_W), lambda b, i: (b, i, 0)),
                  pl.BlockSpec((None, 1, 512), lambda b, i: (layer, 0, 0)),
                  pl.BlockSpec((None, 1, 128), lambda b, i: (layer, 0, 0)),
                  pl.BlockSpec((None, 512, h * 384), lambda b, i: (layer, 0, 0)),
                  pl.BlockSpec((None, 128, h * 256), lambda b, i: (layer, 0, 0)),
                  pl.BlockSpec((tr, LANE), lambda b, i: (i, 0)),
                  pl.BlockSpec((tr, LANE), lambda b, i: (i, 0))],
        out_specs=[pl.BlockSpec((None, h, tr, 256), lambda b, i: (b, 0, i, 0)),
                   pl.BlockSpec((None, h, tr, 256), lambda b, i: (b, 0, i, 0)),
                   pl.BlockSpec((None, h, 128, tr), lambda b, i: (b, 0, 0, i))],
        out_shape=[jax.ShapeDtypeStruct((bsz, h, l_all, 256), MXU_DT),
                   jax.ShapeDtypeStruct((bsz, h, l_all, 256), MXU_DT),
                   jax.ShapeDtypeStruct((bsz, h, 128, l_all), MXU_DT)],
        compiler_params=_params("arbitrary", "arbitrary"),
        name="mla_prep",
    )(p, gq, gkv, wuq, wukv, cos, sin)


def _gqa_prep_kernel(p_ref, gq_ref, gk_ref, cos_ref, sin_ref, q_ref, k_ref, v_ref):
    scale = GQA_HD ** -0.5 * LOG2_E
    cos = cos_ref[...]
    sin = sin_ref[...]

    def norm_rope(x, g):
        r = lax.rsqrt(jnp.mean(x * x, axis=-1, keepdims=True) + RMS_EPS)
        xn = x * r * g
        return xn * cos + pltpu.roll(xn, GQA_HD // 2, 1) * sin

    for hd in range(GQA_HEADS):
        x = p_ref[:, hd * 128:(hd + 1) * 128].astype(F32)
        q_ref[hd] = (norm_rope(x, gq_ref[...]) * scale).astype(q_ref.dtype)
    for hd in range(GQA_KV_HEADS):
        x = p_ref[:, 512 + hd * 128:512 + (hd + 1) * 128].astype(F32)
        k_ref[hd] = norm_rope(x, gk_ref[...]).astype(k_ref.dtype)
        v = p_ref[:, 768 + hd * 128:768 + (hd + 1) * 128].astype(F32)
        v_ref[hd] = v.T.astype(v_ref.dtype)


def gqa_prep(p, gq, gk, cos, sin, layer):
    bsz, l_all, _ = p.shape
    tr = _pick(l_all, (256, 128))
    return pl.pallas_call(
        _gqa_prep_kernel,
        grid=(bsz, l_all // tr),
        in_specs=[pl.BlockSpec((None, tr, GQA_COLS), lambda b, i: (b, i, P_GQA // GQA_COLS)),
                  pl.BlockSpec((None, 1, 128), lambda b, i: (layer, 0, 0)),
                  pl.BlockSpec((None, 1, 128), lambda b, i: (layer, 0, 0)),
                  pl.BlockSpec((tr, LANE), lambda b, i: (i, 0)),
                  pl.BlockSpec((tr, LANE), lambda b, i: (i, 0))],
        out_specs=[pl.BlockSpec((None, GQA_HEADS, tr, 128), lambda b, i: (b, 0, i, 0)),
                   pl.BlockSpec((None, GQA_KV_HEADS, tr, 128), lambda b, i: (b, 0, i, 0)),
                   pl.BlockSpec((None, GQA_KV_HEADS, 128, tr), lambda b, i: (b, 0, 0, i))],
        out_shape=[jax.ShapeDtypeStruct((bsz, GQA_HEADS, l_all, 128), MXU_DT),
                   jax.ShapeDtypeStruct((bsz, GQA_KV_HEADS, l_all, 128), MXU_DT),
                   jax.ShapeDtypeStruct((bsz, GQA_KV_HEADS, 128, l_all), MXU_DT)],
        compiler_params=_params("arbitrary", "arbitrary"),
        name="gqa_prep",
    )(p, gq, gk, cos, sin)


def _attn_kernel(q_ref, k_ref, v_ref, o_ref, *, n_lat, n_lat_tiles, sub):
    i = pl.program_id(2)

    def attend_all(k, vt):
        groups = [slice(r, r + sub) for r in range(0, q_ref.shape[0], sub)]
        sts = [_dot_nt(k, q_ref[g, :]) for g in groups]
        es, ls = [], []
        for st in sts:
            m = jnp.max(st, axis=0, keepdims=True)
            e = jnp.exp2(st - m)
            ls.append(jnp.sum(e, axis=0, keepdims=True))
            es.append(e.astype(vt.dtype))
        for g, e, l in zip(groups, es, ls):
            ot = _dot(vt, e) / l
            o_ref[g, :] = ot.T.astype(o_ref.dtype)

    @pl.when(i < n_lat_tiles)
    def _():
        attend_all(k_ref[...], v_ref[...])

    @pl.when(i >= n_lat_tiles)
    def _():
        attend_all(k_ref[n_lat:, :], v_ref[:, n_lat:])


def attention(q, k, v, n_lat):
    bsz, h, l_all, dk = q.shape
    hkv = k.shape[1]
    grp = h // hkv
    dv = v.shape[2]
    tq = _pick(n_lat, (1024, 512, 256, 128))
    assert l_all - n_lat <= tq
    kern = functools.partial(_attn_kernel, n_lat=n_lat, n_lat_tiles=n_lat // tq, sub=min(256, tq))
    return pl.pallas_call(
        kern,
        grid=(bsz, h, pl.cdiv(l_all, tq)),
        in_specs=[pl.BlockSpec((None, None, tq, dk), lambda b, hh, i: (b, hh, i, 0)),
                  pl.BlockSpec((None, None, l_all, dk), lambda b, hh, i: (b, hh // grp, 0, 0)),
                  pl.BlockSpec((None, None, dv, l_all), lambda b, hh, i: (b, hh // grp, 0, 0))],
        out_specs=pl.BlockSpec((None, tq, dv), lambda b, hh, i: (b, i, hh)),
        out_shape=jax.ShapeDtypeStruct((bsz, l_all, h * dv), MXU_DT),
        compiler_params=_params("arbitrary", "arbitrary", "arbitrary"),
        name="attention",
    )(q, k, v)


def _conv3(u, w, b, n_lat):
    u = u.astype(F32)
    n = u.shape[0]
    row = lax.broadcasted_iota(jnp.int32, (n, 1), 0)
    prev = jnp.where((row == 0) | (row == n_lat), 0.0, pltpu.roll(u, 1, 0))
    nxt = jnp.where((row == n_lat - 1) | (row == n - 1), 0.0, pltpu.roll(u, n - 1, 0))
    return w[0:1, :] * prev + w[1:2, :] * u + w[2:3, :] * nxt + b


def _hy_conv_kernel(p0_ref, p1_ref, pv_ref, w0_ref, w1_ref, wv_ref, b0_ref, b1_ref, bv_ref,
                    x0_ref, vxt_ref, vxc_ref, *, n_lat):
    x0_ref[...] = _conv3(p0_ref[...], w0_ref[...], b0_ref[...], n_lat)
    x1 = _conv3(p1_ref[...], w1_ref[...], b1_ref[...], n_lat)
    v = _conv3(pv_ref[...], wv_ref[...], bv_ref[...], n_lat)
    vx = v * x1
    vxt_ref[...] = vx[:n_lat, :].T
    vxc_ref[...] = vx[n_lat:, :]


def hyena_conv(p, conv_w, conv_b3, layer, n_lat):
    bsz, l_all, _ = p.shape
    nb = HY_W // LANE
    base = P_HY // LANE

    def pspec(off):
        return pl.BlockSpec((None, l_all, LANE), lambda b, c: (b, 0, base + off + c))

    def wspec(off):
        return pl.BlockSpec((None, 3, LANE), lambda b, c: (layer, 0, off + c))

    def bspec(off):
        return pl.BlockSpec((None, 1, LANE), lambda b, c: (layer, 0, off + c))

    o_spec = pl.BlockSpec((None, l_all, LANE), lambda b, c: (b, 0, c))
    return pl.pallas_call(
        functools.partial(_hy_conv_kernel, n_lat=n_lat),
        grid=(bsz, nb),
        in_specs=[pspec(0), pspec(nb), pspec(2 * nb), wspec(0), wspec(nb), wspec(2 * nb),
                  bspec(0), bspec(nb), bspec(2 * nb)],
        out_specs=[o_spec,
                   pl.BlockSpec((None, LANE, n_lat), lambda b, c: (b, c, 0)),
                   pl.BlockSpec((None, l_all - n_lat, LANE), lambda b, c: (b, 0, c))],
        out_shape=[jax.ShapeDtypeStruct((bsz, l_all, HY_W), F32),
                   jax.ShapeDtypeStruct((bsz, HY_W, n_lat), F32),
                   jax.ShapeDtypeStruct((bsz, l_all - n_lat, HY_W), F32)],
        compiler_params=_params("arbitrary", "arbitrary"),
        name="hyena_conv",
    )(p, p, p, conv_w, conv_w, conv_w, conv_b3, conv_b3, conv_b3)


def _mb_conv_kernel(p_ref, w_ref, b_ref, o_ref, *, n_lat):
    o_ref[...] = _silu(_conv3(p_ref[...], w_ref[...], b_ref[...], n_lat))


def mamba_conv(p, conv_w, conv_b3, layer, n_lat):
    bsz, l_all, _ = p.shape
    nb = MB_CONV_CH // LANE
    base = P_XBC // LANE
    return pl.pallas_call(
        functools.partial(_mb_conv_kernel, n_lat=n_lat),
        grid=(bsz, nb),
        in_specs=[pl.BlockSpec((None, l_all, LANE), lambda b, c: (b, 0, base + c)),
                  pl.BlockSpec((None, 3, LANE), lambda b, c: (layer, 0, c)),
                  pl.BlockSpec((None, 1, LANE), lambda b, c: (layer, 0, c))],
        out_specs=pl.BlockSpec((None, l_all, LANE), lambda b, c: (b, 0, c)),
        out_shape=jax.ShapeDtypeStruct((bsz, l_all, MB_CONV_CH), F32),
        compiler_params=_params("arbitrary", "arbitrary"),
        name="mamba_conv",
    )(p, conv_w, conv_b3)


def _softplus(x):
    return jnp.maximum(x, 0.0) + jnp.log(1.0 + jnp.exp(-jnp.abs(x)))


def _mb_dt_kernel(p_ref, bias_ref, o_ref):
    dt = _softplus(p_ref[...] + bias_ref[...])
    o_ref[0] = dt
    o_ref[1] = pltpu.roll(dt, LANE - MB_HEADS, 1)


def mamba_dt(p, dt_bias_row, layer):
    bsz, l_all, _ = p.shape
    return pl.pallas_call(
        _mb_dt_kernel,
        grid=(bsz,),
        in_specs=[pl.BlockSpec((None, l_all, LANE), lambda b: (b, 0, 0)),
                  pl.BlockSpec((None, 1, LANE), lambda b: (layer, 0, 0))],
        out_specs=pl.BlockSpec((2, None, l_all, LANE), lambda b: (0, b, 0, 0)),
        out_shape=jax.ShapeDtypeStruct((2, bsz, l_all, LANE), F32),
        compiler_params=_params("arbitrary"),
        name="mamba_dt",
    )(p, dt_bias_row)


def _hy_filter_kernel(f_ref, aux_ref, w1_ref, b1_ref, w2_ref, b2_ref, w3_ref, fr_ref, dl_ref, o_ref, *,
                      channel_major):
    fr = fr_ref[...]
    hdn = jnp.sin(fr * (_dot_hi(f_ref[...], w1_ref[...]) + b1_ref[...]))
    hdn = jnp.sin(fr * (_dot_hi(hdn, w2_ref[...]) + b2_ref[...]))
    filt = _dot_hi(hdn, w3_ref[...])
    aux = aux_ref[...]
    t = aux[:, 0:1]
    is_fwd = aux[:, 1:2] > 0.5
    valid = aux[:, 2:3]
    window = jnp.exp(-t * dl_ref[...]) * valid
    k2 = jnp.where(is_fwd, filt[:, :HY_W], filt[:, HY_W:]) * window
    o_ref[...] = k2.T if channel_major else k2


def hyena_filter(feats2, aux, w1p, b1p, w2p, b2p, w3p, frp, deltas, layer, channel_major):
    rows = feats2.shape[0]
    tr = _pick(rows, (512, 256))

    def lspec(shape):
        return pl.BlockSpec((None,) + shape, lambda i: (layer, 0, 0))

    if channel_major:
        out_spec = pl.BlockSpec((HY_W, tr), lambda i: (0, i))
        out_shape = jax.ShapeDtypeStruct((HY_W, rows), F32)
    else:
        out_spec = pl.BlockSpec((tr, HY_W), lambda i: (i, 0))
        out_shape = jax.ShapeDtypeStruct((rows, HY_W), F32)
    return pl.pallas_call(
        functools.partial(_hy_filter_kernel, channel_major=channel_major),
        grid=(rows // tr,),
        in_specs=[pl.BlockSpec((tr, LANE), lambda i: (i, 0)),
                  pl.BlockSpec((tr, LANE), lambda i: (i, 0)),
                  lspec((LANE, LANE)), lspec((1, LANE)), lspec((LANE, LANE)), lspec((1, LANE)),
                  lspec((LANE, 2 * HY_W)), lspec((1, LANE)),
                  pl.BlockSpec((1, HY_W), lambda i: (0, 0))],
        out_specs=out_spec,
        out_shape=out_shape,
        compiler_params=_params("arbitrary"),
        name="hyena_filter",
    )(feats2, aux, w1p, b1p, w2p, b2p, w3p, frp, deltas)


DFT_MINOR = 256


def _snap(c):
    for v in (0.0, 1.0, -1.0):
        if abs(c - v) < 1e-12:
            return v
    return float(c)


def _lincomb(terms):
    acc = None
    for cf, tile in terms:
        if cf == 0.0:
            continue
        v = tile()
        if acc is None:
            acc = v if cf == 1.0 else (-v if cf == -1.0 else cf * v)
        elif cf == 1.0:
            acc = acc + v
        elif cf == -1.0:
            acc = acc - v
        else:
            acc = acc + cf * v
    return acc


def _dft_consts(n_seq):
    n = 2 * n_seq
    n2 = DFT_MINOR
    n1 = n // n2
    nkj = n1 // 2 + 1
    ang1 = 2.0 * np.pi * np.outer(np.arange(nkj), np.arange(n1)) / n1
    cos1 = [[_snap(v) for v in r] for r in np.cos(ang1)]
    sin1 = [[_snap(v) for v in r] for r in np.sin(ang1)]
    ang_t = 2.0 * np.pi * np.outer(np.arange(nkj), np.arange(n2)) / n
    rows = ((nkj + 7) // 8) * 8
    tw_c = np.zeros((rows, n2))
    tw_s = np.zeros((rows, n2))
    tw_c[:nkj] = np.cos(ang_t)
    tw_s[:nkj] = np.sin(ang_t)
    ang2 = 2.0 * np.pi * np.outer(np.arange(n2), np.arange(n2)) / n2
    c2, s2 = np.cos(ang2), np.sin(ang2)
    m_fwd = np.block([[c2, -s2], [s2, c2]])
    m_inv = np.block([[c2, s2], [-s2, c2]])
    as32 = lambda a: jnp.asarray(a, dtype=F32)

    def split(m):
        m32 = as32(m)
        hi = m32.astype(MXU_DT)
        lo = (m32 - hi.astype(F32)).astype(MXU_DT)
        return jnp.stack([hi, lo])

    return dict(n=n, n1=n1, nkj=nkj, cos1=cos1, sin1=sin1, tw_c=as32(tw_c), tw_s=as32(tw_s),
                m_fwd=split(m_fwd), m_inv=split(m_inv))


def _dot_split(a, m_ref):
    a_hi = a.astype(MXU_DT)
    a_lo = (a - a_hi.astype(F32)).astype(MXU_DT)
    return _dot(a_hi, m_ref[0]) + _dot(a_lo, m_ref[0]) + _dot(a_hi, m_ref[1])


def _outer_fwd(src_ref, z_ref, ct, n_in, cst):
    n2 = DFT_MINOR
    nkj, cos1, sin1 = cst["nkj"], cst["cos1"], cst["sin1"]

    def body(cb, carry):
        r0 = pl.multiple_of(cb * 8, 8)
        tile = lambda j: (lambda: src_ref[pl.ds(r0, 8), j * n2:(j + 1) * n2])
        for kj in range(nkj):
            re = _lincomb([(cos1[kj][j], tile(j)) for j in range(n_in)])
            im = _lincomb([(-sin1[kj][j], tile(j)) for j in range(n_in)])
            z_ref[pl.ds(kj * ct + r0, 8), 0:n2] = re
            z_ref[pl.ds(kj * ct + r0, 8), n2:2 * n2] = jnp.zeros_like(re) if im is None else im
        return carry

    lax.fori_loop(0, ct // 8, body, 0)


def _twiddle(z_ref, twc_ref, tws_ref, ct, nkj, inverse):
    n2 = DFT_MINOR
    for kj in range(1, nkj):
        re = z_ref[kj * ct:(kj + 1) * ct, 0:n2]
        im = z_ref[kj * ct:(kj + 1) * ct, n2:2 * n2]
        c = twc_ref[kj:kj + 1, :]
        s = tws_ref[kj:kj + 1, :]
        if inverse:
            s = -s
        z_ref[kj * ct:(kj + 1) * ct, 0:n2] = re * c + im * s
        z_ref[kj * ct:(kj + 1) * ct, n2:2 * n2] = im * c - re * s


def _spectrum_kernel(k_ref, twc_ref, tws_ref, mf_ref, h_ref, z_ref, *, ct, cst, n_chunks):
    nkj = cst["nkj"]
    _outer_fwd(k_ref, z_ref, ct, cst["n1"], cst)
    _twiddle(z_ref, twc_ref, tws_ref, ct, nkj, False)
    rc = nkj * ct // n_chunks

    def chunk(r, carry):
        rows = pl.ds(pl.multiple_of(r * rc, 8), rc)
        h_ref[rows, :] = _dot_split(z_ref[rows, :], mf_ref)
        return carry

    lax.fori_loop(0, n_chunks, chunk, 0)


def _long_conv_kernel(x_ref, h_ref, twc_ref, tws_ref, mf_ref, mi_ref, x0_ref, vxc_ref, yc_ref, skip_ref,
                      skip_row_ref, o_ref, z_ref, yt_ref, *, ct, cst, n_chunks, n_lat):
    n2 = DFT_MINOR
    nkj, n1, cos1, sin1 = cst["nkj"], cst["n1"], cst["cos1"], cst["sin1"]
    _outer_fwd(x_ref, z_ref, ct, n1 // 2, cst)
    _twiddle(z_ref, twc_ref, tws_ref, ct, nkj, False)
    rc = nkj * ct // n_chunks

    def chunk(r, carry):
        rows = pl.ds(pl.multiple_of(r * rc, 8), rc)
        x = _dot_split(z_ref[rows, :], mf_ref)
        xre, xim = x[:, :n2], x[:, n2:]
        hre, him = h_ref[rows, 0:n2], h_ref[rows, n2:2 * n2]
        y = jnp.concatenate([xre * hre - xim * him, xre * him + xim * hre], axis=1)
        z_ref[rows, :] = _dot_split(y, mi_ref)
        return carry

    lax.fori_loop(0, n_chunks, chunk, 0)
    _twiddle(z_ref, twc_ref, tws_ref, ct, nkj, True)

    inv_n = 1.0 / cst["n"]

    def body(cb, carry):
        r0 = pl.multiple_of(cb * 8, 8)
        g_re = lambda kj: (lambda: z_ref[pl.ds(kj * ct + r0, 8), 0:n2])
        g_im = lambda kj: (lambda: z_ref[pl.ds(kj * ct + r0, 8), n2:2 * n2])
        for j in range(n1 // 2):
            terms = []
            for kj in range(nkj):
                wgt = inv_n if kj in (0, nkj - 1) else 2.0 * inv_n
                terms.append((wgt * cos1[kj][j], g_re(kj)))
                terms.append((-wgt * sin1[kj][j], g_im(kj)))
            yt_ref[pl.ds(r0, 8), j * n2:(j + 1) * n2] = _lincomb(terms)
        return carry

    lax.fori_loop(0, ct // 8, body, 0)

    y = (yt_ref[...] + x_ref[...] * skip_ref[...]).T
    o_ref[0:n_lat, :] = (x0_ref[0:n_lat, :] * y).astype(o_ref.dtype)
    yc = yc_ref[...] + vxc_ref[...] * skip_row_ref[...]
    o_ref[n_lat:, :] = (x0_ref[n_lat:, :] * yc).astype(o_ref.dtype)


def filter_spectrum(k2t, cst):
    ch, n = k2t.shape
    ct = LANE
    nkj = cst["nkj"]
    n_chunks = _pick(nkj * ct // 8, (4, 5, 1))
    full = lambda a: pl.BlockSpec(a.shape, lambda c: (0,) * a.ndim)
    return pl.pallas_call(
        functools.partial(_spectrum_kernel, ct=ct, cst=cst, n_chunks=n_chunks),
        grid=(ch // ct,),
        in_specs=[pl.BlockSpec((ct, n), lambda c: (c, 0)),
                  full(cst["tw_c"]), full(cst["tw_s"]), full(cst["m_fwd"])],
        out_specs=pl.BlockSpec((None, nkj * ct, 2 * DFT_MINOR), lambda c: (c, 0, 0)),
        out_shape=jax.ShapeDtypeStruct((ch // ct, nkj * ct, 2 * DFT_MINOR), F32),
        scratch_shapes=[pltpu.VMEM((nkj * ct, 2 * DFT_MINOR), F32)],
        compiler_params=_params("arbitrary"),
        name="filter_spectrum",
    )(k2t, cst["tw_c"], cst["tw_s"], cst["m_fwd"])


def hyena_long_conv(vxt, h_spec, x0, vx_ctx, y_ctx, skip_col, skip_row, cst, layer, n_lat):
    bsz, ch, _ = vxt.shape
    l_all = x0.shape[1]
    n_ctx = l_all - n_lat
    ct = LANE
    nkj = cst["nkj"]
    n_chunks = _pick(nkj * ct // 8, (4, 5, 1))
    full = lambda a: pl.BlockSpec(a.shape, lambda b, c: (0,) * a.ndim)
    return pl.pallas_call(
        functools.partial(_long_conv_kernel, ct=ct, cst=cst, n_chunks=n_chunks, n_lat=n_lat),
        grid=(bsz, ch // ct),
        in_specs=[pl.BlockSpec((None, ct, n_lat), lambda b, c: (b, c, 0)),
                  pl.BlockSpec((None, nkj * ct, 2 * DFT_MINOR), lambda b, c: (c, 0, 0)),
                  full(cst["tw_c"]), full(cst["tw_s"]), full(cst["m_fwd"]), full(cst["m_inv"]),
                  pl.BlockSpec((None, l_all, ct), lambda b, c: (b, 0, c)),
                  pl.BlockSpec((None, n_ctx, ct), lambda b, c: (b, 0, c)),
                  pl.BlockSpec((None, n_ctx, ct), lambda b, c: (b, 0, c)),
                  pl.BlockSpec((None, ct, 1), lambda b, c: (layer, c, 0)),
                  pl.BlockSpec((None, 1, ct), lambda b, c: (layer, 0, c))],
        out_specs=pl.BlockSpec((None, l_all, ct), lambda b, c: (b, 0, c)),
        out_shape=jax.ShapeDtypeStruct((bsz, l_all, ch), MXU_DT),
        scratch_shapes=[pltpu.VMEM((nkj * ct, 2 * DFT_MINOR), F32), pltpu.VMEM((ct, n_lat), F32)],
        compiler_params=_params("arbitrary", "arbitrary"),
        name="hyena_long_conv",
    )(vxt, h_spec, cst["tw_c"], cst["tw_s"], cst["m_fwd"], cst["m_inv"], x0, vx_ctx, y_ctx, skip_col, skip_row)


def _ctx_conv_kernel(v_ref, k_ref, fd_ref, fk_ref, gi_ref, o_ref, *, kp):
    x = _dot_hi(fd_ref[...], v_ref[...])
    h = _dot_hi(fk_ref[...], k_ref[...])
    xre, xim = x[:kp], x[kp:]
    hre, him = h[:kp], h[kp:]
    y = jnp.concatenate([xre * hre - xim * him, xre * him + xim * hre], axis=0)
    o_ref[...] = _dot_hi(gi_ref[...], y)


def _ctx_tables(n_ctx):
    n = 2 * n_ctx
    kh = n_ctx + 1
    kp = ((kh + 7) // 8) * 8
    k = np.arange(kp)[:, None]
    live = (k < kh).astype(np.float64)
    ang = 2.0 * np.pi * k * np.arange(n)[None, :] / n
    ck, sk = np.cos(ang) * live, np.sin(ang) * live
    fk = np.concatenate([ck, -sk], axis=0)
    fd = fk[:, :n_ctx]
    wk = np.where((k == 0) | (k == n_ctx), 1.0, 2.0) * live / n
    gi = np.concatenate([(ck * wk).T, (-sk * wk).T], axis=1)[:n_ctx]
    as32 = lambda a: jnp.asarray(a, dtype=F32)
    return dict(kp=kp, fd=as32(fd), fk=as32(fk), gi=as32(gi))


def long_conv_ctx(vx_ctx, k2c, tb):
    bsz, n_ctx, ch = vx_ctx.shape
    kp = tb["kp"]
    full = lambda a: pl.BlockSpec(a.shape, lambda b, c: (0,) * a.ndim)
    return pl.pallas_call(
        functools.partial(_ctx_conv_kernel, kp=kp),
        grid=(bsz, ch // LANE),
        in_specs=[pl.BlockSpec((None, n_ctx, LANE), lambda b, c: (b, 0, c)),
                  pl.BlockSpec((2 * n_ctx, LANE), lambda b, c: (0, c)),
                  full(tb["fd"]), full(tb["fk"]), full(tb["gi"])],
        out_specs=pl.BlockSpec((None, n_ctx, LANE), lambda b, c: (b, 0, c)),
        out_shape=jax.ShapeDtypeStruct((bsz, n_ctx, ch), F32),
        compiler_params=_params("arbitrary", "arbitrary"),
        name="ctx_long_conv",
    )(vx_ctx, k2c, tb["fd"], tb["fk"], tb["gi"])


def _split3(a):
    a1 = a.astype(MXU_DT)
    r = a - a1.astype(F32)
    a2 = r.astype(MXU_DT)
    a3 = (r - a2.astype(F32)).astype(MXU_DT)
    return a1, a2, a3


def _ssd_chunk(backward, xs_ref, b_ref, c_ref, dt_ref, alog, ex, y_ref, state_ref):
    n = MB_CHUNK
    li = lax.broadcasted_iota(jnp.int32, (n, n), 0)
    si = lax.broadcasted_iota(jnp.int32, (n, n), 1)
    mask = (li <= si) if backward else (li >= si)
    tri = mask.astype(MXU_DT)
    lane = lax.broadcasted_iota(jnp.int32, (n, LANE), 1)

    dt = dt_ref[...]
    da = dt * (-jnp.exp(alog))
    cum = sum(_dot(tri, t) for t in _split3(da))
    cum_t = cum.T
    both = jnp.concatenate([cum, dt], axis=0)
    both_e = sum(_dot(t, ex) for t in _split3(both))
    cum_e = both_e[:n]
    dt_e = both_e[n:]
    total_e = cum_e[0:1] if backward else cum_e[n - 1:n]

    xdt = xs_ref[...] * dt_e
    xdt_b = xdt.astype(MXU_DT)
    w_b = (jnp.exp(total_e - cum_e) * xdt).astype(MXU_DT)
    st = state_ref[...]
    y_parts = []
    s_parts = []
    gw = MB_STATE
    hw = (MB_HEADS // MB_GROUPS) * MB_HEADDIM
    for g in range(MB_GROUPS):
        cg = c_ref[:, g * gw:(g + 1) * gw].astype(MXU_DT)
        bg = b_ref[:, g * gw:(g + 1) * gw]
        cb = _dot_nt(cg, bg.astype(MXU_DT))
        y_off = _dot(cg, st[:, g * hw:(g + 1) * hw].astype(MXU_DT))
        s_parts.append(_dot(bg.T.astype(MXU_DT), w_b[:, g * hw:(g + 1) * hw]))
        diag = []
        for j in range(hw // LANE):
            lo = g * hw + j * LANE
            xp = xdt_b[:, lo:lo + LANE]
            pair = []
            for e in range(2):
                hd = (lo // MB_HEADDIM) + e
                seg = cum[:, hd:hd + 1] - cum_t[hd:hd + 1, :]
                decay = jnp.exp(jnp.where(mask, seg, -jnp.inf))
                pair.append(_dot((cb * decay).astype(MXU_DT), xp))
            diag.append(jnp.where(lane < MB_HEADDIM, pair[0], pair[1]))
        y_parts.append(jnp.concatenate(diag, axis=1) + y_off * jnp.exp(cum_e[:, g * hw:(g + 1) * hw]))
    y_ref[...] = jnp.concatenate(y_parts, axis=1)
    state_ref[...] = st * jnp.exp(total_e) + jnp.concatenate(s_parts, axis=1)


def _ssd_kernel(xf_ref, bf_ref, cf_ref, dtf_ref, xb_ref, bb_ref, cb_ref, dtb_ref, alog_ref, ex_ref,
                yf_ref, yb_ref, state_ref):
    @pl.when(pl.program_id(1) == 0)
    def _():
        state_ref[...] = jnp.zeros_like(state_ref)

    ex = ex_ref[...]
    _ssd_chunk(False, xf_ref, bf_ref, cf_ref, dtf_ref, alog_ref[0], ex, yf_ref, state_ref.at[0])
    _ssd_chunk(True, xb_ref, bb_ref, cb_ref, dtb_ref, alog_ref[1], ex, yb_ref, state_ref.at[1])


def ssd_scan(xbc, dt2, a_log_rows, expand, layer, n_lat):
    bsz, l_all, _ = xbc.shape
    nc = l_all // MB_CHUNK
    ncl = n_lat // MB_CHUNK
    fwd = lambda c: (c + ncl) % nc
    bwd = lambda c: nc - 1 - c

    def specs(blk, direction):
        return [pl.BlockSpec((None, MB_CHUNK, MB_INNER), lambda b, c: (b, blk(c), 0)),
                pl.BlockSpec((None, MB_CHUNK, 256), lambda b, c: (b, blk(c), 2)),
                pl.BlockSpec((None, MB_CHUNK, 256), lambda b, c: (b, blk(c), 3)),
                pl.BlockSpec((None, None, MB_CHUNK, LANE), lambda b, c: (direction, b, blk(c), 0))]

    y_shape = jax.ShapeDtypeStruct((bsz, l_all, MB_INNER), F32)
    return pl.pallas_call(
        _ssd_kernel,
        grid=(bsz, nc),
        in_specs=specs(fwd, 0) + specs(bwd, 1)
        + [pl.BlockSpec((None, 2, 1, LANE), lambda b, c: (layer, 0, 0, 0)),
           pl.BlockSpec((LANE, MB_INNER), lambda b, c: (0, 0))],
        out_specs=[pl.BlockSpec((None, MB_CHUNK, MB_INNER), lambda b, c: (b, fwd(c), 0)),
                   pl.BlockSpec((None, MB_CHUNK, MB_INNER), lambda b, c: (b, bwd(c), 0))],
        out_shape=[y_shape, y_shape],
        scratch_shapes=[pltpu.VMEM((2, MB_STATE, MB_INNER), F32)],
        compiler_params=_params("arbitrary", "arbitrary"),
        name="ssd_scan",
    )(xbc, xbc, xbc, dt2, xbc, xbc, xbc, dt2, a_log_rows, expand)


def _mb_finish_kernel(yf_ref, yb_ref, xs_ref, z_ref, dsk_ref, g_ref, o_ref):
    y = yf_ref[...] + yb_ref[...] + xs_ref[...] * dsk_ref[...]
    y = y * _silu(z_ref[...].astype(F32))
    r = lax.rsqrt(jnp.mean(y * y, axis=-1, keepdims=True) + RMS_EPS)
    o_ref[...] = (y * r * g_ref[...]).astype(o_ref.dtype)


def mamba_finish(y_f, y_b, xbc, p, d_skip_e, norm_g3, layer):
    bsz, l_all, _ = xbc.shape
    tr = _pick(l_all, (544, 384, 128))
    w = MB_INNER
    return pl.pallas_call(
        _mb_finish_kernel,
        grid=(bsz, l_all // tr),
        in_specs=[pl.BlockSpec((None, tr, w), lambda b, i: (b, i, 0)),
                  pl.BlockSpec((None, tr, w), lambda b, i: (b, i, 0)),
                  pl.BlockSpec((None, tr, w), lambda b, i: (b, i, 0)),
                  pl.BlockSpec((None, tr, w), lambda b, i: (b, i, P_Z // w)),
                  pl.BlockSpec((None, 1, w), lambda b, i: (layer, 0, 0)),
                  pl.BlockSpec((None, 1, w), lambda b, i: (layer, 0, 0))],
        out_specs=pl.BlockSpec((None, tr, w), lambda b, i: (b, i, 0)),
        out_shape=jax.ShapeDtypeStruct((bsz, l_all, w), MXU_DT),
        compiler_params=_params("arbitrary", "arbitrary"),
        name="mamba_finish",
    )(y_f, y_b, xbc, p, d_skip_e, norm_g3)


def _deinterleave(w, heads):
    lead = w.shape[:-1]
    w = w.reshape(lead + (heads, GQA_HD // 2, 2))
    return jnp.concatenate([w[..., 0], w[..., 1]], axis=-1).reshape(lead + (heads * GQA_HD,))


def _prep_w_in(w_in):
    nl, d, _ = w_in.shape
    z64 = jnp.zeros((nl, d, 64), w_in.dtype)
    mla = w_in[..., :MLA_COLS]
    cq, ckv, kr = mla[..., :448], mla[..., 448:576], mla[..., 576:640]
    ev, od = kr[..., 0::2], kr[..., 1::2]
    mla_blk = jnp.concatenate([cq, z64, ckv, ev, od, z64, -od, ev, z64], axis=-1)
    gqa = w_in[..., MLA_COLS:MLA_COLS + GQA_COLS]
    gqa_blk = jnp.concatenate([_deinterleave(gqa[..., :512], GQA_HEADS),
                               _deinterleave(gqa[..., 512:768], GQA_KV_HEADS), gqa[..., 768:]], axis=-1)
    hy = w_in[..., MLA_COLS + GQA_COLS:MLA_COLS + GQA_COLS + HY_COLS]
    mb = w_in[..., MLA_COLS + GQA_COLS + HY_COLS:]
    z, xbc, dt = mb[..., :MB_INNER], mb[..., MB_INNER:MB_INNER + MB_CONV_CH], mb[..., MB_INNER + MB_CONV_CH:]
    dt_blk = jnp.concatenate([dt, jnp.zeros((nl, d, LANE - 2 * MB_HEADS), w_in.dtype)], axis=-1)
    out = jnp.concatenate([mla_blk, dt_blk, gqa_blk, z, hy, xbc], axis=-1)
    assert out.shape[-1] == P_COLS
    return out.astype(MXU_DT)


def _prep_w_uq(w_uq):
    nl = w_uq.shape[0]
    w = w_uq.reshape(nl, MLA_Q_RANK, MLA_HEADS, MLA_NOPE + MLA_ROPE)
    nope, rot = w[..., :MLA_NOPE], w[..., MLA_NOPE:]
    ev, od = rot[..., 0::2], rot[..., 1::2]
    z64 = jnp.zeros_like(rot)
    per_head = jnp.concatenate([nope, ev, od, z64, -od, ev, z64], axis=-1)
    w = per_head.reshape(nl, MLA_Q_RANK, MLA_HEADS * 384)
    w = jnp.concatenate([w, jnp.zeros((nl, 512 - MLA_Q_RANK, MLA_HEADS * 384), w.dtype)], axis=1)
    return w.astype(MXU_DT)


def _rope_tables(n_lat, n_ctx, rot_dim, sign_folded):
    rows = n_lat // GRID_W
    row = jnp.repeat(jnp.arange(rows, dtype=F32), GRID_W)
    col = jnp.tile(jnp.arange(GRID_W, dtype=F32), rows)
    n_freq = rot_dim // 4
    inv_freq = ROPE_THETA ** (-jnp.arange(n_freq, dtype=F32) / n_freq)
    ang = jnp.concatenate([row[:, None] * inv_freq, col[:, None] * inv_freq], axis=-1)
    half = rot_dim // 2
    cos = jnp.concatenate([jnp.cos(ang), jnp.ones((n_ctx, half), F32)], axis=0)
    sin = jnp.concatenate([jnp.sin(ang), jnp.zeros((n_ctx, half), F32)], axis=0)
    pad = jnp.zeros((n_lat + n_ctx, LANE - rot_dim), F32)
    cos_t = jnp.concatenate([cos, cos, pad], axis=-1)
    sin_t = jnp.concatenate([sin if sign_folded else -sin, sin, pad], axis=-1)
    return cos_t, sin_t


def _filter_inputs(n):
    t = jnp.linspace(0.0, 1.0, n, dtype=F32)[:, None]
    omega = 2.0 * math.pi * jnp.arange(n, dtype=F32) / n
    bands = jnp.linspace(1e-4, HY_BANDS - 1, HY_BANDS, dtype=F32)
    ang = omega[:, None] * bands[None, :]
    feats = jnp.concatenate([t, jnp.cos(ang), -jnp.sin(ang)], axis=-1)
    zero = jnp.zeros((1, HY_EMB), F32)
    feats2 = jnp.concatenate([feats, zero, feats[1:][::-1]], axis=0)
    feats2 = jnp.concatenate([feats2, jnp.zeros((2 * n, LANE - HY_EMB), F32)], axis=-1)
    t2 = jnp.concatenate([t, jnp.zeros((1, 1), F32), t[1:][::-1]], axis=0)
    idx = jnp.arange(2 * n)[:, None]
    aux = jnp.concatenate([t2, (idx < n).astype(F32), (idx != n).astype(F32),
                           jnp.zeros((2 * n, LANE - 3), F32)], axis=-1)
    return feats2, aux


def _pad_to(a, shape):
    pads = [(0, s - d) for d, s in zip(a.shape, shape)]
    return jnp.pad(a, pads)


def kernel(x, c, ctx, c_ctx, w_ada, b_ada, w_in, mla_q_norm, mla_kv_norm, mla_w_uq, mla_w_ukv, gqa_q_norm, gqa_k_norm, hy_conv_w, hy_conv_b, hy_w1, hy_b1, hy_w2, hy_b2, hy_w3, hy_freq, hy_skip, mb_conv_w, mb_conv_b, mb_a_log, mb_dt_bias, mb_d, mb_norm, w_mgate, b_mgate, w_branch, w_out, ln1_g, ln1_b, w_ffn_in, w_ffn_out, ln2_g, ln2_b):
    bsz, n_lat, d = x.shape
    n_ctx = ctx.shape[1]
    nl = w_in.shape[0]
    assert d == D_MODEL and bsz < 8 and n_lat % n_ctx == 0 and n_ctx % MB_CHUNK == 0

    w_in_p = _prep_w_in(w_in)
    w_uq_p = _prep_w_uq(mla_w_uq)
    w_ukv_p = mla_w_ukv.astype(MXU_DT)
    gq_mla = _pad_to(mla_q_norm, (nl, 512))[:, None, :]
    gkv_mla = mla_kv_norm[:, None, :]
    gq_gqa = _deinterleave(gqa_q_norm, 1)[:, None, :]
    gk_gqa = _deinterleave(gqa_k_norm, 1)[:, None, :]
    w_gate_b = w_mgate.astype(MXU_DT)
    w_branch_b = w_branch.astype(MXU_DT)
    w_out_b = w_out.astype(MXU_DT)
    w_ffn_in_b = w_ffn_in.astype(MXU_DT)
    w_ffn_out_b = w_ffn_out.astype(MXU_DT)
    b_gate4 = b_mgate[:, :, None, :]
    b_ada3 = b_ada[:, None, :]
    hy_conv_b3 = hy_conv_b[:, None, :]
    mb_conv_b3 = mb_conv_b[:, None, :]
    hy_skip3 = hy_skip[:, None, :]
    hy_skip_col = hy_skip[:, :, None]
    ln1_g3, ln1_b3, ln2_g3, ln2_b3 = (a[:, None, :] for a in (ln1_g, ln1_b, ln2_g, ln2_b))
    hy_w1p = _pad_to(hy_w1, (nl, LANE, LANE))
    hy_b1p = _pad_to(hy_b1, (nl, LANE))[:, None, :]
    hy_w2p = _pad_to(hy_w2, (nl, LANE, LANE))
    hy_b2p = _pad_to(hy_b2, (nl, LANE))[:, None, :]
    hy_w3p = _pad_to(hy_w3, (nl, LANE, 2 * HY_W))
    hy_frp = _pad_to(hy_freq, (nl, LANE))[:, None, :]
    deltas = jnp.abs(jnp.linspace(HY_MIN_DECAY, HY_MAX_DECAY, HY_W, dtype=F32))[None, :]
    dt_bias_row = _pad_to(mb_dt_bias.reshape(nl, 2 * MB_HEADS), (nl, LANE))[:, None, :]
    a_log_rows = _pad_to(mb_a_log, (nl, 2, LANE))[:, :, None, :]
    d_skip_e = jnp.repeat(mb_d, MB_HEADDIM, axis=-1)[:, None, :]
    mb_norm3 = mb_norm[:, None, :]
    expand = jnp.asarray(np.kron(np.eye(LANE, MB_HEADS), np.ones((1, MB_HEADDIM))), dtype=MXU_DT)

    cos_m, sin_m = _rope_tables(n_lat, n_ctx, MLA_ROPE, True)
    cos_g, sin_g = _rope_tables(n_lat, n_ctx, GQA_HD, False)
    feats_lat, aux_lat = _filter_inputs(n_lat)
    feats_ctx, aux_ctx = _filter_inputs(n_ctx)
    dft_lat = _dft_consts(n_lat)
    tb_ctx = _ctx_tables(n_ctx)

    c8 = jnp.concatenate([c, c_ctx[None, :], jnp.zeros((8 - bsz - 1, d), F32)], axis=0)
    xz = jnp.concatenate([x, ctx], axis=1)

    for l in range(nl):
        last = l == nl - 1
        mod = ada_mod(c8, w_ada, b_ada3, l).reshape(8, 1, 6 * d)
        p, p_dt, h = in_projection(xz, mod, w_in_p, l, n_lat)

        q, k, v = mla_prep(p, gq_mla, gkv_mla, w_uq_p, w_ukv_p, cos_m, sin_m, l)
        oa = attention(q, k, v, n_lat)
        q, k, v = gqa_prep(p, gq_gqa, gk_gqa, cos_g, sin_g, l)
        ob = attention(q, k, v, n_lat)
        x0, vxt, vx_ctx = hyena_conv(p, hy_conv_w, hy_conv_b3, l, n_lat)
        k2t = hyena_filter(feats_lat, aux_lat, hy_w1p, hy_b1p, hy_w2p, hy_b2p, hy_w3p, hy_frp, deltas, l, True)
        h_spec = filter_spectrum(k2t, dft_lat)
        if last:
            y_ctx = jnp.zeros((bsz, n_ctx, HY_W), F32)
        else:
            k2c = hyena_filter(feats_ctx, aux_ctx, hy_w1p, hy_b1p, hy_w2p, hy_b2p, hy_w3p, hy_frp, deltas, l,
                               False)
            y_ctx = long_conv_ctx(vx_ctx, k2c, tb_ctx)
        oc = hyena_long_conv(vxt, h_spec, x0, vx_ctx, y_ctx, hy_skip_col, hy_skip3, dft_lat, l, n_lat)
        xbc = mamba_conv(p, mb_conv_w, mb_conv_b3, l, n_lat)
        dt2 = mamba_dt(p_dt, dt_bias_row, l)
        y_f, y_b = ssd_scan(xbc, dt2, a_log_rows, expand, l, n_lat)
        od = mamba_finish(y_f, y_b, xbc, p, d_skip_e, mb_norm3, l)

        acc = merge_branches(h, (oa, ob, oc, od), w_gate_b, b_gate4, w_branch_b, l)
        x1 = matmul_res_ln(acc, w_out_b, xz, mod, 2, ln1_g3, ln1_b3, l, n_lat, n_lat + n_ctx,
                           (544, 384, 128), d)
        act = ffn_in(x1, mod, w_ffn_in_b, l, n_lat)
        out_rows = n_lat if last else n_lat + n_ctx
        xz = matmul_res_ln(act, w_ffn_out_b, x1, mod, 5, ln2_g3, ln2_b3, l, n_lat, out_rows,
                           (1088, 384, 128), 512)
    return xz
```

```python
import functools
import math

import jax
import jax.numpy as jnp
import numpy as np
from jax import lax
from jax.experimental import pallas as pl
from jax.experimental.pallas import tpu as pltpu

F32 = jnp.float32
MXU_DT = jnp.bfloat16

D_MODEL = 2048
DEPTH = 2
GRID_W = 64
N_BRANCH = 4
BRANCH_W = D_MODEL // N_BRANCH
ROPE_THETA = 10000.0
LN_EPS = 1e-6
RMS_EPS = 1e-6
DEEPNORM_ALPHA = (2 * DEPTH) ** 0.25

MLA_HEADS = 4
MLA_Q_RANK = 448
MLA_KV_RANK = 128
MLA_NOPE = 128
MLA_ROPE = 64
MLA_V = 128
MLA_COLS = MLA_Q_RANK + MLA_KV_RANK + MLA_ROPE

GQA_HEADS = 4
GQA_KV_HEADS = 2
GQA_HD = 128
GQA_COLS = (GQA_HEADS + 2 * GQA_KV_HEADS) * GQA_HD

HY_W = BRANCH_W
HY_EMB = 33
HY_BANDS = (HY_EMB - 1) // 2
HY_FFN = 64
HY_MIN_DECAY = math.log(1e-2) / 1.5
HY_MAX_DECAY = math.log(1e-2) / 0.3
HY_COLS = 3 * HY_W

MB_INNER = BRANCH_W
MB_HEADDIM = 64
MB_HEADS = 8
MB_GROUPS = 2
MB_STATE = 128
MB_CHUNK = 128
MB_CONV_CH = MB_INNER + 2 * MB_GROUPS * MB_STATE
MB_COLS = MB_INNER + MB_CONV_CH + 2 * MB_HEADS

FFN_HIDDEN = 5632

LANE = 128
VMEM_LIMIT = 56 * 1024 * 1024

P_MLA = 0
P_MLA_W = 896
P_DT = 896
P_GQA = 1024
P_Z = 2048
P_HY = 2560
P_XBC = 4096
P_COLS = 5120

LOG2_E = math.log2(math.e)
_HI = lax.Precision.HIGHEST


def _dot(a, b):
    return jnp.dot(a, b, preferred_element_type=F32)


def _dot_hi(a, b):
    return jnp.dot(a, b, precision=_HI, preferred_element_type=F32)


def _dot_nt(a, b):
    return lax.dot_general(a, b, (((1,), (1,)), ((), ())), preferred_element_type=F32)


def _sigmoid(x):
    return 1.0 / (1.0 + jnp.exp(-x))


def _silu(x):
    return x * _sigmoid(x)


def _params(*sem):
    return pltpu.CompilerParams(dimension_semantics=sem, vmem_limit_bytes=VMEM_LIMIT)


def _pick(n, prefs):
    for p in prefs:
        if n % p == 0:
            return p
    raise ValueError(f"no tile for {n} in {prefs}")


def _standardize(x):
    mu = jnp.mean(x, axis=-1, keepdims=True)
    xc = x - mu
    var = jnp.mean(xc * xc, axis=-1, keepdims=True)
    return xc * lax.rsqrt(var + LN_EPS)


def _is_ctx_rows(tile_idx, tm, n_lat):
    row = tile_idx * tm + lax.broadcasted_iota(jnp.int32, (tm, 1), 0)
    return row >= n_lat


def _ada_kernel(c_ref, w_ref, b_ref, o_ref):
    cs = _silu(c_ref[...])
    o_ref[...] = _dot(cs.astype(MXU_DT), w_ref[...].astype(MXU_DT)) + b_ref[...]


def ada_mod(c8, w_ada, b_ada3, layer):
    d = c8.shape[1]
    n = w_ada.shape[2]
    tn = 1024
    return pl.pallas_call(
        _ada_kernel,
        grid=(n // tn,),
        in_specs=[pl.BlockSpec((8, d), lambda j: (0, 0)),
                  pl.BlockSpec((None, d, tn), lambda j: (layer, 0, j)),
                  pl.BlockSpec((None, 1, tn), lambda j: (layer, 0, j))],
        out_specs=pl.BlockSpec((8, tn), lambda j: (0, j)),
        out_shape=jax.ShapeDtypeStruct((8, n), F32),
        compiler_params=_params("arbitrary"),
        name="ada_mod",
    )(c8, w_ada, b_ada3)


ROW_CHUNKS = 4


def _modulated(x, sh_ref, sc_ref, shc_ref, scc_ref, row0, n_lat):
    xn = _standardize(x)
    row = row0 + lax.broadcasted_iota(jnp.int32, (x.shape[0], 1), 0)
    is_ctx = row >= n_lat
    scale = jnp.where(is_ctx, scc_ref[...], sc_ref[...])
    shift = jnp.where(is_ctx, shc_ref[...], sh_ref[...])
    return (xn * (1.0 + scale) + shift).astype(MXU_DT)


def _inproj_kernel(x_ref, sh_ref, sc_ref, shc_ref, scc_ref, w_ref, p_ref, pdt_ref, h_ref, *, tm, n_lat):
    j = pl.program_id(2)
    rc = tm // ROW_CHUNKS

    @pl.when(j == 0)
    def _():
        for r in range(ROW_CHUNKS):
            rows = slice(r * rc, (r + 1) * rc)
            hb = _modulated(x_ref[rows, :], sh_ref, sc_ref, shc_ref, scc_ref,
                            pl.program_id(1) * tm + r * rc, n_lat)
            h_ref[rows, :] = hb
            acc = _dot(hb, w_ref[...])
            p_ref[rows, :] = acc.astype(p_ref.dtype)
            pdt_ref[rows, :] = acc[:, P_DT:P_DT + LANE]

    @pl.when(j > 0)
    def _():
        p_ref[...] = _dot(h_ref[...], w_ref[...]).astype(p_ref.dtype)


def _ffn_kernel(x_ref, sh_ref, sc_ref, shc_ref, scc_ref, g_ref, gc_ref, lng_ref, lnb_ref,
                wu_ref, wg_ref, wd_ref, o_ref, h_ref, *, tm, n_lat, nj):
    i = pl.program_id(1)
    j = pl.program_id(2)
    n_ch = 4 if tm % 64 == 0 else 2
    rc = tm // n_ch
    chunks = [(r * rc, slice(r * rc, (r + 1) * rc)) for r in range(n_ch)]

    def down(h):
        up = _dot(h, wu_ref[...])
        gate = _dot(h, wg_ref[...])
        return _dot((_silu(gate) * up).astype(MXU_DT), wd_ref[...])

    @pl.when(j == 0)
    def _():
        for r0, rows in chunks:
            hb = _modulated(x_ref[rows, :], sh_ref, sc_ref, shc_ref, scc_ref, i * tm + r0, n_lat)
            h_ref[rows, :] = hb
            o_ref[rows, :] = down(hb)

    @pl.when((j > 0) & (j < nj - 1))
    def _():
        for r0, rows in chunks:
            o_ref[rows, :] += down(h_ref[rows, :])

    @pl.when(j == nj - 1)
    def _():
        for r0, rows in chunks:
            f = o_ref[rows, :] + down(h_ref[rows, :])
            row = i * tm + r0 + lax.broadcasted_iota(jnp.int32, (rc, 1), 0)
            gate = jnp.where(row >= n_lat, gc_ref[...], g_ref[...])
            y = DEEPNORM_ALPHA * x_ref[rows, :] + gate * f
            o_ref[rows, :] = _standardize(y) * lng_ref[...] + lnb_ref[...]


def _mod_specs(k_shift, k_scale, n_batch):
    d = D_MODEL
    return [pl.BlockSpec((None, 1, d), lambda b, i, j: (b, 0, k_shift)),
            pl.BlockSpec((None, 1, d), lambda b, i, j: (b, 0, k_scale)),
            pl.BlockSpec((None, 1, d), lambda b, i, j: (n_batch, 0, k_shift)),
            pl.BlockSpec((None, 1, d), lambda b, i, j: (n_batch, 0, k_scale))]


def in_projection(xz, mod, w_in_p, layer, n_lat):
    bsz, l_all, d = xz.shape
    n = w_in_p.shape[2]
    tm = _pick(l_all, (1088, 544, 384, 128))
    tn = 1024
    assert n % tn == 0 and P_DT + LANE <= tn
    kern = functools.partial(_inproj_kernel, tm=tm, n_lat=n_lat)
    return pl.pallas_call(
        kern,
        grid=(bsz, l_all // tm, n // tn),
        in_specs=[pl.BlockSpec((None, tm, d), lambda b, i, j: (b, i, 0))]
        + _mod_specs(0, 1, bsz)
        + [pl.BlockSpec((None, d, tn), lambda b, i, j: (layer, 0, j))],
        out_specs=[pl.BlockSpec((None, tm, tn), lambda b, i, j: (b, i, j)),
                   pl.BlockSpec((None, tm, LANE), lambda b, i, j: (b, i, 0)),
                   pl.BlockSpec((None, tm, d), lambda b, i, j: (b, i, 0))],
        out_shape=[jax.ShapeDtypeStruct((bsz, l_all, n), MXU_DT),
                   jax.ShapeDtypeStruct((bsz, l_all, LANE), F32),
                   jax.ShapeDtypeStruct((bsz, l_all, d), MXU_DT)],
        compiler_params=_params("arbitrary", "arbitrary", "arbitrary"),
        name="in_projection",
    )(xz, mod, mod, mod, mod, w_in_p)


def ffn_block(x1, mod, w_ffn_in, w_ffn_out, ln_g, ln_b, layer, n_lat, out_rows):
    bsz, l_all, d = x1.shape
    hid = w_ffn_in.shape[2] // 2
    tm = _pick(l_all, (1088, 384, 128))
    tn = 512
    nj = hid // tn
    kern = functools.partial(_ffn_kernel, tm=tm, n_lat=n_lat, nj=nj)
    vec = lambda sel: pl.BlockSpec((None, 1, d), sel)
    return pl.pallas_call(
        kern,
        grid=(bsz, pl.cdiv(out_rows, tm), nj),
        in_specs=[pl.BlockSpec((None, tm, d), lambda b, i, j: (b, i, 0))]
        + _mod_specs(3, 4, bsz)
        + [vec(lambda b, i, j: (b, 0, 5)), vec(lambda b, i, j: (bsz, 0, 5)),
           vec(lambda b, i, j: (layer, 0, 0)), vec(lambda b, i, j: (layer, 0, 0)),
           pl.BlockSpec((None, d, tn), lambda b, i, j: (layer, 0, j)),
           pl.BlockSpec((None, d, tn), lambda b, i, j: (layer, 0, j + nj)),
           pl.BlockSpec((None, tn, d), lambda b, i, j: (layer, j, 0))],
        out_specs=pl.BlockSpec((None, tm, d), lambda b, i, j: (b, i, 0), pipeline_mode=pl.Buffered(1)),
        out_shape=jax.ShapeDtypeStruct((bsz, out_rows, d), F32),
        scratch_shapes=[pltpu.VMEM((tm, d), MXU_DT)],
        compiler_params=_params("arbitrary", "arbitrary", "arbitrary"),
        name="ffn_block",
    )(x1, mod, mod, mod, mod, mod, mod, ln_g, ln_b, w_ffn_in, w_ffn_in, w_ffn_out)


def _resln_kernel(a_ref, w_ref, res_ref, g_ref, gc_ref, lng_ref, lnb_ref, o_ref, *, tm, n_lat, nk, rc):
    k = pl.program_id(2)
    i = pl.program_id(1)

    if nk > 1:
        @pl.when(k == 0)
        def _():
            o_ref[...] = _dot(a_ref[...], w_ref[...])

        @pl.when((k > 0) & (k < nk - 1))
        def _():
            o_ref[...] += _dot(a_ref[...], w_ref[...])

    @pl.when(k == nk - 1)
    def _():
        for r in range(tm // rc):
            rows = slice(r * rc, (r + 1) * rc)
            acc = _dot(a_ref[rows, :], w_ref[...])
            if nk > 1:
                acc = acc + o_ref[rows, :]
            row = i * tm + r * rc + lax.broadcasted_iota(jnp.int32, (rc, 1), 0)
            gate = jnp.where(row >= n_lat, gc_ref[...], g_ref[...])
            y = DEEPNORM_ALPHA * res_ref[rows, :] + gate * acc
            o_ref[rows, :] = _standardize(y) * lng_ref[...] + lnb_ref[...]


def matmul_res_ln(a, w, res, mod, k_gate, ln_g, ln_b, layer, n_lat, out_rows, tm_prefs, tk):
    bsz, l_all, kdim = a.shape
    d = w.shape[2]
    tm = _pick(l_all, tm_prefs)
    nk = kdim // tk
    rc = _pick(tm, (272, 192, 128))
    kern = functools.partial(_resln_kernel, tm=tm, n_lat=n_lat, nk=nk, rc=rc)
    return pl.pallas_call(
        kern,
        grid=(bsz, pl.cdiv(out_rows, tm), nk),
        in_specs=[pl.BlockSpec((None, tm, tk), lambda b, i, k: (b, i, k)),
                  pl.BlockSpec((None, tk, d), lambda b, i, k: (layer, k, 0)),
                  pl.BlockSpec((None, tm, d), lambda b, i, k: (b, i, 0)),
                  pl.BlockSpec((None, 1, d), lambda b, i, k: (b, 0, k_gate)),
                  pl.BlockSpec((None, 1, d), lambda b, i, k: (bsz, 0, k_gate)),
                  pl.BlockSpec((None, 1, d), lambda b, i, k: (layer, 0, 0)),
                  pl.BlockSpec((None, 1, d), lambda b, i, k: (layer, 0, 0))],
        out_specs=pl.BlockSpec((None, tm, d), lambda b, i, k: (b, i, 0)),
        out_shape=jax.ShapeDtypeStruct((bsz, out_rows, d), F32),
        compiler_params=_params("arbitrary", "arbitrary", "arbitrary"),
        name="matmul_res_ln",
    )(a, w, res, mod, mod, ln_g, ln_b)


def _merge_kernel(h_ref, oa_ref, ob_ref, oc_ref, od_ref, wg_ref, bg_ref, wb_ref, o_ref):
    h = h_ref[...]
    acc = None
    for i, o_r in enumerate((oa_ref, ob_ref, oc_ref, od_ref)):
        g = _dot(h, wg_ref[i]) + bg_ref[i]
        t = _dot(o_r[...], wb_ref[i])
        term = _sigmoid(g) * t
        acc = term if acc is None else acc + term
    o_ref[...] = acc.astype(o_ref.dtype)


def merge_branches(h, outs, w_gate, b_gate4, w_branch, layer):
    bsz, l_all, d = h.shape
    bw = outs[0].shape[2]
    tm = _pick(l_all, (1088, 544, 384, 128))
    tn = 512
    o_spec = pl.BlockSpec((None, tm, bw), lambda b, i, j: (b, i, 0))
    return pl.pallas_call(
        _merge_kernel,
        grid=(bsz, l_all // tm, d // tn),
        in_specs=[pl.BlockSpec((None, tm, d), lambda b, i, j: (b, i, 0)), o_spec, o_spec, o_spec, o_spec,
                  pl.BlockSpec((None, N_BRANCH, d, tn), lambda b, i, j: (layer, 0, 0, j)),
                  pl.BlockSpec((None, N_BRANCH, 1, tn), lambda b, i, j: (layer, 0, 0, j)),
                  pl.BlockSpec((None, N_BRANCH, bw, tn), lambda b, i, j: (layer, 0, 0, j))],
        out_specs=pl.BlockSpec((None, tm, tn), lambda b, i, j: (b, i, j)),
        out_shape=jax.ShapeDtypeStruct((bsz, l_all, d), MXU_DT),
        compiler_params=_params("arbitrary", "arbitrary", "arbitrary"),
        name="merge_branches",
    )(h, *outs, w_gate, b_gate4, w_branch)


def _mla_prep_kernel(p_ref, gq_ref, gkv_ref, wuq_ref, wukv_ref, cos_ref, sin_ref, q_ref, k_ref, v_ref):
    scale = (MLA_NOPE + MLA_ROPE) ** -0.5 * LOG2_E
    p = p_ref[...].astype(F32)
    cq = p[:, 0:512]
    ckv = p[:, 512:640]
    ka = p[:, 640:768]
    kb = p[:, 768:896]
    rq = lax.rsqrt(jnp.sum(cq * cq, axis=-1, keepdims=True) * (1.0 / MLA_Q_RANK) + RMS_EPS)
    cqn = (cq * rq * gq_ref[...]).astype(MXU_DT)
    rkv = lax.rsqrt(jnp.mean(ckv * ckv, axis=-1, keepdims=True) + RMS_EPS)
    ckvn = (ckv * rkv * gkv_ref[...]).astype(MXU_DT)
    qf = _dot(cqn, wuq_ref[...])
    kvf = _dot(ckvn, wukv_ref[...])
    cos = cos_ref[...]
    sin = sin_ref[...]
    k_rope = (ka * cos + kb * sin).astype(k_ref.dtype)
    for hd in range(MLA_HEADS):
        qb = hd * 384
        q_ref[hd, :, 0:128] = (qf[:, qb:qb + 128] * scale).astype(q_ref.dtype)
        q_rope = qf[:, qb + 128:qb + 256] * cos + qf[:, qb + 256:qb + 384] * sin
        q_ref[hd, :, 128:256] = (q_rope * scale).astype(q_ref.dtype)
        k_ref[hd, :, 0:128] = kvf[:, hd * 256:hd * 256 + 128].astype(k_ref.dtype)
        k_ref[hd, :, 128:256] = k_rope
        v_ref[hd] = kvf[:, hd * 256 + 128:hd * 256 + 256].T.astype(v_ref.dtype)


def mla_prep(p, gq, gkv, wuq, wukv, cos, sin, layer):
    bsz, l_all, _ = p.shape
    tr = _pick(l_all, (256, 128))
    h = MLA_HEADS
    return pl.pallas_call(
        _mla_prep_kernel,
        grid=(bsz, l_all // tr),
        in_specs=[pl.BlockSpec((None, tr, P_MLA_W), lambda b, i: (b, i, 0)),
                  pl.BlockSpec((None, 1, 512), lambda b, i: (layer, 0, 0)),
                  pl.BlockSpec((None, 1, 128), lambda b, i: (layer, 0, 0)),
                  pl.BlockSpec((None, 512, h * 384), lambda b, i: (layer, 0, 0)),
                  pl.BlockSpec((None, 128, h * 256), lambda b, i: (layer, 0, 0)),
                  pl.BlockSpec((tr, LANE), lambda b, i: (i, 0)),
                  pl.BlockSpec((tr, LANE), lambda b, i: (i, 0))],
        out_specs=[pl.BlockSpec((None, h, tr, 256), lambda b, i: (b, 0, i, 0)),
                   pl.BlockSpec((None, h, tr, 256), lambda b, i: (b, 0, i, 0)),
                   pl.BlockSpec((None, h, 128, tr), lambda b, i: (b, 0, 0, i))],
        out_shape=[jax.ShapeDtypeStruct((bsz, h, l_all, 256), MXU_DT),
                   jax.ShapeDtypeStruct((bsz, h, l_all, 256), MXU_DT),
                   jax.ShapeDtypeStruct((bsz, h, 128, l_all), MXU_DT)],
        compiler_params=_params("arbitrary", "arbitrary"),
        name="mla_prep",
    )(p, gq, gkv, wuq, wukv, cos, sin)


def _gqa_prep_kernel(p_ref, gq_ref, gk_ref, cos_ref, sin_ref, q_ref, k_ref, v_ref):
    scale = GQA_HD ** -0.5 * LOG2_E
    cos = cos_ref[...]
    sin = sin_ref[...]

    def norm_rope(x, g):
        r = lax.rsqrt(jnp.mean(x * x, axis=-1, keepdims=True) + RMS_EPS)
        xn = x * r * g
        return xn * cos + pltpu.roll(xn, GQA_HD // 2, 1) * sin

    for hd in range(GQA_HEADS):
        x = p_ref[:, hd * 128:(hd + 1) * 128].astype(F32)
        q_ref[hd] = (norm_rope(x, gq_ref[...]) * scale).astype(q_ref.dtype)
    for hd in range(GQA_KV_HEADS):
        x = p_ref[:, 512 + hd * 128:512 + (hd + 1) * 128].astype(F32)
        k_ref[hd] = norm_rope(x, gk_ref[...]).astype(k_ref.dtype)
        v = p_ref[:, 768 + hd * 128:768 + (hd + 1) * 128].astype(F32)
        v_ref[hd] = v.T.astype(v_ref.dtype)


def gqa_prep(p, gq, gk, cos, sin, layer):
    bsz, l_all, _ = p.shape
    tr = _pick(l_all, (1088 * 2, 256, 128))
    return pl.pallas_call(
        _gqa_prep_kernel,
        grid=(bsz, l_all // tr),
        in_specs=[pl.BlockSpec((None, tr, GQA_COLS), lambda b, i: (b, i, P_GQA // GQA_COLS)),
                  pl.BlockSpec((None, 1, 128), lambda b, i: (layer, 0, 0)),
                  pl.BlockSpec((None, 1, 128), lambda b, i: (layer, 0, 0)),
                  pl.BlockSpec((tr, LANE), lambda b, i: (i, 0)),
                  pl.BlockSpec((tr, LANE), lambda b, i: (i, 0))],
        out_specs=[pl.BlockSpec((None, GQA_HEADS, tr, 128), lambda b, i: (b, 0, i, 0)),
                   pl.BlockSpec((None, GQA_KV_HEADS, tr, 128), lambda b, i: (b, 0, i, 0)),
                   pl.BlockSpec((None, GQA_KV_HEADS, 128, tr), lambda b, i: (b, 0, 0, i))],
        out_shape=[jax.ShapeDtypeStruct((bsz, GQA_HEADS, l_all, 128), MXU_DT),
                   jax.ShapeDtypeStruct((bsz, GQA_KV_HEADS, l_all, 128), MXU_DT),
                   jax.ShapeDtypeStruct((bsz, GQA_KV_HEADS, 128, l_all), MXU_DT)],
        compiler_params=_params("arbitrary", "arbitrary"),
        name="gqa_prep",
    )(p, gq, gk, cos, sin)


def _attn_kernel(q_ref, k_ref, v_ref, o_ref, *, n_lat, n_lat_tiles, sub):
    i = pl.program_id(2)

    def attend_all(k, vt):
        groups = [slice(r, r + sub) for r in range(0, q_ref.shape[0], sub)]
        sts = [_dot_nt(k, q_ref[g, :]) for g in groups]
        es, ls = [], []
        for st in sts:
            m = jnp.max(st, axis=0, keepdims=True)
            e = jnp.exp2(st - m)
            ls.append(jnp.sum(e, axis=0, keepdims=True))
            es.append(e.astype(vt.dtype))
        for g, e, l in zip(groups, es, ls):
            ot = _dot(vt, e) / l
            o_ref[g, :] = ot.T.astype(o_ref.dtype)

    @pl.when(i < n_lat_tiles)
    def _():
        attend_all(k_ref[...], v_ref[...])

    @pl.when(i >= n_lat_tiles)
    def _():
        attend_all(k_ref[n_lat:, :], v_ref[:, n_lat:])


def attention(q, k, v, n_lat):
    bsz, h, l_all, dk = q.shape
    hkv = k.shape[1]
    grp = h // hkv
    dv = v.shape[2]
    tq = _pick(n_lat, (1024, 512, 256, 128))
    assert l_all - n_lat <= tq
    kern = functools.partial(_attn_kernel, n_lat=n_lat, n_lat_tiles=n_lat // tq, sub=min(256, tq))
    return pl.pallas_call(
        kern,
        grid=(bsz, h, pl.cdiv(l_all, tq)),
        in_specs=[pl.BlockSpec((None, None, tq, dk), lambda b, hh, i: (b, hh, i, 0)),
                  pl.BlockSpec((None, None, l_all, dk), lambda b, hh, i: (b, hh // grp, 0, 0)),
                  pl.BlockSpec((None, None, dv, l_all), lambda b, hh, i: (b, hh // grp, 0, 0))],
        out_specs=pl.BlockSpec((None, tq, dv), lambda b, hh, i: (b, i, hh)),
        out_shape=jax.ShapeDtypeStruct((bsz, l_all, h * dv), MXU_DT),
        compiler_params=_params("arbitrary", "arbitrary", "arbitrary"),
        name="attention",
    )(q, k, v)


def _conv3(u, w, b, n_lat):
    u = u.astype(F32)
    n = u.shape[0]
    row = lax.broadcasted_iota(jnp.int32, (n, 1), 0)
    prev = jnp.where((row == 0) | (row == n_lat), 0.0, pltpu.roll(u, 1, 0))
    nxt = jnp.where((row == n_lat - 1) | (row == n - 1), 0.0, pltpu.roll(u, n - 1, 0))
    return w[0:1, :] * prev + w[1:2, :] * u + w[2:3, :] * nxt + b


def _hy_conv_kernel(p0_ref, p1_ref, pv_ref, w0_ref, w1_ref, wv_ref, b0_ref, b1_ref, bv_ref,
                    x0_ref, vxt_ref, vxc_ref, *, n_lat):
    x0_ref[...] = _conv3(p0_ref[...], w0_ref[...], b0_ref[...], n_lat)
    x1 = _conv3(p1_ref[...], w1_ref[...], b1_ref[...], n_lat)
    v = _conv3(pv_ref[...], wv_ref[...], bv_ref[...], n_lat)
    vx = v * x1
    vxt_ref[...] = vx[:n_lat, :].T
    vxc_ref[...] = vx[n_lat:, :]


def hyena_conv(p, conv_w, conv_b3, layer, n_lat):
    bsz, l_all, _ = p.shape
    nb = HY_W // LANE
    base = P_HY // LANE

    def pspec(off):
        return pl.BlockSpec((None, l_all, LANE), lambda b, c: (b, 0, base + off + c))

    def wspec(off):
        return pl.BlockSpec((None, 3, LANE), lambda b, c: (layer, 0, off + c))

    def bspec(off):
        return pl.BlockSpec((None, 1, LANE), lambda b, c: (layer, 0, off + c))

    o_spec = pl.BlockSpec((None, l_all, LANE), lambda b, c: (b, 0, c))
    return pl.pallas_call(
        functools.partial(_hy_conv_kernel, n_lat=n_lat),
        grid=(bsz, nb),
        in_specs=[pspec(0), pspec(nb), pspec(2 * nb), wspec(0), wspec(nb), wspec(2 * nb),
                  bspec(0), bspec(nb), bspec(2 * nb)],
        out_specs=[o_spec,
                   pl.BlockSpec((None, LANE, n_lat), lambda b, c: (b, c, 0)),
                   pl.BlockSpec((None, l_all - n_lat, LANE), lambda b, c: (b, 0, c))],
        out_shape=[jax.ShapeDtypeStruct((bsz, l_all, HY_W), F32),
                   jax.ShapeDtypeStruct((bsz, HY_W, n_lat), F32),
                   jax.ShapeDtypeStruct((bsz, l_all - n_lat, HY_W), F32)],
        compiler_params=_params("arbitrary", "arbitrary"),
        name="hyena_conv",
    )(p, p, p, conv_w, conv_w, conv_w, conv_b3, conv_b3, conv_b3)


def _mb_conv_kernel(p_ref, w_ref, b_ref, o_ref, *, n_lat):
    o_ref[...] = _silu(_conv3(p_ref[...], w_ref[...], b_ref[...], n_lat))


def mamba_conv(p, conv_w, conv_b3, layer, n_lat):
    bsz, l_all, _ = p.shape
    nb = MB_CONV_CH // LANE
    base = P_XBC // LANE
    return pl.pallas_call(
        functools.partial(_mb_conv_kernel, n_lat=n_lat),
        grid=(bsz, nb),
        in_specs=[pl.BlockSpec((None, l_all, LANE), lambda b, c: (b, 0, base + c)),
                  pl.BlockSpec((None, 3, LANE), lambda b, c: (layer, 0, c)),
                  pl.BlockSpec((None, 1, LANE), lambda b, c: (layer, 0, c))],
        out_specs=pl.BlockSpec((None, l_all, LANE), lambda b, c: (b, 0, c)),
        out_shape=jax.ShapeDtypeStruct((bsz, l_all, MB_CONV_CH), F32),
        compiler_params=_params("arbitrary", "arbitrary"),
        name="mamba_conv",
    )(p, conv_w, conv_b3)


def _softplus(x):
    return jnp.maximum(x, 0.0) + jnp.log(1.0 + jnp.exp(-jnp.abs(x)))


def _mb_dt_kernel(p_ref, bias_ref, o_ref):
    dt = _softplus(p_ref[...] + bias_ref[...])
    o_ref[0] = dt
    o_ref[1] = pltpu.roll(dt, LANE - MB_HEADS, 1)


def mamba_dt(p, dt_bias_row, layer):
    bsz, l_all, _ = p.shape
    return pl.pallas_call(
        _mb_dt_kernel,
        grid=(bsz,),
        in_specs=[pl.BlockSpec((None, l_all, LANE), lambda b: (b, 0, 0)),
                  pl.BlockSpec((None, 1, LANE), lambda b: (layer, 0, 0))],
        out_specs=pl.BlockSpec((2, None, l_all, LANE), lambda b: (0, b, 0, 0)),
        out_shape=jax.ShapeDtypeStruct((2, bsz, l_all, LANE), F32),
        compiler_params=_params("arbitrary"),
        name="mamba_dt",
    )(p, dt_bias_row)


def _hy_filter_kernel(f_ref, aux_ref, w1_ref, b1_ref, w2_ref, b2_ref, w3_ref, fr_ref, dl_ref, o_ref, *,
                      channel_major):
    fr = fr_ref[...]
    hdn = jnp.sin(fr * (_dot_hi(f_ref[...], w1_ref[...]) + b1_ref[...]))
    hdn = jnp.sin(fr * (_dot_hi(hdn, w2_ref[...]) + b2_ref[...]))
    filt = _dot_hi(hdn, w3_ref[...])
    aux = aux_ref[...]
    t = aux[:, 0:1]
    is_fwd = aux[:, 1:2] > 0.5
    valid = aux[:, 2:3]
    window = jnp.exp(-t * dl_ref[...]) * valid
    k2 = jnp.where(is_fwd, filt[:, :HY_W], filt[:, HY_W:]) * window
    o_ref[...] = k2.T if channel_major else k2


def hyena_filter(feats2, aux, w1p, b1p, w2p, b2p, w3p, frp, deltas, layer, channel_major):
    rows = feats2.shape[0]
    tr = _pick(rows, (512, 256))

    def lspec(shape):
        return pl.BlockSpec((None,) + shape, lambda i: (layer, 0, 0))

    if channel_major:
        out_spec = pl.BlockSpec((HY_W, tr), lambda i: (0, i))
        out_shape = jax.ShapeDtypeStruct((HY_W, rows), F32)
    else:
        out_spec = pl.BlockSpec((tr, HY_W), lambda i: (i, 0))
        out_shape = jax.ShapeDtypeStruct((rows, HY_W), F32)
    return pl.pallas_call(
        functools.partial(_hy_filter_kernel, channel_major=channel_major),
        grid=(rows // tr,),
        in_specs=[pl.BlockSpec((tr, LANE), lambda i: (i, 0)),
                  pl.BlockSpec((tr, LANE), lambda i: (i, 0)),
                  lspec((LANE, LANE)), lspec((1, LANE)), lspec((LANE, LANE)), lspec((1, LANE)),
                  lspec((LANE, 2 * HY_W)), lspec((1, LANE)),
                  pl.BlockSpec((1, HY_W), lambda i: (0, 0))],
        out_specs=out_spec,
        out_shape=out_shape,
        compiler_params=_params("arbitrary"),
        name="hyena_filter",
    )(feats2, aux, w1p, b1p, w2p, b2p, w3p, frp, deltas)


DFT_MINOR = 256


def _snap(c):
    for v in (0.0, 1.0, -1.0):
        if abs(c - v) < 1e-12:
            return v
    return float(c)


def _lincomb(terms):
    acc = None
    for cf, tile in terms:
        if cf == 0.0:
            continue
        v = tile()
        if acc is None:
            acc = v if cf == 1.0 else (-v if cf == -1.0 else cf * v)
        elif cf == 1.0:
            acc = acc + v
        elif cf == -1.0:
            acc = acc - v
        else:
            acc = acc + cf * v
    return acc


def _dft_consts(n_seq):
    n = 2 * n_seq
    n2 = DFT_MINOR
    n1 = n // n2
    nkj = n1 // 2 + 1
    ang1 = 2.0 * np.pi * np.outer(np.arange(nkj), np.arange(n1)) / n1
    cos1 = [[_snap(v) for v in r] for r in np.cos(ang1)]
    sin1 = [[_snap(v) for v in r] for r in np.sin(ang1)]
    ang_t = 2.0 * np.pi * np.outer(np.arange(nkj), np.arange(n2)) / n
    rows = ((nkj + 7) // 8) * 8
    tw_c = np.zeros((rows, n2))
    tw_s = np.zeros((rows, n2))
    tw_c[:nkj] = np.cos(ang_t)
    tw_s[:nkj] = np.sin(ang_t)
    ang2 = 2.0 * np.pi * np.outer(np.arange(n2), np.arange(n2)) / n2
    c2, s2 = np.cos(ang2), np.sin(ang2)
    m_fwd = np.block([[c2, -s2], [s2, c2]])
    m_inv = np.block([[c2, s2], [-s2, c2]])
    as32 = lambda a: jnp.asarray(a, dtype=F32)

    def split(m):
        m32 = as32(m)
        hi = m32.astype(MXU_DT)
        lo = (m32 - hi.astype(F32)).astype(MXU_DT)
        return jnp.stack([hi, lo])

    return dict(n=n, n1=n1, nkj=nkj, cos1=cos1, sin1=sin1, tw_c=as32(tw_c), tw_s=as32(tw_s),
                m_fwd=split(m_fwd), m_inv=split(m_inv))


def _dot_split(a, m_ref):
    a_hi = a.astype(MXU_DT)
    a_lo = (a - a_hi.astype(F32)).astype(MXU_DT)
    return _dot(a_hi, m_ref[0]) + _dot(a_lo, m_ref[0]) + _dot(a_hi, m_ref[1])


def _outer_fwd(src_ref, z_ref, ct, n_in, cst):
    n2 = DFT_MINOR
    nkj, cos1, sin1 = cst["nkj"], cst["cos1"], cst["sin1"]

    def body(cb, carry):
        r0 = pl.multiple_of(cb * 8, 8)
        tile = lambda j: (lambda: src_ref[pl.ds(r0, 8), j * n2:(j + 1) * n2])
        for kj in range(nkj):
            re = _lincomb([(cos1[kj][j], tile(j)) for j in range(n_in)])
            im = _lincomb([(-sin1[kj][j], tile(j)) for j in range(n_in)])
            z_ref[pl.ds(kj * ct + r0, 8), 0:n2] = re
            z_ref[pl.ds(kj * ct + r0, 8), n2:2 * n2] = jnp.zeros_like(re) if im is None else im
        return carry

    lax.fori_loop(0, ct // 8, body, 0)


def _twiddle(z_ref, twc_ref, tws_ref, ct, nkj, inverse):
    n2 = DFT_MINOR
    for kj in range(1, nkj):
        re = z_ref[kj * ct:(kj + 1) * ct, 0:n2]
        im = z_ref[kj * ct:(kj + 1) * ct, n2:2 * n2]
        c = twc_ref[kj:kj + 1, :]
        s = tws_ref[kj:kj + 1, :]
        if inverse:
            s = -s
        z_ref[kj * ct:(kj + 1) * ct, 0:n2] = re * c + im * s
        z_ref[kj * ct:(kj + 1) * ct, n2:2 * n2] = im * c - re * s


def _spectrum_kernel(k_ref, twc_ref, tws_ref, mf_ref, h_ref, z_ref, *, ct, cst, n_chunks):
    nkj = cst["nkj"]
    _outer_fwd(k_ref, z_ref, ct, cst["n1"], cst)
    _twiddle(z_ref, twc_ref, tws_ref, ct, nkj, False)
    rc = nkj * ct // n_chunks

    def chunk(r, carry):
        rows = pl.ds(pl.multiple_of(r * rc, 8), rc)
        h_ref[rows, :] = _dot_split(z_ref[rows, :], mf_ref)
        return carry

    lax.fori_loop(0, n_chunks, chunk, 0)


def _long_conv_kernel(x_ref, h_ref, twc_ref, tws_ref, mf_ref, mi_ref, x0_ref, vxc_ref, yc_ref, skip_ref,
                      skip_row_ref, o_ref, z_ref, yt_ref, *, ct, cst, n_chunks, n_lat):
    n2 = DFT_MINOR
    nkj, n1, cos1, sin1 = cst["nkj"], cst["n1"], cst["cos1"], cst["sin1"]
    _outer_fwd(x_ref, z_ref, ct, n1 // 2, cst)
    _twiddle(z_ref, twc_ref, tws_ref, ct, nkj, False)
    rc = nkj * ct // n_chunks

    def chunk(r, carry):
        rows = pl.ds(pl.multiple_of(r * rc, 8), rc)
        x = _dot_split(z_ref[rows, :], mf_ref)
        xre, xim = x[:, :n2], x[:, n2:]
        hre, him = h_ref[rows, 0:n2], h_ref[rows, n2:2 * n2]
        y = jnp.concatenate([xre * hre - xim * him, xre * him + xim * hre], axis=1)
        z_ref[rows, :] = _dot_split(y, mi_ref)
        return carry

    lax.fori_loop(0, n_chunks, chunk, 0)
    _twiddle(z_ref, twc_ref, tws_ref, ct, nkj, True)

    inv_n = 1.0 / cst["n"]

    def body(cb, carry):
        r0 = pl.multiple_of(cb * 8, 8)
        g_re = lambda kj: (lambda: z_ref[pl.ds(kj * ct + r0, 8), 0:n2])
        g_im = lambda kj: (lambda: z_ref[pl.ds(kj * ct + r0, 8), n2:2 * n2])
        for j in range(n1 // 2):
            terms = []
            for kj in range(nkj):
                wgt = inv_n if kj in (0, nkj - 1) else 2.0 * inv_n
                terms.append((wgt * cos1[kj][j], g_re(kj)))
                terms.append((-wgt * sin1[kj][j], g_im(kj)))
            yt_ref[pl.ds(r0, 8), j * n2:(j + 1) * n2] = _lincomb(terms)
        return carry

    lax.fori_loop(0, ct // 8, body, 0)

    y = (yt_ref[...] + x_ref[...] * skip_ref[...]).T
    o_ref[0:n_lat, :] = (x0_ref[0:n_lat, :] * y).astype(o_ref.dtype)
    yc = yc_ref[...] + vxc_ref[...] * skip_row_ref[...]
    o_ref[n_lat:, :] = (x0_ref[n_lat:, :] * yc).astype(o_ref.dtype)


def filter_spectrum(k2t, cst):
    ch, n = k2t.shape
    ct = LANE
    nkj = cst["nkj"]
    n_chunks = _pick(nkj * ct // 8, (4, 5, 1))
    full = lambda a: pl.BlockSpec(a.shape, lambda c: (0,) * a.ndim)
    return pl.pallas_call(
        functools.partial(_spectrum_kernel, ct=ct, cst=cst, n_chunks=n_chunks),
        grid=(ch // ct,),
        in_specs=[pl.BlockSpec((ct, n), lambda c: (c, 0)),
                  full(cst["tw_c"]), full(cst["tw_s"]), full(cst["m_fwd"])],
        out_specs=pl.BlockSpec((None, nkj * ct, 2 * DFT_MINOR), lambda c: (c, 0, 0)),
        out_shape=jax.ShapeDtypeStruct((ch // ct, nkj * ct, 2 * DFT_MINOR), F32),
        scratch_shapes=[pltpu.VMEM((nkj * ct, 2 * DFT_MINOR), F32)],
        compiler_params=_params("arbitrary"),
        name="filter_spectrum",
    )(k2t, cst["tw_c"], cst["tw_s"], cst["m_fwd"])


def hyena_long_conv(vxt, h_spec, x0, vx_ctx, y_ctx, skip_col, skip_row, cst, layer, n_lat):
    bsz, ch, _ = vxt.shape
    l_all = x0.shape[1]
    n_ctx = l_all - n_lat
    ct = LANE
    nkj = cst["nkj"]
    n_chunks = _pick(nkj * ct // 8, (4, 5, 1))
    full = lambda a: pl.BlockSpec(a.shape, lambda b, c: (0,) * a.ndim)
    return pl.pallas_call(
        functools.partial(_long_conv_kernel, ct=ct, cst=cst, n_chunks=n_chunks, n_lat=n_lat),
        grid=(bsz, ch // ct),
        in_specs=[pl.BlockSpec((None, ct, n_lat), lambda b, c: (b, c, 0)),
                  pl.BlockSpec((None, nkj * ct, 2 * DFT_MINOR), lambda b, c: (c, 0, 0)),
                  full(cst["tw_c"]), full(cst["tw_s"]), full(cst["m_fwd"]), full(cst["m_inv"]),
                  pl.BlockSpec((None, l_all, ct), lambda b, c: (b, 0, c)),
                  pl.BlockSpec((None, n_ctx, ct), lambda b, c: (b, 0, c)),
                  pl.BlockSpec((None, n_ctx, ct), lambda b, c: (b, 0, c)),
                  pl.BlockSpec((None, ct, 1), lambda b, c: (layer, c, 0)),
                  pl.BlockSpec((None, 1, ct), lambda b, c: (layer, 0, c))],
        out_specs=pl.BlockSpec((None, l_all, ct), lambda b, c: (b, 0, c)),
        out_shape=jax.ShapeDtypeStruct((bsz, l_all, ch), MXU_DT),
        scratch_shapes=[pltpu.VMEM((nkj * ct, 2 * DFT_MINOR), F32), pltpu.VMEM((ct, n_lat), F32)],
        compiler_params=_params("arbitrary", "arbitrary"),
        name="hyena_long_conv",
    )(vxt, h_spec, cst["tw_c"], cst["tw_s"], cst["m_fwd"], cst["m_inv"], x0, vx_ctx, y_ctx, skip_col, skip_row)


def _ctx_conv_kernel(v_ref, k_ref, fd_ref, fk_ref, gi_ref, o_ref, *, kp):
    x = _dot_hi(fd_ref[...], v_ref[...])
    h = _dot_hi(fk_ref[...], k_ref[...])
    xre, xim = x[:kp], x[kp:]
    hre, him = h[:kp], h[kp:]
    y = jnp.concatenate([xre * hre - xim * him, xre * him + xim * hre], axis=0)
    o_ref[...] = _dot_hi(gi_ref[...], y)


def _ctx_tables(n_ctx):
    n = 2 * n_ctx
    kh = n_ctx + 1
    kp = ((kh + 7) // 8) * 8
    k = np.arange(kp)[:, None]
    live = (k < kh).astype(np.float64)
    ang = 2.0 * np.pi * k * np.arange(n)[None, :] / n
    ck, sk = np.cos(ang) * live, np.sin(ang) * live
    fk = np.concatenate([ck, -sk], axis=0)
    fd = fk[:, :n_ctx]
    wk = np.where((k == 0) | (k == n_ctx), 1.0, 2.0) * live / n
    gi = np.concatenate([(ck * wk).T, (-sk * wk).T], axis=1)[:n_ctx]
    as32 = lambda a: jnp.asarray(a, dtype=F32)
    return dict(kp=kp, fd=as32(fd), fk=as32(fk), gi=as32(gi))


def long_conv_ctx(vx_ctx, k2c, tb):
    bsz, n_ctx, ch = vx_ctx.shape
    kp = tb["kp"]
    full = lambda a: pl.BlockSpec(a.shape, lambda b, c: (0,) * a.ndim)
    return pl.pallas_call(
        functools.partial(_ctx_conv_kernel, kp=kp),
        grid=(bsz, ch // LANE),
        in_specs=[pl.BlockSpec((None, n_ctx, LANE), lambda b, c: (b, 0, c)),
                  pl.BlockSpec((2 * n_ctx, LANE), lambda b, c: (0, c)),
                  full(tb["fd"]), full(tb["fk"]), full(tb["gi"])],
        out_specs=pl.BlockSpec((None, n_ctx, LANE), lambda b, c: (b, 0, c)),
        out_shape=jax.ShapeDtypeStruct((bsz, n_ctx, ch), F32),
        compiler_params=_params("arbitrary", "arbitrary"),
        name="ctx_long_conv",
    )(vx_ctx, k2c, tb["fd"], tb["fk"], tb["gi"])


def _split3(a):
    a1 = a.astype(MXU_DT)
    r = a - a1.astype(F32)
    a2 = r.astype(MXU_DT)
    a3 = (r - a2.astype(F32)).astype(MXU_DT)
    return a1, a2, a3


def _ssd_chunk(backward, xs_ref, b_ref, c_ref, dt_ref, alog, ex, y_ref, state_ref):
    n = MB_CHUNK
    li = lax.broadcasted_iota(jnp.int32, (n, n), 0)
    si = lax.broadcasted_iota(jnp.int32, (n, n), 1)
    mask = (li <= si) if backward else (li >= si)
    tri = mask.astype(MXU_DT)
    lane = lax.broadcasted_iota(jnp.int32, (n, LANE), 1)

    dt = dt_ref[...]
    da = dt * (-jnp.exp(alog))
    cum = sum(_dot(tri, t) for t in _split3(da))
    cum_t = cum.T
    both = jnp.concatenate([cum, dt], axis=0)
    both_e = sum(_dot(t, ex) for t in _split3(both))
    cum_e = both_e[:n]
    dt_e = both_e[n:]
    total_e = cum_e[0:1] if backward else cum_e[n - 1:n]

    xdt = xs_ref[...] * dt_e
    xdt_b = xdt.astype(MXU_DT)
    w_b = (jnp.exp(total_e - cum_e) * xdt).astype(MXU_DT)
    st = state_ref[...]
    y_parts = []
    s_parts = []
    gw = MB_STATE
    hw = (MB_HEADS // MB_GROUPS) * MB_HEADDIM
    for g in range(MB_GROUPS):
        cg = c_ref[:, g * gw:(g + 1) * gw].astype(MXU_DT)
        bg = b_ref[:, g * gw:(g + 1) * gw]
        cb = _dot_nt(cg, bg.astype(MXU_DT))
        y_off = _dot(cg, st[:, g * hw:(g + 1) * hw].astype(MXU_DT))
        s_parts.append(_dot(bg.T.astype(MXU_DT), w_b[:, g * hw:(g + 1) * hw]))
        diag = []
        for j in range(hw // LANE):
            lo = g * hw + j * LANE
            xp = xdt_b[:, lo:lo + LANE]
            pair = []
            for e in range(2):
                hd = (lo // MB_HEADDIM) + e
                seg = cum[:, hd:hd + 1] - cum_t[hd:hd + 1, :]
                decay = jnp.exp(jnp.where(mask, seg, -jnp.inf))
                pair.append(_dot((cb * decay).astype(MXU_DT), xp))
            diag.append(jnp.where(lane < MB_HEADDIM, pair[0], pair[1]))
        y_parts.append(jnp.concatenate(diag, axis=1) + y_off * jnp.exp(cum_e[:, g * hw:(g + 1) * hw]))
    y_ref[...] = jnp.concatenate(y_parts, axis=1)
    state_ref[...] = st * jnp.exp(total_e) + jnp.concatenate(s_parts, axis=1)


def _ssd_kernel(xf_ref, bf_ref, cf_ref, dtf_ref, xb_ref, bb_ref, cb_ref, dtb_ref, alog_ref, ex_ref,
                yf_ref, yb_ref, state_ref):
    @pl.when(pl.program_id(1) == 0)
    def _():
        state_ref[...] = jnp.zeros_like(state_ref)

    ex = ex_ref[...]
    _ssd_chunk(False, xf_ref, bf_ref, cf_ref, dtf_ref, alog_ref[0], ex, yf_ref, state_ref.at[0])
    _ssd_chunk(True, xb_ref, bb_ref, cb_ref, dtb_ref, alog_ref[1], ex, yb_ref, state_ref.at[1])


def ssd_scan(xbc, dt2, a_log_rows, expand, layer, n_lat):
    bsz, l_all, _ = xbc.shape
    nc = l_all // MB_CHUNK
    ncl = n_lat // MB_CHUNK
    fwd = lambda c: (c + ncl) % nc
    bwd = lambda c: nc - 1 - c

    def specs(blk, direction):
        return [pl.BlockSpec((None, MB_CHUNK, MB_INNER), lambda b, c: (b, blk(c), 0)),
                pl.BlockSpec((None, MB_CHUNK, 256), lambda b, c: (b, blk(c), 2)),
                pl.BlockSpec((None, MB_CHUNK, 256), lambda b, c: (b, blk(c), 3)),
                pl.BlockSpec((None, None, MB_CHUNK, LANE), lambda b, c: (direction, b, blk(c), 0))]

    y_shape = jax.ShapeDtypeStruct((bsz, l_all, MB_INNER), F32)
    return pl.pallas_call(
        _ssd_kernel,
        grid=(bsz, nc),
        in_specs=specs(fwd, 0) + specs(bwd, 1)
        + [pl.BlockSpec((None, 2, 1, LANE), lambda b, c: (layer, 0, 0, 0)),
           pl.BlockSpec((LANE, MB_INNER), lambda b, c: (0, 0))],
        out_specs=[pl.BlockSpec((None, MB_CHUNK, MB_INNER), lambda b, c: (b, fwd(c), 0)),
                   pl.BlockSpec((None, MB_CHUNK, MB_INNER), lambda b, c: (b, bwd(c), 0))],
        out_shape=[y_shape, y_shape],
        scratch_shapes=[pltpu.VMEM((2, MB_STATE, MB_INNER), F32)],
        compiler_params=_params("arbitrary", "arbitrary"),
        name="ssd_scan",
    )(xbc, xbc, xbc, dt2, xbc, xbc, xbc, dt2, a_log_rows, expand)


def _mb_finish_kernel(yf_ref, yb_ref, xs_ref, z_ref, dsk_ref, g_ref, o_ref):
    y = yf_ref[...] + yb_ref[...] + xs_ref[...] * dsk_ref[...]
    y = y * _silu(z_ref[...].astype(F32))
    r = lax.rsqrt(jnp.mean(y * y, axis=-1, keepdims=True) + RMS_EPS)
    o_ref[...] = (y * r * g_ref[...]).astype(o_ref.dtype)


def mamba_finish(y_f, y_b, xbc, p, d_skip_e, norm_g3, layer):
    bsz, l_all, _ = xbc.shape
    tr = _pick(l_all, (544, 384, 128))
    w = MB_INNER
    return pl.pallas_call(
        _mb_finish_kernel,
        grid=(bsz, l_all // tr),
        in_specs=[pl.BlockSpec((None, tr, w), lambda b, i: (b, i, 0)),
                  pl.BlockSpec((None, tr, w), lambda b, i: (b, i, 0)),
                  pl.BlockSpec((None, tr, w), lambda b, i: (b, i, 0)),
                  pl.BlockSpec((None, tr, w), lambda b, i: (b, i, P_Z // w)),
                  pl.BlockSpec((None, 1, w), lambda b, i: (layer, 0, 0)),
                  pl.BlockSpec((None, 1, w), lambda b, i: (layer, 0, 0))],
        out_specs=pl.BlockSpec((None, tr, w), lambda b, i: (b, i, 0)),
        out_shape=jax.ShapeDtypeStruct((bsz, l_all, w), MXU_DT),
        compiler_params=_params("arbitrary", "arbitrary"),
        name="mamba_finish",
    )(y_f, y_b, xbc, p, d_skip_e, norm_g3)


def _deinterleave(w, heads):
    lead = w.shape[:-1]
    w = w.reshape(lead + (heads, GQA_HD // 2, 2))
    return jnp.concatenate([w[..., 0], w[..., 1]], axis=-1).reshape(lead + (heads * GQA_HD,))


def _prep_w_in(w_in):
    nl, d, _ = w_in.shape
    z64 = jnp.zeros((nl, d, 64), w_in.dtype)
    mla = w_in[..., :MLA_COLS]
    cq, ckv, kr = mla[..., :448], mla[..., 448:576], mla[..., 576:640]
    ev, od = kr[..., 0::2], kr[..., 1::2]
    mla_blk = jnp.concatenate([cq, z64, ckv, ev, od, z64, -od, ev, z64], axis=-1)
    gqa = w_in[..., MLA_COLS:MLA_COLS + GQA_COLS]
    gqa_blk = jnp.concatenate([_deinterleave(gqa[..., :512], GQA_HEADS),
                               _deinterleave(gqa[..., 512:768], GQA_KV_HEADS), gqa[..., 768:]], axis=-1)
    hy = w_in[..., MLA_COLS + GQA_COLS:MLA_COLS + GQA_COLS + HY_COLS]
    mb = w_in[..., MLA_COLS + GQA_COLS + HY_COLS:]
    z, xbc, dt = mb[..., :MB_INNER], mb[..., MB_INNER:MB_INNER + MB_CONV_CH], mb[..., MB_INNER + MB_CONV_CH:]
    dt_blk = jnp.concatenate([dt, jnp.zeros((nl, d, LANE - 2 * MB_HEADS), w_in.dtype)], axis=-1)
    out = jnp.concatenate([mla_blk, dt_blk, gqa_blk, z, hy, xbc], axis=-1)
    assert out.shape[-1] == P_COLS
    return out.astype(MXU_DT)


def _prep_w_uq(w_uq):
    nl = w_uq.shape[0]
    w = w_uq.reshape(nl, MLA_Q_RANK, MLA_HEADS, MLA_NOPE + MLA_ROPE)
    nope, rot = w[..., :MLA_NOPE], w[..., MLA_NOPE:]
    ev, od = rot[..., 0::2], rot[..., 1::2]
    z64 = jnp.zeros_like(rot)
    per_head = jnp.concatenate([nope, ev, od, z64, -od, ev, z64], axis=-1)
    w = per_head.reshape(nl, MLA_Q_RANK, MLA_HEADS * 384)
    w = jnp.concatenate([w, jnp.zeros((nl, 512 - MLA_Q_RANK, MLA_HEADS * 384), w.dtype)], axis=1)
    return w.astype(MXU_DT)


def _rope_tables(n_lat, n_ctx, rot_dim, sign_folded):
    rows = n_lat // GRID_W
    row = jnp.repeat(jnp.arange(rows, dtype=F32), GRID_W)
    col = jnp.tile(jnp.arange(GRID_W, dtype=F32), rows)
    n_freq = rot_dim // 4
    inv_freq = ROPE_THETA ** (-jnp.arange(n_freq, dtype=F32) / n_freq)
    ang = jnp.concatenate([row[:, None] * inv_freq, col[:, None] * inv_freq], axis=-1)
    half = rot_dim // 2
    cos = jnp.concatenate([jnp.cos(ang), jnp.ones((n_ctx, half), F32)], axis=0)
    sin = jnp.concatenate([jnp.sin(ang), jnp.zeros((n_ctx, half), F32)], axis=0)
    pad = jnp.zeros((n_lat + n_ctx, LANE - rot_dim), F32)
    cos_t = jnp.concatenate([cos, cos, pad], axis=-1)
    sin_t = jnp.concatenate([sin if sign_folded else -sin, sin, pad], axis=-1)
    return cos_t, sin_t


def _filter_inputs(n):
    t = jnp.linspace(0.0, 1.0, n, dtype=F32)[:, None]
    omega = 2.0 * math.pi * jnp.arange(n, dtype=F32) / n
    bands = jnp.linspace(1e-4, HY_BANDS - 1, HY_BANDS, dtype=F32)
    ang = omega[:, None] * bands[None, :]
    feats = jnp.concatenate([t, jnp.cos(ang), -jnp.sin(ang)], axis=-1)
    zero = jnp.zeros((1, HY_EMB), F32)
    feats2 = jnp.concatenate([feats, zero, feats[1:][::-1]], axis=0)
    feats2 = jnp.concatenate([feats2, jnp.zeros((2 * n, LANE - HY_EMB), F32)], axis=-1)
    t2 = jnp.concatenate([t, jnp.zeros((1, 1), F32), t[1:][::-1]], axis=0)
    idx = jnp.arange(2 * n)[:, None]
    aux = jnp.concatenate([t2, (idx < n).astype(F32), (idx != n).astype(F32),
                           jnp.zeros((2 * n, LANE - 3), F32)], axis=-1)
    return feats2, aux


def _pad_to(a, shape):
    pads = [(0, s - d) for d, s in zip(a.shape, shape)]
    return jnp.pad(a, pads)


def kernel(x, c, ctx, c_ctx, w_ada, b_ada, w_in, mla_q_norm, mla_kv_norm, mla_w_uq, mla_w_ukv, gqa_q_norm, gqa_k_norm, hy_conv_w, hy_conv_b, hy_w1, hy_b1, hy_w2, hy_b2, hy_w3, hy_freq, hy_skip, mb_conv_w, mb_conv_b, mb_a_log, mb_dt_bias, mb_d, mb_norm, w_mgate, b_mgate, w_branch, w_out, ln1_g, ln1_b, w_ffn_in, w_ffn_out, ln2_g, ln2_b):
    bsz, n_lat, d = x.shape
    n_ctx = ctx.shape[1]
    nl = w_in.shape[0]
    assert d == D_MODEL and bsz < 8 and n_lat % n_ctx == 0 and n_ctx % MB_CHUNK == 0

    w_in_p = _prep_w_in(w_in)
    w_uq_p = _prep_w_uq(mla_w_uq)
    w_ukv_p = mla_w_ukv.astype(MXU_DT)
    gq_mla = _pad_to(mla_q_norm, (nl, 512))[:, None, :]
    gkv_mla = mla_kv_norm[:, None, :]
    gq_gqa = _deinterleave(gqa_q_norm, 1)[:, None, :]
    gk_gqa = _deinterleave(gqa_k_norm, 1)[:, None, :]
    w_gate_b = w_mgate.astype(MXU_DT)
    w_branch_b = w_branch.astype(MXU_DT)
    w_out_b = w_out.astype(MXU_DT)
    w_ffn_in_b = w_ffn_in.astype(MXU_DT)
    w_ffn_out_b = w_ffn_out.astype(MXU_DT)
    b_gate4 = b_mgate[:, :, None, :]
    b_ada3 = b_ada[:, None, :]
    hy_conv_b3 = hy_conv_b[:, None, :]
    mb_conv_b3 = mb_conv_b[:, None, :]
    hy_skip3 = hy_skip[:, None, :]
    hy_skip_col = hy_skip[:, :, None]
    ln1_g3, ln1_b3, ln2_g3, ln2_b3 = (a[:, None, :] for a in (ln1_g, ln1_b, ln2_g, ln2_b))
    hy_w1p = _pad_to(hy_w1, (nl, LANE, LANE))
    hy_b1p = _pad_to(hy_b1, (nl, LANE))[:, None, :]
    hy_w2p = _pad_to(hy_w2, (nl, LANE, LANE))
    hy_b2p = _pad_to(hy_b2, (nl, LANE))[:, None, :]
    hy_w3p = _pad_to(hy_w3, (nl, LANE, 2 * HY_W))
    hy_frp = _pad_to(hy_freq, (nl, LANE))[:, None, :]
    deltas = jnp.abs(jnp.linspace(HY_MIN_DECAY, HY_MAX_DECAY, HY_W, dtype=F32))[None, :]
    dt_bias_row = _pad_to(mb_dt_bias.reshape(nl, 2 * MB_HEADS), (nl, LANE))[:, None, :]
    a_log_rows = _pad_to(mb_a_log, (nl, 2, LANE))[:, :, None, :]
    d_skip_e = jnp.repeat(mb_d, MB_HEADDIM, axis=-1)[:, None, :]
    mb_norm3 = mb_norm[:, None, :]
    expand = jnp.asarray(np.kron(np.eye(LANE, MB_HEADS), np.ones((1, MB_HEADDIM))), dtype=MXU_DT)

    cos_m, sin_m = _rope_tables(n_lat, n_ctx, MLA_ROPE, True)
    cos_g, sin_g = _rope_tables(n_lat, n_ctx, GQA_HD, False)
    feats_lat, aux_lat = _filter_inputs(n_lat)
    feats_ctx, aux_ctx = _filter_inputs(n_ctx)
    dft_lat = _dft_consts(n_lat)
    tb_ctx = _ctx_tables(n_ctx)

    c8 = jnp.concatenate([c, c_ctx[None, :], jnp.zeros((8 - bsz - 1, d), F32)], axis=0)
    xz = jnp.concatenate([x, ctx], axis=1)

    for l in range(nl):
        last = l == nl - 1
        mod = ada_mod(c8, w_ada, b_ada3, l).reshape(8, 1, 6 * d)
        p, p_dt, h = in_projection(xz, mod, w_in_p, l, n_lat)

        q, k, v = mla_prep(p, gq_mla, gkv_mla, w_uq_p, w_ukv_p, cos_m, sin_m, l)
        oa = attention(q, k, v, n_lat)
        q, k, v = gqa_prep(p, gq_gqa, gk_gqa, cos_g, sin_g, l)
        ob = attention(q, k, v, n_lat)
        x0, vxt, vx_ctx = hyena_conv(p, hy_conv_w, hy_conv_b3, l, n_lat)
        k2t = hyena_filter(feats_lat, aux_lat, hy_w1p, hy_b1p, hy_w2p, hy_b2p, hy_w3p, hy_frp, deltas, l, True)
        h_spec = filter_spectrum(k2t, dft_lat)
        if last:
            y_ctx = jnp.zeros((bsz, n_ctx, HY_W), F32)
        else:
            k2c = hyena_filter(feats_ctx, aux_ctx, hy_w1p, hy_b1p, hy_w2p, hy_b2p, hy_w3p, hy_frp, deltas, l,
                               False)
            y_ctx = long_conv_ctx(vx_ctx, k2c, tb_ctx)
        oc = hyena_long_conv(vxt, h_spec, x0, vx_ctx, y_ctx, hy_skip_col, hy_skip3, dft_lat, l, n_lat)
        xbc = mamba_conv(p, mb_conv_w, mb_conv_b3, l, n_lat)
        dt2 = mamba_dt(p_dt, dt_bias_row, l)
        y_f, y_b = ssd_scan(xbc, dt2, a_log_rows, expand, l, n_lat)
        od = mamba_finish(y_f, y_b, xbc, p, d_skip_e, mb_norm3, l)

        acc = merge_branches(h, (oa, ob, oc, od), w_gate_b, b_gate4, w_branch_b, l)
        x1 = matmul_res_ln(acc, w_out_b, xz, mod, 2, ln1_g3, ln1_b3, l, n_lat, n_lat + n_ctx,
                           (544, 384, 128), d)
        out_rows = n_lat if last else n_lat + n_ctx
        xz = ffn_block(x1, mod, w_ffn_in_b, w_ffn_out_b, ln2_g3, ln2_b3, l, n_lat, out_rows)
    return xz
```

```python
import functools
import math

import jax
import jax.numpy as jnp
import numpy as np
from jax import lax
from jax.experimental import pallas as pl
from jax.experimental.pallas import tpu as pltpu

F32 = jnp.float32
MXU_DT = jnp.bfloat16

D_MODEL = 2048
DEPTH = 2
GRID_W = 64
N_BRANCH = 4
BRANCH_W = D_MODEL // N_BRANCH
ROPE_THETA = 10000.0
LN_EPS = 1e-6
RMS_EPS = 1e-6
DEEPNORM_ALPHA = (2 * DEPTH) ** 0.25

MLA_HEADS = 4
MLA_Q_RANK = 448
MLA_KV_RANK = 128
MLA_NOPE = 128
MLA_ROPE = 64
MLA_V = 128
MLA_COLS = MLA_Q_RANK + MLA_KV_RANK + MLA_ROPE

GQA_HEADS = 4
GQA_KV_HEADS = 2
GQA_HD = 128
GQA_COLS = (GQA_HEADS + 2 * GQA_KV_HEADS) * GQA_HD

HY_W = BRANCH_W
HY_EMB = 33
HY_BANDS = (HY_EMB - 1) // 2
HY_FFN = 64
HY_MIN_DECAY = math.log(1e-2) / 1.5
HY_MAX_DECAY = math.log(1e-2) / 0.3
HY_COLS = 3 * HY_W

MB_INNER = BRANCH_W
MB_HEADDIM = 64
MB_HEADS = 8
MB_GROUPS = 2
MB_STATE = 128
MB_CHUNK = 128
MB_CONV_CH = MB_INNER + 2 * MB_GROUPS * MB_STATE
MB_COLS = MB_INNER + MB_CONV_CH + 2 * MB_HEADS

FFN_HIDDEN = 5632

LANE = 128
VMEM_LIMIT = 56 * 1024 * 1024

P_MLA = 0
P_MLA_W = 768
P_DT = 896
P_GQA = 1024
P_Z = 2048
P_HY = 2560
P_XBC = 4096
P_COLS = 5120

LOG2_E = math.log2(math.e)
_HI = lax.Precision.HIGHEST


def _dot(a, b):
    return jnp.dot(a, b, preferred_element_type=F32)


def _dot_hi(a, b):
    return jnp.dot(a, b, precision=_HI, preferred_element_type=F32)


def _dot_nt(a, b):
    return lax.dot_general(a, b, (((1,), (1,)), ((), ())), preferred_element_type=F32)


def _sigmoid(x):
    return 1.0 / (1.0 + jnp.exp(-x))


def _silu(x):
    return x * _sigmoid(x)


def _params(*sem):
    return pltpu.CompilerParams(dimension_semantics=sem, vmem_limit_bytes=VMEM_LIMIT)


def _pick(n, prefs):
    for p in prefs:
        if n % p == 0:
            return p
    raise ValueError(f"no tile for {n} in {prefs}")


def _standardize(x):
    mu = jnp.mean(x, axis=-1, keepdims=True)
    xc = x - mu
    var = jnp.mean(xc * xc, axis=-1, keepdims=True)
    return xc * lax.rsqrt(var + LN_EPS)


def _is_ctx_rows(tile_idx, tm, n_lat):
    row = tile_idx * tm + lax.broadcasted_iota(jnp.int32, (tm, 1), 0)
    return row >= n_lat


def _ada_kernel(c_ref, w_ref, b_ref, o_ref):
    cs = _silu(c_ref[...])
    o_ref[...] = _dot(cs.astype(MXU_DT), w_ref[...].astype(MXU_DT)) + b_ref[...]


def ada_mod(c8, w_ada, b_ada3, layer):
    d = c8.shape[1]
    n = w_ada.shape[2]
    tn = 1024
    return pl.pallas_call(
        _ada_kernel,
        grid=(n // tn,),
        in_specs=[pl.BlockSpec((8, d), lambda j: (0, 0)),
                  pl.BlockSpec((None, d, tn), lambda j: (layer, 0, j)),
                  pl.BlockSpec((None, 1, tn), lambda j: (layer, 0, j))],
        out_specs=pl.BlockSpec((8, tn), lambda j: (0, j)),
        out_shape=jax.ShapeDtypeStruct((8, n), F32),
        compiler_params=_params("arbitrary"),
        name="ada_mod",
    )(c8, w_ada, b_ada3)


ROW_CHUNKS = 4


def _modulated(x, sh_ref, sc_ref, shc_ref, scc_ref, row0, n_lat):
    xn = _standardize(x)
    row = row0 + lax.broadcasted_iota(jnp.int32, (x.shape[0], 1), 0)
    is_ctx = row >= n_lat
    scale = jnp.where(is_ctx, scc_ref[...], sc_ref[...])
    shift = jnp.where(is_ctx, shc_ref[...], sh_ref[...])
    return (xn * (1.0 + scale) + shift).astype(MXU_DT)


def _inproj_kernel(x_ref, sh_ref, sc_ref, shc_ref, scc_ref, w_ref, p_ref, pdt_ref, h_ref, *, tm, n_lat):
    j = pl.program_id(2)
    rc = tm // ROW_CHUNKS

    @pl.when(j == 0)
    def _():
        for r in range(ROW_CHUNKS):
            rows = slice(r * rc, (r + 1) * rc)
            hb = _modulated(x_ref[rows, :], sh_ref, sc_ref, shc_ref, scc_ref,
                            pl.program_id(1) * tm + r * rc, n_lat)
            h_ref[rows, :] = hb
            acc = _dot(hb, w_ref[...])
            p_ref[rows, :] = acc.astype(p_ref.dtype)
            pdt_ref[rows, :] = acc[:, P_DT:P_DT + LANE]

    @pl.when(j > 0)
    def _():
        p_ref[...] = _dot(h_ref[...], w_ref[...]).astype(p_ref.dtype)


def _ffn_in_kernel(x_ref, sh_ref, sc_ref, shc_ref, scc_ref, wu_ref, wg_ref, a_ref, h_ref, *, tm, n_lat):
    j = pl.program_id(2)
    rc = tm // ROW_CHUNKS

    def swiglu(h):
        up = _dot(h, wu_ref[...])
        gate = _dot(h, wg_ref[...])
        return (_silu(gate) * up).astype(a_ref.dtype)

    @pl.when(j == 0)
    def _():
        for r in range(ROW_CHUNKS):
            rows = slice(r * rc, (r + 1) * rc)
            hb = _modulated(x_ref[rows, :], sh_ref, sc_ref, shc_ref, scc_ref,
                            pl.program_id(1) * tm + r * rc, n_lat)
            h_ref[rows, :] = hb
            a_ref[rows, :] = swiglu(hb)

    @pl.when(j > 0)
    def _():
        half = tm // 2
        for r in range(2):
            rows = slice(r * half, (r + 1) * half)
            a_ref[rows, :] = swiglu(h_ref[rows, :])


def _mod_specs(k_shift, k_scale, n_batch):
    d = D_MODEL
    return [pl.BlockSpec((None, 1, d), lambda b, i, j: (b, 0, k_shift)),
            pl.BlockSpec((None, 1, d), lambda b, i, j: (b, 0, k_scale)),
            pl.BlockSpec((None, 1, d), lambda b, i, j: (n_batch, 0, k_shift)),
            pl.BlockSpec((None, 1, d), lambda b, i, j: (n_batch, 0, k_scale))]


def in_projection(xz, mod, w_in_p, layer, n_lat):
    bsz, l_all, d = xz.shape
    n = w_in_p.shape[2]
    tm = _pick(l_all, (1088, 544, 384, 128))
    tn = 1024
    assert n % tn == 0 and P_DT + LANE <= tn
    kern = functools.partial(_inproj_kernel, tm=tm, n_lat=n_lat)
    return pl.pallas_call(
        kern,
        grid=(bsz, l_all // tm, n // tn),
        in_specs=[pl.BlockSpec((None, tm, d), lambda b, i, j: (b, i, 0))]
        + _mod_specs(0, 1, bsz)
        + [pl.BlockSpec((None, d, tn), lambda b, i, j: (layer, 0, j))],
        out_specs=[pl.BlockSpec((None, tm, tn), lambda b, i, j: (b, i, j)),
                   pl.BlockSpec((None, tm, LANE), lambda b, i, j: (b, i, 0)),
                   pl.BlockSpec((None, tm, d), lambda b, i, j: (b, i, 0))],
        out_shape=[jax.ShapeDtypeStruct((bsz, l_all, n), MXU_DT),
                   jax.ShapeDtypeStruct((bsz, l_all, LANE), F32),
                   jax.ShapeDtypeStruct((bsz, l_all, d), MXU_DT)],
        compiler_params=_params("arbitrary", "arbitrary", "arbitrary"),
        name="in_projection",
    )(xz, mod, mod, mod, mod, w_in_p)


def ffn_in(xz, mod, w_ffn_in, layer, n_lat):
    bsz, l_all, d = xz.shape
    hid = w_ffn_in.shape[2] // 2
    tm = _pick(l_all, (1088, 544, 384, 128))
    tn = 512
    nj = hid // tn
    kern = functools.partial(_ffn_in_kernel, tm=tm, n_lat=n_lat)
    return pl.pallas_call(
        kern,
        grid=(bsz, l_all // tm, nj),
        in_specs=[pl.BlockSpec((None, tm, d), lambda b, i, j: (b, i, 0))]
        + _mod_specs(3, 4, bsz)
        + [pl.BlockSpec((None, d, tn), lambda b, i, j: (layer, 0, j)),
           pl.BlockSpec((None, d, tn), lambda b, i, j: (layer, 0, j + nj))],
        out_specs=pl.BlockSpec((None, tm, tn), lambda b, i, j: (b, i, j)),
        out_shape=jax.ShapeDtypeStruct((bsz, l_all, hid), MXU_DT),
        scratch_shapes=[pltpu.VMEM((tm, d), MXU_DT)],
        compiler_params=_params("arbitrary", "arbitrary", "arbitrary"),
        name="ffn_in",
    )(xz, mod, mod, mod, mod, w_ffn_in, w_ffn_in)


def _resln_kernel(a_ref, w_ref, res_ref, g_ref, gc_ref, lng_ref, lnb_ref, o_ref, *, tm, n_lat, nk, rc):
    k = pl.program_id(2)
    i = pl.program_id(1)

    if nk > 1:
        @pl.when(k == 0)
        def _():
            o_ref[...] = _dot(a_ref[...], w_ref[...])

        @pl.when((k > 0) & (k < nk - 1))
        def _():
            o_ref[...] += _dot(a_ref[...], w_ref[...])

    @pl.when(k == nk - 1)
    def _():
        for r in range(tm // rc):
            rows = slice(r * rc, (r + 1) * rc)
            acc = _dot(a_ref[rows, :], w_ref[...])
            if nk > 1:
                acc = acc + o_ref[rows, :]
            row = i * tm + r * rc + lax.broadcasted_iota(jnp.int32, (rc, 1), 0)
            gate = jnp.where(row >= n_lat, gc_ref[...], g_ref[...])
            y = DEEPNORM_ALPHA * res_ref[rows, :] + gate * acc
            o_ref[rows, :] = _standardize(y) * lng_ref[...] + lnb_ref[...]


def matmul_res_ln(a, w, res, mod, k_gate, ln_g, ln_b, layer, n_lat, out_rows, tm_prefs, tk):
    bsz, l_all, kdim = a.shape
    d = w.shape[2]
    tm = _pick(l_all, tm_prefs)
    nk = kdim // tk
    rc = _pick(tm, (272, 192, 128))
    kern = functools.partial(_resln_kernel, tm=tm, n_lat=n_lat, nk=nk, rc=rc)
    w_mode = dict(pipeline_mode=pl.Buffered(1)) if nk == 1 else {}
    return pl.pallas_call(
        kern,
        grid=(bsz, pl.cdiv(out_rows, tm), nk),
        in_specs=[pl.BlockSpec((None, tm, tk), lambda b, i, k: (b, i, k)),
                  pl.BlockSpec((None, tk, d), lambda b, i, k: (layer, k, 0), **w_mode),
                  pl.BlockSpec((None, tm, d), lambda b, i, k: (b, i, 0)),
                  pl.BlockSpec((None, 1, d), lambda b, i, k: (b, 0, k_gate)),
                  pl.BlockSpec((None, 1, d), lambda b, i, k: (bsz, 0, k_gate)),
                  pl.BlockSpec((None, 1, d), lambda b, i, k: (layer, 0, 0)),
                  pl.BlockSpec((None, 1, d), lambda b, i, k: (layer, 0, 0))],
        out_specs=pl.BlockSpec((None, tm, d), lambda b, i, k: (b, i, 0)),
        out_shape=jax.ShapeDtypeStruct((bsz, out_rows, d), F32),
        compiler_params=_params("arbitrary", "arbitrary", "arbitrary"),
        name="matmul_res_ln",
    )(a, w, res, mod, mod, ln_g, ln_b)


def _merge_kernel(h_ref, oa_ref, ob_ref, oc_ref, od_ref, wg_ref, bg_ref, wb_ref, o_ref):
    h = h_ref[...]
    acc = None
    for i, o_r in enumerate((oa_ref, ob_ref, oc_ref, od_ref)):
        g = _dot(h, wg_ref[i]) + bg_ref[i]
        t = _dot(o_r[...], wb_ref[i])
        term = _sigmoid(g) * t
        acc = term if acc is None else acc + term
    o_ref[...] = acc.astype(o_ref.dtype)


def merge_branches(h, outs, w_gate, b_gate4, w_branch, layer):
    bsz, l_all, d = h.shape
    bw = outs[0].shape[2]
    tm = _pick(l_all, (1088, 544, 384, 128))
    tn = 512
    o_spec = pl.BlockSpec((None, tm, bw), lambda b, i, j: (b, i, 0))
    return pl.pallas_call(
        _merge_kernel,
        grid=(bsz, l_all // tm, d // tn),
        in_specs=[pl.BlockSpec((None, tm, d), lambda b, i, j: (b, i, 0)), o_spec, o_spec, o_spec, o_spec,
                  pl.BlockSpec((None, N_BRANCH, d, tn), lambda b, i, j: (layer, 0, 0, j)),
                  pl.BlockSpec((None, N_BRANCH, 1, tn), lambda b, i, j: (layer, 0, 0, j)),
                  pl.BlockSpec((None, N_BRANCH, bw, tn), lambda b, i, j: (layer, 0, 0, j))],
        out_specs=pl.BlockSpec((None, tm, tn), lambda b, i, j: (b, i, j)),
        out_shape=jax.ShapeDtypeStruct((bsz, l_all, d), MXU_DT),
        compiler_params=_params("arbitrary", "arbitrary", "arbitrary"),
        name="merge_branches",
    )(h, *outs, w_gate, b_gate4, w_branch)


def _rope(x, cos, sin):
    lane = lax.broadcasted_iota(jnp.int32, x.shape, 1)
    partner = jnp.where(lane % 2 == 0, pltpu.roll(x, LANE - 1, 1), pltpu.roll(x, 1, 1))
    return x * cos + partner * sin


def _mla_prep_kernel(p_ref, gq_ref, gkv_ref, wuq_ref, wukv_ref, cos_ref, sin_ref, q_ref, k_ref, v_ref):
    scale = (MLA_NOPE + MLA_ROPE) ** -0.5 * LOG2_E
    p = p_ref[...].astype(F32)
    cq = p[:, 0:512]
    ckv = p[:, 512:640]
    k_rot = p[:, 640:768]
    rq = lax.rsqrt(jnp.sum(cq * cq, axis=-1, keepdims=True) * (1.0 / MLA_Q_RANK) + RMS_EPS)
    cqn = (cq * rq * gq_ref[...]).astype(MXU_DT)
    rkv = lax.rsqrt(jnp.mean(ckv * ckv, axis=-1, keepdims=True) + RMS_EPS)
    ckvn = (ckv * rkv * gkv_ref[...]).astype(MXU_DT)
    qf = _dot(cqn, wuq_ref[...])
    kvf = _dot(ckvn, wukv_ref[...])
    cos = cos_ref[...]
    sin = sin_ref[...]
    k_rope = _rope(k_rot, cos, sin).astype(k_ref.dtype)
    for hd in range(MLA_HEADS):
        qb = hd * 256
        q_ref[hd, :, 0:128] = (qf[:, qb:qb + 128] * scale).astype(q_ref.dtype)
        q_rope = _rope(qf[:, qb + 128:qb + 256], cos, sin)
        q_ref[hd, :, 128:256] = (q_rope * scale).astype(q_ref.dtype)
        k_ref[hd, :, 0:128] = kvf[:, hd * 256:hd * 256 + 128].astype(k_ref.dtype)
        k_ref[hd, :, 128:256] = k_rope
        v_ref[hd] = kvf[:, hd * 256 + 128:hd * 256 + 256].T.astype(v_ref.dtype)


def mla_prep(p, gq, gkv, wuq, wukv, cos, sin, layer):
    bsz, l_all, _ = p.shape
    tr = _pick(l_all, (256, 128))
    h = MLA_HEADS
    return pl.pallas_call(
        _mla_prep_kernel,
        grid=(bsz, l_all // tr),
        in_specs=[pl.BlockSpec((None, tr, P_MLA_W), lambda b, i: (b, i, 0)),
                  pl.BlockSpec((None, 1, 512), lambda b, i: (layer, 0, 0)),
                  pl.BlockSpec((None, 1, 128), lambda b, i: (layer, 0, 0)),
                  pl.BlockSpec((None, 512, h * 256), lambda b, i: (layer, 0, 0)),
                  pl.BlockSpec((None, 128, h * 256), lambda b, i: (layer, 0, 0)),
                  pl.BlockSpec((tr, LANE), lambda b, i: (i, 0)),
                  pl.BlockSpec((tr, LANE), lambda b, i: (i, 0))],
        out_specs=[pl.BlockSpec((None, h, tr, 256), lambda b, i: (b, 0, i, 0)),
                   pl.BlockSpec((None, h, tr, 256), lambda b, i: (b, 0, i, 0)),
                   pl.BlockSpec((None, h, 128, tr), lambda b, i: (b, 0, 0, i))],
        out_shape=[jax.ShapeDtypeStruct((bsz, h, l_all, 256), MXU_DT),
                   jax.ShapeDtypeStruct((bsz, h, l_all, 256), MXU_DT),
                   jax.ShapeDtypeStruct((bsz, h, 128, l_all), MXU_DT)],
        compiler_params=_params("arbitrary", "arbitrary"),
        name="mla_prep",
    )(p, gq, gkv, wuq, wukv, cos, sin)


def _gqa_prep_kernel(p_ref, gq_ref, gk_ref, cos_ref, sin_ref, q_ref, k_ref, v_ref):
    scale = GQA_HD ** -0.5 * LOG2_E
    cos = cos_ref[...]
    sin = sin_ref[...]

    def norm_rope(x, g):
        r = lax.rsqrt(jnp.mean(x * x, axis=-1, keepdims=True) + RMS_EPS)
        return _rope(x * r * g, cos, sin)

    for hd in range(GQA_HEADS):
        x = p_ref[:, hd * 128:(hd + 1) * 128].astype(F32)
        q_ref[hd] = (norm_rope(x, gq_ref[...]) * scale).astype(q_ref.dtype)
    for hd in range(GQA_KV_HEADS):
        x = p_ref[:, 512 + hd * 128:512 + (hd + 1) * 128].astype(F32)
        k_ref[hd] = norm_rope(x, gk_ref[...]).astype(k_ref.dtype)
        v = p_ref[:, 768 + hd * 128:768 + (hd + 1) * 128].astype(F32)
        v_ref[hd] = v.T.astype(v_ref.dtype)


def gqa_prep(p, gq, gk, cos, sin, layer):
    bsz, l_all, _ = p.shape
    tr = _pick(l_all, (1088 * 2, 256, 128))
    return pl.pallas_call(
        _gqa_prep_kernel,
        grid=(bsz, l_all // tr),
        in_specs=[pl.BlockSpec((None, tr, GQA_COLS), lambda b, i: (b, i, P_GQA // GQA_COLS)),
                  pl.BlockSpec((None, 1, 128), lambda b, i: (layer, 0, 0)),
                  pl.BlockSpec((None, 1, 128), lambda b, i: (layer, 0, 0)),
                  pl.BlockSpec((tr, LANE), lambda b, i: (i, 0)),
                  pl.BlockSpec((tr, LANE), lambda b, i: (i, 0))],
        out_specs=[pl.BlockSpec((None, GQA_HEADS, tr, 128), lambda b, i: (b, 0, i, 0)),
                   pl.BlockSpec((None, GQA_KV_HEADS, tr, 128), lambda b, i: (b, 0, i, 0)),
                   pl.BlockSpec((None, GQA_KV_HEADS, 128, tr), lambda b, i: (b, 0, 0, i))],
        out_shape=[jax.ShapeDtypeStruct((bsz, GQA_HEADS, l_all, 128), MXU_DT),
                   jax.ShapeDtypeStruct((bsz, GQA_KV_HEADS, l_all, 128), MXU_DT),
                   jax.ShapeDtypeStruct((bsz, GQA_KV_HEADS, 128, l_all), MXU_DT)],
        compiler_params=_params("arbitrary", "arbitrary"),
        name="gqa_prep",
    )(p, gq, gk, cos, sin)


def _attn_kernel(q_ref, k_ref, v_ref, o_ref, *, n_lat, n_lat_tiles, sub):
    i = pl.program_id(2)

    def attend_all(k, vt):
        groups = [slice(r, r + sub) for r in range(0, q_ref.shape[0], sub)]
        sts = [_dot_nt(k, q_ref[g, :]) for g in groups]
        es, ls = [], []
        for st in sts:
            m = jnp.max(st, axis=0, keepdims=True)
            e = jnp.exp2(st - m)
            ls.append(jnp.sum(e, axis=0, keepdims=True))
            es.append(e.astype(vt.dtype))
        for g, e, l in zip(groups, es, ls):
            ot = _dot(vt, e) / l
            o_ref[g, :] = ot.T.astype(o_ref.dtype)

    @pl.when(i < n_lat_tiles)
    def _():
        attend_all(k_ref[...], v_ref[...])

    @pl.when(i >= n_lat_tiles)
    def _():
        attend_all(k_ref[n_lat:, :], v_ref[:, n_lat:])


def attention(q, k, v, n_lat):
    bsz, h, l_all, dk = q.shape
    hkv = k.shape[1]
    grp = h // hkv
    dv = v.shape[2]
    tq = _pick(n_lat, (1024, 512, 256, 128))
    assert l_all - n_lat <= tq
    kern = functools.partial(_attn_kernel, n_lat=n_lat, n_lat_tiles=n_lat // tq, sub=min(256, tq))
    return pl.pallas_call(
        kern,
        grid=(bsz, h, pl.cdiv(l_all, tq)),
        in_specs=[pl.BlockSpec((None, None, tq, dk), lambda b, hh, i: (b, hh, i, 0)),
                  pl.BlockSpec((None, None, l_all, dk), lambda b, hh, i: (b, hh // grp, 0, 0)),
                  pl.BlockSpec((None, None, dv, l_all), lambda b, hh, i: (b, hh // grp, 0, 0))],
        out_specs=pl.BlockSpec((None, tq, dv), lambda b, hh, i: (b, i, hh)),
        out_shape=jax.ShapeDtypeStruct((bsz, l_all, h * dv), MXU_DT),
        compiler_params=_params("arbitrary", "arbitrary", "arbitrary"),
        name="attention",
    )(q, k, v)


CONV_TC = 256


def _conv3(u, w, b, n_lat):
    u = u.astype(F32)
    n = u.shape[0]
    row = lax.broadcasted_iota(jnp.int32, (n, 1), 0)
    prev = jnp.where((row == 0) | (row == n_lat), 0.0, pltpu.roll(u, 1, 0))
    nxt = jnp.where((row == n_lat - 1) | (row == n - 1), 0.0, pltpu.roll(u, n - 1, 0))
    return w[0:1, :] * prev + w[1:2, :] * u + w[2:3, :] * nxt + b


def _hy_conv_kernel(p0_ref, p1_ref, pv_ref, w0_ref, w1_ref, wv_ref, b0_ref, b1_ref, bv_ref,
                    x0_ref, vxt_ref, vxc_ref, *, n_lat):
    x0_ref[...] = _conv3(p0_ref[...], w0_ref[...], b0_ref[...], n_lat)
    x1 = _conv3(p1_ref[...], w1_ref[...], b1_ref[...], n_lat)
    v = _conv3(pv_ref[...], wv_ref[...], bv_ref[...], n_lat)
    vx = v * x1
    vxt_ref[...] = vx[:n_lat, :].T
    vxc_ref[...] = vx[n_lat:, :]


def hyena_conv(p, conv_w, conv_b3, layer, n_lat):
    bsz, l_all, _ = p.shape
    tc = CONV_TC
    nb = HY_W // tc
    base = P_HY // tc

    def pspec(off):
        return pl.BlockSpec((None, l_all, tc), lambda b, c: (b, 0, base + off + c))

    def wspec(off):
        return pl.BlockSpec((None, 3, tc), lambda b, c: (layer, 0, off + c))

    def bspec(off):
        return pl.BlockSpec((None, 1, tc), lambda b, c: (layer, 0, off + c))

    o_spec = pl.BlockSpec((None, l_all, tc), lambda b, c: (b, 0, c))
    return pl.pallas_call(
        functools.partial(_hy_conv_kernel, n_lat=n_lat),
        grid=(bsz, nb),
        in_specs=[pspec(0), pspec(nb), pspec(2 * nb), wspec(0), wspec(nb), wspec(2 * nb),
                  bspec(0), bspec(nb), bspec(2 * nb)],
        out_specs=[o_spec,
                   pl.BlockSpec((None, tc, n_lat), lambda b, c: (b, c, 0)),
                   pl.BlockSpec((None, l_all - n_lat, tc), lambda b, c: (b, 0, c))],
        out_shape=[jax.ShapeDtypeStruct((bsz, l_all, HY_W), F32),
                   jax.ShapeDtypeStruct((bsz, HY_W, n_lat), F32),
                   jax.ShapeDtypeStruct((bsz, l_all - n_lat, HY_W), F32)],
        compiler_params=_params("arbitrary", "arbitrary"),
        name="hyena_conv",
    )(p, p, p, conv_w, conv_w, conv_w, conv_b3, conv_b3, conv_b3)


def _mb_conv_kernel(p_ref, w_ref, b_ref, o_ref, *, n_lat):
    o_ref[...] = _silu(_conv3(p_ref[...], w_ref[...], b_ref[...], n_lat))


def mamba_conv(p, conv_w, conv_b3, layer, n_lat):
    bsz, l_all, _ = p.shape
    tc = CONV_TC
    nb = MB_CONV_CH // tc
    base = P_XBC // tc
    return pl.pallas_call(
        functools.partial(_mb_conv_kernel, n_lat=n_lat),
        grid=(bsz, nb),
        in_specs=[pl.BlockSpec((None, l_all, tc), lambda b, c: (b, 0, base + c)),
                  pl.BlockSpec((None, 3, tc), lambda b, c: (layer, 0, c)),
                  pl.BlockSpec((None, 1, tc), lambda b, c: (layer, 0, c))],
        out_specs=pl.BlockSpec((None, l_all, tc), lambda b, c: (b, 0, c)),
        out_shape=jax.ShapeDtypeStruct((bsz, l_all, MB_CONV_CH), F32),
        compiler_params=_params("arbitrary", "arbitrary"),
        name="mamba_conv",
    )(p, conv_w, conv_b3)


def _softplus(x):
    return jnp.maximum(x, 0.0) + jnp.log(1.0 + jnp.exp(-jnp.abs(x)))


def _mb_dt_kernel(p_ref, bias_ref, o_ref):
    dt = _softplus(p_ref[...] + bias_ref[...])
    o_ref[0] = dt
    o_ref[1] = pltpu.roll(dt, LANE - MB_HEADS, 1)


def mamba_dt(p, dt_bias_row, layer):
    bsz, l_all, _ = p.shape
    return pl.pallas_call(
        _mb_dt_kernel,
        grid=(bsz,),
        in_specs=[pl.BlockSpec((None, l_all, LANE), lambda b: (b, 0, 0)),
                  pl.BlockSpec((None, 1, LANE), lambda b: (layer, 0, 0))],
        out_specs=pl.BlockSpec((2, None, l_all, LANE), lambda b: (0, b, 0, 0)),
        out_shape=jax.ShapeDtypeStruct((2, bsz, l_all, LANE), F32),
        compiler_params=_params("arbitrary"),
        name="mamba_dt",
    )(p, dt_bias_row)


def _hy_filter_kernel(f_ref, aux_ref, w1_ref, b1_ref, w2_ref, b2_ref, w3_ref, fr_ref, dl_ref, o_ref, *,
                      channel_major):
    fr = fr_ref[...]
    hdn = jnp.sin(fr * (_dot_hi(f_ref[...], w1_ref[...]) + b1_ref[...]))
    hdn = jnp.sin(fr * (_dot_hi(hdn, w2_ref[...]) + b2_ref[...]))
    filt = _dot_hi(hdn, w3_ref[...])
    aux = aux_ref[...]
    t = aux[:, 0:1]
    is_fwd = aux[:, 1:2] > 0.5
    valid = aux[:, 2:3]
    window = jnp.exp(-t * dl_ref[...]) * valid
    k2 = jnp.where(is_fwd, filt[:, :HY_W], filt[:, HY_W:]) * window
    o_ref[...] = k2.T if channel_major else k2


def hyena_filter(feats2, aux, w1p, b1p, w2p, b2p, w3p, frp, deltas, layer, channel_major):
    rows = feats2.shape[0]
    tr = _pick(rows, (512, 256))

    def lspec(shape):
        return pl.BlockSpec((None,) + shape, lambda i: (layer, 0, 0))

    if channel_major:
        out_spec = pl.BlockSpec((HY_W, tr), lambda i: (0, i))
        out_shape = jax.ShapeDtypeStruct((HY_W, rows), F32)
    else:
        out_spec = pl.BlockSpec((tr, HY_W), lambda i: (i, 0))
        out_shape = jax.ShapeDtypeStruct((rows, HY_W), F32)
    return pl.pallas_call(
        functools.partial(_hy_filter_kernel, channel_major=channel_major),
        grid=(rows // tr,),
        in_specs=[pl.BlockSpec((tr, LANE), lambda i: (i, 0)),
                  pl.BlockSpec((tr, LANE), lambda i: (i, 0)),
                  lspec((LANE, LANE)), lspec((1, LANE)), lspec((LANE, LANE)), lspec((1, LANE)),
                  lspec((LANE, 2 * HY_W)), lspec((1, LANE)),
                  pl.BlockSpec((1, HY_W), lambda i: (0, 0))],
        out_specs=out_spec,
        out_shape=out_shape,
        compiler_params=_params("arbitrary"),
        name="hyena_filter",
    )(feats2, aux, w1p, b1p, w2p, b2p, w3p, frp, deltas)


DFT_MINOR = 256


def _snap(c):
    for v in (0.0, 1.0, -1.0):
        if abs(c - v) < 1e-12:
            return v
    return float(c)


def _lincomb(terms):
    acc = None
    for cf, tile in terms:
        if cf == 0.0:
            continue
        v = tile()
        if acc is None:
            acc = v if cf == 1.0 else (-v if cf == -1.0 else cf * v)
        elif cf == 1.0:
            acc = acc + v
        elif cf == -1.0:
            acc = acc - v
        else:
            acc = acc + cf * v
    return acc


def _dft_consts(n_seq):
    n = 2 * n_seq
    n2 = DFT_MINOR
    n1 = n // n2
    nkj = n1 // 2 + 1
    ang1 = 2.0 * np.pi * np.outer(np.arange(nkj), np.arange(n1)) / n1
    cos1 = [[_snap(v) for v in r] for r in np.cos(ang1)]
    sin1 = [[_snap(v) for v in r] for r in np.sin(ang1)]
    ang_t = 2.0 * np.pi * np.outer(np.arange(nkj), np.arange(n2)) / n
    rows = ((nkj + 7) // 8) * 8
    tw_c = np.zeros((rows, n2))
    tw_s = np.zeros((rows, n2))
    tw_c[:nkj] = np.cos(ang_t)
    tw_s[:nkj] = np.sin(ang_t)
    ang2 = 2.0 * np.pi * np.outer(np.arange(n2), np.arange(n2)) / n2
    c2, s2 = np.cos(ang2), np.sin(ang2)
    m_fwd = np.block([[c2, -s2], [s2, c2]])
    m_inv = np.block([[c2, s2], [-s2, c2]])
    as32 = lambda a: jnp.asarray(a, dtype=F32)

    def split(m):
        m32 = as32(m)
        hi = m32.astype(MXU_DT)
        lo = (m32 - hi.astype(F32)).astype(MXU_DT)
        return jnp.stack([hi, lo])

    return dict(n=n, n1=n1, nkj=nkj, cos1=cos1, sin1=sin1, tw_c=as32(tw_c), tw_s=as32(tw_s),
                m_fwd=split(m_fwd), m_inv=split(m_inv))


def _dot_split(a, m_ref):
    a_hi = a.astype(MXU_DT)
    a_lo = (a - a_hi.astype(F32)).astype(MXU_DT)
    return _dot(a_hi, m_ref[0]) + _dot(a_lo, m_ref[0]) + _dot(a_hi, m_ref[1])


def _outer_fwd(src_ref, z_ref, ct, n_in, cst):
    n2 = DFT_MINOR
    nkj, cos1, sin1 = cst["nkj"], cst["cos1"], cst["sin1"]

    def body(cb, carry):
        r0 = pl.multiple_of(cb * 8, 8)
        tile = lambda j: (lambda: src_ref[pl.ds(r0, 8), j * n2:(j + 1) * n2])
        for kj in range(nkj):
            re = _lincomb([(cos1[kj][j], tile(j)) for j in range(n_in)])
            im = _lincomb([(-sin1[kj][j], tile(j)) for j in range(n_in)])
            z_ref[pl.ds(kj * ct + r0, 8), 0:n2] = re
            z_ref[pl.ds(kj * ct + r0, 8), n2:2 * n2] = jnp.zeros_like(re) if im is None else im
        return carry

    lax.fori_loop(0, ct // 8, body, 0)


def _twiddle(z_ref, twc_ref, tws_ref, ct, nkj, inverse):
    n2 = DFT_MINOR
    for kj in range(1, nkj):
        re = z_ref[kj * ct:(kj + 1) * ct, 0:n2]
        im = z_ref[kj * ct:(kj + 1) * ct, n2:2 * n2]
        c = twc_ref[kj:kj + 1, :]
        s = tws_ref[kj:kj + 1, :]
        if inverse:
            s = -s
        z_ref[kj * ct:(kj + 1) * ct, 0:n2] = re * c + im * s
        z_ref[kj * ct:(kj + 1) * ct, n2:2 * n2] = im * c - re * s


def _spectrum_kernel(k_ref, twc_ref, tws_ref, mf_ref, h_ref, z_ref, *, ct, cst, n_chunks):
    nkj = cst["nkj"]
    _outer_fwd(k_ref, z_ref, ct, cst["n1"], cst)
    _twiddle(z_ref, twc_ref, tws_ref, ct, nkj, False)
    rc = nkj * ct // n_chunks

    def chunk(r, carry):
        rows = pl.ds(pl.multiple_of(r * rc, 8), rc)
        h_ref[rows, :] = _dot_split(z_ref[rows, :], mf_ref)
        return carry

    lax.fori_loop(0, n_chunks, chunk, 0)


def _long_conv_kernel(x_ref, h_ref, twc_ref, tws_ref, mf_ref, mi_ref, x0_ref, vxc_ref, yc_ref, skip_ref,
                      skip_row_ref, o_ref, z_ref, yt_ref, *, ct, cst, n_chunks, n_lat):
    n2 = DFT_MINOR
    nkj, n1, cos1, sin1 = cst["nkj"], cst["n1"], cst["cos1"], cst["sin1"]
    _outer_fwd(x_ref, z_ref, ct, n1 // 2, cst)
    _twiddle(z_ref, twc_ref, tws_ref, ct, nkj, False)
    rc = nkj * ct // n_chunks

    def chunk(r, carry):
        rows = pl.ds(pl.multiple_of(r * rc, 8), rc)
        x = _dot_split(z_ref[rows, :], mf_ref)
        xre, xim = x[:, :n2], x[:, n2:]
        hre, him = h_ref[rows, 0:n2], h_ref[rows, n2:2 * n2]
        y = jnp.concatenate([xre * hre - xim * him, xre * him + xim * hre], axis=1)
        z_ref[rows, :] = _dot_split(y, mi_ref)
        return carry

    lax.fori_loop(0, n_chunks, chunk, 0)
    _twiddle(z_ref, twc_ref, tws_ref, ct, nkj, True)

    inv_n = 1.0 / cst["n"]

    def body(cb, carry):
        r0 = pl.multiple_of(cb * 8, 8)
        g_re = lambda kj: (lambda: z_ref[pl.ds(kj * ct + r0, 8), 0:n2])
        g_im = lambda kj: (lambda: z_ref[pl.ds(kj * ct + r0, 8), n2:2 * n2])
        for j in range(n1 // 2):
            terms = []
            for kj in range(nkj):
                wgt = inv_n if kj in (0, nkj - 1) else 2.0 * inv_n
                terms.append((wgt * cos1[kj][j], g_re(kj)))
                terms.append((-wgt * sin1[kj][j], g_im(kj)))
            yt_ref[pl.ds(r0, 8), j * n2:(j + 1) * n2] = _lincomb(terms)
        return carry

    lax.fori_loop(0, ct // 8, body, 0)

    y = (yt_ref[...] + x_ref[...] * skip_ref[...]).T
    o_ref[0:n_lat, :] = (x0_ref[0:n_lat, :] * y).astype(o_ref.dtype)
    yc = yc_ref[...] + vxc_ref[...] * skip_row_ref[...]
    o_ref[n_lat:, :] = (x0_ref[n_lat:, :] * yc).astype(o_ref.dtype)


def filter_spectrum(k2t, cst):
    ch, n = k2t.shape
    ct = LANE
    nkj = cst["nkj"]
    n_chunks = _pick(nkj * ct // 8, (4, 5, 1))
    full = lambda a: pl.BlockSpec(a.shape, lambda c: (0,) * a.ndim)
    return pl.pallas_call(
        functools.partial(_spectrum_kernel, ct=ct, cst=cst, n_chunks=n_chunks),
        grid=(ch // ct,),
        in_specs=[pl.BlockSpec((ct, n), lambda c: (c, 0)),
                  full(cst["tw_c"]), full(cst["tw_s"]), full(cst["m_fwd"])],
        out_specs=pl.BlockSpec((None, nkj * ct, 2 * DFT_MINOR), lambda c: (c, 0, 0)),
        out_shape=jax.ShapeDtypeStruct((ch // ct, nkj * ct, 2 * DFT_MINOR), F32),
        scratch_shapes=[pltpu.VMEM((nkj * ct, 2 * DFT_MINOR), F32)],
        compiler_params=_params("arbitrary"),
        name="filter_spectrum",
    )(k2t, cst["tw_c"], cst["tw_s"], cst["m_fwd"])


def hyena_long_conv(vxt, h_spec, x0, vx_ctx, y_ctx, skip_col, skip_row, cst, layer, n_lat):
    bsz, ch, _ = vxt.shape
    l_all = x0.shape[1]
    n_ctx = l_all - n_lat
    ct = LANE
    nkj = cst["nkj"]
    n_chunks = _pick(nkj * ct // 8, (4, 5, 1))
    full = lambda a: pl.BlockSpec(a.shape, lambda b, c: (0,) * a.ndim)
    return pl.pallas_call(
        functools.partial(_long_conv_kernel, ct=ct, cst=cst, n_chunks=n_chunks, n_lat=n_lat),
        grid=(bsz, ch // ct),
        in_specs=[pl.BlockSpec((None, ct, n_lat), lambda b, c: (b, c, 0)),
                  pl.BlockSpec((None, nkj * ct, 2 * DFT_MINOR), lambda b, c: (c, 0, 0)),
                  full(cst["tw_c"]), full(cst["tw_s"]), full(cst["m_fwd"]), full(cst["m_inv"]),
                  pl.BlockSpec((None, l_all, ct), lambda b, c: (b, 0, c)),
                  pl.BlockSpec((None, n_ctx, ct), lambda b, c: (b, 0, c)),
                  pl.BlockSpec((None, n_ctx, ct), lambda b, c: (b, 0, c)),
                  pl.BlockSpec((None, ct, 1), lambda b, c: (layer, c, 0)),
                  pl.BlockSpec((None, 1, ct), lambda b, c: (layer, 0, c))],
        out_specs=pl.BlockSpec((None, l_all, ct), lambda b, c: (b, 0, c)),
        out_shape=jax.ShapeDtypeStruct((bsz, l_all, ch), MXU_DT),
        scratch_shapes=[pltpu.VMEM((nkj * ct, 2 * DFT_MINOR), F32), pltpu.VMEM((ct, n_lat), F32)],
        compiler_params=_params("arbitrary", "arbitrary"),
        name="hyena_long_conv",
    )(vxt, h_spec, cst["tw_c"], cst["tw_s"], cst["m_fwd"], cst["m_inv"], x0, vx_ctx, y_ctx, skip_col, skip_row)


def _ctx_conv_kernel(v_ref, k_ref, fd_ref, fk_ref, gi_ref, o_ref, *, kp):
    x = _dot_hi(fd_ref[...], v_ref[...])
    h = _dot_hi(fk_ref[...], k_ref[...])
    xre, xim = x[:kp], x[kp:]
    hre, him = h[:kp], h[kp:]
    y = jnp.concatenate([xre * hre - xim * him, xre * him + xim * hre], axis=0)
    o_ref[...] = _dot_hi(gi_ref[...], y)


def _ctx_tables(n_ctx):
    n = 2 * n_ctx
    kh = n_ctx + 1
    kp = ((kh + 7) // 8) * 8
    k = np.arange(kp)[:, None]
    live = (k < kh).astype(np.float64)
    ang = 2.0 * np.pi * k * np.arange(n)[None, :] / n
    ck, sk = np.cos(ang) * live, np.sin(ang) * live
    fk = np.concatenate([ck, -sk], axis=0)
    fd = fk[:, :n_ctx]
    wk = np.where((k == 0) | (k == n_ctx), 1.0, 2.0) * live / n
    gi = np.concatenate([(ck * wk).T, (-sk * wk).T], axis=1)[:n_ctx]
    as32 = lambda a: jnp.asarray(a, dtype=F32)
    return dict(kp=kp, fd=as32(fd), fk=as32(fk), gi=as32(gi))


def long_conv_ctx(vx_ctx, k2c, tb):
    bsz, n_ctx, ch = vx_ctx.shape
    kp = tb["kp"]
    full = lambda a: pl.BlockSpec(a.shape, lambda b, c: (0,) * a.ndim)
    return pl.pallas_call(
        functools.partial(_ctx_conv_kernel, kp=kp),
        grid=(bsz, ch // LANE),
        in_specs=[pl.BlockSpec((None, n_ctx, LANE), lambda b, c: (b, 0, c)),
                  pl.BlockSpec((2 * n_ctx, LANE), lambda b, c: (0, c)),
                  full(tb["fd"]), full(tb["fk"]), full(tb["gi"])],
        out_specs=pl.BlockSpec((None, n_ctx, LANE), lambda b, c: (b, 0, c)),
        out_shape=jax.ShapeDtypeStruct((bsz, n_ctx, ch), F32),
        compiler_params=_params("arbitrary", "arbitrary"),
        name="ctx_long_conv",
    )(vx_ctx, k2c, tb["fd"], tb["fk"], tb["gi"])


def _split3(a):
    a1 = a.astype(MXU_DT)
    r = a - a1.astype(F32)
    a2 = r.astype(MXU_DT)
    a3 = (r - a2.astype(F32)).astype(MXU_DT)
    return a1, a2, a3


def _ssd_chunk(backward, xs_ref, b_ref, c_ref, dt_ref, alog, ex, y_ref, state_ref):
    n = MB_CHUNK
    li = lax.broadcasted_iota(jnp.int32, (n, n), 0)
    si = lax.broadcasted_iota(jnp.int32, (n, n), 1)
    mask = (li <= si) if backward else (li >= si)
    tri = mask.astype(MXU_DT)
    lane = lax.broadcasted_iota(jnp.int32, (n, LANE), 1)

    dt = dt_ref[...]
    da = dt * (-jnp.exp(alog))
    cum = sum(_dot(tri, t) for t in _split3(da))
    cum_t = cum.T
    both = jnp.concatenate([cum, dt], axis=0)
    both_e = sum(_dot(t, ex) for t in _split3(both))
    cum_e = both_e[:n]
    dt_e = both_e[n:]
    total_e = cum_e[0:1] if backward else cum_e[n - 1:n]

    xdt = xs_ref[...] * dt_e
    xdt_b = xdt.astype(MXU_DT)
    w_b = (jnp.exp(total_e - cum_e) * xdt).astype(MXU_DT)
    st = state_ref[...]
    y_parts = []
    s_parts = []
    gw = MB_STATE
    hw = (MB_HEADS // MB_GROUPS) * MB_HEADDIM
    for g in range(MB_GROUPS):
        cg = c_ref[:, g * gw:(g + 1) * gw].astype(MXU_DT)
        bg = b_ref[:, g * gw:(g + 1) * gw]
        cb = _dot_nt(cg, bg.astype(MXU_DT))
        y_off = _dot(cg, st[:, g * hw:(g + 1) * hw].astype(MXU_DT))
        s_parts.append(_dot(bg.T.astype(MXU_DT), w_b[:, g * hw:(g + 1) * hw]))
        diag = []
        for j in range(hw // LANE):
            lo = g * hw + j * LANE
            xp = xdt_b[:, lo:lo + LANE]
            pair = []
            for e in range(2):
                hd = (lo // MB_HEADDIM) + e
                seg = cum[:, hd:hd + 1] - cum_t[hd:hd + 1, :]
                decay = jnp.exp(jnp.where(mask, seg, -jnp.inf))
                pair.append(_dot((cb * decay).astype(MXU_DT), xp))
            diag.append(jnp.where(lane < MB_HEADDIM, pair[0], pair[1]))
        y_parts.append(jnp.concatenate(diag, axis=1) + y_off * jnp.exp(cum_e[:, g * hw:(g + 1) * hw]))
    y_ref[...] = jnp.concatenate(y_parts, axis=1)
    state_ref[...] = st * jnp.exp(total_e) + jnp.concatenate(s_parts, axis=1)


def _ssd_kernel(xf_ref, bf_ref, cf_ref, dtf_ref, xb_ref, bb_ref, cb_ref, dtb_ref, alog_ref, ex_ref,
                yf_ref, yb_ref, state_ref):
    @pl.when(pl.program_id(1) == 0)
    def _():
        state_ref[...] = jnp.zeros_like(state_ref)

    ex = ex_ref[...]
    _ssd_chunk(False, xf_ref, bf_ref, cf_ref, dtf_ref, alog_ref[0], ex, yf_ref, state_ref.at[0])
    _ssd_chunk(True, xb_ref, bb_ref, cb_ref, dtb_ref, alog_ref[1], ex, yb_ref, state_ref.at[1])


def ssd_scan(xbc, dt2, a_log_rows, expand, layer, n_lat):
    bsz, l_all, _ = xbc.shape
    nc = l_all // MB_CHUNK
    ncl = n_lat // MB_CHUNK
    fwd = lambda c: (c + ncl) % nc
    bwd = lambda c: nc - 1 - c

    def specs(blk, direction):
        return [pl.BlockSpec((None, MB_CHUNK, MB_INNER), lambda b, c: (b, blk(c), 0)),
                pl.BlockSpec((None, MB_CHUNK, 256), lambda b, c: (b, blk(c), 2)),
                pl.BlockSpec((None, MB_CHUNK, 256), lambda b, c: (b, blk(c), 3)),
                pl.BlockSpec((None, None, MB_CHUNK, LANE), lambda b, c: (direction, b, blk(c), 0))]

    y_shape = jax.ShapeDtypeStruct((bsz, l_all, MB_INNER), F32)
    return pl.pallas_call(
        _ssd_kernel,
        grid=(bsz, nc),
        in_specs=specs(fwd, 0) + specs(bwd, 1)
        + [pl.BlockSpec((None, 2, 1, LANE), lambda b, c: (layer, 0, 0, 0)),
           pl.BlockSpec((LANE, MB_INNER), lambda b, c: (0, 0))],
        out_specs=[pl.BlockSpec((None, MB_CHUNK, MB_INNER), lambda b, c: (b, fwd(c), 0)),
                   pl.BlockSpec((None, MB_CHUNK, MB_INNER), lambda b, c: (b, bwd(c), 0))],
        out_shape=[y_shape, y_shape],
        scratch_shapes=[pltpu.VMEM((2, MB_STATE, MB_INNER), F32)],
        compiler_params=_params("arbitrary", "arbitrary"),
        name="ssd_scan",
    )(xbc, xbc, xbc, dt2, xbc, xbc, xbc, dt2, a_log_rows, expand)


def _mb_finish_kernel(yf_ref, yb_ref, xs_ref, z_ref, dsk_ref, g_ref, o_ref):
    y = yf_ref[...] + yb_ref[...] + xs_ref[...] * dsk_ref[...]
    y = y * _silu(z_ref[...].astype(F32))
    r = lax.rsqrt(jnp.mean(y * y, axis=-1, keepdims=True) + RMS_EPS)
    o_ref[...] = (y * r * g_ref[...]).astype(o_ref.dtype)


def mamba_finish(y_f, y_b, xbc, p, d_skip_e, norm_g3, layer):
    bsz, l_all, _ = xbc.shape
    tr = _pick(l_all, (544, 384, 128))
    w = MB_INNER
    return pl.pallas_call(
        _mb_finish_kernel,
        grid=(bsz, l_all // tr),
        in_specs=[pl.BlockSpec((None, tr, w), lambda b, i: (b, i, 0)),
                  pl.BlockSpec((None, tr, w), lambda b, i: (b, i, 0)),
                  pl.BlockSpec((None, tr, w), lambda b, i: (b, i, 0)),
                  pl.BlockSpec((None, tr, w), lambda b, i: (b, i, P_Z // w)),
                  pl.BlockSpec((None, 1, w), lambda b, i: (layer, 0, 0)),
                  pl.BlockSpec((None, 1, w), lambda b, i: (layer, 0, 0))],
        out_specs=pl.BlockSpec((None, tr, w), lambda b, i: (b, i, 0)),
        out_shape=jax.ShapeDtypeStruct((bsz, l_all, w), MXU_DT),
        compiler_params=_params("arbitrary", "arbitrary"),
        name="mamba_finish",
    )(y_f, y_b, xbc, p, d_skip_e, norm_g3)


def _prep_w_in(w_in):
    nl, d, _ = w_in.shape
    zeros = lambda n: jnp.zeros((nl, d, n), w_in.dtype)
    mla = w_in[..., :MLA_COLS]
    cq, ckv, kr = mla[..., :448], mla[..., 448:576], mla[..., 576:640]
    mla_blk = jnp.concatenate([cq, zeros(64), ckv, kr, zeros(64), zeros(P_DT - P_MLA_W)], axis=-1)
    gqa = w_in[..., MLA_COLS:MLA_COLS + GQA_COLS]
    hy = w_in[..., MLA_COLS + GQA_COLS:MLA_COLS + GQA_COLS + HY_COLS]
    mb = w_in[..., MLA_COLS + GQA_COLS + HY_COLS:]
    z, xbc, dt = mb[..., :MB_INNER], mb[..., MB_INNER:MB_INNER + MB_CONV_CH], mb[..., MB_INNER + MB_CONV_CH:]
    out = jnp.concatenate([mla_blk, dt, zeros(LANE - 2 * MB_HEADS), gqa, z, hy, xbc], axis=-1)
    assert out.shape[-1] == P_COLS
    return out.astype(MXU_DT)


def _prep_w_uq(w_uq):
    nl = w_uq.shape[0]
    w = w_uq.reshape(nl, MLA_Q_RANK, MLA_HEADS, MLA_NOPE + MLA_ROPE)
    w = _pad_to(w, (nl, 512, MLA_HEADS, 256))
    return w.reshape(nl, 512, MLA_HEADS * 256).astype(MXU_DT)


def _rope_tables(n_lat, n_ctx, rot_dim):
    rows = n_lat // GRID_W
    row = jnp.repeat(jnp.arange(rows, dtype=F32), GRID_W)
    col = jnp.tile(jnp.arange(GRID_W, dtype=F32), rows)
    n_freq = rot_dim // 4
    inv_freq = ROPE_THETA ** (-jnp.arange(n_freq, dtype=F32) / n_freq)
    ang = jnp.concatenate([row[:, None] * inv_freq, col[:, None] * inv_freq], axis=-1)
    half = rot_dim // 2
    cos = jnp.concatenate([jnp.cos(ang), jnp.ones((n_ctx, half), F32)], axis=0)
    sin = jnp.concatenate([jnp.sin(ang), jnp.zeros((n_ctx, half), F32)], axis=0)
    cos_t = jnp.repeat(cos, 2, axis=-1)
    sin_t = jnp.stack([-sin, sin], axis=-1).reshape(n_lat + n_ctx, rot_dim)
    shape = (n_lat + n_ctx, LANE)
    return _pad_to(cos_t, shape), _pad_to(sin_t, shape)


def _filter_inputs(n):
    t = jnp.linspace(0.0, 1.0, n, dtype=F32)[:, None]
    omega = 2.0 * math.pi * jnp.arange(n, dtype=F32) / n
    bands = jnp.linspace(1e-4, HY_BANDS - 1, HY_BANDS, dtype=F32)
    ang = omega[:, None] * bands[None, :]
    feats = jnp.concatenate([t, jnp.cos(ang), -jnp.sin(ang)], axis=-1)
    zero = jnp.zeros((1, HY_EMB), F32)
    feats2 = jnp.concatenate([feats, zero, feats[1:][::-1]], axis=0)
    feats2 = jnp.concatenate([feats2, jnp.zeros((2 * n, LANE - HY_EMB), F32)], axis=-1)
    t2 = jnp.concatenate([t, jnp.zeros((1, 1), F32), t[1:][::-1]], axis=0)
    idx = jnp.arange(2 * n)[:, None]
    aux = jnp.concatenate([t2, (idx < n).astype(F32), (idx != n).astype(F32),
                           jnp.zeros((2 * n, LANE - 3), F32)], axis=-1)
    return feats2, aux


def _pad_to(a, shape):
    pads = [(0, s - d) for d, s in zip(a.shape, shape)]
    return jnp.pad(a, pads)


def kernel(x, c, ctx, c_ctx, w_ada, b_ada, w_in, mla_q_norm, mla_kv_norm, mla_w_uq, mla_w_ukv, gqa_q_norm, gqa_k_norm, hy_conv_w, hy_conv_b, hy_w1, hy_b1, hy_w2, hy_b2, hy_w3, hy_freq, hy_skip, mb_conv_w, mb_conv_b, mb_a_log, mb_dt_bias, mb_d, mb_norm, w_mgate, b_mgate, w_branch, w_out, ln1_g, ln1_b, w_ffn_in, w_ffn_out, ln2_g, ln2_b):
    bsz, n_lat, d = x.shape
    n_ctx = ctx.shape[1]
    nl = w_in.shape[0]
    assert d == D_MODEL and bsz < 8 and n_lat % n_ctx == 0 and n_ctx % MB_CHUNK == 0

    w_in_p = _prep_w_in(w_in)
    w_uq_p = _prep_w_uq(mla_w_uq)
    w_ukv_p = mla_w_ukv.astype(MXU_DT)
    gq_mla = _pad_to(mla_q_norm, (nl, 512))[:, None, :]
    gkv_mla = mla_kv_norm[:, None, :]
    gq_gqa = gqa_q_norm[:, None, :]
    gk_gqa = gqa_k_norm[:, None, :]
    w_gate_b = w_mgate.astype(MXU_DT)
    w_branch_b = w_branch.astype(MXU_DT)
    w_out_b = w_out.astype(MXU_DT)
    w_ffn_in_b = w_ffn_in.astype(MXU_DT)
    w_ffn_out_b = w_ffn_out.astype(MXU_DT)
    b_gate4 = b_mgate[:, :, None, :]
    b_ada3 = b_ada[:, None, :]
    hy_conv_b3 = hy_conv_b[:, None, :]
    mb_conv_b3 = mb_conv_b[:, None, :]
    hy_skip3 = hy_skip[:, None, :]
    hy_skip_col = hy_skip[:, :, None]
    ln1_g3, ln1_b3, ln2_g3, ln2_b3 = (a[:, None, :] for a in (ln1_g, ln1_b, ln2_g, ln2_b))
    hy_w1p = _pad_to(hy_w1, (nl, LANE, LANE))
    hy_b1p = _pad_to(hy_b1, (nl, LANE))[:, None, :]
    hy_w2p = _pad_to(hy_w2, (nl, LANE, LANE))
    hy_b2p = _pad_to(hy_b2, (nl, LANE))[:, None, :]
    hy_w3p = _pad_to(hy_w3, (nl, LANE, 2 * HY_W))
    hy_frp = _pad_to(hy_freq, (nl, LANE))[:, None, :]
    deltas = jnp.abs(jnp.linspace(HY_MIN_DECAY, HY_MAX_DECAY, HY_W, dtype=F32))[None, :]
    dt_bias_row = _pad_to(mb_dt_bias.reshape(nl, 2 * MB_HEADS), (nl, LANE))[:, None, :]
    a_log_rows = _pad_to(mb_a_log, (nl, 2, LANE))[:, :, None, :]
    d_skip_e = jnp.repeat(mb_d, MB_HEADDIM, axis=-1)[:, None, :]
    mb_norm3 = mb_norm[:, None, :]
    expand = jnp.asarray(np.kron(np.eye(LANE, MB_HEADS), np.ones((1, MB_HEADDIM))), dtype=MXU_DT)

    cos_m, sin_m = _rope_tables(n_lat, n_ctx, MLA_ROPE)
    cos_g, sin_g = _rope_tables(n_lat, n_ctx, GQA_HD)
    feats_lat, aux_lat = _filter_inputs(n_lat)
    feats_ctx, aux_ctx = _filter_inputs(n_ctx)
    dft_lat = _dft_consts(n_lat)
    tb_ctx = _ctx_tables(n_ctx)

    c8 = jnp.concatenate([c, c_ctx[None, :], jnp.zeros((8 - bsz - 1, d), F32)], axis=0)
    xz = jnp.concatenate([x, ctx], axis=1)

    for l in range(nl):
        last = l == nl - 1
        mod = ada_mod(c8, w_ada, b_ada3, l).reshape(8, 1, 6 * d)
        p, p_dt, h = in_projection(xz, mod, w_in_p, l, n_lat)

        q, k, v = mla_prep(p, gq_mla, gkv_mla, w_uq_p, w_ukv_p, cos_m, sin_m, l)
        oa = attention(q, k, v, n_lat)
        q, k, v = gqa_prep(p, gq_gqa, gk_gqa, cos_g, sin_g, l)
        ob = attention(q, k, v, n_lat)
        x0, vxt, vx_ctx = hyena_conv(p, hy_conv_w, hy_conv_b3, l, n_lat)
        k2t = hyena_filter(feats_lat, aux_lat, hy_w1p, hy_b1p, hy_w2p, hy_b2p, hy_w3p, hy_frp, deltas, l, True)
        h_spec = filter_spectrum(k2t, dft_lat)
        if last:
            y_ctx = jnp.zeros((bsz, n_ctx, HY_W), F32)
        else:
            k2c = hyena_filter(feats_ctx, aux_ctx, hy_w1p, hy_b1p, hy_w2p, hy_b2p, hy_w3p, hy_frp, deltas, l,
                               False)
            y_ctx = long_conv_ctx(vx_ctx, k2c, tb_ctx)
        oc = hyena_long_conv(vxt, h_spec, x0, vx_ctx, y_ctx, hy_skip_col, hy_skip3, dft_lat, l, n_lat)
        xbc = mamba_conv(p, mb_conv_w, mb_conv_b3, l, n_lat)
        dt2 = mamba_dt(p_dt, dt_bias_row, l)
        y_f, y_b = ssd_scan(xbc, dt2, a_log_rows, expand, l, n_lat)
        od = mamba_finish(y_f, y_b, xbc, p, d_skip_e, mb_norm3, l)

        acc = merge_branches(h, (oa, ob, oc, od), w_gate_b, b_gate4, w_branch_b, l)
        x1 = matmul_res_ln(acc, w_out_b, xz, mod, 2, ln1_g3, ln1_b3, l, n_lat, n_lat + n_ctx,
                           (544, 384, 128), d)
        act = ffn_in(x1, mod, w_ffn_in_b, l, n_lat)
        out_rows = n_lat if last else n_lat + n_ctx
        xz = matmul_res_ln(act, w_ffn_out_b, x1, mod, 5, ln2_g3, ln2_b3, l, n_lat, out_rows,
                           (256, 128), act.shape[2])
    return xz
```

```python
import functools
import math

import jax
import jax.numpy as jnp
import numpy as np
from jax import lax
from jax.experimental import pallas as pl
from jax.experimental.pallas import tpu as pltpu

F32 = jnp.float32
MXU_DT = jnp.bfloat16

D_MODEL = 2048
DEPTH = 2
GRID_W = 64
N_BRANCH = 4
BRANCH_W = D_MODEL // N_BRANCH
ROPE_THETA = 10000.0
LN_EPS = 1e-6
RMS_EPS = 1e-6
DEEPNORM_ALPHA = (2 * DEPTH) ** 0.25

MLA_HEADS = 4
MLA_Q_RANK = 448
MLA_KV_RANK = 128
MLA_NOPE = 128
MLA_ROPE = 64
MLA_V = 128
MLA_COLS = MLA_Q_RANK + MLA_KV_RANK + MLA_ROPE

GQA_HEADS = 4
GQA_KV_HEADS = 2
GQA_HD = 128
GQA_COLS = (GQA_HEADS + 2 * GQA_KV_HEADS) * GQA_HD

HY_W = BRANCH_W
HY_EMB = 33
HY_BANDS = (HY_EMB - 1) // 2
HY_FFN = 64
HY_MIN_DECAY = math.log(1e-2) / 1.5
HY_MAX_DECAY = math.log(1e-2) / 0.3
HY_COLS = 3 * HY_W

MB_INNER = BRANCH_W
MB_HEADDIM = 64
MB_HEADS = 8
MB_GROUPS = 2
MB_STATE = 128
MB_CHUNK = 128
MB_CONV_CH = MB_INNER + 2 * MB_GROUPS * MB_STATE
MB_COLS = MB_INNER + MB_CONV_CH + 2 * MB_HEADS

FFN_HIDDEN = 5632

LANE = 128
VMEM_LIMIT = 56 * 1024 * 1024

P_MLA = 0
P_MLA_W = 768
P_DT = 896
P_GQA = 1024
P_Z = 2048
P_HY = 2560
P_XBC = 4096
P_COLS = 5120

LOG2_E = math.log2(math.e)
_HI = lax.Precision.HIGHEST


def _dot(a, b):
    return jnp.dot(a, b, preferred_element_type=F32)


def _dot_hi(a, b):
    return jnp.dot(a, b, precision=_HI, preferred_element_type=F32)


def _dot_nt(a, b):
    return lax.dot_general(a, b, (((1,), (1,)), ((), ())), preferred_element_type=F32)


def _sigmoid(x):
    return 1.0 / (1.0 + jnp.exp(-x))


def _silu(x):
    return x * _sigmoid(x)


def _params(*sem):
    return pltpu.CompilerParams(dimension_semantics=sem, vmem_limit_bytes=VMEM_LIMIT)


def _pick(n, prefs):
    for p in prefs:
        if n % p == 0:
            return p
    raise ValueError(f"no tile for {n} in {prefs}")


def _standardize(x):
    mu = jnp.mean(x, axis=-1, keepdims=True)
    xc = x - mu
    var = jnp.mean(xc * xc, axis=-1, keepdims=True)
    return xc * lax.rsqrt(var + LN_EPS)


def _is_ctx_rows(tile_idx, tm, n_lat):
    row = tile_idx * tm + lax.broadcasted_iota(jnp.int32, (tm, 1), 0)
    return row >= n_lat


def _ada_kernel(c_ref, w_ref, b_ref, o_ref):
    cs = _silu(c_ref[...])
    o_ref[...] = _dot(cs.astype(MXU_DT), w_ref[...].astype(MXU_DT)) + b_ref[...]


def ada_mod(c8, w_ada, b_ada3, layer):
    d = c8.shape[1]
    n = w_ada.shape[2]
    tn = 1024
    return pl.pallas_call(
        _ada_kernel,
        grid=(n // tn,),
        in_specs=[pl.BlockSpec((8, d), lambda j: (0, 0)),
                  pl.BlockSpec((None, d, tn), lambda j: (layer, 0, j)),
                  pl.BlockSpec((None, 1, tn), lambda j: (layer, 0, j))],
        out_specs=pl.BlockSpec((8, tn), lambda j: (0, j)),
        out_shape=jax.ShapeDtypeStruct((8, n), F32),
        compiler_params=_params("arbitrary"),
        name="ada_mod",
    )(c8, w_ada, b_ada3)


ROW_CHUNKS = 4


def _modulated(x, sh_ref, sc_ref, shc_ref, scc_ref, row0, n_lat):
    xn = _standardize(x)
    row = row0 + lax.broadcasted_iota(jnp.int32, (x.shape[0], 1), 0)
    is_ctx = row >= n_lat
    scale = jnp.where(is_ctx, scc_ref[...], sc_ref[...])
    shift = jnp.where(is_ctx, shc_ref[...], sh_ref[...])
    return (xn * (1.0 + scale) + shift).astype(MXU_DT)


def _inproj_kernel(x_ref, sh_ref, sc_ref, shc_ref, scc_ref, w_ref, p_ref, pdt_ref, h_ref, *, tm, n_lat):
    j = pl.program_id(2)
    rc = tm // ROW_CHUNKS

    @pl.when(j == 0)
    def _():
        for r in range(ROW_CHUNKS):
            rows = slice(r * rc, (r + 1) * rc)
            hb = _modulated(x_ref[rows, :], sh_ref, sc_ref, shc_ref, scc_ref,
                            pl.program_id(1) * tm + r * rc, n_lat)
            h_ref[rows, :] = hb
            acc = _dot(hb, w_ref[...])
            p_ref[rows, :] = acc.astype(p_ref.dtype)
            pdt_ref[rows, :] = acc[:, P_DT:P_DT + LANE]

    @pl.when(j > 0)
    def _():
        p_ref[...] = _dot(h_ref[...], w_ref[...]).astype(p_ref.dtype)


def _ffn_in_kernel(x_ref, sh_ref, sc_ref, shc_ref, scc_ref, wu_ref, wg_ref, a_ref, h_ref, *, tm, n_lat):
    j = pl.program_id(2)
    rc = tm // ROW_CHUNKS

    def swiglu(h):
        up = _dot(h, wu_ref[...])
        gate = _dot(h, wg_ref[...])
        return (_silu(gate) * up).astype(a_ref.dtype)

    @pl.when(j == 0)
    def _():
        for r in range(ROW_CHUNKS):
            rows = slice(r * rc, (r + 1) * rc)
            hb = _modulated(x_ref[rows, :], sh_ref, sc_ref, shc_ref, scc_ref,
                            pl.program_id(1) * tm + r * rc, n_lat)
            h_ref[rows, :] = hb
            a_ref[rows, :] = swiglu(hb)

    @pl.when(j > 0)
    def _():
        half = tm // 2
        for r in range(2):
            rows = slice(r * half, (r + 1) * half)
            a_ref[rows, :] = swiglu(h_ref[rows, :])


def _mod_specs(k_shift, k_scale, n_batch):
    d = D_MODEL
    return [pl.BlockSpec((None, 1, d), lambda b, i, j: (b, 0, k_shift)),
            pl.BlockSpec((None, 1, d), lambda b, i, j: (b, 0, k_scale)),
            pl.BlockSpec((None, 1, d), lambda b, i, j: (n_batch, 0, k_shift)),
            pl.BlockSpec((None, 1, d), lambda b, i, j: (n_batch, 0, k_scale))]


def in_projection(xz, mod, w_in_p, layer, n_lat):
    bsz, l_all, d = xz.shape
    n = w_in_p.shape[2]
    tm = _pick(l_all, (1088, 544, 384, 128))
    tn = 1024
    assert n % tn == 0 and P_DT + LANE <= tn
    kern = functools.partial(_inproj_kernel, tm=tm, n_lat=n_lat)
    return pl.pallas_call(
        kern,
        grid=(bsz, l_all // tm, n // tn),
        in_specs=[pl.BlockSpec((None, tm, d), lambda b, i, j: (b, i, 0))]
        + _mod_specs(0, 1, bsz)
        + [pl.BlockSpec((None, d, tn), lambda b, i, j: (layer, 0, j))],
        out_specs=[pl.BlockSpec((None, tm, tn), lambda b, i, j: (b, i, j)),
                   pl.BlockSpec((None, tm, LANE), lambda b, i, j: (b, i, 0)),
                   pl.BlockSpec((None, tm, d), lambda b, i, j: (b, i, 0))],
        out_shape=[jax.ShapeDtypeStruct((bsz, l_all, n), MXU_DT),
                   jax.ShapeDtypeStruct((bsz, l_all, LANE), F32),
                   jax.ShapeDtypeStruct((bsz, l_all, d), MXU_DT)],
        compiler_params=_params("arbitrary", "arbitrary", "arbitrary"),
        name="in_projection",
    )(xz, mod, mod, mod, mod, w_in_p)


def ffn_in(xz, mod, w_ffn_in, layer, n_lat):
    bsz, l_all, d = xz.shape
    hid = w_ffn_in.shape[2] // 2
    tm = _pick(l_all, (1088, 544, 384, 128))
    tn = 512
    nj = hid // tn
    kern = functools.partial(_ffn_in_kernel, tm=tm, n_lat=n_lat)
    return pl.pallas_call(
        kern,
        grid=(bsz, l_all // tm, nj),
        in_specs=[pl.BlockSpec((None, tm, d), lambda b, i, j: (b, i, 0))]
        + _mod_specs(3, 4, bsz)
        + [pl.BlockSpec((None, d, tn), lambda b, i, j: (layer, 0, j)),
           pl.BlockSpec((None, d, tn), lambda b, i, j: (layer, 0, j + nj))],
        out_specs=pl.BlockSpec((None, tm, tn), lambda b, i, j: (b, i, j)),
        out_shape=jax.ShapeDtypeStruct((bsz, l_all, hid), MXU_DT),
        scratch_shapes=[pltpu.VMEM((tm, d), MXU_DT)],
        compiler_params=_params("arbitrary", "arbitrary", "arbitrary"),
        name="ffn_in",
    )(xz, mod, mod, mod, mod, w_ffn_in, w_ffn_in)


def _resln_kernel(a_ref, w_ref, res_ref, g_ref, gc_ref, lng_ref, lnb_ref, o_ref, *, tm, n_lat, nk, rc):
    k = pl.program_id(2)
    i = pl.program_id(1)

    if nk > 1:
        @pl.when(k == 0)
        def _():
            o_ref[...] = _dot(a_ref[...], w_ref[...])

        @pl.when((k > 0) & (k < nk - 1))
        def _():
            o_ref[...] += _dot(a_ref[...], w_ref[...])

    @pl.when(k == nk - 1)
    def _():
        for r in range(tm // rc):
            rows = slice(r * rc, (r + 1) * rc)
            acc = _dot(a_ref[rows, :], w_ref[...])
            if nk > 1:
                acc = acc + o_ref[rows, :]
            row = i * tm + r * rc + lax.broadcasted_iota(jnp.int32, (rc, 1), 0)
            gate = jnp.where(row >= n_lat, gc_ref[...], g_ref[...])
            y = DEEPNORM_ALPHA * res_ref[rows, :] + gate * acc
            o_ref[rows, :] = _standardize(y) * lng_ref[...] + lnb_ref[...]


def matmul_res_ln(a, w, res, mod, k_gate, ln_g, ln_b, layer, n_lat, out_rows, tm_prefs, tk):
    bsz, l_all, kdim = a.shape
    d = w.shape[2]
    tm = _pick(l_all, tm_prefs)
    nk = kdim // tk
    rc = _pick(tm, (272, 192, 128))
    kern = functools.partial(_resln_kernel, tm=tm, n_lat=n_lat, nk=nk, rc=rc)
    w_mode = dict(pipeline_mode=pl.Buffered(1)) if nk == 1 else {}
    return pl.pallas_call(
        kern,
        grid=(bsz, pl.cdiv(out_rows, tm), nk),
        in_specs=[pl.BlockSpec((None, tm, tk), lambda b, i, k: (b, i, k)),
                  pl.BlockSpec((None, tk, d), lambda b, i, k: (layer, k, 0), **w_mode),
                  pl.BlockSpec((None, tm, d), lambda b, i, k: (b, i, 0)),
                  pl.BlockSpec((None, 1, d), lambda b, i, k: (b, 0, k_gate)),
                  pl.BlockSpec((None, 1, d), lambda b, i, k: (bsz, 0, k_gate)),
                  pl.BlockSpec((None, 1, d), lambda b, i, k: (layer, 0, 0)),
                  pl.BlockSpec((None, 1, d), lambda b, i, k: (layer, 0, 0))],
        out_specs=pl.BlockSpec((None, tm, d), lambda b, i, k: (b, i, 0)),
        out_shape=jax.ShapeDtypeStruct((bsz, out_rows, d), F32),
        compiler_params=_params("arbitrary", "arbitrary", "arbitrary"),
        name="matmul_res_ln",
    )(a, w, res, mod, mod, ln_g, ln_b)


def _merge_kernel(h_ref, oa_ref, ob_ref, oc_ref, od_ref, wg_ref, bg_ref, wb_ref, o_ref):
    h = h_ref[...]
    acc = None
    for i, o_r in enumerate((oa_ref, ob_ref, oc_ref, od_ref)):
        g = _dot(h, wg_ref[i]) + bg_ref[i]
        t = _dot(o_r[...], wb_ref[i])
        term = _sigmoid(g) * t
        acc = term if acc is None else acc + term
    o_ref[...] = acc.astype(o_ref.dtype)


def merge_branches(h, outs, w_gate, b_gate4, w_branch, layer):
    bsz, l_all, d = h.shape
    bw = outs[0].shape[2]
    tm = _pick(l_all, (1088, 544, 384, 128))
    tn = 512
    o_spec = pl.BlockSpec((None, tm, bw), lambda b, i, j: (b, i, 0))
    return pl.pallas_call(
        _merge_kernel,
        grid=(bsz, l_all // tm, d // tn),
        in_specs=[pl.BlockSpec((None, tm, d), lambda b, i, j: (b, i, 0)), o_spec, o_spec, o_spec, o_spec,
                  pl.BlockSpec((None, N_BRANCH, d, tn), lambda b, i, j: (layer, 0, 0, j)),
                  pl.BlockSpec((None, N_BRANCH, 1, tn), lambda b, i, j: (layer, 0, 0, j)),
                  pl.BlockSpec((None, N_BRANCH, bw, tn), lambda b, i, j: (layer, 0, 0, j))],
        out_specs=pl.BlockSpec((None, tm, tn), lambda b, i, j: (b, i, j)),
        out_shape=jax.ShapeDtypeStruct((bsz, l_all, d), MXU_DT),
        compiler_params=_params("arbitrary", "arbitrary", "arbitrary"),
        name="merge_branches",
    )(h, *outs, w_gate, b_gate4, w_branch)


def _rope(x, cos, sin):
    lane = lax.broadcasted_iota(jnp.int32, x.shape, 1)
    partner = jnp.where(lane % 2 == 0, pltpu.roll(x, LANE - 1, 1), pltpu.roll(x, 1, 1))
    return x * cos + partner * sin


def _mla_prep_kernel(p_ref, gq_ref, gkv_ref, wuq_ref, wukv_ref, cos_ref, sin_ref, q_ref, k_ref, v_ref):
    scale = (MLA_NOPE + MLA_ROPE) ** -0.5 * LOG2_E
    p = p_ref[...].astype(F32)
    cq = p[:, 0:512]
    ckv = p[:, 512:640]
    k_rot = p[:, 640:768]
    rq = lax.rsqrt(jnp.sum(cq * cq, axis=-1, keepdims=True) * (1.0 / MLA_Q_RANK) + RMS_EPS)
    cqn = (cq * rq * gq_ref[...]).astype(MXU_DT)
    rkv = lax.rsqrt(jnp.mean(ckv * ckv, axis=-1, keepdims=True) + RMS_EPS)
    ckvn = (ckv * rkv * gkv_ref[...]).astype(MXU_DT)
    qf = _dot(cqn, wuq_ref[...])
    kvf = _dot(ckvn, wukv_ref[...])
    cos = cos_ref[...]
    sin = sin_ref[...]
    k_rope = _rope(k_rot, cos, sin).astype(k_ref.dtype)
    for hd in range(MLA_HEADS):
        qb = hd * 256
        q_ref[hd, :, 0:128] = (qf[:, qb:qb + 128] * scale).astype(q_ref.dtype)
        q_rope = _rope(qf[:, qb + 128:qb + 256], cos, sin)
        q_ref[hd, :, 128:256] = (q_rope * scale).astype(q_ref.dtype)
        k_ref[hd, :, 0:128] = kvf[:, hd * 256:hd * 256 + 128].astype(k_ref.dtype)
        k_ref[hd, :, 128:256] = k_rope
        v_ref[hd] = kvf[:, hd * 256 + 128:hd * 256 + 256].T.astype(v_ref.dtype)


def mla_prep(p, gq, gkv, wuq, wukv, cos, sin, layer):
    bsz, l_all, _ = p.shape
    tr = _pick(l_all, (256, 128))
    h = MLA_HEADS
    return pl.pallas_call(
        _mla_prep_kernel,
        grid=(bsz, l_all // tr),
        in_specs=[pl.BlockSpec((None, tr, P_MLA_W), lambda b, i: (b, i, 0)),
                  pl.BlockSpec((None, 1, 512), lambda b, i: (layer, 0, 0)),
                  pl.BlockSpec((None, 1, 128), lambda b, i: (layer, 0, 0)),
                  pl.BlockSpec((None, 512, h * 256), lambda b, i: (layer, 0, 0)),
                  pl.BlockSpec((None, 128, h * 256), lambda b, i: (layer, 0, 0)),
                  pl.BlockSpec((tr, LANE), lambda b, i: (i, 0)),
                  pl.BlockSpec((tr, LANE), lambda b, i: (i, 0))],
        out_specs=[pl.BlockSpec((None, h, tr, 256), lambda b, i: (b, 0, i, 0)),
                   pl.BlockSpec((None, h, tr, 256), lambda b, i: (b, 0, i, 0)),
                   pl.BlockSpec((None, h, 128, tr), lambda b, i: (b, 0, 0, i))],
        out_shape=[jax.ShapeDtypeStruct((bsz, h, l_all, 256), MXU_DT),
                   jax.ShapeDtypeStruct((bsz, h, l_all, 256), MXU_DT),
                   jax.ShapeDtypeStruct((bsz, h, 128, l_all), MXU_DT)],
        compiler_params=_params("arbitrary", "arbitrary"),
        name="mla_prep",
    )(p, gq, gkv, wuq, wukv, cos, sin)


def _gqa_prep_kernel(p_ref, gq_ref, gk_ref, cos_ref, sin_ref, q_ref, k_ref, v_ref):
    scale = GQA_HD ** -0.5 * LOG2_E
    cos = cos_ref[...]
    sin = sin_ref[...]

    def norm_rope(x, g):
        r = lax.rsqrt(jnp.mean(x * x, axis=-1, keepdims=True) + RMS_EPS)
        return _rope(x * r * g, cos, sin)

    for hd in range(GQA_HEADS):
        x = p_ref[:, hd * 128:(hd + 1) * 128].astype(F32)
        q_ref[hd] = (norm_rope(x, gq_ref[...]) * scale).astype(q_ref.dtype)
    for hd in range(GQA_KV_HEADS):
        x = p_ref[:, 512 + hd * 128:512 + (hd + 1) * 128].astype(F32)
        k_ref[hd] = norm_rope(x, gk_ref[...]).astype(k_ref.dtype)
        v = p_ref[:, 768 + hd * 128:768 + (hd + 1) * 128].astype(F32)
        v_ref[hd] = v.T.astype(v_ref.dtype)


def gqa_prep(p, gq, gk, cos, sin, layer):
    bsz, l_all, _ = p.shape
    tr = _pick(l_all, (1088 * 2, 256, 128))
    return pl.pallas_call(
        _gqa_prep_kernel,
        grid=(bsz, l_all // tr),
        in_specs=[pl.BlockSpec((None, tr, GQA_COLS), lambda b, i: (b, i, P_GQA // GQA_COLS)),
                  pl.BlockSpec((None, 1, 128), lambda b, i: (layer, 0, 0)),
                  pl.BlockSpec((None, 1, 128), lambda b, i: (layer, 0, 0)),
                  pl.BlockSpec((tr, LANE), lambda b, i: (i, 0)),
                  pl.BlockSpec((tr, LANE), lambda b, i: (i, 0))],
        out_specs=[pl.BlockSpec((None, GQA_HEADS, tr, 128), lambda b, i: (b, 0, i, 0)),
                   pl.BlockSpec((None, GQA_KV_HEADS, tr, 128), lambda b, i: (b, 0, i, 0)),
                   pl.BlockSpec((None, GQA_KV_HEADS, 128, tr), lambda b, i: (b, 0, 0, i))],
        out_shape=[jax.ShapeDtypeStruct((bsz, GQA_HEADS, l_all, 128), MXU_DT),
                   jax.ShapeDtypeStruct((bsz, GQA_KV_HEADS, l_all, 128), MXU_DT),
                   jax.ShapeDtypeStruct((bsz, GQA_KV_HEADS, 128, l_all), MXU_DT)],
        compiler_params=_params("arbitrary", "arbitrary"),
        name="gqa_prep",
    )(p, gq, gk, cos, sin)


def _attn_kernel(q_ref, k_ref, v_ref, o_ref, *, n_lat, n_lat_tiles, sub):
    i = pl.program_id(2)

    def attend_all(k, vt):
        groups = [slice(r, r + sub) for r in range(0, q_ref.shape[0], sub)]
        sts = [_dot_nt(k, q_ref[g, :]) for g in groups]
        es, ls = [], []
        for st in sts:
            m = jnp.max(st, axis=0, keepdims=True)
            e = jnp.exp2(st - m)
            ls.append(jnp.sum(e, axis=0, keepdims=True))
            es.append(e.astype(vt.dtype))
        for g, e, l in zip(groups, es, ls):
            ot = _dot(vt, e) / l
            o_ref[g, :] = ot.T.astype(o_ref.dtype)

    @pl.when(i < n_lat_tiles)
    def _():
        attend_all(k_ref[...], v_ref[...])

    @pl.when(i >= n_lat_tiles)
    def _():
        attend_all(k_ref[n_lat:, :], v_ref[:, n_lat:])


def attention(q, k, v, n_lat):
    bsz, h, l_all, dk = q.shape
    hkv = k.shape[1]
    grp = h // hkv
    dv = v.shape[2]
    tq = _pick(n_lat, (1024, 512, 256, 128))
    assert l_all - n_lat <= tq
    kern = functools.partial(_attn_kernel, n_lat=n_lat, n_lat_tiles=n_lat // tq, sub=min(256, tq))
    return pl.pallas_call(
        kern,
        grid=(bsz, h, pl.cdiv(l_all, tq)),
        in_specs=[pl.BlockSpec((None, None, tq, dk), lambda b, hh, i: (b, hh, i, 0)),
                  pl.BlockSpec((None, None, l_all, dk), lambda b, hh, i: (b, hh // grp, 0, 0)),
                  pl.BlockSpec((None, None, dv, l_all), lambda b, hh, i: (b, hh // grp, 0, 0))],
        out_specs=pl.BlockSpec((None, tq, dv), lambda b, hh, i: (b, i, hh)),
        out_shape=jax.ShapeDtypeStruct((bsz, l_all, h * dv), MXU_DT),
        compiler_params=_params("arbitrary", "arbitrary", "arbitrary"),
        name="attention",
    )(q, k, v)


CONV_TC = 256


def _conv3(u, w, b, n_lat):
    u = u.astype(F32)
    n = u.shape[0]
    row = lax.broadcasted_iota(jnp.int32, (n, 1), 0)
    prev = jnp.where((row == 0) | (row == n_lat), 0.0, pltpu.roll(u, 1, 0))
    nxt = jnp.where((row == n_lat - 1) | (row == n - 1), 0.0, pltpu.roll(u, n - 1, 0))
    return w[0:1, :] * prev + w[1:2, :] * u + w[2:3, :] * nxt + b


def _hy_conv_kernel(p0_ref, p1_ref, pv_ref, w0_ref, w1_ref, wv_ref, b0_ref, b1_ref, bv_ref,
                    x0_ref, vxt_ref, vxc_ref, *, n_lat):
    x0_ref[...] = _conv3(p0_ref[...], w0_ref[...], b0_ref[...], n_lat)
    x1 = _conv3(p1_ref[...], w1_ref[...], b1_ref[...], n_lat)
    v = _conv3(pv_ref[...], wv_ref[...], bv_ref[...], n_lat)
    vx = v * x1
    vxt_ref[...] = vx[:n_lat, :].T
    vxc_ref[...] = vx[n_lat:, :]


def hyena_conv(p, conv_w, conv_b3, layer, n_lat):
    bsz, l_all, _ = p.shape
    tc = CONV_TC
    nb = HY_W // tc
    base = P_HY // tc

    def pspec(off):
        return pl.BlockSpec((None, l_all, tc), lambda b, c: (b, 0, base + off + c))

    def wspec(off):
        return pl.BlockSpec((None, 3, tc), lambda b, c: (layer, 0, off + c))

    def bspec(off):
        return pl.BlockSpec((None, 1, tc), lambda b, c: (layer, 0, off + c))

    o_spec = pl.BlockSpec((None, l_all, tc), lambda b, c: (b, 0, c))
    return pl.pallas_call(
        functools.partial(_hy_conv_kernel, n_lat=n_lat),
        grid=(bsz, nb),
        in_specs=[pspec(0), pspec(nb), pspec(2 * nb), wspec(0), wspec(nb), wspec(2 * nb),
                  bspec(0), bspec(nb), bspec(2 * nb)],
        out_specs=[o_spec,
                   pl.BlockSpec((None, tc, n_lat), lambda b, c: (b, c, 0)),
                   pl.BlockSpec((None, l_all - n_lat, tc), lambda b, c: (b, 0, c))],
        out_shape=[jax.ShapeDtypeStruct((bsz, l_all, HY_W), F32),
                   jax.ShapeDtypeStruct((bsz, HY_W, n_lat), F32),
                   jax.ShapeDtypeStruct((bsz, l_all - n_lat, HY_W), F32)],
        compiler_params=_params("arbitrary", "arbitrary"),
        name="hyena_conv",
    )(p, p, p, conv_w, conv_w, conv_w, conv_b3, conv_b3, conv_b3)


def _mb_conv_kernel(p_ref, w_ref, b_ref, o_ref, *, n_lat):
    o_ref[...] = _silu(_conv3(p_ref[...], w_ref[...], b_ref[...], n_lat))


def mamba_conv(p, conv_w, conv_b3, layer, n_lat):
    bsz, l_all, _ = p.shape
    tc = CONV_TC
    nb = MB_CONV_CH // tc
    base = P_XBC // tc
    return pl.pallas_call(
        functools.partial(_mb_conv_kernel, n_lat=n_lat),
        grid=(bsz, nb),
        in_specs=[pl.BlockSpec((None, l_all, tc), lambda b, c: (b, 0, base + c)),
                  pl.BlockSpec((None, 3, tc), lambda b, c: (layer, 0, c)),
                  pl.BlockSpec((None, 1, tc), lambda b, c: (layer, 0, c))],
        out_specs=pl.BlockSpec((None, l_all, tc), lambda b, c: (b, 0, c)),
        out_shape=jax.ShapeDtypeStruct((bsz, l_all, MB_CONV_CH), F32),
        compiler_params=_params("arbitrary", "arbitrary"),
        name="mamba_conv",
    )(p, conv_w, conv_b3)


def _softplus(x):
    return jnp.maximum(x, 0.0) + jnp.log(1.0 + jnp.exp(-jnp.abs(x)))


def _mb_dt_kernel(p_ref, bias_ref, o_ref):
    dt = _softplus(p_ref[...] + bias_ref[...])
    o_ref[0] = dt
    o_ref[1] = pltpu.roll(dt, LANE - MB_HEADS, 1)


def mamba_dt(p, dt_bias_row, layer):
    bsz, l_all, _ = p.shape
    return pl.pallas_call(
        _mb_dt_kernel,
        grid=(bsz,),
        in_specs=[pl.BlockSpec((None, l_all, LANE), lambda b: (b, 0, 0)),
                  pl.BlockSpec((None, 1, LANE), lambda b: (layer, 0, 0))],
        out_specs=pl.BlockSpec((2, None, l_all, LANE), lambda b: (0, b, 0, 0)),
        out_shape=jax.ShapeDtypeStruct((2, bsz, l_all, LANE), F32),
        compiler_params=_params("arbitrary"),
        name="mamba_dt",
    )(p, dt_bias_row)


def _hy_filter_kernel(f_ref, aux_ref, w1_ref, b1_ref, w2_ref, b2_ref, w3_ref, fr_ref, dl_ref, o_ref, *,
                      channel_major):
    fr = fr_ref[...]
    hdn = jnp.sin(fr * (_dot_hi(f_ref[...], w1_ref[...]) + b1_ref[...]))
    hdn = jnp.sin(fr * (_dot_hi(hdn, w2_ref[...]) + b2_ref[...]))
    filt = _dot_hi(hdn, w3_ref[...])
    aux = aux_ref[...]
    t = aux[:, 0:1]
    is_fwd = aux[:, 1:2] > 0.5
    valid = aux[:, 2:3]
    window = jnp.exp(-t * dl_ref[...]) * valid
    k2 = jnp.where(is_fwd, filt[:, :HY_W], filt[:, HY_W:]) * window
    o_ref[...] = k2.T if channel_major else k2


def hyena_filter(feats2, aux, w1p, b1p, w2p, b2p, w3p, frp, deltas, layer, channel_major):
    rows = feats2.shape[0]
    tr = _pick(rows, (512, 256))

    def lspec(shape):
        return pl.BlockSpec((None,) + shape, lambda i: (layer, 0, 0))

    if channel_major:
        out_spec = pl.BlockSpec((HY_W, tr), lambda i: (0, i))
        out_shape = jax.ShapeDtypeStruct((HY_W, rows), F32)
    else:
        out_spec = pl.BlockSpec((tr, HY_W), lambda i: (i, 0))
        out_shape = jax.ShapeDtypeStruct((rows, HY_W), F32)
    return pl.pallas_call(
        functools.partial(_hy_filter_kernel, channel_major=channel_major),
        grid=(rows // tr,),
        in_specs=[pl.BlockSpec((tr, LANE), lambda i: (i, 0)),
                  pl.BlockSpec((tr, LANE), lambda i: (i, 0)),
                  lspec((LANE, LANE)), lspec((1, LANE)), lspec((LANE, LANE)), lspec((1, LANE)),
                  lspec((LANE, 2 * HY_W)), lspec((1, LANE)),
                  pl.BlockSpec((1, HY_W), lambda i: (0, 0))],
        out_specs=out_spec,
        out_shape=out_shape,
        compiler_params=_params("arbitrary"),
        name="hyena_filter",
    )(feats2, aux, w1p, b1p, w2p, b2p, w3p, frp, deltas)


DFT_MINOR = 256


def _snap(c):
    for v in (0.0, 1.0, -1.0):
        if abs(c - v) < 1e-12:
            return v
    return float(c)


def _lincomb(terms):
    acc = None
    for cf, tile in terms:
        if cf == 0.0:
            continue
        v = tile()
        if acc is None:
            acc = v if cf == 1.0 else (-v if cf == -1.0 else cf * v)
        elif cf == 1.0:
            acc = acc + v
        elif cf == -1.0:
            acc = acc - v
        else:
            acc = acc + cf * v
    return acc


def _dft_consts(n_seq):
    n = 2 * n_seq
    n2 = DFT_MINOR
    n1 = n // n2
    nkj = n1 // 2 + 1
    ang1 = 2.0 * np.pi * np.outer(np.arange(nkj), np.arange(n1)) / n1
    cos1 = [[_snap(v) for v in r] for r in np.cos(ang1)]
    sin1 = [[_snap(v) for v in r] for r in np.sin(ang1)]
    ang_t = 2.0 * np.pi * np.outer(np.arange(nkj), np.arange(n2)) / n
    rows = ((nkj + 7) // 8) * 8
    tw_c = np.zeros((rows, n2))
    tw_s = np.zeros((rows, n2))
    tw_c[:nkj] = np.cos(ang_t)
    tw_s[:nkj] = np.sin(ang_t)
    ang2 = 2.0 * np.pi * np.outer(np.arange(n2), np.arange(n2)) / n2
    c2, s2 = np.cos(ang2), np.sin(ang2)
    m_fwd = np.block([[c2, -s2], [s2, c2]])
    m_inv = np.block([[c2, s2], [-s2, c2]])
    as32 = lambda a: jnp.asarray(a, dtype=F32)

    def split(m):
        m32 = as32(m)
        hi = m32.astype(MXU_DT)
        lo = (m32 - hi.astype(F32)).astype(MXU_DT)
        return jnp.stack([hi, lo])

    return dict(n=n, n1=n1, nkj=nkj, cos1=cos1, sin1=sin1, tw_c=as32(tw_c), tw_s=as32(tw_s),
                m_fwd=split(m_fwd), m_inv=split(m_inv))


def _dot_split(a, m_ref):
    a_hi = a.astype(MXU_DT)
    a_lo = (a - a_hi.astype(F32)).astype(MXU_DT)
    return _dot(a_hi, m_ref[0]) + _dot(a_lo, m_ref[0]) + _dot(a_hi, m_ref[1])


def _outer_fwd_block(src_ref, z_ref, twc_ref, tws_ref, cb, n_in, cst):
    n2 = DFT_MINOR
    nkj, cos1, sin1 = cst["nkj"], cst["cos1"], cst["sin1"]
    tile = lambda j: (lambda: src_ref[cb * 8:cb * 8 + 8, j * n2:(j + 1) * n2])
    for kj in range(nkj):
        re = _lincomb([(cos1[kj][j], tile(j)) for j in range(n_in)])
        im = _lincomb([(-sin1[kj][j], tile(j)) for j in range(n_in)])
        if kj > 0 and im is not None:
            c = twc_ref[kj:kj + 1, :]
            s = tws_ref[kj:kj + 1, :]
            re, im = re * c + im * s, im * c - re * s
        elif kj > 0:
            re, im = re * twc_ref[kj:kj + 1, :], -re * tws_ref[kj:kj + 1, :]
        r = (cb * nkj + kj) * 8
        z_ref[r:r + 8, 0:n2] = re
        z_ref[r:r + 8, n2:2 * n2] = jnp.zeros_like(re) if im is None else im


def _outer_inv_block(z_ref, yt_ref, twc_ref, tws_ref, cb, cst):
    n2 = DFT_MINOR
    nkj, n1, cos1, sin1 = cst["nkj"], cst["n1"], cst["cos1"], cst["sin1"]
    inv_n = 1.0 / cst["n"]
    for kj in range(1, nkj):
        r = (cb * nkj + kj) * 8
        re = z_ref[r:r + 8, 0:n2]
        im = z_ref[r:r + 8, n2:2 * n2]
        c = twc_ref[kj:kj + 1, :]
        s = tws_ref[kj:kj + 1, :]
        z_ref[r:r + 8, 0:n2] = re * c - im * s
        z_ref[r:r + 8, n2:2 * n2] = im * c + re * s
    row = lambda kj: (cb * nkj + kj) * 8
    g_re = lambda kj: (lambda: z_ref[row(kj):row(kj) + 8, 0:n2])
    g_im = lambda kj: (lambda: z_ref[row(kj):row(kj) + 8, n2:2 * n2])
    for j in range(n1 // 2):
        terms = []
        for kj in range(nkj):
            wgt = inv_n if kj in (0, nkj - 1) else 2.0 * inv_n
            terms.append((wgt * cos1[kj][j], g_re(kj)))
            terms.append((-wgt * sin1[kj][j], g_im(kj)))
        yt_ref[cb * 8:cb * 8 + 8, j * n2:(j + 1) * n2] = _lincomb(terms)


def _spectrum_kernel(k_ref, twc_ref, tws_ref, mf_ref, h_ref, z_ref, *, ct, cst, n_chunks):
    nkj = cst["nkj"]
    per = ct // 8 // n_chunks
    rc = per * nkj * 8
    for c in range(n_chunks + 1):
        if c < n_chunks:
            for cb in range(c * per, (c + 1) * per):
                _outer_fwd_block(k_ref, z_ref, twc_ref, tws_ref, cb, cst["n1"], cst)
        if c >= 1:
            rows = slice((c - 1) * rc, c * rc)
            h_ref[rows, :] = _dot_split(z_ref[rows, :], mf_ref)


def _long_conv_kernel(x_ref, h_ref, twc_ref, tws_ref, mf_ref, mi_ref, x0_ref, vxc_ref, yc_ref, skip_ref,
                      skip_row_ref, o_ref, z_ref, yt_ref, *, ct, cst, n_chunks, n_lat):
    n2 = DFT_MINOR
    nkj, n1 = cst["nkj"], cst["n1"]
    per = ct // 8 // n_chunks
    rc = per * nkj * 8
    for c in range(n_chunks + 2):
        if c < n_chunks:
            for cb in range(c * per, (c + 1) * per):
                _outer_fwd_block(x_ref, z_ref, twc_ref, tws_ref, cb, n1 // 2, cst)
        if 1 <= c <= n_chunks:
            rows = slice((c - 1) * rc, c * rc)
            x = _dot_split(z_ref[rows, :], mf_ref)
            xre, xim = x[:, :n2], x[:, n2:]
            hre, him = h_ref[rows, 0:n2], h_ref[rows, n2:2 * n2]
            y = jnp.concatenate([xre * hre - xim * him, xre * him + xim * hre], axis=1)
            z_ref[rows, :] = _dot_split(y, mi_ref)
        if c >= 2:
            for cb in range((c - 2) * per, (c - 1) * per):
                _outer_inv_block(z_ref, yt_ref, twc_ref, tws_ref, cb, cst)

    y = (yt_ref[...] + x_ref[...] * skip_ref[...]).T
    o_ref[0:n_lat, :] = (x0_ref[0:n_lat, :] * y).astype(o_ref.dtype)
    yc = yc_ref[...] + vxc_ref[...] * skip_row_ref[...]
    o_ref[n_lat:, :] = (x0_ref[n_lat:, :] * yc).astype(o_ref.dtype)


def filter_spectrum(k2t, cst):
    ch, n = k2t.shape
    ct = LANE
    nkj = cst["nkj"]
    n_chunks = _pick(nkj * ct // 8, (4, 5, 1))
    full = lambda a: pl.BlockSpec(a.shape, lambda c: (0,) * a.ndim)
    return pl.pallas_call(
        functools.partial(_spectrum_kernel, ct=ct, cst=cst, n_chunks=n_chunks),
        grid=(ch // ct,),
        in_specs=[pl.BlockSpec((ct, n), lambda c: (c, 0)),
                  full(cst["tw_c"]), full(cst["tw_s"]), full(cst["m_fwd"])],
        out_specs=pl.BlockSpec((None, nkj * ct, 2 * DFT_MINOR), lambda c: (c, 0, 0)),
        out_shape=jax.ShapeDtypeStruct((ch // ct, nkj * ct, 2 * DFT_MINOR), F32),
        scratch_shapes=[pltpu.VMEM((nkj * ct, 2 * DFT_MINOR), F32)],
        compiler_params=_params("arbitrary"),
        name="filter_spectrum",
    )(k2t, cst["tw_c"], cst["tw_s"], cst["m_fwd"])


def hyena_long_conv(vxt, h_spec, x0, vx_ctx, y_ctx, skip_col, skip_row, cst, layer, n_lat):
    bsz, ch, _ = vxt.shape
    l_all = x0.shape[1]
    n_ctx = l_all - n_lat
    ct = LANE
    nkj = cst["nkj"]
    n_chunks = _pick(nkj * ct // 8, (4, 5, 1))
    full = lambda a: pl.BlockSpec(a.shape, lambda b, c: (0,) * a.ndim)
    return pl.pallas_call(
        functools.partial(_long_conv_kernel, ct=ct, cst=cst, n_chunks=n_chunks, n_lat=n_lat),
        grid=(bsz, ch // ct),
        in_specs=[pl.BlockSpec((None, ct, n_lat), lambda b, c: (b, c, 0)),
                  pl.BlockSpec((None, nkj * ct, 2 * DFT_MINOR), lambda b, c: (c, 0, 0)),
                  full(cst["tw_c"]), full(cst["tw_s"]), full(cst["m_fwd"]), full(cst["m_inv"]),
                  pl.BlockSpec((None, l_all, ct), lambda b, c: (b, 0, c)),
                  pl.BlockSpec((None, n_ctx, ct), lambda b, c: (b, 0, c)),
                  pl.BlockSpec((None, n_ctx, ct), lambda b, c: (b, 0, c)),
                  pl.BlockSpec((None, ct, 1), lambda b, c: (layer, c, 0)),
                  pl.BlockSpec((None, 1, ct), lambda b, c: (layer, 0, c))],
        out_specs=pl.BlockSpec((None, l_all, ct), lambda b, c: (b, 0, c)),
        out_shape=jax.ShapeDtypeStruct((bsz, l_all, ch), MXU_DT),
        scratch_shapes=[pltpu.VMEM((nkj * ct, 2 * DFT_MINOR), F32), pltpu.VMEM((ct, n_lat), F32)],
        compiler_params=_params("arbitrary", "arbitrary"),
        name="hyena_long_conv",
    )(vxt, h_spec, cst["tw_c"], cst["tw_s"], cst["m_fwd"], cst["m_inv"], x0, vx_ctx, y_ctx, skip_col, skip_row)


def _ctx_conv_kernel(v_ref, k_ref, fd_ref, fk_ref, gi_ref, o_ref, *, kp):
    x = _dot_hi(fd_ref[...], v_ref[...])
    h = _dot_hi(fk_ref[...], k_ref[...])
    xre, xim = x[:kp], x[kp:]
    hre, him = h[:kp], h[kp:]
    y = jnp.concatenate([xre * hre - xim * him, xre * him + xim * hre], axis=0)
    o_ref[...] = _dot_hi(gi_ref[...], y)


def _ctx_tables(n_ctx):
    n = 2 * n_ctx
    kh = n_ctx + 1
    kp = ((kh + 7) // 8) * 8
    k = np.arange(kp)[:, None]
    live = (k < kh).astype(np.float64)
    ang = 2.0 * np.pi * k * np.arange(n)[None, :] / n
    ck, sk = np.cos(ang) * live, np.sin(ang) * live
    fk = np.concatenate([ck, -sk], axis=0)
    fd = fk[:, :n_ctx]
    wk = np.where((k == 0) | (k == n_ctx), 1.0, 2.0) * live / n
    gi = np.concatenate([(ck * wk).T, (-sk * wk).T], axis=1)[:n_ctx]
    as32 = lambda a: jnp.asarray(a, dtype=F32)
    return dict(kp=kp, fd=as32(fd), fk=as32(fk), gi=as32(gi))


def long_conv_ctx(vx_ctx, k2c, tb):
    bsz, n_ctx, ch = vx_ctx.shape
    kp = tb["kp"]
    full = lambda a: pl.BlockSpec(a.shape, lambda b, c: (0,) * a.ndim)
    return pl.pallas_call(
        functools.partial(_ctx_conv_kernel, kp=kp),
        grid=(bsz, ch // LANE),
        in_specs=[pl.BlockSpec((None, n_ctx, LANE), lambda b, c: (b, 0, c)),
                  pl.BlockSpec((2 * n_ctx, LANE), lambda b, c: (0, c)),
                  full(tb["fd"]), full(tb["fk"]), full(tb["gi"])],
        out_specs=pl.BlockSpec((None, n_ctx, LANE), lambda b, c: (b, 0, c)),
        out_shape=jax.ShapeDtypeStruct((bsz, n_ctx, ch), F32),
        compiler_params=_params("arbitrary", "arbitrary"),
        name="ctx_long_conv",
    )(vx_ctx, k2c, tb["fd"], tb["fk"], tb["gi"])


def _split3(a):
    a1 = a.astype(MXU_DT)
    r = a - a1.astype(F32)
    a2 = r.astype(MXU_DT)
    a3 = (r - a2.astype(F32)).astype(MXU_DT)
    return a1, a2, a3


def _ssd_chunk(backward, xs_ref, b_ref, c_ref, dt_ref, alog, ex, y_ref, state_ref):
    n = MB_CHUNK
    li = lax.broadcasted_iota(jnp.int32, (n, n), 0)
    si = lax.broadcasted_iota(jnp.int32, (n, n), 1)
    mask = (li <= si) if backward else (li >= si)
    tri = mask.astype(MXU_DT)
    lane = lax.broadcasted_iota(jnp.int32, (n, LANE), 1)

    dt = dt_ref[...]
    da = dt * (-jnp.exp(alog))
    cum = sum(_dot(tri, t) for t in _split3(da))
    cum_t = cum.T
    both = jnp.concatenate([cum, dt], axis=0)
    both_e = sum(_dot(t, ex) for t in _split3(both))
    cum_e = both_e[:n]
    dt_e = both_e[n:]
    total_e = cum_e[0:1] if backward else cum_e[n - 1:n]

    xdt = xs_ref[...] * dt_e
    xdt_b = xdt.astype(MXU_DT)
    w_b = (jnp.exp(total_e - cum_e) * xdt).astype(MXU_DT)
    st = state_ref[...]
    y_parts = []
    s_parts = []
    gw = MB_STATE
    hw = (MB_HEADS // MB_GROUPS) * MB_HEADDIM
    for g in range(MB_GROUPS):
        cg = c_ref[:, g * gw:(g + 1) * gw].astype(MXU_DT)
        bg = b_ref[:, g * gw:(g + 1) * gw]
        cb = _dot_nt(cg, bg.astype(MXU_DT))
        y_off = _dot(cg, st[:, g * hw:(g + 1) * hw].astype(MXU_DT))
        s_parts.append(_dot(bg.T.astype(MXU_DT), w_b[:, g * hw:(g + 1) * hw]))
        diag = []
        for j in range(hw // LANE):
            lo = g * hw + j * LANE
            xp = xdt_b[:, lo:lo + LANE]
            pair = []
            for e in range(2):
                hd = (lo // MB_HEADDIM) + e
                seg = cum[:, hd:hd + 1] - cum_t[hd:hd + 1, :]
                decay = jnp.exp(jnp.where(mask, seg, -jnp.inf))
                pair.append(_dot((cb * decay).astype(MXU_DT), xp))
            diag.append(jnp.where(lane < MB_HEADDIM, pair[0], pair[1]))
        y_parts.append(jnp.concatenate(diag, axis=1) + y_off * jnp.exp(cum_e[:, g * hw:(g + 1) * hw]))
    y_ref[...] = jnp.concatenate(y_parts, axis=1)
    state_ref[...] = st * jnp.exp(total_e) + jnp.concatenate(s_parts, axis=1)


def _ssd_kernel(xf_ref, bf_ref, cf_ref, dtf_ref, xb_ref, bb_ref, cb_ref, dtb_ref, alog_ref, ex_ref,
                yf_ref, yb_ref, state_ref):
    @pl.when(pl.program_id(1) == 0)
    def _():
        state_ref[...] = jnp.zeros_like(state_ref)

    ex = ex_ref[...]
    for e in range(xf_ref.shape[0]):
        _ssd_chunk(False, xf_ref.at[e], bf_ref.at[e], cf_ref.at[e], dtf_ref.at[e], alog_ref[0], ex,
                   yf_ref.at[e], state_ref.at[2 * e])
        _ssd_chunk(True, xb_ref.at[e], bb_ref.at[e], cb_ref.at[e], dtb_ref.at[e], alog_ref[1], ex,
                   yb_ref.at[e], state_ref.at[2 * e + 1])


def ssd_scan(xbc, dt2, a_log_rows, expand, layer, n_lat):
    bsz, l_all, _ = xbc.shape
    nc = l_all // MB_CHUNK
    ncl = n_lat // MB_CHUNK
    fwd = lambda c: (c + ncl) % nc
    bwd = lambda c: nc - 1 - c

    nb = 2 if bsz % 2 == 0 else 1

    def specs(blk, direction):
        return [pl.BlockSpec((nb, MB_CHUNK, MB_INNER), lambda b, c: (b, blk(c), 0)),
                pl.BlockSpec((nb, MB_CHUNK, 256), lambda b, c: (b, blk(c), 2)),
                pl.BlockSpec((nb, MB_CHUNK, 256), lambda b, c: (b, blk(c), 3)),
                pl.BlockSpec((None, nb, MB_CHUNK, LANE), lambda b, c: (direction, b, blk(c), 0))]

    y_shape = jax.ShapeDtypeStruct((bsz, l_all, MB_INNER), F32)
    return pl.pallas_call(
        _ssd_kernel,
        grid=(bsz // nb, nc),
        in_specs=specs(fwd, 0) + specs(bwd, 1)
        + [pl.BlockSpec((None, 2, 1, LANE), lambda b, c: (layer, 0, 0, 0)),
           pl.BlockSpec((LANE, MB_INNER), lambda b, c: (0, 0))],
        out_specs=[pl.BlockSpec((nb, MB_CHUNK, MB_INNER), lambda b, c: (b, fwd(c), 0)),
                   pl.BlockSpec((nb, MB_CHUNK, MB_INNER), lambda b, c: (b, bwd(c), 0))],
        out_shape=[y_shape, y_shape],
        scratch_shapes=[pltpu.VMEM((2 * nb, MB_STATE, MB_INNER), F32)],
        compiler_params=_params("arbitrary", "arbitrary"),
        name="ssd_scan",
    )(xbc, xbc, xbc, dt2, xbc, xbc, xbc, dt2, a_log_rows, expand)


def _mb_finish_kernel(yf_ref, yb_ref, xs_ref, z_ref, dsk_ref, g_ref, o_ref):
    y = yf_ref[...] + yb_ref[...] + xs_ref[...] * dsk_ref[...]
    y = y * _silu(z_ref[...].astype(F32))
    r = lax.rsqrt(jnp.mean(y * y, axis=-1, keepdims=True) + RMS_EPS)
    o_ref[...] = (y * r * g_ref[...]).astype(o_ref.dtype)


def mamba_finish(y_f, y_b, xbc, p, d_skip_e, norm_g3, layer):
    bsz, l_all, _ = xbc.shape
    tr = _pick(l_all, (544, 384, 128))
    w = MB_INNER
    return pl.pallas_call(
        _mb_finish_kernel,
        grid=(bsz, l_all // tr),
        in_specs=[pl.BlockSpec((None, tr, w), lambda b, i: (b, i, 0)),
                  pl.BlockSpec((None, tr, w), lambda b, i: (b, i, 0)),
                  pl.BlockSpec((None, tr, w), lambda b, i: (b, i, 0)),
                  pl.BlockSpec((None, tr, w), lambda b, i: (b, i, P_Z // w)),
                  pl.BlockSpec((None, 1, w), lambda b, i: (layer, 0, 0)),
                  pl.BlockSpec((None, 1, w), lambda b, i: (layer, 0, 0))],
        out_specs=pl.BlockSpec((None, tr, w), lambda b, i: (b, i, 0)),
        out_shape=jax.ShapeDtypeStruct((bsz, l_all, w), MXU_DT),
        compiler_params=_params("arbitrary", "arbitrary"),
        name="mamba_finish",
    )(y_f, y_b, xbc, p, d_skip_e, norm_g3)


_W_IN_MOVES = (
    (0, 0, MLA_Q_RANK),
    (512, MLA_Q_RANK, MLA_KV_RANK),
    (640, MLA_Q_RANK + MLA_KV_RANK, MLA_ROPE),
    (P_DT, MLA_COLS + GQA_COLS + HY_COLS + MB_INNER + MB_CONV_CH, 2 * MB_HEADS),
    (P_GQA, MLA_COLS, GQA_COLS),
    (P_Z, MLA_COLS + GQA_COLS + HY_COLS, MB_INNER),
    (P_HY, MLA_COLS + GQA_COLS, HY_COLS),
    (P_XBC, MLA_COLS + GQA_COLS + HY_COLS + MB_INNER, MB_CONV_CH),
)


def _w_in_prep_kernel(w_ref, o_ref):
    o_ref[...] = jnp.zeros_like(o_ref)
    for dst, src, n in _W_IN_MOVES:
        o_ref[:, dst:dst + n] = w_ref[:, src:src + n].astype(o_ref.dtype)


def _prep_w_in(w_in):
    nl, d, cols = w_in.shape
    tr = 256
    return pl.pallas_call(
        _w_in_prep_kernel,
        grid=(nl, d // tr),
        in_specs=[pl.BlockSpec((None, tr, cols), lambda l, i: (l, i, 0))],
        out_specs=pl.BlockSpec((None, tr, P_COLS), lambda l, i: (l, i, 0)),
        out_shape=jax.ShapeDtypeStruct((nl, d, P_COLS), MXU_DT),
        compiler_params=_params("arbitrary", "arbitrary"),
        name="w_in_layout",
    )(w_in)


def _prep_w_uq(w_uq):
    nl = w_uq.shape[0]
    w = w_uq.reshape(nl, MLA_Q_RANK, MLA_HEADS, MLA_NOPE + MLA_ROPE)
    w = _pad_to(w, (nl, 512, MLA_HEADS, 256))
    return w.reshape(nl, 512, MLA_HEADS * 256).astype(MXU_DT)


def _rope_tables(n_lat, n_ctx, rot_dim):
    rows = n_lat // GRID_W
    row = jnp.repeat(jnp.arange(rows, dtype=F32), GRID_W)
    col = jnp.tile(jnp.arange(GRID_W, dtype=F32), rows)
    n_freq = rot_dim // 4
    inv_freq = ROPE_THETA ** (-jnp.arange(n_freq, dtype=F32) / n_freq)
    ang = jnp.concatenate([row[:, None] * inv_freq, col[:, None] * inv_freq], axis=-1)
    half = rot_dim // 2
    cos = jnp.concatenate([jnp.cos(ang), jnp.ones((n_ctx, half), F32)], axis=0)
    sin = jnp.concatenate([jnp.sin(ang), jnp.zeros((n_ctx, half), F32)], axis=0)
    cos_t = jnp.repeat(cos, 2, axis=-1)
    sin_t = jnp.stack([-sin, sin], axis=-1).reshape(n_lat + n_ctx, rot_dim)
    shape = (n_lat + n_ctx, LANE)
    return _pad_to(cos_t, shape), _pad_to(sin_t, shape)


def _filter_inputs(n):
    t = jnp.linspace(0.0, 1.0, n, dtype=F32)[:, None]
    omega = 2.0 * math.pi * jnp.arange(n, dtype=F32) / n
    bands = jnp.linspace(1e-4, HY_BANDS - 1, HY_BANDS, dtype=F32)
    ang = omega[:, None] * bands[None, :]
    feats = jnp.concatenate([t, jnp.cos(ang), -jnp.sin(ang)], axis=-1)
    zero = jnp.zeros((1, HY_EMB), F32)
    feats2 = jnp.concatenate([feats, zero, feats[1:][::-1]], axis=0)
    feats2 = jnp.concatenate([feats2, jnp.zeros((2 * n, LANE - HY_EMB), F32)], axis=-1)
    t2 = jnp.concatenate([t, jnp.zeros((1, 1), F32), t[1:][::-1]], axis=0)
    idx = jnp.arange(2 * n)[:, None]
    aux = jnp.concatenate([t2, (idx < n).astype(F32), (idx != n).astype(F32),
                           jnp.zeros((2 * n, LANE - 3), F32)], axis=-1)
    return feats2, aux


def _pad_to(a, shape):
    pads = [(0, s - d) for d, s in zip(a.shape, shape)]
    return jnp.pad(a, pads)


def kernel(x, c, ctx, c_ctx, w_ada, b_ada, w_in, mla_q_norm, mla_kv_norm, mla_w_uq, mla_w_ukv, gqa_q_norm, gqa_k_norm, hy_conv_w, hy_conv_b, hy_w1, hy_b1, hy_w2, hy_b2, hy_w3, hy_freq, hy_skip, mb_conv_w, mb_conv_b, mb_a_log, mb_dt_bias, mb_d, mb_norm, w_mgate, b_mgate, w_branch, w_out, ln1_g, ln1_b, w_ffn_in, w_ffn_out, ln2_g, ln2_b):
    bsz, n_lat, d = x.shape
    n_ctx = ctx.shape[1]
    nl = w_in.shape[0]
    assert d == D_MODEL and bsz < 8 and n_lat % n_ctx == 0 and n_ctx % MB_CHUNK == 0

    w_in_p = _prep_w_in(w_in)
    w_uq_p = _prep_w_uq(mla_w_uq)
    w_ukv_p = mla_w_ukv.astype(MXU_DT)
    gq_mla = _pad_to(mla_q_norm, (nl, 512))[:, None, :]
    gkv_mla = mla_kv_norm[:, None, :]
    gq_gqa = gqa_q_norm[:, None, :]
    gk_gqa = gqa_k_norm[:, None, :]
    w_gate_b = w_mgate.astype(MXU_DT)
    w_branch_b = w_branch.astype(MXU_DT)
    w_out_b = w_out.astype(MXU_DT)
    w_ffn_in_b = w_ffn_in.astype(MXU_DT)
    w_ffn_out_b = w_ffn_out.astype(MXU_DT)
    b_gate4 = b_mgate[:, :, None, :]
    b_ada3 = b_ada[:, None, :]
    hy_conv_b3 = hy_conv_b[:, None, :]
    mb_conv_b3 = mb_conv_b[:, None, :]
    hy_skip3 = hy_skip[:, None, :]
    hy_skip_col = hy_skip[:, :, None]
    ln1_g3, ln1_b3, ln2_g3, ln2_b3 = (a[:, None, :] for a in (ln1_g, ln1_b, ln2_g, ln2_b))
    hy_w1p = _pad_to(hy_w1, (nl, LANE, LANE))
    hy_b1p = _pad_to(hy_b1, (nl, LANE))[:, None, :]
    hy_w2p = _pad_to(hy_w2, (nl, LANE, LANE))
    hy_b2p = _pad_to(hy_b2, (nl, LANE))[:, None, :]
    hy_w3p = _pad_to(hy_w3, (nl, LANE, 2 * HY_W))
    hy_frp = _pad_to(hy_freq, (nl, LANE))[:, None, :]
    deltas = jnp.abs(jnp.linspace(HY_MIN_DECAY, HY_MAX_DECAY, HY_W, dtype=F32))[None, :]
    dt_bias_row = _pad_to(mb_dt_bias.reshape(nl, 2 * MB_HEADS), (nl, LANE))[:, None, :]
    a_log_rows = _pad_to(mb_a_log, (nl, 2, LANE))[:, :, None, :]
    d_skip_e = jnp.repeat(mb_d, MB_HEADDIM, axis=-1)[:, None, :]
    mb_norm3 = mb_norm[:, None, :]
    expand = jnp.asarray(np.kron(np.eye(LANE, MB_HEADS), np.ones((1, MB_HEADDIM))), dtype=MXU_DT)

    cos_m, sin_m = _rope_tables(n_lat, n_ctx, MLA_ROPE)
    cos_g, sin_g = _rope_tables(n_lat, n_ctx, GQA_HD)
    feats_lat, aux_lat = _filter_inputs(n_lat)
    feats_ctx, aux_ctx = _filter_inputs(n_ctx)
    dft_lat = _dft_consts(n_lat)
    tb_ctx = _ctx_tables(n_ctx)

    c8 = jnp.concatenate([c, c_ctx[None, :], jnp.zeros((8 - bsz - 1, d), F32)], axis=0)
    xz = jnp.concatenate([x, ctx], axis=1)

    for l in range(nl):
        last = l == nl - 1
        mod = ada_mod(c8, w_ada, b_ada3, l).reshape(8, 1, 6 * d)
        p, p_dt, h = in_projection(xz, mod, w_in_p, l, n_lat)

        q, k, v = mla_prep(p, gq_mla, gkv_mla, w_uq_p, w_ukv_p, cos_m, sin_m, l)
        oa = attention(q, k, v, n_lat)
        q, k, v = gqa_prep(p, gq_gqa, gk_gqa, cos_g, sin_g, l)
        ob = attention(q, k, v, n_lat)
        x0, vxt, vx_ctx = hyena_conv(p, hy_conv_w, hy_conv_b3, l, n_lat)
        k2t = hyena_filter(feats_lat, aux_lat, hy_w1p, hy_b1p, hy_w2p, hy_b2p, hy_w3p, hy_frp, deltas, l, True)
        h_spec = filter_spectrum(k2t, dft_lat)
        if last:
            y_ctx = jnp.zeros((bsz, n_ctx, HY_W), F32)
        else:
            k2c = hyena_filter(feats_ctx, aux_ctx, hy_w1p, hy_b1p, hy_w2p, hy_b2p, hy_w3p, hy_frp, deltas, l,
                               False)
            y_ctx = long_conv_ctx(vx_ctx, k2c, tb_ctx)
        oc = hyena_long_conv(vxt, h_spec, x0, vx_ctx, y_ctx, hy_skip_col, hy_skip3, dft_lat, l, n_lat)
        xbc = mamba_conv(p, mb_conv_w, mb_conv_b3, l, n_lat)
        dt2 = mamba_dt(p_dt, dt_bias_row, l)
        y_f, y_b = ssd_scan(xbc, dt2, a_log_rows, expand, l, n_lat)
        od = mamba_finish(y_f, y_b, xbc, p, d_skip_e, mb_norm3, l)

        acc = merge_branches(h, (oa, ob, oc, od), w_gate_b, b_gate4, w_branch_b, l)
        x1 = matmul_res_ln(acc, w_out_b, xz, mod, 2, ln1_g3, ln1_b3, l, n_lat, n_lat + n_ctx,
                           (544, 384, 128), d)
        act = ffn_in(x1, mod, w_ffn_in_b, l, n_lat)
        out_rows = n_lat if last else n_lat + n_ctx
        xz = matmul_res_ln(act, w_ffn_out_b, x1, mod, 5, ln2_g3, ln2_b3, l, n_lat, out_rows,
                           (256, 128), act.shape[2])
    return xz
```

```python
import functools
import math

import jax
import jax.numpy as jnp
import numpy as np
from jax import lax
from jax.experimental import pallas as pl
from jax.experimental.pallas import tpu as pltpu

F32 = jnp.float32
MXU_DT = jnp.bfloat16

D_MODEL = 2048
DEPTH = 2
GRID_W = 64
N_BRANCH = 4
BRANCH_W = D_MODEL // N_BRANCH
ROPE_THETA = 10000.0
LN_EPS = 1e-6
RMS_EPS = 1e-6
DEEPNORM_ALPHA = (2 * DEPTH) ** 0.25

MLA_HEADS = 4
MLA_Q_RANK = 448
MLA_KV_RANK = 128
MLA_NOPE = 128
MLA_ROPE = 64
MLA_V = 128
MLA_COLS = MLA_Q_RANK + MLA_KV_RANK + MLA_ROPE

GQA_HEADS = 4
GQA_KV_HEADS = 2
GQA_HD = 128
GQA_COLS = (GQA_HEADS + 2 * GQA_KV_HEADS) * GQA_HD

HY_W = BRANCH_W
HY_EMB = 33
HY_BANDS = (HY_EMB - 1) // 2
HY_FFN = 64
HY_MIN_DECAY = math.log(1e-2) / 1.5
HY_MAX_DECAY = math.log(1e-2) / 0.3
HY_COLS = 3 * HY_W

MB_INNER = BRANCH_W
MB_HEADDIM = 64
MB_HEADS = 8
MB_GROUPS = 2
MB_STATE = 128
MB_CHUNK = 128
MB_CONV_CH = MB_INNER + 2 * MB_GROUPS * MB_STATE
MB_COLS = MB_INNER + MB_CONV_CH + 2 * MB_HEADS

FFN_HIDDEN = 5632

LANE = 128
VMEM_LIMIT = 56 * 1024 * 1024

P_MLA = 0
P_MLA_W = 768
P_DT = 896
P_GQA = 1024
P_Z = 2048
P_HY = 2560
P_XBC = 4096
P_COLS = 5120

LOG2_E = math.log2(math.e)
_HI = lax.Precision.HIGHEST


def _dot(a, b):
    return jnp.dot(a, b, preferred_element_type=F32)


def _dot_hi(a, b):
    return jnp.dot(a, b, precision=_HI, preferred_element_type=F32)


def _dot_nt(a, b):
    return lax.dot_general(a, b, (((1,), (1,)), ((), ())), preferred_element_type=F32)


def _sigmoid(x):
    return 1.0 / (1.0 + jnp.exp(-x))


def _silu(x):
    return x * _sigmoid(x)


def _params(*sem):
    return pltpu.CompilerParams(dimension_semantics=sem, vmem_limit_bytes=VMEM_LIMIT)


def _pick(n, prefs):
    for p in prefs:
        if n % p == 0:
            return p
    raise ValueError(f"no tile for {n} in {prefs}")


def _standardize(x):
    mu = jnp.mean(x, axis=-1, keepdims=True)
    xc = x - mu
    var = jnp.mean(xc * xc, axis=-1, keepdims=True)
    return xc * lax.rsqrt(var + LN_EPS)


def _is_ctx_rows(tile_idx, tm, n_lat):
    row = tile_idx * tm + lax.broadcasted_iota(jnp.int32, (tm, 1), 0)
    return row >= n_lat


def _ada_kernel(c_ref, w_ref, b_ref, o_ref):
    cs = _silu(c_ref[...])
    o_ref[...] = _dot(cs.astype(MXU_DT), w_ref[...].astype(MXU_DT)) + b_ref[...]


def ada_mod(c8, w_ada, b_ada3, layer):
    d = c8.shape[1]
    n = w_ada.shape[2]
    tn = 1024
    return pl.pallas_call(
        _ada_kernel,
        grid=(n // tn,),
        in_specs=[pl.BlockSpec((8, d), lambda j: (0, 0)),
                  pl.BlockSpec((None, d, tn), lambda j: (layer, 0, j)),
                  pl.BlockSpec((None, 1, tn), lambda j: (layer, 0, j))],
        out_specs=pl.BlockSpec((8, tn), lambda j: (0, j)),
        out_shape=jax.ShapeDtypeStruct((8, n), F32),
        compiler_params=_params("arbitrary"),
        name="ada_mod",
    )(c8, w_ada, b_ada3)


ROW_CHUNKS = 4


def _modulated(x, sh_ref, sc_ref, shc_ref, scc_ref, row0, n_lat):
    xn = _standardize(x)
    row = row0 + lax.broadcasted_iota(jnp.int32, (x.shape[0], 1), 0)
    is_ctx = row >= n_lat
    scale = jnp.where(is_ctx, scc_ref[...], sc_ref[...])
    shift = jnp.where(is_ctx, shc_ref[...], sh_ref[...])
    return (xn * (1.0 + scale) + shift).astype(MXU_DT)


def _inproj_kernel(x_ref, sh_ref, sc_ref, shc_ref, scc_ref, w_ref, p_ref, pdt_ref, h_ref, *, tm, n_lat):
    j = pl.program_id(2)
    rc = tm // ROW_CHUNKS

    @pl.when(j == 0)
    def _():
        for r in range(ROW_CHUNKS):
            rows = slice(r * rc, (r + 1) * rc)
            hb = _modulated(x_ref[rows, :], sh_ref, sc_ref, shc_ref, scc_ref,
                            pl.program_id(1) * tm + r * rc, n_lat)
            h_ref[rows, :] = hb
            acc = _dot(hb, w_ref[...])
            p_ref[rows, :] = acc.astype(p_ref.dtype)
            pdt_ref[rows, :] = acc[:, P_DT:P_DT + LANE]

    @pl.when(j > 0)
    def _():
        p_ref[...] = _dot(h_ref[...], w_ref[...]).astype(p_ref.dtype)


def _ffn_in_kernel(x_ref, sh_ref, sc_ref, shc_ref, scc_ref, wu_ref, wg_ref, a_ref, h_ref, *, tm, n_lat):
    j = pl.program_id(2)
    rc = tm // ROW_CHUNKS

    def swiglu(h):
        up = _dot(h, wu_ref[...])
        gate = _dot(h, wg_ref[...])
        return (_silu(gate) * up).astype(a_ref.dtype)

    @pl.when(j == 0)
    def _():
        for r in range(ROW_CHUNKS):
            rows = slice(r * rc, (r + 1) * rc)
            hb = _modulated(x_ref[rows, :], sh_ref, sc_ref, shc_ref, scc_ref,
                            pl.program_id(1) * tm + r * rc, n_lat)
            h_ref[rows, :] = hb
            a_ref[rows, :] = swiglu(hb)

    @pl.when(j > 0)
    def _():
        chunks = [slice(r * rc, (r + 1) * rc) for r in range(ROW_CHUNKS)]
        pending = None
        for rows in chunks + [None]:
            nxt = None
            if rows is not None:
                h = h_ref[rows, :]
                nxt = (rows, _dot(h, wu_ref[...]), _dot(h, wg_ref[...]))
            if pending is not None:
                prow, up, gate = pending
                a_ref[prow, :] = (_silu(gate) * up).astype(a_ref.dtype)
            pending = nxt


def _mod_specs(k_shift, k_scale, n_batch):
    d = D_MODEL
    return [pl.BlockSpec((None, 1, d), lambda b, i, j: (b, 0, k_shift)),
            pl.BlockSpec((None, 1, d), lambda b, i, j: (b, 0, k_scale)),
            pl.BlockSpec((None, 1, d), lambda b, i, j: (n_batch, 0, k_shift)),
            pl.BlockSpec((None, 1, d), lambda b, i, j: (n_batch, 0, k_scale))]


def in_projection(xz, mod, w_in_p, layer, n_lat):
    bsz, l_all, d = xz.shape
    n = w_in_p.shape[2]
    tm = _pick(l_all, (1088, 544, 384, 128))
    tn = 1024
    assert n % tn == 0 and P_DT + LANE <= tn
    kern = functools.partial(_inproj_kernel, tm=tm, n_lat=n_lat)
    return pl.pallas_call(
        kern,
        grid=(bsz, l_all // tm, n // tn),
        in_specs=[pl.BlockSpec((None, tm, d), lambda b, i, j: (b, i, 0))]
        + _mod_specs(0, 1, bsz)
        + [pl.BlockSpec((None, d, tn), lambda b, i, j: (layer, 0, j))],
        out_specs=[pl.BlockSpec((None, tm, tn), lambda b, i, j: (b, i, j)),
                   pl.BlockSpec((None, tm, LANE), lambda b, i, j: (b, i, 0)),
                   pl.BlockSpec((None, tm, d), lambda b, i, j: (b, i, 0))],
        out_shape=[jax.ShapeDtypeStruct((bsz, l_all, n), MXU_DT),
                   jax.ShapeDtypeStruct((bsz, l_all, LANE), F32),
                   jax.ShapeDtypeStruct((bsz, l_all, d), MXU_DT)],
        compiler_params=_params("arbitrary", "arbitrary", "arbitrary"),
        name="in_projection",
    )(xz, mod, mod, mod, mod, w_in_p)


def ffn_in(xz, mod, w_ffn_in, layer, n_lat):
    bsz, l_all, d = xz.shape
    hid = w_ffn_in.shape[2] // 2
    tm = _pick(l_all, (1088, 544, 384, 128))
    tn = 512
    nj = hid // tn
    kern = functools.partial(_ffn_in_kernel, tm=tm, n_lat=n_lat)
    return pl.pallas_call(
        kern,
        grid=(bsz, l_all // tm, nj),
        in_specs=[pl.BlockSpec((None, tm, d), lambda b, i, j: (b, i, 0))]
        + _mod_specs(3, 4, bsz)
        + [pl.BlockSpec((None, d, tn), lambda b, i, j: (layer, 0, j)),
           pl.BlockSpec((None, d, tn), lambda b, i, j: (layer, 0, j + nj))],
        out_specs=pl.BlockSpec((None, tm, tn), lambda b, i, j: (b, i, j)),
        out_shape=jax.ShapeDtypeStruct((bsz, l_all, hid), MXU_DT),
        scratch_shapes=[pltpu.VMEM((tm, d), MXU_DT)],
        compiler_params=_params("arbitrary", "arbitrary", "arbitrary"),
        name="ffn_in",
    )(xz, mod, mod, mod, mod, w_ffn_in, w_ffn_in)


def _resln_kernel(a_ref, w_ref, res_ref, g_ref, gc_ref, lng_ref, lnb_ref, o_ref, *, tm, n_lat, nk, rc):
    k = pl.program_id(2)
    i = pl.program_id(1)

    if nk > 1:
        @pl.when(k == 0)
        def _():
            o_ref[...] = _dot(a_ref[...], w_ref[...])

        @pl.when((k > 0) & (k < nk - 1))
        def _():
            o_ref[...] += _dot(a_ref[...], w_ref[...])

    @pl.when(k == nk - 1)
    def _():
        for r in range(tm // rc):
            rows = slice(r * rc, (r + 1) * rc)
            acc = _dot(a_ref[rows, :], w_ref[...])
            if nk > 1:
                acc = acc + o_ref[rows, :]
            row = i * tm + r * rc + lax.broadcasted_iota(jnp.int32, (rc, 1), 0)
            gate = jnp.where(row >= n_lat, gc_ref[...], g_ref[...])
            y = DEEPNORM_ALPHA * res_ref[rows, :] + gate * acc
            o_ref[rows, :] = _standardize(y) * lng_ref[...] + lnb_ref[...]


def matmul_res_ln(a, w, res, mod, k_gate, ln_g, ln_b, layer, n_lat, out_rows, tm_prefs, tk):
    bsz, l_all, kdim = a.shape
    d = w.shape[2]
    tm = _pick(l_all, tm_prefs)
    nk = kdim // tk
    rc = _pick(tm, (272, 192, 128))
    kern = functools.partial(_resln_kernel, tm=tm, n_lat=n_lat, nk=nk, rc=rc)
    w_mode = dict(pipeline_mode=pl.Buffered(1)) if nk == 1 else {}
    return pl.pallas_call(
        kern,
        grid=(bsz, pl.cdiv(out_rows, tm), nk),
        in_specs=[pl.BlockSpec((None, tm, tk), lambda b, i, k: (b, i, k)),
                  pl.BlockSpec((None, tk, d), lambda b, i, k: (layer, k, 0), **w_mode),
                  pl.BlockSpec((None, tm, d), lambda b, i, k: (b, i, 0)),
                  pl.BlockSpec((None, 1, d), lambda b, i, k: (b, 0, k_gate)),
                  pl.BlockSpec((None, 1, d), lambda b, i, k: (bsz, 0, k_gate)),
                  pl.BlockSpec((None, 1, d), lambda b, i, k: (layer, 0, 0)),
                  pl.BlockSpec((None, 1, d), lambda b, i, k: (layer, 0, 0))],
        out_specs=pl.BlockSpec((None, tm, d), lambda b, i, k: (b, i, 0)),
        out_shape=jax.ShapeDtypeStruct((bsz, out_rows, d), F32),
        compiler_params=_params("arbitrary", "arbitrary", "arbitrary"),
        name="matmul_res_ln",
    )(a, w, res, mod, mod, ln_g, ln_b)


def _merge_kernel(h_ref, oa_ref, ob_ref, oc_ref, od_ref, wg_ref, bg_ref, wb_ref, o_ref):
    h = h_ref[...]
    acc = None
    pending = None
    for i, o_r in list(enumerate((oa_ref, ob_ref, oc_ref, od_ref))) + [(None, None)]:
        nxt = None
        if o_r is not None:
            nxt = (_dot(h, wg_ref[i]) + bg_ref[i], _dot(o_r[...], wb_ref[i]))
        if pending is not None:
            g, t = pending
            term = _sigmoid(g) * t
            acc = term if acc is None else acc + term
        pending = nxt
    o_ref[...] = acc.astype(o_ref.dtype)


def merge_branches(h, outs, w_gate, b_gate4, w_branch, layer):
    bsz, l_all, d = h.shape
    bw = outs[0].shape[2]
    tm = _pick(l_all, (1088, 544, 384, 128))
    tn = 512
    o_spec = pl.BlockSpec((None, tm, bw), lambda b, i, j: (b, i, 0))
    return pl.pallas_call(
        _merge_kernel,
        grid=(bsz, l_all // tm, d // tn),
        in_specs=[pl.BlockSpec((None, tm, d), lambda b, i, j: (b, i, 0)), o_spec, o_spec, o_spec, o_spec,
                  pl.BlockSpec((None, N_BRANCH, d, tn), lambda b, i, j: (layer, 0, 0, j)),
                  pl.BlockSpec((None, N_BRANCH, 1, tn), lambda b, i, j: (layer, 0, 0, j)),
                  pl.BlockSpec((None, N_BRANCH, bw, tn), lambda b, i, j: (layer, 0, 0, j))],
        out_specs=pl.BlockSpec((None, tm, tn), lambda b, i, j: (b, i, j)),
        out_shape=jax.ShapeDtypeStruct((bsz, l_all, d), MXU_DT),
        compiler_params=_params("arbitrary", "arbitrary", "arbitrary"),
        name="merge_branches",
    )(h, *outs, w_gate, b_gate4, w_branch)


def _rope(x, cos, sin):
    lane = lax.broadcasted_iota(jnp.int32, x.shape, 1)
    partner = jnp.where(lane % 2 == 0, pltpu.roll(x, LANE - 1, 1), pltpu.roll(x, 1, 1))
    return x * cos + partner * sin


def _mla_prep_kernel(p_ref, gq_ref, gkv_ref, wuq_ref, wukv_ref, cos_ref, sin_ref, q_ref, k_ref, v_ref,
                     vrow_ref, *, rc):
    scale = (MLA_NOPE + MLA_ROPE) ** -0.5 * LOG2_E

    def body(t, carry):
        rows = pl.ds(pl.multiple_of(t * rc, 16), rc)
        p = p_ref[rows, :].astype(F32)
        cq = p[:, 0:512]
        ckv = p[:, 512:640]
        k_rot = p[:, 640:768]
        rq = lax.rsqrt(jnp.sum(cq * cq, axis=-1, keepdims=True) * (1.0 / MLA_Q_RANK) + RMS_EPS)
        cqn = (cq * rq * gq_ref[...]).astype(MXU_DT)
        rkv = lax.rsqrt(jnp.mean(ckv * ckv, axis=-1, keepdims=True) + RMS_EPS)
        ckvn = (ckv * rkv * gkv_ref[...]).astype(MXU_DT)
        qf = _dot(cqn, wuq_ref[...])
        kvf = _dot(ckvn, wukv_ref[...])
        cos = cos_ref[rows, :]
        sin = sin_ref[rows, :]
        k_rope = _rope(k_rot, cos, sin).astype(k_ref.dtype)
        for hd in range(MLA_HEADS):
            qb = hd * 256
            q_ref[hd, rows, 0:128] = (qf[:, qb:qb + 128] * scale).astype(q_ref.dtype)
            q_rope = _rope(qf[:, qb + 128:qb + 256], cos, sin)
            q_ref[hd, rows, 128:256] = (q_rope * scale).astype(q_ref.dtype)
            k_ref[hd, rows, 0:128] = kvf[:, hd * 256:hd * 256 + 128].astype(k_ref.dtype)
            k_ref[hd, rows, 128:256] = k_rope
            vrow_ref[hd, rows, :] = kvf[:, hd * 256 + 128:hd * 256 + 256]
        return carry

    lax.fori_loop(0, p_ref.shape[0] // rc, body, 0)
    for hd in range(MLA_HEADS):
        v_ref[hd] = vrow_ref[hd].T.astype(v_ref.dtype)


def mla_prep(p, gq, gkv, wuq, wukv, cos, sin, layer):
    bsz, l_all, _ = p.shape
    tr = _pick(l_all, (1088 * 2, 256, 128))
    rc = 272 if tr % 272 == 0 else tr
    h = MLA_HEADS
    return pl.pallas_call(
        functools.partial(_mla_prep_kernel, rc=rc),
        grid=(bsz, l_all // tr),
        in_specs=[pl.BlockSpec((None, tr, P_MLA_W), lambda b, i: (b, i, 0)),
                  pl.BlockSpec((None, 1, 512), lambda b, i: (layer, 0, 0)),
                  pl.BlockSpec((None, 1, 128), lambda b, i: (layer, 0, 0)),
                  pl.BlockSpec((None, 512, h * 256), lambda b, i: (layer, 0, 0)),
                  pl.BlockSpec((None, 128, h * 256), lambda b, i: (layer, 0, 0)),
                  pl.BlockSpec((tr, LANE), lambda b, i: (i, 0)),
                  pl.BlockSpec((tr, LANE), lambda b, i: (i, 0))],
        out_specs=[pl.BlockSpec((None, h, tr, 256), lambda b, i: (b, 0, i, 0)),
                   pl.BlockSpec((None, h, tr, 256), lambda b, i: (b, 0, i, 0)),
                   pl.BlockSpec((None, h, 128, tr), lambda b, i: (b, 0, 0, i))],
        out_shape=[jax.ShapeDtypeStruct((bsz, h, l_all, 256), MXU_DT),
                   jax.ShapeDtypeStruct((bsz, h, l_all, 256), MXU_DT),
                   jax.ShapeDtypeStruct((bsz, h, 128, l_all), MXU_DT)],
        scratch_shapes=[pltpu.VMEM((h, tr, 128), F32)],
        compiler_params=_params("arbitrary", "arbitrary"),
        name="mla_prep",
    )(p, gq, gkv, wuq, wukv, cos, sin)


def _gqa_prep_kernel(p_ref, gq_ref, gk_ref, cos_ref, sin_ref, q_ref, k_ref, v_ref):
    scale = GQA_HD ** -0.5 * LOG2_E
    cos = cos_ref[...]
    sin = sin_ref[...]

    def norm_rope(x, g):
        r = lax.rsqrt(jnp.mean(x * x, axis=-1, keepdims=True) + RMS_EPS)
        return _rope(x * r * g, cos, sin)

    for hd in range(GQA_HEADS):
        x = p_ref[:, hd * 128:(hd + 1) * 128].astype(F32)
        q_ref[hd] = (norm_rope(x, gq_ref[...]) * scale).astype(q_ref.dtype)
    for hd in range(GQA_KV_HEADS):
        x = p_ref[:, 512 + hd * 128:512 + (hd + 1) * 128].astype(F32)
        k_ref[hd] = norm_rope(x, gk_ref[...]).astype(k_ref.dtype)
        v = p_ref[:, 768 + hd * 128:768 + (hd + 1) * 128].astype(F32)
        v_ref[hd] = v.T.astype(v_ref.dtype)


def gqa_prep(p, gq, gk, cos, sin, layer):
    bsz, l_all, _ = p.shape
    tr = _pick(l_all, (1088 * 2, 256, 128))
    return pl.pallas_call(
        _gqa_prep_kernel,
        grid=(bsz, l_all // tr),
        in_specs=[pl.BlockSpec((None, tr, GQA_COLS), lambda b, i: (b, i, P_GQA // GQA_COLS)),
                  pl.BlockSpec((None, 1, 128), lambda b, i: (layer, 0, 0)),
                  pl.BlockSpec((None, 1, 128), lambda b, i: (layer, 0, 0)),
                  pl.BlockSpec((tr, LANE), lambda b, i: (i, 0)),
                  pl.BlockSpec((tr, LANE), lambda b, i: (i, 0))],
        out_specs=[pl.BlockSpec((None, GQA_HEADS, tr, 128), lambda b, i: (b, 0, i, 0)),
                   pl.BlockSpec((None, GQA_KV_HEADS, tr, 128), lambda b, i: (b, 0, i, 0)),
                   pl.BlockSpec((None, GQA_KV_HEADS, 128, tr), lambda b, i: (b, 0, 0, i))],
        out_shape=[jax.ShapeDtypeStruct((bsz, GQA_HEADS, l_all, 128), MXU_DT),
                   jax.ShapeDtypeStruct((bsz, GQA_KV_HEADS, l_all, 128), MXU_DT),
                   jax.ShapeDtypeStruct((bsz, GQA_KV_HEADS, 128, l_all), MXU_DT)],
        compiler_params=_params("arbitrary", "arbitrary"),
        name="gqa_prep",
    )(p, gq, gk, cos, sin)


def _attn_kernel(q_ref, k_ref, v_ref, o_ref, *, n_lat, n_lat_tiles, sub):
    i = pl.program_id(2)

    def attend_all(k, vt):
        groups = [slice(r, r + sub) for r in range(0, q_ref.shape[0], sub)]
        sts = [_dot_nt(k, q_ref[g, :]) for g in groups]
        es, ls = [], []
        for st in sts:
            m = jnp.max(st, axis=0, keepdims=True)
            e = jnp.exp2(st - m)
            ls.append(jnp.sum(e, axis=0, keepdims=True))
            es.append(e.astype(vt.dtype))
        for g, e, l in zip(groups, es, ls):
            ot = _dot(vt, e) / l
            o_ref[g, :] = ot.T.astype(o_ref.dtype)

    @pl.when(i < n_lat_tiles)
    def _():
        attend_all(k_ref[...], v_ref[...])

    @pl.when(i >= n_lat_tiles)
    def _():
        attend_all(k_ref[n_lat:, :], v_ref[:, n_lat:])


def attention(q, k, v, n_lat):
    bsz, h, l_all, dk = q.shape
    hkv = k.shape[1]
    grp = h // hkv
    dv = v.shape[2]
    tq = _pick(n_lat, (1024, 512, 256, 128))
    assert l_all - n_lat <= tq
    kern = functools.partial(_attn_kernel, n_lat=n_lat, n_lat_tiles=n_lat // tq, sub=min(256, tq))
    return pl.pallas_call(
        kern,
        grid=(bsz, h, pl.cdiv(l_all, tq)),
        in_specs=[pl.BlockSpec((None, None, tq, dk), lambda b, hh, i: (b, hh, i, 0)),
                  pl.BlockSpec((None, None, l_all, dk), lambda b, hh, i: (b, hh // grp, 0, 0)),
                  pl.BlockSpec((None, None, dv, l_all), lambda b, hh, i: (b, hh // grp, 0, 0))],
        out_specs=pl.BlockSpec((None, tq, dv), lambda b, hh, i: (b, i, hh)),
        out_shape=jax.ShapeDtypeStruct((bsz, l_all, h * dv), MXU_DT),
        compiler_params=_params("arbitrary", "arbitrary", "arbitrary"),
        name="attention",
    )(q, k, v)


CONV_TC = 256


def _conv3(u, w, b, n_lat):
    u = u.astype(F32)
    n = u.shape[0]
    row = lax.broadcasted_iota(jnp.int32, (n, 1), 0)
    prev = jnp.where((row == 0) | (row == n_lat), 0.0, pltpu.roll(u, 1, 0))
    nxt = jnp.where((row == n_lat - 1) | (row == n - 1), 0.0, pltpu.roll(u, n - 1, 0))
    return w[0:1, :] * prev + w[1:2, :] * u + w[2:3, :] * nxt + b


def _hy_conv_kernel(p0_ref, p1_ref, pv_ref, w0_ref, w1_ref, wv_ref, b0_ref, b1_ref, bv_ref,
                    x0_ref, vxt_ref, vxc_ref, *, n_lat):
    x0_ref[...] = _conv3(p0_ref[...], w0_ref[...], b0_ref[...], n_lat)
    x1 = _conv3(p1_ref[...], w1_ref[...], b1_ref[...], n_lat)
    v = _conv3(pv_ref[...], wv_ref[...], bv_ref[...], n_lat)
    vx = v * x1
    vxt_ref[...] = vx[:n_lat, :].T
    vxc_ref[...] = vx[n_lat:, :]


def hyena_conv(p, conv_w, conv_b3, layer, n_lat):
    bsz, l_all, _ = p.shape
    tc = CONV_TC
    nb = HY_W // tc
    base = P_HY // tc

    def pspec(off):
        return pl.BlockSpec((None, l_all, tc), lambda b, c: (b, 0, base + off + c))

    def wspec(off):
        return pl.BlockSpec((None, 3, tc), lambda b, c: (layer, 0, off + c))

    def bspec(off):
        return pl.BlockSpec((None, 1, tc), lambda b, c: (layer, 0, off + c))

    o_spec = pl.BlockSpec((None, l_all, tc), lambda b, c: (b, 0, c))
    return pl.pallas_call(
        functools.partial(_hy_conv_kernel, n_lat=n_lat),
        grid=(bsz, nb),
        in_specs=[pspec(0), pspec(nb), pspec(2 * nb), wspec(0), wspec(nb), wspec(2 * nb),
                  bspec(0), bspec(nb), bspec(2 * nb)],
        out_specs=[o_spec,
                   pl.BlockSpec((None, tc, n_lat), lambda b, c: (b, c, 0)),
                   pl.BlockSpec((None, l_all - n_lat, tc), lambda b, c: (b, 0, c))],
        out_shape=[jax.ShapeDtypeStruct((bsz, l_all, HY_W), F32),
                   jax.ShapeDtypeStruct((bsz, HY_W, n_lat), F32),
                   jax.ShapeDtypeStruct((bsz, l_all - n_lat, HY_W), F32)],
        compiler_params=_params("arbitrary", "arbitrary"),
        name="hyena_conv",
    )(p, p, p, conv_w, conv_w, conv_w, conv_b3, conv_b3, conv_b3)


def _mb_conv_kernel(p_ref, w_ref, b_ref, o_ref, *, n_lat):
    o_ref[...] = _silu(_conv3(p_ref[...], w_ref[...], b_ref[...], n_lat))


def mamba_conv(p, conv_w, conv_b3, layer, n_lat):
    bsz, l_all, _ = p.shape
    tc = CONV_TC
    nb = MB_CONV_CH // tc
    base = P_XBC // tc
    return pl.pallas_call(
        functools.partial(_mb_conv_kernel, n_lat=n_lat),
        grid=(bsz, nb),
        in_specs=[pl.BlockSpec((None, l_all, tc), lambda b, c: (b, 0, base + c)),
                  pl.BlockSpec((None, 3, tc), lambda b, c: (layer, 0, c)),
                  pl.BlockSpec((None, 1, tc), lambda b, c: (layer, 0, c))],
        out_specs=pl.BlockSpec((None, l_all, tc), lambda b, c: (b, 0, c)),
        out_shape=jax.ShapeDtypeStruct((bsz, l_all, MB_CONV_CH), F32),
        compiler_params=_params("arbitrary", "arbitrary"),
        name="mamba_conv",
    )(p, conv_w, conv_b3)


def _softplus(x):
    return jnp.maximum(x, 0.0) + jnp.log(1.0 + jnp.exp(-jnp.abs(x)))


def _mb_dt_kernel(p_ref, bias_ref, o_ref):
    dt = _softplus(p_ref[...] + bias_ref[...])
    o_ref[0] = dt
    o_ref[1] = pltpu.roll(dt, LANE - MB_HEADS, 1)


def mamba_dt(p, dt_bias_row, layer):
    bsz, l_all, _ = p.shape
    return pl.pallas_call(
        _mb_dt_kernel,
        grid=(bsz,),
        in_specs=[pl.BlockSpec((None, l_all, LANE), lambda b: (b, 0, 0)),
                  pl.BlockSpec((None, 1, LANE), lambda b: (layer, 0, 0))],
        out_specs=pl.BlockSpec((2, None, l_all, LANE), lambda b: (0, b, 0, 0)),
        out_shape=jax.ShapeDtypeStruct((2, bsz, l_all, LANE), F32),
        compiler_params=_params("arbitrary"),
        name="mamba_dt",
    )(p, dt_bias_row)


def _hy_filter_kernel(f_ref, aux_ref, w1_ref, b1_ref, w2_ref, b2_ref, w3_ref, fr_ref, dl_ref, o_ref, *,
                      channel_major):
    fr = fr_ref[...]
    tr = f_ref.shape[0]
    n_ch = 4 if tr % 512 == 0 else 1
    rc = tr // n_ch
    chunks = [slice(r * rc, (r + 1) * rc) for r in range(n_ch)]
    hdn = [_dot_hi(f_ref[rows, :], w1_ref[...]) for rows in chunks]
    hdn = [jnp.sin(fr * (x + b1_ref[...])) for x in hdn]
    hdn = [_dot_hi(x, w2_ref[...]) for x in hdn]
    hdn = [jnp.sin(fr * (x + b2_ref[...])) for x in hdn]
    filts = [_dot_hi(x, w3_ref[...]) for x in hdn]
    for rows, filt in zip(chunks, filts):
        aux = aux_ref[rows, :]
        t = aux[:, 0:1]
        is_fwd = aux[:, 1:2] > 0.5
        valid = aux[:, 2:3]
        window = jnp.exp(-t * dl_ref[...]) * valid
        k2 = jnp.where(is_fwd, filt[:, :HY_W], filt[:, HY_W:]) * window
        if channel_major:
            o_ref[:, rows] = k2.T
        else:
            o_ref[rows, :] = k2


def hyena_filter(feats2, aux, w1p, b1p, w2p, b2p, w3p, frp, deltas, layer, channel_major):
    rows = feats2.shape[0]
    tr = _pick(rows, (512, 256))

    def lspec(shape):
        return pl.BlockSpec((None,) + shape, lambda i: (layer, 0, 0))

    if channel_major:
        out_spec = pl.BlockSpec((HY_W, tr), lambda i: (0, i))
        out_shape = jax.ShapeDtypeStruct((HY_W, rows), F32)
    else:
        out_spec = pl.BlockSpec((tr, HY_W), lambda i: (i, 0))
        out_shape = jax.ShapeDtypeStruct((rows, HY_W), F32)
    return pl.pallas_call(
        functools.partial(_hy_filter_kernel, channel_major=channel_major),
        grid=(rows // tr,),
        in_specs=[pl.BlockSpec((tr, LANE), lambda i: (i, 0)),
                  pl.BlockSpec((tr, LANE), lambda i: (i, 0)),
                  lspec((LANE, LANE)), lspec((1, LANE)), lspec((LANE, LANE)), lspec((1, LANE)),
                  lspec((LANE, 2 * HY_W)), lspec((1, LANE)),
                  pl.BlockSpec((1, HY_W), lambda i: (0, 0))],
        out_specs=out_spec,
        out_shape=out_shape,
        compiler_params=_params("arbitrary"),
        name="hyena_filter",
    )(feats2, aux, w1p, b1p, w2p, b2p, w3p, frp, deltas)


DFT_MINOR = 256


def _snap(c):
    for v in (0.0, 1.0, -1.0):
        if abs(c - v) < 1e-12:
            return v
    return float(c)


def _lincomb(terms):
    acc = None
    for cf, tile in terms:
        if cf == 0.0:
            continue
        v = tile()
        if acc is None:
            acc = v if cf == 1.0 else (-v if cf == -1.0 else cf * v)
        elif cf == 1.0:
            acc = acc + v
        elif cf == -1.0:
            acc = acc - v
        else:
            acc = acc + cf * v
    return acc


def _dft_consts(n_seq):
    n = 2 * n_seq
    n2 = DFT_MINOR
    n1 = n // n2
    nkj = n1 // 2 + 1
    ang1 = 2.0 * np.pi * np.outer(np.arange(nkj), np.arange(n1)) / n1
    cos1 = [[_snap(v) for v in r] for r in np.cos(ang1)]
    sin1 = [[_snap(v) for v in r] for r in np.sin(ang1)]
    ang_t = 2.0 * np.pi * np.outer(np.arange(nkj), np.arange(n2)) / n
    rows = ((nkj + 7) // 8) * 8
    tw_c = np.zeros((rows, n2))
    tw_s = np.zeros((rows, n2))
    tw_c[:nkj] = np.cos(ang_t)
    tw_s[:nkj] = np.sin(ang_t)
    ang2 = 2.0 * np.pi * np.outer(np.arange(n2), np.arange(n2)) / n2
    c2, s2 = np.cos(ang2), np.sin(ang2)
    m_fwd = np.block([[c2, -s2], [s2, c2]])
    m_inv = np.block([[c2, s2], [-s2, c2]])
    as32 = lambda a: jnp.asarray(a, dtype=F32)

    def split(m):
        m32 = as32(m)
        hi = m32.astype(MXU_DT)
        lo = (m32 - hi.astype(F32)).astype(MXU_DT)
        return jnp.stack([hi, lo])

    return dict(n=n, n1=n1, nkj=nkj, cos1=cos1, sin1=sin1, tw_c=as32(tw_c), tw_s=as32(tw_s),
                m_fwd=split(m_fwd), m_inv=split(m_inv))


def _dot_split(a, m_ref):
    a_hi = a.astype(MXU_DT)
    a_lo = (a - a_hi.astype(F32)).astype(MXU_DT)
    return _dot(a_hi, m_ref[0]) + _dot(a_lo, m_ref[0]) + _dot(a_hi, m_ref[1])


def _outer_fwd_block(src_ref, z_ref, twc_ref, tws_ref, cb, n_in, cst):
    n2 = DFT_MINOR
    nkj, cos1, sin1 = cst["nkj"], cst["cos1"], cst["sin1"]
    tile = lambda j: (lambda: src_ref[cb * 8:cb * 8 + 8, j * n2:(j + 1) * n2])
    for kj in range(nkj):
        re = _lincomb([(cos1[kj][j], tile(j)) for j in range(n_in)])
        im = _lincomb([(-sin1[kj][j], tile(j)) for j in range(n_in)])
        if kj > 0 and im is not None:
            c = twc_ref[kj:kj + 1, :]
            s = tws_ref[kj:kj + 1, :]
            re, im = re * c + im * s, im * c - re * s
        elif kj > 0:
            re, im = re * twc_ref[kj:kj + 1, :], -re * tws_ref[kj:kj + 1, :]
        r = (cb * nkj + kj) * 8
        z_ref[r:r + 8, 0:n2] = re
        z_ref[r:r + 8, n2:2 * n2] = jnp.zeros_like(re) if im is None else im


def _outer_inv_block(z_ref, yt_ref, twc_ref, tws_ref, cb, cst):
    n2 = DFT_MINOR
    nkj, n1, cos1, sin1 = cst["nkj"], cst["n1"], cst["cos1"], cst["sin1"]
    inv_n = 1.0 / cst["n"]
    for kj in range(1, nkj):
        r = (cb * nkj + kj) * 8
        re = z_ref[r:r + 8, 0:n2]
        im = z_ref[r:r + 8, n2:2 * n2]
        c = twc_ref[kj:kj + 1, :]
        s = tws_ref[kj:kj + 1, :]
        z_ref[r:r + 8, 0:n2] = re * c - im * s
        z_ref[r:r + 8, n2:2 * n2] = im * c + re * s
    row = lambda kj: (cb * nkj + kj) * 8
    g_re = lambda kj: (lambda: z_ref[row(kj):row(kj) + 8, 0:n2])
    g_im = lambda kj: (lambda: z_ref[row(kj):row(kj) + 8, n2:2 * n2])
    for j in range(n1 // 2):
        terms = []
        for kj in range(nkj):
            wgt = inv_n if kj in (0, nkj - 1) else 2.0 * inv_n
            terms.append((wgt * cos1[kj][j], g_re(kj)))
            terms.append((-wgt * sin1[kj][j], g_im(kj)))
        yt_ref[cb * 8:cb * 8 + 8, j * n2:(j + 1) * n2] = _lincomb(terms)


def _spectrum_kernel(k_ref, twc_ref, tws_ref, mf_ref, h_ref, z_ref, *, ct, cst, n_chunks):
    nkj = cst["nkj"]
    per = ct // 8 // n_chunks
    rc = per * nkj * 8
    for c in range(n_chunks + 1):
        if c < n_chunks:
            for cb in range(c * per, (c + 1) * per):
                _outer_fwd_block(k_ref, z_ref, twc_ref, tws_ref, cb, cst["n1"], cst)
        if c >= 1:
            rows = slice((c - 1) * rc, c * rc)
            h_ref[rows, :] = _dot_split(z_ref[rows, :], mf_ref)


def _long_conv_kernel(x_ref, h_ref, twc_ref, tws_ref, mf_ref, mi_ref, x0_ref, vxc_ref, yc_ref, skip_ref,
                      skip_row_ref, o_ref, z_ref, yt_ref, *, ct, cst, n_chunks, n_lat):
    n2 = DFT_MINOR
    nkj, n1 = cst["nkj"], cst["n1"]
    per = ct // 8 // n_chunks
    rc = per * nkj * 8
    for c in range(n_chunks + 2):
        if c < n_chunks:
            for cb in range(c * per, (c + 1) * per):
                _outer_fwd_block(x_ref, z_ref, twc_ref, tws_ref, cb, n1 // 2, cst)
        if 1 <= c <= n_chunks:
            rows = slice((c - 1) * rc, c * rc)
            x = _dot_split(z_ref[rows, :], mf_ref)
            xre, xim = x[:, :n2], x[:, n2:]
            hre, him = h_ref[rows, 0:n2], h_ref[rows, n2:2 * n2]
            y = jnp.concatenate([xre * hre - xim * him, xre * him + xim * hre], axis=1)
            z_ref[rows, :] = _dot_split(y, mi_ref)
        if c >= 2:
            for cb in range((c - 2) * per, (c - 1) * per):
                _outer_inv_block(z_ref, yt_ref, twc_ref, tws_ref, cb, cst)

    y = (yt_ref[...] + x_ref[...] * skip_ref[...]).T
    o_ref[0:n_lat, :] = (x0_ref[0:n_lat, :] * y).astype(o_ref.dtype)
    yc = yc_ref[...] + vxc_ref[...] * skip_row_ref[...]
    o_ref[n_lat:, :] = (x0_ref[n_lat:, :] * yc).astype(o_ref.dtype)


def filter_spectrum(k2t, cst):
    ch, n = k2t.shape
    ct = LANE
    nkj = cst["nkj"]
    n_chunks = _pick(nkj * ct // 8, (4, 5, 1))
    full = lambda a: pl.BlockSpec(a.shape, lambda c: (0,) * a.ndim)
    return pl.pallas_call(
        functools.partial(_spectrum_kernel, ct=ct, cst=cst, n_chunks=n_chunks),
        grid=(ch // ct,),
        in_specs=[pl.BlockSpec((ct, n), lambda c: (c, 0)),
                  full(cst["tw_c"]), full(cst["tw_s"]), full(cst["m_fwd"])],
        out_specs=pl.BlockSpec((None, nkj * ct, 2 * DFT_MINOR), lambda c: (c, 0, 0)),
        out_shape=jax.ShapeDtypeStruct((ch // ct, nkj * ct, 2 * DFT_MINOR), F32),
        scratch_shapes=[pltpu.VMEM((nkj * ct, 2 * DFT_MINOR), F32)],
        compiler_params=_params("arbitrary"),
        name="filter_spectrum",
    )(k2t, cst["tw_c"], cst["tw_s"], cst["m_fwd"])


def hyena_long_conv(vxt, h_spec, x0, vx_ctx, y_ctx, skip_col, skip_row, cst, layer, n_lat):
    bsz, ch, _ = vxt.shape
    l_all = x0.shape[1]
    n_ctx = l_all - n_lat
    ct = LANE
    nkj = cst["nkj"]
    n_chunks = _pick(nkj * ct // 8, (4, 5, 1))
    full = lambda a: pl.BlockSpec(a.shape, lambda b, c: (0,) * a.ndim)
    return pl.pallas_call(
        functools.partial(_long_conv_kernel, ct=ct, cst=cst, n_chunks=n_chunks, n_lat=n_lat),
        grid=(bsz, ch // ct),
        in_specs=[pl.BlockSpec((None, ct, n_lat), lambda b, c: (b, c, 0)),
                  pl.BlockSpec((None, nkj * ct, 2 * DFT_MINOR), lambda b, c: (c, 0, 0)),
                  full(cst["tw_c"]), full(cst["tw_s"]), full(cst["m_fwd"]), full(cst["m_inv"]),
                  pl.BlockSpec((None, l_all, ct), lambda b, c: (b, 0, c)),
                  pl.BlockSpec((None, n_ctx, ct), lambda b, c: (b, 0, c)),
                  pl.BlockSpec((None, n_ctx, ct), lambda b, c: (b, 0, c)),
                  pl.BlockSpec((None, ct, 1), lambda b, c: (layer, c, 0)),
                  pl.BlockSpec((None, 1, ct), lambda b, c: (layer, 0, c))],
        out_specs=pl.BlockSpec((None, l_all, ct), lambda b, c: (b, 0, c)),
        out_shape=jax.ShapeDtypeStruct((bsz, l_all, ch), MXU_DT),
        scratch_shapes=[pltpu.VMEM((nkj * ct, 2 * DFT_MINOR), F32), pltpu.VMEM((ct, n_lat), F32)],
        compiler_params=_params("arbitrary", "arbitrary"),
        name="hyena_long_conv",
    )(vxt, h_spec, cst["tw_c"], cst["tw_s"], cst["m_fwd"], cst["m_inv"], x0, vx_ctx, y_ctx, skip_col, skip_row)


def _ctx_conv_kernel(v_ref, k_ref, fd_ref, fk_ref, gi_ref, o_ref, *, kp):
    x = _dot_hi(fd_ref[...], v_ref[...])
    h = _dot_hi(fk_ref[...], k_ref[...])
    xre, xim = x[:kp], x[kp:]
    hre, him = h[:kp], h[kp:]
    y = jnp.concatenate([xre * hre - xim * him, xre * him + xim * hre], axis=0)
    o_ref[...] = _dot_hi(gi_ref[...], y)


def _ctx_tables(n_ctx):
    n = 2 * n_ctx
    kh = n_ctx + 1
    kp = ((kh + 7) // 8) * 8
    k = np.arange(kp)[:, None]
    live = (k < kh).astype(np.float64)
    ang = 2.0 * np.pi * k * np.arange(n)[None, :] / n
    ck, sk = np.cos(ang) * live, np.sin(ang) * live
    fk = np.concatenate([ck, -sk], axis=0)
    fd = fk[:, :n_ctx]
    wk = np.where((k == 0) | (k == n_ctx), 1.0, 2.0) * live / n
    gi = np.concatenate([(ck * wk).T, (-sk * wk).T], axis=1)[:n_ctx]
    as32 = lambda a: jnp.asarray(a, dtype=F32)
    return dict(kp=kp, fd=as32(fd), fk=as32(fk), gi=as32(gi))


def long_conv_ctx(vx_ctx, k2c, tb):
    bsz, n_ctx, ch = vx_ctx.shape
    kp = tb["kp"]
    full = lambda a: pl.BlockSpec(a.shape, lambda b, c: (0,) * a.ndim)
    return pl.pallas_call(
        functools.partial(_ctx_conv_kernel, kp=kp),
        grid=(bsz, ch // LANE),
        in_specs=[pl.BlockSpec((None, n_ctx, LANE), lambda b, c: (b, 0, c)),
                  pl.BlockSpec((2 * n_ctx, LANE), lambda b, c: (0, c)),
                  full(tb["fd"]), full(tb["fk"]), full(tb["gi"])],
        out_specs=pl.BlockSpec((None, n_ctx, LANE), lambda b, c: (b, 0, c)),
        out_shape=jax.ShapeDtypeStruct((bsz, n_ctx, ch), F32),
        compiler_params=_params("arbitrary", "arbitrary"),
        name="ctx_long_conv",
    )(vx_ctx, k2c, tb["fd"], tb["fk"], tb["gi"])


def _split3(a):
    a1 = a.astype(MXU_DT)
    r = a - a1.astype(F32)
    a2 = r.astype(MXU_DT)
    a3 = (r - a2.astype(F32)).astype(MXU_DT)
    return a1, a2, a3


def _ssd_chunk(backward, xs_ref, b_ref, c_ref, dt_ref, alog, ex, y_ref, state_ref):
    n = MB_CHUNK
    li = lax.broadcasted_iota(jnp.int32, (n, n), 0)
    si = lax.broadcasted_iota(jnp.int32, (n, n), 1)
    mask = (li <= si) if backward else (li >= si)
    tri = mask.astype(MXU_DT)
    lane = lax.broadcasted_iota(jnp.int32, (n, LANE), 1)

    dt = dt_ref[...]
    da = dt * (-jnp.exp(alog))
    cum = sum(_dot(tri, t) for t in _split3(da))
    cum_t = cum.T
    both = jnp.concatenate([cum, dt], axis=0)
    both_e = sum(_dot(t, ex) for t in _split3(both))
    cum_e = both_e[:n]
    dt_e = both_e[n:]
    total_e = cum_e[0:1] if backward else cum_e[n - 1:n]

    xdt = xs_ref[...] * dt_e
    xdt_b = xdt.astype(MXU_DT)
    w_b = (jnp.exp(total_e - cum_e) * xdt).astype(MXU_DT)
    st = state_ref[...]
    y_parts = []
    s_parts = []
    gw = MB_STATE
    hw = (MB_HEADS // MB_GROUPS) * MB_HEADDIM
    for g in range(MB_GROUPS):
        cg = c_ref[:, g * gw:(g + 1) * gw].astype(MXU_DT)
        bg = b_ref[:, g * gw:(g + 1) * gw]
        cb = _dot_nt(cg, bg.astype(MXU_DT))
        y_off = _dot(cg, st[:, g * hw:(g + 1) * hw].astype(MXU_DT))
        s_parts.append(_dot(bg.T.astype(MXU_DT), w_b[:, g * hw:(g + 1) * hw]))
        diag = []
        for j in range(hw // LANE):
            lo = g * hw + j * LANE
            xp = xdt_b[:, lo:lo + LANE]
            pair = []
            for e in range(2):
                hd = (lo // MB_HEADDIM) + e
                seg = cum[:, hd:hd + 1] - cum_t[hd:hd + 1, :]
                decay = jnp.exp(jnp.where(mask, seg, -jnp.inf))
                pair.append(_dot((cb * decay).astype(MXU_DT), xp))
            diag.append(jnp.where(lane < MB_HEADDIM, pair[0], pair[1]))
        y_parts.append(jnp.concatenate(diag, axis=1) + y_off * jnp.exp(cum_e[:, g * hw:(g + 1) * hw]))
    y_ref[...] = jnp.concatenate(y_parts, axis=1)
    state_ref[...] = st * jnp.exp(total_e) + jnp.concatenate(s_parts, axis=1)


def _ssd_kernel(xf_ref, bf_ref, cf_ref, dtf_ref, xb_ref, bb_ref, cb_ref, dtb_ref, alog_ref, ex_ref,
                yf_ref, yb_ref, state_ref):
    @pl.when(pl.program_id(1) == 0)
    def _():
        state_ref[...] = jnp.zeros_like(state_ref)

    ex = ex_ref[...]
    for e in range(xf_ref.shape[0]):
        _ssd_chunk(False, xf_ref.at[e], bf_ref.at[e], cf_ref.at[e], dtf_ref.at[e], alog_ref[0], ex,
                   yf_ref.at[e], state_ref.at[2 * e])
        _ssd_chunk(True, xb_ref.at[e], bb_ref.at[e], cb_ref.at[e], dtb_ref.at[e], alog_ref[1], ex,
                   yb_ref.at[e], state_ref.at[2 * e + 1])


def ssd_scan(xbc, dt2, a_log_rows, expand, layer, n_lat):
    bsz, l_all, _ = xbc.shape
    nc = l_all // MB_CHUNK
    ncl = n_lat // MB_CHUNK
    fwd = lambda c: (c + ncl) % nc
    bwd = lambda c: nc - 1 - c

    nb = 2 if bsz % 2 == 0 else 1

    def specs(blk, direction):
        return [pl.BlockSpec((nb, MB_CHUNK, MB_INNER), lambda b, c: (b, blk(c), 0)),
                pl.BlockSpec((nb, MB_CHUNK, 256), lambda b, c: (b, blk(c), 2)),
                pl.BlockSpec((nb, MB_CHUNK, 256), lambda b, c: (b, blk(c), 3)),
                pl.BlockSpec((None, nb, MB_CHUNK, LANE), lambda b, c: (direction, b, blk(c), 0))]

    y_shape = jax.ShapeDtypeStruct((bsz, l_all, MB_INNER), F32)
    return pl.pallas_call(
        _ssd_kernel,
        grid=(bsz // nb, nc),
        in_specs=specs(fwd, 0) + specs(bwd, 1)
        + [pl.BlockSpec((None, 2, 1, LANE), lambda b, c: (layer, 0, 0, 0)),
           pl.BlockSpec((LANE, MB_INNER), lambda b, c: (0, 0))],
        out_specs=[pl.BlockSpec((nb, MB_CHUNK, MB_INNER), lambda b, c: (b, fwd(c), 0)),
                   pl.BlockSpec((nb, MB_CHUNK, MB_INNER), lambda b, c: (b, bwd(c), 0))],
        out_shape=[y_shape, y_shape],
        scratch_shapes=[pltpu.VMEM((2 * nb, MB_STATE, MB_INNER), F32)],
        compiler_params=_params("arbitrary", "arbitrary"),
        name="ssd_scan",
    )(xbc, xbc, xbc, dt2, xbc, xbc, xbc, dt2, a_log_rows, expand)


def _mb_finish_kernel(yf_ref, yb_ref, xs_ref, z_ref, dsk_ref, g_ref, o_ref):
    y = yf_ref[...] + yb_ref[...] + xs_ref[...] * dsk_ref[...]
    y = y * _silu(z_ref[...].astype(F32))
    r = lax.rsqrt(jnp.mean(y * y, axis=-1, keepdims=True) + RMS_EPS)
    o_ref[...] = (y * r * g_ref[...]).astype(o_ref.dtype)


def mamba_finish(y_f, y_b, xbc, p, d_skip_e, norm_g3, layer):
    bsz, l_all, _ = xbc.shape
    tr = _pick(l_all, (544, 384, 128))
    w = MB_INNER
    return pl.pallas_call(
        _mb_finish_kernel,
        grid=(bsz, l_all // tr),
        in_specs=[pl.BlockSpec((None, tr, w), lambda b, i: (b, i, 0)),
                  pl.BlockSpec((None, tr, w), lambda b, i: (b, i, 0)),
                  pl.BlockSpec((None, tr, w), lambda b, i: (b, i, 0)),
                  pl.BlockSpec((None, tr, w), lambda b, i: (b, i, P_Z // w)),
                  pl.BlockSpec((None, 1, w), lambda b, i: (layer, 0, 0)),
                  pl.BlockSpec((None, 1, w), lambda b, i: (layer, 0, 0))],
        out_specs=pl.BlockSpec((None, tr, w), lambda b, i: (b, i, 0)),
        out_shape=jax.ShapeDtypeStruct((bsz, l_all, w), MXU_DT),
        compiler_params=_params("arbitrary", "arbitrary"),
        name="mamba_finish",
    )(y_f, y_b, xbc, p, d_skip_e, norm_g3)


_W_IN_MOVES = (
    (0, 0, MLA_Q_RANK),
    (512, MLA_Q_RANK, MLA_KV_RANK),
    (640, MLA_Q_RANK + MLA_KV_RANK, MLA_ROPE),
    (P_DT, MLA_COLS + GQA_COLS + HY_COLS + MB_INNER + MB_CONV_CH, 2 * MB_HEADS),
    (P_GQA, MLA_COLS, GQA_COLS),
    (P_Z, MLA_COLS + GQA_COLS + HY_COLS, MB_INNER),
    (P_HY, MLA_COLS + GQA_COLS, HY_COLS),
    (P_XBC, MLA_COLS + GQA_COLS + HY_COLS + MB_INNER, MB_CONV_CH),
)


def _w_in_prep_kernel(w_ref, o_ref):
    o_ref[...] = jnp.zeros_like(o_ref)
    for dst, src, n in _W_IN_MOVES:
        o_ref[:, dst:dst + n] = w_ref[:, src:src + n].astype(o_ref.dtype)


def _prep_w_in(w_in):
    nl, d, cols = w_in.shape
    tr = 256
    return pl.pallas_call(
        _w_in_prep_kernel,
        grid=(nl, d // tr),
        in_specs=[pl.BlockSpec((None, tr, cols), lambda l, i: (l, i, 0))],
        out_specs=pl.BlockSpec((None, tr, P_COLS), lambda l, i: (l, i, 0)),
        out_shape=jax.ShapeDtypeStruct((nl, d, P_COLS), MXU_DT),
        compiler_params=_params("arbitrary", "arbitrary"),
        name="w_in_layout",
    )(w_in)


def _prep_w_uq(w_uq):
    nl = w_uq.shape[0]
    w = w_uq.reshape(nl, MLA_Q_RANK, MLA_HEADS, MLA_NOPE + MLA_ROPE)
    w = _pad_to(w, (nl, 512, MLA_HEADS, 256))
    return w.reshape(nl, 512, MLA_HEADS * 256).astype(MXU_DT)


def _rope_tables(n_lat, n_ctx, rot_dim):
    rows = n_lat // GRID_W
    row = jnp.repeat(jnp.arange(rows, dtype=F32), GRID_W)
    col = jnp.tile(jnp.arange(GRID_W, dtype=F32), rows)
    n_freq = rot_dim // 4
    inv_freq = ROPE_THETA ** (-jnp.arange(n_freq, dtype=F32) / n_freq)
    ang = jnp.concatenate([row[:, None] * inv_freq, col[:, None] * inv_freq], axis=-1)
    half = rot_dim // 2
    cos = jnp.concatenate([jnp.cos(ang), jnp.ones((n_ctx, half), F32)], axis=0)
    sin = jnp.concatenate([jnp.sin(ang), jnp.zeros((n_ctx, half), F32)], axis=0)
    cos_t = jnp.repeat(cos, 2, axis=-1)
    sin_t = jnp.stack([-sin, sin], axis=-1).reshape(n_lat + n_ctx, rot_dim)
    shape = (n_lat + n_ctx, LANE)
    return _pad_to(cos_t, shape), _pad_to(sin_t, shape)


def _filter_inputs(n):
    t = jnp.linspace(0.0, 1.0, n, dtype=F32)[:, None]
    omega = 2.0 * math.pi * jnp.arange(n, dtype=F32) / n
    bands = jnp.linspace(1e-4, HY_BANDS - 1, HY_BANDS, dtype=F32)
    ang = omega[:, None] * bands[None, :]
    feats = jnp.concatenate([t, jnp.cos(ang), -jnp.sin(ang)], axis=-1)
    zero = jnp.zeros((1, HY_EMB), F32)
    feats2 = jnp.concatenate([feats, zero, feats[1:][::-1]], axis=0)
    feats2 = jnp.concatenate([feats2, jnp.zeros((2 * n, LANE - HY_EMB), F32)], axis=-1)
    t2 = jnp.concatenate([t, jnp.zeros((1, 1), F32), t[1:][::-1]], axis=0)
    idx = jnp.arange(2 * n)[:, None]
    aux = jnp.concatenate([t2, (idx < n).astype(F32), (idx != n).astype(F32),
                           jnp.zeros((2 * n, LANE - 3), F32)], axis=-1)
    return feats2, aux


def _pad_to(a, shape):
    pads = [(0, s - d) for d, s in zip(a.shape, shape)]
    return jnp.pad(a, pads)


def kernel(x, c, ctx, c_ctx, w_ada, b_ada, w_in, mla_q_norm, mla_kv_norm, mla_w_uq, mla_w_ukv, gqa_q_norm, gqa_k_norm, hy_conv_w, hy_conv_b, hy_w1, hy_b1, hy_w2, hy_b2, hy_w3, hy_freq, hy_skip, mb_conv_w, mb_conv_b, mb_a_log, mb_dt_bias, mb_d, mb_norm, w_mgate, b_mgate, w_branch, w_out, ln1_g, ln1_b, w_ffn_in, w_ffn_out, ln2_g, ln2_b):
    bsz, n_lat, d = x.shape
    n_ctx = ctx.shape[1]
    nl = w_in.shape[0]
    assert d == D_MODEL and bsz < 8 and n_lat % n_ctx == 0 and n_ctx % MB_CHUNK == 0

    w_in_p = _prep_w_in(w_in)
    w_uq_p = _prep_w_uq(mla_w_uq)
    w_ukv_p = mla_w_ukv.astype(MXU_DT)
    gq_mla = _pad_to(mla_q_norm, (nl, 512))[:, None, :]
    gkv_mla = mla_kv_norm[:, None, :]
    gq_gqa = gqa_q_norm[:, None, :]
    gk_gqa = gqa_k_norm[:, None, :]
    w_gate_b = w_mgate.astype(MXU_DT)
    w_branch_b = w_branch.astype(MXU_DT)
    w_out_b = w_out.astype(MXU_DT)
    w_ffn_in_b = w_ffn_in.astype(MXU_DT)
    w_ffn_out_b = w_ffn_out.astype(MXU_DT)
    b_gate4 = b_mgate[:, :, None, :]
    b_ada3 = b_ada[:, None, :]
    hy_conv_b3 = hy_conv_b[:, None, :]
    mb_conv_b3 = mb_conv_b[:, None, :]
    hy_skip3 = hy_skip[:, None, :]
    hy_skip_col = hy_skip[:, :, None]
    ln1_g3, ln1_b3, ln2_g3, ln2_b3 = (a[:, None, :] for a in (ln1_g, ln1_b, ln2_g, ln2_b))
    hy_w1p = _pad_to(hy_w1, (nl, LANE, LANE))
    hy_b1p = _pad_to(hy_b1, (nl, LANE))[:, None, :]
    hy_w2p = _pad_to(hy_w2, (nl, LANE, LANE))
    hy_b2p = _pad_to(hy_b2, (nl, LANE))[:, None, :]
    hy_w3p = _pad_to(hy_w3, (nl, LANE, 2 * HY_W))
    hy_frp = _pad_to(hy_freq, (nl, LANE))[:, None, :]
    deltas = jnp.abs(jnp.linspace(HY_MIN_DECAY, HY_MAX_DECAY, HY_W, dtype=F32))[None, :]
    dt_bias_row = _pad_to(mb_dt_bias.reshape(nl, 2 * MB_HEADS), (nl, LANE))[:, None, :]
    a_log_rows = _pad_to(mb_a_log, (nl, 2, LANE))[:, :, None, :]
    d_skip_e = jnp.repeat(mb_d, MB_HEADDIM, axis=-1)[:, None, :]
    mb_norm3 = mb_norm[:, None, :]
    expand = jnp.asarray(np.kron(np.eye(LANE, MB_HEADS), np.ones((1, MB_HEADDIM))), dtype=MXU_DT)

    cos_m, sin_m = _rope_tables(n_lat, n_ctx, MLA_ROPE)
    cos_g, sin_g = _rope_tables(n_lat, n_ctx, GQA_HD)
    feats_lat, aux_lat = _filter_inputs(n_lat)
    feats_ctx, aux_ctx = _filter_inputs(n_ctx)
    dft_lat = _dft_consts(n_lat)
    tb_ctx = _ctx_tables(n_ctx)

    c8 = jnp.concatenate([c, c_ctx[None, :], jnp.zeros((8 - bsz - 1, d), F32)], axis=0)
    xz = jnp.concatenate([x, ctx], axis=1)

    for l in range(nl):
        last = l == nl - 1
        mod = ada_mod(c8, w_ada, b_ada3, l).reshape(8, 1, 6 * d)
        p, p_dt, h = in_projection(xz, mod, w_in_p, l, n_lat)

        q, k, v = mla_prep(p, gq_mla, gkv_mla, w_uq_p, w_ukv_p, cos_m, sin_m, l)
        oa = attention(q, k, v, n_lat)
        q, k, v = gqa_prep(p, gq_gqa, gk_gqa, cos_g, sin_g, l)
        ob = attention(q, k, v, n_lat)
        x0, vxt, vx_ctx = hyena_conv(p, hy_conv_w, hy_conv_b3, l, n_lat)
        k2t = hyena_filter(feats_lat, aux_lat, hy_w1p, hy_b1p, hy_w2p, hy_b2p, hy_w3p, hy_frp, deltas, l, True)
        h_spec = filter_spectrum(k2t, dft_lat)
        if last:
            y_ctx = jnp.zeros((bsz, n_ctx, HY_W), F32)
        else:
            k2c = hyena_filter(feats_ctx, aux_ctx, hy_w1p, hy_b1p, hy_w2p, hy_b2p, hy_w3p, hy_frp, deltas, l,
                               False)
            y_ctx = long_conv_ctx(vx_ctx, k2c, tb_ctx)
        oc = hyena_long_conv(vxt, h_spec, x0, vx_ctx, y_ctx, hy_skip_col, hy_skip3, dft_lat, l, n_lat)
        xbc = mamba_conv(p, mb_conv_w, mb_conv_b3, l, n_lat)
        dt2 = mamba_dt(p_dt, dt_bias_row, l)
        y_f, y_b = ssd_scan(xbc, dt2, a_log_rows, expand, l, n_lat)
        od = mamba_finish(y_f, y_b, xbc, p, d_skip_e, mb_norm3, l)

        acc = merge_branches(h, (oa, ob, oc, od), w_gate_b, b_gate4, w_branch_b, l)
        x1 = matmul_res_ln(acc, w_out_b, xz, mod, 2, ln1_g3, ln1_b3, l, n_lat, n_lat + n_ctx,
                           (544, 384, 128), d)
        act = ffn_in(x1, mod, w_ffn_in_b, l, n_lat)
        out_rows = n_lat if last else n_lat + n_ctx
        xz = matmul_res_ln(act, w_ffn_out_b, x1, mod, 5, ln2_g3, ln2_b3, l, n_lat, out_rows,
                           (256, 128), act.shape[2])
    return xz
```

```python
import functools
import math

import jax
import jax.numpy as jnp
import numpy as np
from jax import lax
from jax.experimental import pallas as pl
from jax.experimental.pallas import tpu as pltpu

F32 = jnp.float32
MXU_DT = jnp.bfloat16

D_MODEL = 2048
DEPTH = 2
GRID_W = 64
N_BRANCH = 4
BRANCH_W = D_MODEL // N_BRANCH
ROPE_THETA = 10000.0
LN_EPS = 1e-6
RMS_EPS = 1e-6
DEEPNORM_ALPHA = (2 * DEPTH) ** 0.25

MLA_HEADS = 4
MLA_Q_RANK = 448
MLA_KV_RANK = 128
MLA_NOPE = 128
MLA_ROPE = 64
MLA_V = 128
MLA_COLS = MLA_Q_RANK + MLA_KV_RANK + MLA_ROPE

GQA_HEADS = 4
GQA_KV_HEADS = 2
GQA_HD = 128
GQA_COLS = (GQA_HEADS + 2 * GQA_KV_HEADS) * GQA_HD

HY_W = BRANCH_W
HY_EMB = 33
HY_BANDS = (HY_EMB - 1) // 2
HY_FFN = 64
HY_MIN_DECAY = math.log(1e-2) / 1.5
HY_MAX_DECAY = math.log(1e-2) / 0.3
HY_COLS = 3 * HY_W

MB_INNER = BRANCH_W
MB_HEADDIM = 64
MB_HEADS = 8
MB_GROUPS = 2
MB_STATE = 128
MB_CHUNK = 128
MB_CONV_CH = MB_INNER + 2 * MB_GROUPS * MB_STATE
MB_COLS = MB_INNER + MB_CONV_CH + 2 * MB_HEADS

FFN_HIDDEN = 5632

LANE = 128
VMEM_LIMIT = 56 * 1024 * 1024

P_MLA = 0
P_MLA_W = 768
P_DT = 896
P_GQA = 1024
P_Z = 2048
P_HY = 2560
P_XBC = 4096
P_COLS = 5120

LOG2_E = math.log2(math.e)
_HI = lax.Precision.HIGHEST


def _dot(a, b):
    return jnp.dot(a, b, preferred_element_type=F32)


def _dot_hi(a, b):
    return jnp.dot(a, b, precision=_HI, preferred_element_type=F32)


def _dot_nt(a, b):
    return lax.dot_general(a, b, (((1,), (1,)), ((), ())), preferred_element_type=F32)


def _sigmoid(x):
    return 1.0 / (1.0 + jnp.exp(-x))


def _silu(x):
    return x * _sigmoid(x)


def _params(*sem):
    return pltpu.CompilerParams(dimension_semantics=sem, vmem_limit_bytes=VMEM_LIMIT)


def _pick(n, prefs):
    for p in prefs:
        if n % p == 0:
            return p
    raise ValueError(f"no tile for {n} in {prefs}")


def _standardize(x):
    mu = jnp.mean(x, axis=-1, keepdims=True)
    xc = x - mu
    var = jnp.mean(xc * xc, axis=-1, keepdims=True)
    return xc * lax.rsqrt(var + LN_EPS)


def _is_ctx_rows(tile_idx, tm, n_lat):
    row = tile_idx * tm + lax.broadcasted_iota(jnp.int32, (tm, 1), 0)
    return row >= n_lat


def _ada_kernel(c_ref, w_ref, b_ref, o_ref):
    cs = _silu(c_ref[...])
    o_ref[...] = _dot(cs.astype(MXU_DT), w_ref[...].astype(MXU_DT)) + b_ref[...]


def ada_mod(c8, w_ada, b_ada3, layer):
    d = c8.shape[1]
    n = w_ada.shape[2]
    tn = 1024
    return pl.pallas_call(
        _ada_kernel,
        grid=(n // tn,),
        in_specs=[pl.BlockSpec((8, d), lambda j: (0, 0)),
                  pl.BlockSpec((None, d, tn), lambda j: (layer, 0, j)),
                  pl.BlockSpec((None, 1, tn), lambda j: (layer, 0, j))],
        out_specs=pl.BlockSpec((8, tn), lambda j: (0, j)),
        out_shape=jax.ShapeDtypeStruct((8, n), F32),
        compiler_params=_params("arbitrary"),
        name="ada_mod",
    )(c8, w_ada, b_ada3)


ROW_CHUNKS = 4


def _modulated(x, sh_ref, sc_ref, shc_ref, scc_ref, row0, n_lat):
    xn = _standardize(x)
    row = row0 + lax.broadcasted_iota(jnp.int32, (x.shape[0], 1), 0)
    is_ctx = row >= n_lat
    scale = jnp.where(is_ctx, scc_ref[...], sc_ref[...])
    shift = jnp.where(is_ctx, shc_ref[...], sh_ref[...])
    return (xn * (1.0 + scale) + shift).astype(MXU_DT)


def _inproj_kernel(x_ref, sh_ref, sc_ref, shc_ref, scc_ref, w_ref, p_ref, pdt_ref, h_ref, *, tm, n_lat):
    j = pl.program_id(2)
    rc = tm // ROW_CHUNKS

    @pl.when(j == 0)
    def _():
        for r in range(ROW_CHUNKS):
            rows = slice(r * rc, (r + 1) * rc)
            hb = _modulated(x_ref[rows, :], sh_ref, sc_ref, shc_ref, scc_ref,
                            pl.program_id(1) * tm + r * rc, n_lat)
            h_ref[rows, :] = hb
            acc = _dot(hb, w_ref[...])
            p_ref[rows, :] = acc.astype(p_ref.dtype)
            pdt_ref[rows, :] = acc[:, P_DT:P_DT + LANE]

    @pl.when(j > 0)
    def _():
        p_ref[...] = _dot(h_ref[...], w_ref[...]).astype(p_ref.dtype)


def _ffn_in_kernel(x_ref, sh_ref, sc_ref, shc_ref, scc_ref, wu_ref, wg_ref, a_ref, h_ref, *, tm, n_lat):
    j = pl.program_id(2)
    rc = tm // ROW_CHUNKS

    def swiglu(h):
        up = _dot(h, wu_ref[...])
        gate = _dot(h, wg_ref[...])
        return (_silu(gate) * up).astype(a_ref.dtype)

    @pl.when(j == 0)
    def _():
        for r in range(ROW_CHUNKS):
            rows = slice(r * rc, (r + 1) * rc)
            hb = _modulated(x_ref[rows, :], sh_ref, sc_ref, shc_ref, scc_ref,
                            pl.program_id(1) * tm + r * rc, n_lat)
            h_ref[rows, :] = hb
            a_ref[rows, :] = swiglu(hb)

    @pl.when(j > 0)
    def _():
        half = tm // 2
        for r in range(2):
            rows = slice(r * half, (r + 1) * half)
            a_ref[rows, :] = swiglu(h_ref[rows, :])


def _mod_specs(k_shift, k_scale, n_batch):
    d = D_MODEL
    return [pl.BlockSpec((None, 1, d), lambda b, i, j: (b, 0, k_shift)),
            pl.BlockSpec((None, 1, d), lambda b, i, j: (b, 0, k_scale)),
            pl.BlockSpec((None, 1, d), lambda b, i, j: (n_batch, 0, k_shift)),
            pl.BlockSpec((None, 1, d), lambda b, i, j: (n_batch, 0, k_scale))]


def in_projection(xz, mod, w_in_p, layer, n_lat):
    bsz, l_all, d = xz.shape
    n = w_in_p.shape[2]
    tm = _pick(l_all, (1088, 544, 384, 128))
    tn = 1280
    assert n % tn == 0 and P_DT + LANE <= tn
    kern = functools.partial(_inproj_kernel, tm=tm, n_lat=n_lat)
    return pl.pallas_call(
        kern,
        grid=(bsz, l_all // tm, n // tn),
        in_specs=[pl.BlockSpec((None, tm, d), lambda b, i, j: (b, i, 0))]
        + _mod_specs(0, 1, bsz)
        + [pl.BlockSpec((None, d, tn), lambda b, i, j: (layer, 0, j))],
        out_specs=[pl.BlockSpec((None, tm, tn), lambda b, i, j: (b, i, j)),
                   pl.BlockSpec((None, tm, LANE), lambda b, i, j: (b, i, 0)),
                   pl.BlockSpec((None, tm, d), lambda b, i, j: (b, i, 0))],
        out_shape=[jax.ShapeDtypeStruct((bsz, l_all, n), MXU_DT),
                   jax.ShapeDtypeStruct((bsz, l_all, LANE), F32),
                   jax.ShapeDtypeStruct((bsz, l_all, d), MXU_DT)],
        compiler_params=_params("arbitrary", "arbitrary", "arbitrary"),
        name="in_projection",
    )(xz, mod, mod, mod, mod, w_in_p)


def ffn_in(xz, mod, w_ffn_in, layer, n_lat):
    bsz, l_all, d = xz.shape
    hid = w_ffn_in.shape[2] // 2
    tm = _pick(l_all, (1088, 544, 384, 128))
    tn = 512
    nj = hid // tn
    kern = functools.partial(_ffn_in_kernel, tm=tm, n_lat=n_lat)
    return pl.pallas_call(
        kern,
        grid=(bsz, l_all // tm, nj),
        in_specs=[pl.BlockSpec((None, tm, d), lambda b, i, j: (b, i, 0))]
        + _mod_specs(3, 4, bsz)
        + [pl.BlockSpec((None, d, tn), lambda b, i, j: (layer, 0, j)),
           pl.BlockSpec((None, d, tn), lambda b, i, j: (layer, 0, j + nj))],
        out_specs=pl.BlockSpec((None, tm, tn), lambda b, i, j: (b, i, j)),
        out_shape=jax.ShapeDtypeStruct((bsz, l_all, hid), MXU_DT),
        scratch_shapes=[pltpu.VMEM((tm, d), MXU_DT)],
        compiler_params=_params("arbitrary", "arbitrary", "arbitrary"),
        name="ffn_in",
    )(xz, mod, mod, mod, mod, w_ffn_in, w_ffn_in)


def _resln_kernel(a_ref, w_ref, res_ref, g_ref, gc_ref, lng_ref, lnb_ref, o_ref, *, tm, n_lat, nk, rc):
    k = pl.program_id(2)
    i = pl.program_id(1)

    if nk > 1:
        @pl.when(k == 0)
        def _():
            o_ref[...] = _dot(a_ref[...], w_ref[...])

        @pl.when((k > 0) & (k < nk - 1))
        def _():
            o_ref[...] += _dot(a_ref[...], w_ref[...])

    @pl.when(k == nk - 1)
    def _():
        for r in range(tm // rc):
            rows = slice(r * rc, (r + 1) * rc)
            acc = _dot(a_ref[rows, :], w_ref[...])
            if nk > 1:
                acc = acc + o_ref[rows, :]
            row = i * tm + r * rc + lax.broadcasted_iota(jnp.int32, (rc, 1), 0)
            gate = jnp.where(row >= n_lat, gc_ref[...], g_ref[...])
            y = DEEPNORM_ALPHA * res_ref[rows, :] + gate * acc
            o_ref[rows, :] = _standardize(y) * lng_ref[...] + lnb_ref[...]


def matmul_res_ln(a, w, res, mod, k_gate, ln_g, ln_b, layer, n_lat, out_rows, tm_prefs, tk):
    bsz, l_all, kdim = a.shape
    d = w.shape[2]
    tm = _pick(l_all, tm_prefs)
    nk = kdim // tk
    rc = _pick(tm, (272, 192, 128))
    kern = functools.partial(_resln_kernel, tm=tm, n_lat=n_lat, nk=nk, rc=rc)
    w_mode = dict(pipeline_mode=pl.Buffered(1)) if nk == 1 else {}
    return pl.pallas_call(
        kern,
        grid=(bsz, pl.cdiv(out_rows, tm), nk),
        in_specs=[pl.BlockSpec((None, tm, tk), lambda b, i, k: (b, i, k)),
                  pl.BlockSpec((None, tk, d), lambda b, i, k: (layer, k, 0), **w_mode),
                  pl.BlockSpec((None, tm, d), lambda b, i, k: (b, i, 0)),
                  pl.BlockSpec((None, 1, d), lambda b, i, k: (b, 0, k_gate)),
                  pl.BlockSpec((None, 1, d), lambda b, i, k: (bsz, 0, k_gate)),
                  pl.BlockSpec((None, 1, d), lambda b, i, k: (layer, 0, 0)),
                  pl.BlockSpec((None, 1, d), lambda b, i, k: (layer, 0, 0))],
        out_specs=pl.BlockSpec((None, tm, d), lambda b, i, k: (b, i, 0)),
        out_shape=jax.ShapeDtypeStruct((bsz, out_rows, d), F32),
        compiler_params=_params("arbitrary", "arbitrary", "arbitrary"),
        name="matmul_res_ln",
    )(a, w, res, mod, mod, ln_g, ln_b)


def _merge_kernel(h_ref, oa_ref, ob_ref, oc_ref, od_ref, wg_ref, bg_ref, wb_ref, o_ref):
    h = h_ref[...]
    acc = None
    pending = None
    for i, o_r in list(enumerate((oa_ref, ob_ref, oc_ref, od_ref))) + [(None, None)]:
        nxt = None
        if o_r is not None:
            nxt = (_dot(h, wg_ref[i]) + bg_ref[i], _dot(o_r[...], wb_ref[i]))
        if pending is not None:
            g, t = pending
            term = _sigmoid(g) * t
            acc = term if acc is None else acc + term
        pending = nxt
    o_ref[...] = acc.astype(o_ref.dtype)


def merge_branches(h, outs, w_gate, b_gate4, w_branch, layer):
    bsz, l_all, d = h.shape
    bw = outs[0].shape[2]
    tm = _pick(l_all, (1088, 544, 384, 128))
    tn = 512
    o_spec = pl.BlockSpec((None, tm, bw), lambda b, i, j: (b, i, 0))
    return pl.pallas_call(
        _merge_kernel,
        grid=(bsz, l_all // tm, d // tn),
        in_specs=[pl.BlockSpec((None, tm, d), lambda b, i, j: (b, i, 0)), o_spec, o_spec, o_spec, o_spec,
                  pl.BlockSpec((None, N_BRANCH, d, tn), lambda b, i, j: (layer, 0, 0, j)),
                  pl.BlockSpec((None, N_BRANCH, 1, tn), lambda b, i, j: (layer, 0, 0, j)),
                  pl.BlockSpec((None, N_BRANCH, bw, tn), lambda b, i, j: (layer, 0, 0, j))],
        out_specs=pl.BlockSpec((None, tm, tn), lambda b, i, j: (b, i, j)),
        out_shape=jax.ShapeDtypeStruct((bsz, l_all, d), MXU_DT),
        compiler_params=_params("arbitrary", "arbitrary", "arbitrary"),
        name="merge_branches",
    )(h, *outs, w_gate, b_gate4, w_branch)


def _rope(x, cos, sin):
    lane = lax.broadcasted_iota(jnp.int32, x.shape, 1)
    partner = jnp.where(lane % 2 == 0, pltpu.roll(x, LANE - 1, 1), pltpu.roll(x, 1, 1))
    return x * cos + partner * sin


def _mla_prep_kernel(p_ref, gq_ref, gkv_ref, wuq_ref, wukv_ref, cos_ref, sin_ref, q_ref, k_ref, v_ref,
                     vrow_ref, *, rc):
    scale = (MLA_NOPE + MLA_ROPE) ** -0.5 * LOG2_E

    def body(t, carry):
        rows = pl.ds(pl.multiple_of(t * rc, 16), rc)
        p = p_ref[rows, :].astype(F32)
        cq = p[:, 0:512]
        ckv = p[:, 512:640]
        k_rot = p[:, 640:768]
        rq = lax.rsqrt(jnp.sum(cq * cq, axis=-1, keepdims=True) * (1.0 / MLA_Q_RANK) + RMS_EPS)
        cqn = (cq * rq * gq_ref[...]).astype(MXU_DT)
        rkv = lax.rsqrt(jnp.mean(ckv * ckv, axis=-1, keepdims=True) + RMS_EPS)
        ckvn = (ckv * rkv * gkv_ref[...]).astype(MXU_DT)
        qf = _dot(cqn, wuq_ref[...])
        kvf = _dot(ckvn, wukv_ref[...])
        cos = cos_ref[rows, :]
        sin = sin_ref[rows, :]
        k_rope = _rope(k_rot, cos, sin).astype(k_ref.dtype)
        for hd in range(MLA_HEADS):
            qb = hd * 256
            q_ref[hd, rows, 0:128] = (qf[:, qb:qb + 128] * scale).astype(q_ref.dtype)
            q_rope = _rope(qf[:, qb + 128:qb + 256], cos, sin)
            q_ref[hd, rows, 128:256] = (q_rope * scale).astype(q_ref.dtype)
            k_ref[hd, rows, 0:128] = kvf[:, hd * 256:hd * 256 + 128].astype(k_ref.dtype)
            k_ref[hd, rows, 128:256] = k_rope
            vrow_ref[hd, rows, :] = kvf[:, hd * 256 + 128:hd * 256 + 256]
        return carry

    lax.fori_loop(0, p_ref.shape[0] // rc, body, 0)
    for hd in range(MLA_HEADS):
        v_ref[hd] = vrow_ref[hd].T.astype(v_ref.dtype)


def mla_prep(p, gq, gkv, wuq, wukv, cos, sin, layer):
    bsz, l_all, _ = p.shape
    tr = _pick(l_all, (1088 * 2, 256, 128))
    rc = 272 if tr % 272 == 0 else tr
    h = MLA_HEADS
    return pl.pallas_call(
        functools.partial(_mla_prep_kernel, rc=rc),
        grid=(bsz, l_all // tr),
        in_specs=[pl.BlockSpec((None, tr, P_MLA_W), lambda b, i: (b, i, 0)),
                  pl.BlockSpec((None, 1, 512), lambda b, i: (layer, 0, 0)),
                  pl.BlockSpec((None, 1, 128), lambda b, i: (layer, 0, 0)),
                  pl.BlockSpec((None, 512, h * 256), lambda b, i: (layer, 0, 0)),
                  pl.BlockSpec((None, 128, h * 256), lambda b, i: (layer, 0, 0)),
                  pl.BlockSpec((tr, LANE), lambda b, i: (i, 0)),
                  pl.BlockSpec((tr, LANE), lambda b, i: (i, 0))],
        out_specs=[pl.BlockSpec((None, h, tr, 256), lambda b, i: (b, 0, i, 0)),
                   pl.BlockSpec((None, h, tr, 256), lambda b, i: (b, 0, i, 0)),
                   pl.BlockSpec((None, h, 128, tr), lambda b, i: (b, 0, 0, i))],
        out_shape=[jax.ShapeDtypeStruct((bsz, h, l_all, 256), MXU_DT),
                   jax.ShapeDtypeStruct((bsz, h, l_all, 256), MXU_DT),
                   jax.ShapeDtypeStruct((bsz, h, 128, l_all), MXU_DT)],
        scratch_shapes=[pltpu.VMEM((h, tr, 128), F32)],
        compiler_params=_params("arbitrary", "arbitrary"),
        name="mla_prep",
    )(p, gq, gkv, wuq, wukv, cos, sin)


def _gqa_prep_kernel(p_ref, gq_ref, gk_ref, cos_ref, sin_ref, q_ref, k_ref, v_ref):
    scale = GQA_HD ** -0.5 * LOG2_E
    cos = cos_ref[...]
    sin = sin_ref[...]

    def norm_rope(x, g):
        r = lax.rsqrt(jnp.mean(x * x, axis=-1, keepdims=True) + RMS_EPS)
        return _rope(x * r * g, cos, sin)

    for hd in range(GQA_HEADS):
        x = p_ref[:, hd * 128:(hd + 1) * 128].astype(F32)
        q_ref[hd] = (norm_rope(x, gq_ref[...]) * scale).astype(q_ref.dtype)
    for hd in range(GQA_KV_HEADS):
        x = p_ref[:, 512 + hd * 128:512 + (hd + 1) * 128].astype(F32)
        k_ref[hd] = norm_rope(x, gk_ref[...]).astype(k_ref.dtype)
        v = p_ref[:, 768 + hd * 128:768 + (hd + 1) * 128].astype(F32)
        v_ref[hd] = v.T.astype(v_ref.dtype)


def gqa_prep(p, gq, gk, cos, sin, layer):
    bsz, l_all, _ = p.shape
    tr = _pick(l_all, (1088 * 2, 256, 128))
    return pl.pallas_call(
        _gqa_prep_kernel,
        grid=(bsz, l_all // tr),
        in_specs=[pl.BlockSpec((None, tr, GQA_COLS), lambda b, i: (b, i, P_GQA // GQA_COLS)),
                  pl.BlockSpec((None, 1, 128), lambda b, i: (layer, 0, 0)),
                  pl.BlockSpec((None, 1, 128), lambda b, i: (layer, 0, 0)),
                  pl.BlockSpec((tr, LANE), lambda b, i: (i, 0)),
                  pl.BlockSpec((tr, LANE), lambda b, i: (i, 0))],
        out_specs=[pl.BlockSpec((None, GQA_HEADS, tr, 128), lambda b, i: (b, 0, i, 0)),
                   pl.BlockSpec((None, GQA_KV_HEADS, tr, 128), lambda b, i: (b, 0, i, 0)),
                   pl.BlockSpec((None, GQA_KV_HEADS, 128, tr), lambda b, i: (b, 0, 0, i))],
        out_shape=[jax.ShapeDtypeStruct((bsz, GQA_HEADS, l_all, 128), MXU_DT),
                   jax.ShapeDtypeStruct((bsz, GQA_KV_HEADS, l_all, 128), MXU_DT),
                   jax.ShapeDtypeStruct((bsz, GQA_KV_HEADS, 128, l_all), MXU_DT)],
        compiler_params=_params("arbitrary", "arbitrary"),
        name="gqa_prep",
    )(p, gq, gk, cos, sin)


def _attn_kernel(q_ref, k_ref, v_ref, o_ref, *, n_lat, n_lat_tiles, sub):
    i = pl.program_id(2)

    def attend_all(k, vt):
        groups = [slice(r, r + sub) for r in range(0, q_ref.shape[0], sub)]
        sts = [_dot_nt(k, q_ref[g, :]) for g in groups]
        es, ls = [], []
        for st in sts:
            m = jnp.max(st, axis=0, keepdims=True)
            e = jnp.exp2(st - m)
            ls.append(jnp.sum(e, axis=0, keepdims=True))
            es.append(e.astype(vt.dtype))
        for g, e, l in zip(groups, es, ls):
            ot = _dot(vt, e) / l
            o_ref[g, :] = ot.T.astype(o_ref.dtype)

    @pl.when(i < n_lat_tiles)
    def _():
        attend_all(k_ref[...], v_ref[...])

    @pl.when(i >= n_lat_tiles)
    def _():
        attend_all(k_ref[n_lat:, :], v_ref[:, n_lat:])


def attention(q, k, v, n_lat):
    bsz, h, l_all, dk = q.shape
    hkv = k.shape[1]
    grp = h // hkv
    dv = v.shape[2]
    tq = _pick(n_lat, (1024, 512, 256, 128))
    assert l_all - n_lat <= tq
    kern = functools.partial(_attn_kernel, n_lat=n_lat, n_lat_tiles=n_lat // tq, sub=min(256, tq))
    return pl.pallas_call(
        kern,
        grid=(bsz, h, pl.cdiv(l_all, tq)),
        in_specs=[pl.BlockSpec((None, None, tq, dk), lambda b, hh, i: (b, hh, i, 0)),
                  pl.BlockSpec((None, None, l_all, dk), lambda b, hh, i: (b, hh // grp, 0, 0)),
                  pl.BlockSpec((None, None, dv, l_all), lambda b, hh, i: (b, hh // grp, 0, 0))],
        out_specs=pl.BlockSpec((None, tq, dv), lambda b, hh, i: (b, i, hh)),
        out_shape=jax.ShapeDtypeStruct((bsz, l_all, h * dv), MXU_DT),
        compiler_params=_params("arbitrary", "arbitrary", "arbitrary"),
        name="attention",
    )(q, k, v)


CONV_TC = 256


def _conv3(u, w, b, n_lat):
    u = u.astype(F32)
    n = u.shape[0]
    row = lax.broadcasted_iota(jnp.int32, (n, 1), 0)
    prev = jnp.where((row == 0) | (row == n_lat), 0.0, pltpu.roll(u, 1, 0))
    nxt = jnp.where((row == n_lat - 1) | (row == n - 1), 0.0, pltpu.roll(u, n - 1, 0))
    return w[0:1, :] * prev + w[1:2, :] * u + w[2:3, :] * nxt + b


def _hy_conv_kernel(p0_ref, p1_ref, pv_ref, w0_ref, w1_ref, wv_ref, b0_ref, b1_ref, bv_ref,
                    x0_ref, vxt_ref, vxc_ref, *, n_lat):
    x0_ref[...] = _conv3(p0_ref[...], w0_ref[...], b0_ref[...], n_lat)
    x1 = _conv3(p1_ref[...], w1_ref[...], b1_ref[...], n_lat)
    v = _conv3(pv_ref[...], wv_ref[...], bv_ref[...], n_lat)
    vx = v * x1
    vxt_ref[...] = vx[:n_lat, :].T
    vxc_ref[...] = vx[n_lat:, :]


def hyena_conv(p, conv_w, conv_b3, layer, n_lat):
    bsz, l_all, _ = p.shape
    tc = CONV_TC
    nb = HY_W // tc
    base = P_HY // tc

    def pspec(off):
        return pl.BlockSpec((None, l_all, tc), lambda b, c: (b, 0, base + off + c))

    def wspec(off):
        return pl.BlockSpec((None, 3, tc), lambda b, c: (layer, 0, off + c))

    def bspec(off):
        return pl.BlockSpec((None, 1, tc), lambda b, c: (layer, 0, off + c))

    o_spec = pl.BlockSpec((None, l_all, tc), lambda b, c: (b, 0, c))
    return pl.pallas_call(
        functools.partial(_hy_conv_kernel, n_lat=n_lat),
        grid=(bsz, nb),
        in_specs=[pspec(0), pspec(nb), pspec(2 * nb), wspec(0), wspec(nb), wspec(2 * nb),
                  bspec(0), bspec(nb), bspec(2 * nb)],
        out_specs=[o_spec,
                   pl.BlockSpec((None, tc, n_lat), lambda b, c: (b, c, 0)),
                   pl.BlockSpec((None, l_all - n_lat, tc), lambda b, c: (b, 0, c))],
        out_shape=[jax.ShapeDtypeStruct((bsz, l_all, HY_W), F32),
                   jax.ShapeDtypeStruct((bsz, HY_W, n_lat), F32),
                   jax.ShapeDtypeStruct((bsz, l_all - n_lat, HY_W), F32)],
        compiler_params=_params("arbitrary", "arbitrary"),
        name="hyena_conv",
    )(p, p, p, conv_w, conv_w, conv_w, conv_b3, conv_b3, conv_b3)


def _mb_conv_kernel(p_ref, w_ref, b_ref, o_ref, *, n_lat):
    o_ref[...] = _silu(_conv3(p_ref[...], w_ref[...], b_ref[...], n_lat))


def mamba_conv(p, conv_w, conv_b3, layer, n_lat):
    bsz, l_all, _ = p.shape
    tc = CONV_TC
    nb = MB_CONV_CH // tc
    base = P_XBC // tc
    return pl.pallas_call(
        functools.partial(_mb_conv_kernel, n_lat=n_lat),
        grid=(bsz, nb),
        in_specs=[pl.BlockSpec((None, l_all, tc), lambda b, c: (b, 0, base + c)),
                  pl.BlockSpec((None, 3, tc), lambda b, c: (layer, 0, c)),
                  pl.BlockSpec((None, 1, tc), lambda b, c: (layer, 0, c))],
        out_specs=pl.BlockSpec((None, l_all, tc), lambda b, c: (b, 0, c)),
        out_shape=jax.ShapeDtypeStruct((bsz, l_all, MB_CONV_CH), F32),
        compiler_params=_params("arbitrary", "arbitrary"),
        name="mamba_conv",
    )(p, conv_w, conv_b3)


def _softplus(x):
    return jnp.maximum(x, 0.0) + jnp.log(1.0 + jnp.exp(-jnp.abs(x)))


def _mb_dt_kernel(p_ref, bias_ref, o_ref):
    dt = _softplus(p_ref[...] + bias_ref[...])
    o_ref[0] = dt
    o_ref[1] = pltpu.roll(dt, LANE - MB_HEADS, 1)


def mamba_dt(p, dt_bias_row, layer):
    bsz, l_all, _ = p.shape
    return pl.pallas_call(
        _mb_dt_kernel,
        grid=(bsz,),
        in_specs=[pl.BlockSpec((None, l_all, LANE), lambda b: (b, 0, 0)),
                  pl.BlockSpec((None, 1, LANE), lambda b: (layer, 0, 0))],
        out_specs=pl.BlockSpec((2, None, l_all, LANE), lambda b: (0, b, 0, 0)),
        out_shape=jax.ShapeDtypeStruct((2, bsz, l_all, LANE), F32),
        compiler_params=_params("arbitrary"),
        name="mamba_dt",
    )(p, dt_bias_row)


def _hy_filter_kernel(f_ref, aux_ref, w1_ref, b1_ref, w2_ref, b2_ref, w3_ref, fr_ref, dl_ref, o_ref, *,
                      channel_major):
    fr = fr_ref[...]
    tr = f_ref.shape[0]
    n_ch = 4 if tr % 512 == 0 else 1
    rc = tr // n_ch
    chunks = [slice(r * rc, (r + 1) * rc) for r in range(n_ch)]
    hdn = [_dot_hi(f_ref[rows, :], w1_ref[...]) for rows in chunks]
    hdn = [jnp.sin(fr * (x + b1_ref[...])) for x in hdn]
    hdn = [_dot_hi(x, w2_ref[...]) for x in hdn]
    hdn = [jnp.sin(fr * (x + b2_ref[...])) for x in hdn]
    filts = [_dot_hi(x, w3_ref[...]) for x in hdn]
    for rows, filt in zip(chunks, filts):
        aux = aux_ref[rows, :]
        t = aux[:, 0:1]
        is_fwd = aux[:, 1:2] > 0.5
        valid = aux[:, 2:3]
        window = jnp.exp(-t * dl_ref[...]) * valid
        k2 = jnp.where(is_fwd, filt[:, :HY_W], filt[:, HY_W:]) * window
        if channel_major:
            o_ref[:, rows] = k2.T
        else:
            o_ref[rows, :] = k2


def hyena_filter(feats2, aux, w1p, b1p, w2p, b2p, w3p, frp, deltas, layer, channel_major):
    rows = feats2.shape[0]
    tr = _pick(rows, (512, 256))

    def lspec(shape):
        return pl.BlockSpec((None,) + shape, lambda i: (layer, 0, 0))

    if channel_major:
        out_spec = pl.BlockSpec((HY_W, tr), lambda i: (0, i))
        out_shape = jax.ShapeDtypeStruct((HY_W, rows), F32)
    else:
        out_spec = pl.BlockSpec((tr, HY_W), lambda i: (i, 0))
        out_shape = jax.ShapeDtypeStruct((rows, HY_W), F32)
    return pl.pallas_call(
        functools.partial(_hy_filter_kernel, channel_major=channel_major),
        grid=(rows // tr,),
        in_specs=[pl.BlockSpec((tr, LANE), lambda i: (i, 0)),
                  pl.BlockSpec((tr, LANE), lambda i: (i, 0)),
                  lspec((LANE, LANE)), lspec((1, LANE)), lspec((LANE, LANE)), lspec((1, LANE)),
                  lspec((LANE, 2 * HY_W)), lspec((1, LANE)),
                  pl.BlockSpec((1, HY_W), lambda i: (0, 0))],
        out_specs=out_spec,
        out_shape=out_shape,
        compiler_params=_params("arbitrary"),
        name="hyena_filter",
    )(feats2, aux, w1p, b1p, w2p, b2p, w3p, frp, deltas)


DFT_MINOR = 256


def _snap(c):
    for v in (0.0, 1.0, -1.0):
        if abs(c - v) < 1e-12:
            return v
    return float(c)


def _lincomb(terms):
    acc = None
    for cf, tile in terms:
        if cf == 0.0:
            continue
        v = tile()
        if acc is None:
            acc = v if cf == 1.0 else (-v if cf == -1.0 else cf * v)
        elif cf == 1.0:
            acc = acc + v
        elif cf == -1.0:
            acc = acc - v
        else:
            acc = acc + cf * v
    return acc


def _dft_consts(n_seq):
    n = 2 * n_seq
    n2 = DFT_MINOR
    n1 = n // n2
    nkj = n1 // 2 + 1
    ang1 = 2.0 * np.pi * np.outer(np.arange(nkj), np.arange(n1)) / n1
    cos1 = [[_snap(v) for v in r] for r in np.cos(ang1)]
    sin1 = [[_snap(v) for v in r] for r in np.sin(ang1)]
    ang_t = 2.0 * np.pi * np.outer(np.arange(nkj), np.arange(n2)) / n
    rows = ((nkj + 7) // 8) * 8
    tw_c = np.zeros((rows, n2))
    tw_s = np.zeros((rows, n2))
    tw_c[:nkj] = np.cos(ang_t)
    tw_s[:nkj] = np.sin(ang_t)
    ang2 = 2.0 * np.pi * np.outer(np.arange(n2), np.arange(n2)) / n2
    c2, s2 = np.cos(ang2), np.sin(ang2)
    m_fwd = np.block([[c2, -s2], [s2, c2]])
    m_inv = np.block([[c2, s2], [-s2, c2]])
    as32 = lambda a: jnp.asarray(a, dtype=F32)

    def split(m):
        m32 = as32(m)
        hi = m32.astype(MXU_DT)
        lo = (m32 - hi.astype(F32)).astype(MXU_DT)
        return jnp.stack([hi, lo])

    return dict(n=n, n1=n1, nkj=nkj, cos1=cos1, sin1=sin1, tw_c=as32(tw_c), tw_s=as32(tw_s),
                m_fwd=split(m_fwd), m_inv=split(m_inv))


def _dot_split(a, m_ref):
    a_hi = a.astype(MXU_DT)
    a_lo = (a - a_hi.astype(F32)).astype(MXU_DT)
    return _dot(a_hi, m_ref[0]) + _dot(a_lo, m_ref[0]) + _dot(a_hi, m_ref[1])


def _outer_fwd_block(src_ref, z_ref, twc_ref, tws_ref, cb, n_in, cst):
    n2 = DFT_MINOR
    nkj, cos1, sin1 = cst["nkj"], cst["cos1"], cst["sin1"]
    tile = lambda j: (lambda: src_ref[cb * 8:cb * 8 + 8, j * n2:(j + 1) * n2])
    for kj in range(nkj):
        re = _lincomb([(cos1[kj][j], tile(j)) for j in range(n_in)])
        im = _lincomb([(-sin1[kj][j], tile(j)) for j in range(n_in)])
        if kj > 0 and im is not None:
            c = twc_ref[kj:kj + 1, :]
            s = tws_ref[kj:kj + 1, :]
            re, im = re * c + im * s, im * c - re * s
        elif kj > 0:
            re, im = re * twc_ref[kj:kj + 1, :], -re * tws_ref[kj:kj + 1, :]
        r = (cb * nkj + kj) * 8
        z_ref[r:r + 8, 0:n2] = re
        z_ref[r:r + 8, n2:2 * n2] = jnp.zeros_like(re) if im is None else im


def _outer_inv_block(z_ref, yt_ref, twc_ref, tws_ref, cb, cst):
    n2 = DFT_MINOR
    nkj, n1, cos1, sin1 = cst["nkj"], cst["n1"], cst["cos1"], cst["sin1"]
    inv_n = 1.0 / cst["n"]
    for kj in range(1, nkj):
        r = (cb * nkj + kj) * 8
        re = z_ref[r:r + 8, 0:n2]
        im = z_ref[r:r + 8, n2:2 * n2]
        c = twc_ref[kj:kj + 1, :]
        s = tws_ref[kj:kj + 1, :]
        z_ref[r:r + 8, 0:n2] = re * c - im * s
        z_ref[r:r + 8, n2:2 * n2] = im * c + re * s
    row = lambda kj: (cb * nkj + kj) * 8
    g_re = lambda kj: (lambda: z_ref[row(kj):row(kj) + 8, 0:n2])
    g_im = lambda kj: (lambda: z_ref[row(kj):row(kj) + 8, n2:2 * n2])
    for j in range(n1 // 2):
        terms = []
        for kj in range(nkj):
            wgt = inv_n if kj in (0, nkj - 1) else 2.0 * inv_n
            terms.append((wgt * cos1[kj][j], g_re(kj)))
            terms.append((-wgt * sin1[kj][j], g_im(kj)))
        yt_ref[cb * 8:cb * 8 + 8, j * n2:(j + 1) * n2] = _lincomb(terms)


def _spectrum_kernel(k_ref, twc_ref, tws_ref, mf_ref, h_ref, z_ref, *, ct, cst, n_chunks):
    nkj = cst["nkj"]
    per = ct // 8 // n_chunks
    rc = per * nkj * 8
    for c in range(n_chunks + 1):
        if c < n_chunks:
            for cb in range(c * per, (c + 1) * per):
                _outer_fwd_block(k_ref, z_ref, twc_ref, tws_ref, cb, cst["n1"], cst)
        if c >= 1:
            rows = slice((c - 1) * rc, c * rc)
            h_ref[rows, :] = _dot_split(z_ref[rows, :], mf_ref)


def _long_conv_kernel(x_ref, h_ref, twc_ref, tws_ref, mf_ref, mi_ref, x0_ref, vxc_ref, yc_ref, skip_ref,
                      skip_row_ref, o_ref, z_ref, yt_ref, *, ct, cst, n_chunks, n_lat):
    n2 = DFT_MINOR
    nkj, n1 = cst["nkj"], cst["n1"]
    per = ct // 8 // n_chunks
    rc = per * nkj * 8
    for c in range(n_chunks + 2):
        if c < n_chunks:
            for cb in range(c * per, (c + 1) * per):
                _outer_fwd_block(x_ref, z_ref, twc_ref, tws_ref, cb, n1 // 2, cst)
        if 1 <= c <= n_chunks:
            rows = slice((c - 1) * rc, c * rc)
            x = _dot_split(z_ref[rows, :], mf_ref)
            xre, xim = x[:, :n2], x[:, n2:]
            hre, him = h_ref[rows, 0:n2], h_ref[rows, n2:2 * n2]
            y = jnp.concatenate([xre * hre - xim * him, xre * him + xim * hre], axis=1)
            z_ref[rows, :] = _dot_split(y, mi_ref)
        if c >= 2:
            for cb in range((c - 2) * per, (c - 1) * per):
                _outer_inv_block(z_ref, yt_ref, twc_ref, tws_ref, cb, cst)

    y = (yt_ref[...] + x_ref[...] * skip_ref[...]).T
    o_ref[0:n_lat, :] = (x0_ref[0:n_lat, :] * y).astype(o_ref.dtype)
    yc = yc_ref[...] + vxc_ref[...] * skip_row_ref[...]
    o_ref[n_lat:, :] = (x0_ref[n_lat:, :] * yc).astype(o_ref.dtype)


def filter_spectrum(k2t, cst):
    ch, n = k2t.shape
    ct = LANE
    nkj = cst["nkj"]
    n_chunks = _pick(nkj * ct // 8, (4, 5, 1))
    full = lambda a: pl.BlockSpec(a.shape, lambda c: (0,) * a.ndim)
    return pl.pallas_call(
        functools.partial(_spectrum_kernel, ct=ct, cst=cst, n_chunks=n_chunks),
        grid=(ch // ct,),
        in_specs=[pl.BlockSpec((ct, n), lambda c: (c, 0)),
                  full(cst["tw_c"]), full(cst["tw_s"]), full(cst["m_fwd"])],
        out_specs=pl.BlockSpec((None, nkj * ct, 2 * DFT_MINOR), lambda c: (c, 0, 0)),
        out_shape=jax.ShapeDtypeStruct((ch // ct, nkj * ct, 2 * DFT_MINOR), F32),
        scratch_shapes=[pltpu.VMEM((nkj * ct, 2 * DFT_MINOR), F32)],
        compiler_params=_params("arbitrary"),
        name="filter_spectrum",
    )(k2t, cst["tw_c"], cst["tw_s"], cst["m_fwd"])


def hyena_long_conv(vxt, h_spec, x0, vx_ctx, y_ctx, skip_col, skip_row, cst, layer, n_lat):
    bsz, ch, _ = vxt.shape
    l_all = x0.shape[1]
    n_ctx = l_all - n_lat
    ct = LANE
    nkj = cst["nkj"]
    n_chunks = _pick(nkj * ct // 8, (4, 5, 1))
    full = lambda a: pl.BlockSpec(a.shape, lambda b, c: (0,) * a.ndim)
    return pl.pallas_call(
        functools.partial(_long_conv_kernel, ct=ct, cst=cst, n_chunks=n_chunks, n_lat=n_lat),
        grid=(bsz, ch // ct),
        in_specs=[pl.BlockSpec((None, ct, n_lat), lambda b, c: (b, c, 0)),
                  pl.BlockSpec((None, nkj * ct, 2 * DFT_MINOR), lambda b, c: (c, 0, 0)),
                  full(cst["tw_c"]), full(cst["tw_s"]), full(cst["m_fwd"]), full(cst["m_inv"]),
                  pl.BlockSpec((None, l_all, ct), lambda b, c: (b, 0, c)),
                  pl.BlockSpec((None, n_ctx, ct), lambda b, c: (b, 0, c)),
                  pl.BlockSpec((None, n_ctx, ct), lambda b, c: (b, 0, c)),
                  pl.BlockSpec((None, ct, 1), lambda b, c: (layer, c, 0)),
                  pl.BlockSpec((None, 1, ct), lambda b, c: (layer, 0, c))],
        out_specs=pl.BlockSpec((None, l_all, ct), lambda b, c: (b, 0, c)),
        out_shape=jax.ShapeDtypeStruct((bsz, l_all, ch), MXU_DT),
        scratch_shapes=[pltpu.VMEM((nkj * ct, 2 * DFT_MINOR), F32), pltpu.VMEM((ct, n_lat), F32)],
        compiler_params=_params("arbitrary", "arbitrary"),
        name="hyena_long_conv",
    )(vxt, h_spec, cst["tw_c"], cst["tw_s"], cst["m_fwd"], cst["m_inv"], x0, vx_ctx, y_ctx, skip_col, skip_row)


def _ctx_conv_kernel(v_ref, k_ref, fd_ref, fk_ref, gi_ref, o_ref, *, kp):
    x = _dot_hi(fd_ref[...], v_ref[...])
    h = _dot_hi(fk_ref[...], k_ref[...])
    xre, xim = x[:kp], x[kp:]
    hre, him = h[:kp], h[kp:]
    y = jnp.concatenate([xre * hre - xim * him, xre * him + xim * hre], axis=0)
    o_ref[...] = _dot_hi(gi_ref[...], y)


def _ctx_tables(n_ctx):
    n = 2 * n_ctx
    kh = n_ctx + 1
    kp = ((kh + 7) // 8) * 8
    k = np.arange(kp)[:, None]
    live = (k < kh).astype(np.float64)
    ang = 2.0 * np.pi * k * np.arange(n)[None, :] / n
    ck, sk = np.cos(ang) * live, np.sin(ang) * live
    fk = np.concatenate([ck, -sk], axis=0)
    fd = fk[:, :n_ctx]
    wk = np.where((k == 0) | (k == n_ctx), 1.0, 2.0) * live / n
    gi = np.concatenate([(ck * wk).T, (-sk * wk).T], axis=1)[:n_ctx]
    as32 = lambda a: jnp.asarray(a, dtype=F32)
    return dict(kp=kp, fd=as32(fd), fk=as32(fk), gi=as32(gi))


def long_conv_ctx(vx_ctx, k2c, tb):
    bsz, n_ctx, ch = vx_ctx.shape
    kp = tb["kp"]
    full = lambda a: pl.BlockSpec(a.shape, lambda b, c: (0,) * a.ndim)
    return pl.pallas_call(
        functools.partial(_ctx_conv_kernel, kp=kp),
        grid=(bsz, ch // LANE),
        in_specs=[pl.BlockSpec((None, n_ctx, LANE), lambda b, c: (b, 0, c)),
                  pl.BlockSpec((2 * n_ctx, LANE), lambda b, c: (0, c)),
                  full(tb["fd"]), full(tb["fk"]), full(tb["gi"])],
        out_specs=pl.BlockSpec((None, n_ctx, LANE), lambda b, c: (b, 0, c)),
        out_shape=jax.ShapeDtypeStruct((bsz, n_ctx, ch), F32),
        compiler_params=_params("arbitrary", "arbitrary"),
        name="ctx_long_conv",
    )(vx_ctx, k2c, tb["fd"], tb["fk"], tb["gi"])


def _split3(a):
    a1 = a.astype(MXU_DT)
    r = a - a1.astype(F32)
    a2 = r.astype(MXU_DT)
    a3 = (r - a2.astype(F32)).astype(MXU_DT)
    return a1, a2, a3


def _ssd_chunk(backward, xs_ref, b_ref, c_ref, dt_ref, alog, ex, y_ref, state_ref):
    n = MB_CHUNK
    li = lax.broadcasted_iota(jnp.int32, (n, n), 0)
    si = lax.broadcasted_iota(jnp.int32, (n, n), 1)
    mask = (li <= si) if backward else (li >= si)
    tri = mask.astype(MXU_DT)
    lane = lax.broadcasted_iota(jnp.int32, (n, LANE), 1)

    dt = dt_ref[...]
    da = dt * (-jnp.exp(alog))
    cum = sum(_dot(tri, t) for t in _split3(da))
    cum_t = cum.T
    both = jnp.concatenate([cum, dt], axis=0)
    both_e = sum(_dot(t, ex) for t in _split3(both))
    cum_e = both_e[:n]
    dt_e = both_e[n:]
    total_e = cum_e[0:1] if backward else cum_e[n - 1:n]

    xdt = xs_ref[...] * dt_e
    xdt_b = xdt.astype(MXU_DT)
    w_b = (jnp.exp(total_e - cum_e) * xdt).astype(MXU_DT)
    st = state_ref[...]
    y_parts = []
    s_parts = []
    gw = MB_STATE
    hw = (MB_HEADS // MB_GROUPS) * MB_HEADDIM
    for g in range(MB_GROUPS):
        cg = c_ref[:, g * gw:(g + 1) * gw].astype(MXU_DT)
        bg = b_ref[:, g * gw:(g + 1) * gw]
        cb = _dot_nt(cg, bg.astype(MXU_DT))
        y_off = _dot(cg, st[:, g * hw:(g + 1) * hw].astype(MXU_DT))
        s_parts.append(_dot(bg.T.astype(MXU_DT), w_b[:, g * hw:(g + 1) * hw]))
        diag = []
        for j in range(hw // LANE):
            lo = g * hw + j * LANE
            xp = xdt_b[:, lo:lo + LANE]
            pair = []
            for e in range(2):
                hd = (lo // MB_HEADDIM) + e
                seg = cum[:, hd:hd + 1] - cum_t[hd:hd + 1, :]
                decay = jnp.exp(jnp.where(mask, seg, -jnp.inf))
                pair.append(_dot((cb * decay).astype(MXU_DT), xp))
            diag.append(jnp.where(lane < MB_HEADDIM, pair[0], pair[1]))
        y_parts.append(jnp.concatenate(diag, axis=1) + y_off * jnp.exp(cum_e[:, g * hw:(g + 1) * hw]))
    y_ref[...] = jnp.concatenate(y_parts, axis=1)
    state_ref[...] = st * jnp.exp(total_e) + jnp.concatenate(s_parts, axis=1)


def _ssd_kernel(xf_ref, bf_ref, cf_ref, dtf_ref, xb_ref, bb_ref, cb_ref, dtb_ref, alog_ref, ex_ref,
                yf_ref, yb_ref, state_ref):
    @pl.when(pl.program_id(1) == 0)
    def _():
        state_ref[...] = jnp.zeros_like(state_ref)

    ex = ex_ref[...]
    for e in range(xf_ref.shape[0]):
        _ssd_chunk(False, xf_ref.at[e], bf_ref.at[e], cf_ref.at[e], dtf_ref.at[e], alog_ref[0], ex,
                   yf_ref.at[e], state_ref.at[2 * e])
        _ssd_chunk(True, xb_ref.at[e], bb_ref.at[e], cb_ref.at[e], dtb_ref.at[e], alog_ref[1], ex,
                   yb_ref.at[e], state_ref.at[2 * e + 1])


def ssd_scan(xbc, dt2, a_log_rows, expand, layer, n_lat):
    bsz, l_all, _ = xbc.shape
    nc = l_all // MB_CHUNK
    ncl = n_lat // MB_CHUNK
    fwd = lambda c: (c + ncl) % nc
    bwd = lambda c: nc - 1 - c

    nb = 2 if bsz % 2 == 0 else 1

    def specs(blk, direction):
        return [pl.BlockSpec((nb, MB_CHUNK, MB_INNER), lambda b, c: (b, blk(c), 0)),
                pl.BlockSpec((nb, MB_CHUNK, 256), lambda b, c: (b, blk(c), 2)),
                pl.BlockSpec((nb, MB_CHUNK, 256), lambda b, c: (b, blk(c), 3)),
                pl.BlockSpec((None, nb, MB_CHUNK, LANE), lambda b, c: (direction, b, blk(c), 0))]

    y_shape = jax.ShapeDtypeStruct((bsz, l_all, MB_INNER), F32)
    return pl.pallas_call(
        _ssd_kernel,
        grid=(bsz // nb, nc),
        in_specs=specs(fwd, 0) + specs(bwd, 1)
        + [pl.BlockSpec((None, 2, 1, LANE), lambda b, c: (layer, 0, 0, 0)),
           pl.BlockSpec((LANE, MB_INNER), lambda b, c: (0, 0))],
        out_specs=[pl.BlockSpec((nb, MB_CHUNK, MB_INNER), lambda b, c: (b, fwd(c), 0)),
                   pl.BlockSpec((nb, MB_CHUNK, MB_INNER), lambda b, c: (b, bwd(c), 0))],
        out_shape=[y_shape, y_shape],
        scratch_shapes=[pltpu.VMEM((2 * nb, MB_STATE, MB_INNER), F32)],
        compiler_params=_params("arbitrary", "arbitrary"),
        name="ssd_scan",
    )(xbc, xbc, xbc, dt2, xbc, xbc, xbc, dt2, a_log_rows, expand)


def _mb_finish_kernel(yf_ref, yb_ref, xs_ref, z_ref, dsk_ref, g_ref, o_ref):
    y = yf_ref[...] + yb_ref[...] + xs_ref[...] * dsk_ref[...]
    y = y * _silu(z_ref[...].astype(F32))
    r = lax.rsqrt(jnp.mean(y * y, axis=-1, keepdims=True) + RMS_EPS)
    o_ref[...] = (y * r * g_ref[...]).astype(o_ref.dtype)


def mamba_finish(y_f, y_b, xbc, p, d_skip_e, norm_g3, layer):
    bsz, l_all, _ = xbc.shape
    tr = _pick(l_all, (544, 384, 128))
    w = MB_INNER
    return pl.pallas_call(
        _mb_finish_kernel,
        grid=(bsz, l_all // tr),
        in_specs=[pl.BlockSpec((None, tr, w), lambda b, i: (b, i, 0)),
                  pl.BlockSpec((None, tr, w), lambda b, i: (b, i, 0)),
                  pl.BlockSpec((None, tr, w), lambda b, i: (b, i, 0)),
                  pl.BlockSpec((None, tr, w), lambda b, i: (b, i, P_Z // w)),
                  pl.BlockSpec((None, 1, w), lambda b, i: (layer, 0, 0)),
                  pl.BlockSpec((None, 1, w), lambda b, i: (layer, 0, 0))],
        out_specs=pl.BlockSpec((None, tr, w), lambda b, i: (b, i, 0)),
        out_shape=jax.ShapeDtypeStruct((bsz, l_all, w), MXU_DT),
        compiler_params=_params("arbitrary", "arbitrary"),
        name="mamba_finish",
    )(y_f, y_b, xbc, p, d_skip_e, norm_g3)


_W_IN_MOVES = (
    (0, 0, MLA_Q_RANK),
    (512, MLA_Q_RANK, MLA_KV_RANK),
    (640, MLA_Q_RANK + MLA_KV_RANK, MLA_ROPE),
    (P_DT, MLA_COLS + GQA_COLS + HY_COLS + MB_INNER + MB_CONV_CH, 2 * MB_HEADS),
    (P_GQA, MLA_COLS, GQA_COLS),
    (P_Z, MLA_COLS + GQA_COLS + HY_COLS, MB_INNER),
    (P_HY, MLA_COLS + GQA_COLS, HY_COLS),
    (P_XBC, MLA_COLS + GQA_COLS + HY_COLS + MB_INNER, MB_CONV_CH),
)


def _w_in_prep_kernel(w_ref, o_ref):
    o_ref[...] = jnp.zeros_like(o_ref)
    for dst, src, n in _W_IN_MOVES:
        o_ref[:, dst:dst + n] = w_ref[:, src:src + n].astype(o_ref.dtype)


def _prep_w_in(w_in):
    nl, d, cols = w_in.shape
    tr = 256
    return pl.pallas_call(
        _w_in_prep_kernel,
        grid=(nl, d // tr),
        in_specs=[pl.BlockSpec((None, tr, cols), lambda l, i: (l, i, 0))],
        out_specs=pl.BlockSpec((None, tr, P_COLS), lambda l, i: (l, i, 0)),
        out_shape=jax.ShapeDtypeStruct((nl, d, P_COLS), MXU_DT),
        compiler_params=_params("arbitrary", "arbitrary"),
        name="w_in_layout",
    )(w_in)


def _prep_w_uq(w_uq):
    nl = w_uq.shape[0]
    w = w_uq.reshape(nl, MLA_Q_RANK, MLA_HEADS, MLA_NOPE + MLA_ROPE)
    w = _pad_to(w, (nl, 512, MLA_HEADS, 256))
    return w.reshape(nl, 512, MLA_HEADS * 256).astype(MXU_DT)


def _rope_tables(n_lat, n_ctx, rot_dim):
    rows = n_lat // GRID_W
    row = jnp.repeat(jnp.arange(rows, dtype=F32), GRID_W)
    col = jnp.tile(jnp.arange(GRID_W, dtype=F32), rows)
    n_freq = rot_dim // 4
    inv_freq = ROPE_THETA ** (-jnp.arange(n_freq, dtype=F32) / n_freq)
    ang = jnp.concatenate([row[:, None] * inv_freq, col[:, None] * inv_freq], axis=-1)
    half = rot_dim // 2
    cos = jnp.concatenate([jnp.cos(ang), jnp.ones((n_ctx, half), F32)], axis=0)
    sin = jnp.concatenate([jnp.sin(ang), jnp.zeros((n_ctx, half), F32)], axis=0)
    cos_t = jnp.repeat(cos, 2, axis=-1)
    sin_t = jnp.stack([-sin, sin], axis=-1).reshape(n_lat + n_ctx, rot_dim)
    shape = (n_lat + n_ctx, LANE)
    return _pad_to(cos_t, shape), _pad_to(sin_t, shape)


def _filter_inputs(n):
    t = jnp.linspace(0.0, 1.0, n, dtype=F32)[:, None]
    omega = 2.0 * math.pi * jnp.arange(n, dtype=F32) / n
    bands = jnp.linspace(1e-4, HY_BANDS - 1, HY_BANDS, dtype=F32)
    ang = omega[:, None] * bands[None, :]
    feats = jnp.concatenate([t, jnp.cos(ang), -jnp.sin(ang)], axis=-1)
    zero = jnp.zeros((1, HY_EMB), F32)
    feats2 = jnp.concatenate([feats, zero, feats[1:][::-1]], axis=0)
    feats2 = jnp.concatenate([feats2, jnp.zeros((2 * n, LANE - HY_EMB), F32)], axis=-1)
    t2 = jnp.concatenate([t, jnp.zeros((1, 1), F32), t[1:][::-1]], axis=0)
    idx = jnp.arange(2 * n)[:, None]
    aux = jnp.concatenate([t2, (idx < n).astype(F32), (idx != n).astype(F32),
                           jnp.zeros((2 * n, LANE - 3), F32)], axis=-1)
    return feats2, aux


def _pad_to(a, shape):
    pads = [(0, s - d) for d, s in zip(a.shape, shape)]
    return jnp.pad(a, pads)


def kernel(x, c, ctx, c_ctx, w_ada, b_ada, w_in, mla_q_norm, mla_kv_norm, mla_w_uq, mla_w_ukv, gqa_q_norm, gqa_k_norm, hy_conv_w, hy_conv_b, hy_w1, hy_b1, hy_w2, hy_b2, hy_w3, hy_freq, hy_skip, mb_conv_w, mb_conv_b, mb_a_log, mb_dt_bias, mb_d, mb_norm, w_mgate, b_mgate, w_branch, w_out, ln1_g, ln1_b, w_ffn_in, w_ffn_out, ln2_g, ln2_b):
    bsz, n_lat, d = x.shape
    n_ctx = ctx.shape[1]
    nl = w_in.shape[0]
    assert d == D_MODEL and bsz < 8 and n_lat % n_ctx == 0 and n_ctx % MB_CHUNK == 0

    w_in_p = _prep_w_in(w_in)
    w_uq_p = _prep_w_uq(mla_w_uq)
    w_ukv_p = mla_w_ukv.astype(MXU_DT)
    gq_mla = _pad_to(mla_q_norm, (nl, 512))[:, None, :]
    gkv_mla = mla_kv_norm[:, None, :]
    gq_gqa = gqa_q_norm[:, None, :]
    gk_gqa = gqa_k_norm[:, None, :]
    w_gate_b = w_mgate.astype(MXU_DT)
    w_branch_b = w_branch.astype(MXU_DT)
    w_out_b = w_out.astype(MXU_DT)
    w_ffn_in_b = w_ffn_in.astype(MXU_DT)
    w_ffn_out_b = w_ffn_out.astype(MXU_DT)
    b_gate4 = b_mgate[:, :, None, :]
    b_ada3 = b_ada[:, None, :]
    hy_conv_b3 = hy_conv_b[:, None, :]
    mb_conv_b3 = mb_conv_b[:, None, :]
    hy_skip3 = hy_skip[:, None, :]
    hy_skip_col = hy_skip[:, :, None]
    ln1_g3, ln1_b3, ln2_g3, ln2_b3 = (a[:, None, :] for a in (ln1_g, ln1_b, ln2_g, ln2_b))
    hy_w1p = _pad_to(hy_w1, (nl, LANE, LANE))
    hy_b1p = _pad_to(hy_b1, (nl, LANE))[:, None, :]
    hy_w2p = _pad_to(hy_w2, (nl, LANE, LANE))
    hy_b2p = _pad_to(hy_b2, (nl, LANE))[:, None, :]
    hy_w3p = _pad_to(hy_w3, (nl, LANE, 2 * HY_W))
    hy_frp = _pad_to(hy_freq, (nl, LANE))[:, None, :]
    deltas = jnp.abs(jnp.linspace(HY_MIN_DECAY, HY_MAX_DECAY, HY_W, dtype=F32))[None, :]
    dt_bias_row = _pad_to(mb_dt_bias.reshape(nl, 2 * MB_HEADS), (nl, LANE))[:, None, :]
    a_log_rows = _pad_to(mb_a_log, (nl, 2, LANE))[:, :, None, :]
    d_skip_e = jnp.repeat(mb_d, MB_HEADDIM, axis=-1)[:, None, :]
    mb_norm3 = mb_norm[:, None, :]
    expand = jnp.asarray(np.kron(np.eye(LANE, MB_HEADS), np.ones((1, MB_HEADDIM))), dtype=MXU_DT)

    cos_m, sin_m = _rope_tables(n_lat, n_ctx, MLA_ROPE)
    cos_g, sin_g = _rope_tables(n_lat, n_ctx, GQA_HD)
    feats_lat, aux_lat = _filter_inputs(n_lat)
    feats_ctx, aux_ctx = _filter_inputs(n_ctx)
    dft_lat = _dft_consts(n_lat)
    tb_ctx = _ctx_tables(n_ctx)

    c8 = jnp.concatenate([c, c_ctx[None, :], jnp.zeros((8 - bsz - 1, d), F32)], axis=0)
    xz = jnp.concatenate([x, ctx], axis=1)

    for l in range(nl):
        last = l == nl - 1
        mod = ada_mod(c8, w_ada, b_ada3, l).reshape(8, 1, 6 * d)
        p, p_dt, h = in_projection(xz, mod, w_in_p, l, n_lat)

        q, k, v = mla_prep(p, gq_mla, gkv_mla, w_uq_p, w_ukv_p, cos_m, sin_m, l)
        oa = attention(q, k, v, n_lat)
        q, k, v = gqa_prep(p, gq_gqa, gk_gqa, cos_g, sin_g, l)
        ob = attention(q, k, v, n_lat)
        x0, vxt, vx_ctx = hyena_conv(p, hy_conv_w, hy_conv_b3, l, n_lat)
        k2t = hyena_filter(feats_lat, aux_lat, hy_w1p, hy_b1p, hy_w2p, hy_b2p, hy_w3p, hy_frp, deltas, l, True)
        h_spec = filter_spectrum(k2t, dft_lat)
        if last:
            y_ctx = jnp.zeros((bsz, n_ctx, HY_W), F32)
        else:
            k2c = hyena_filter(feats_ctx, aux_ctx, hy_w1p, hy_b1p, hy_w2p, hy_b2p, hy_w3p, hy_frp, deltas, l,
                               False)
            y_ctx = long_conv_ctx(vx_ctx, k2c, tb_ctx)
        oc = hyena_long_conv(vxt, h_spec, x0, vx_ctx, y_ctx, hy_skip_col, hy_skip3, dft_lat, l, n_lat)
        xbc = mamba_conv(p, mb_conv_w, mb_conv_b3, l, n_lat)
        dt2 = mamba_dt(p_dt, dt_bias_row, l)
        y_f, y_b = ssd_scan(xbc, dt2, a_log_rows, expand, l, n_lat)
        od = mamba_finish(y_f, y_b, xbc, p, d_skip_e, mb_norm3, l)

        acc = merge_branches(h, (oa, ob, oc, od), w_gate_b, b_gate4, w_branch_b, l)
        x1 = matmul_res_ln(acc, w_out_b, xz, mod, 2, ln1_g3, ln1_b3, l, n_lat, n_lat + n_ctx,
                           (544, 384, 128), d)
        act = ffn_in(x1, mod, w_ffn_in_b, l, n_lat)
        out_rows = n_lat if last else n_lat + n_ctx
        xz = matmul_res_ln(act, w_ffn_out_b, x1, mod, 5, ln2_g3, ln2_b3, l, n_lat, out_rows,
                           (256, 128), act.shape[2])
    return xz
```

```python
import functools
import math

import jax
import jax.numpy as jnp
import numpy as np
from jax import lax
from jax.experimental import pallas as pl
from jax.experimental.pallas import tpu as pltpu

F32 = jnp.float32
MXU_DT = jnp.bfloat16

D_MODEL = 2048
DEPTH = 2
GRID_W = 64
N_BRANCH = 4
BRANCH_W = D_MODEL // N_BRANCH
ROPE_THETA = 10000.0
LN_EPS = 1e-6
RMS_EPS = 1e-6
DEEPNORM_ALPHA = (2 * DEPTH) ** 0.25

MLA_HEADS = 4
MLA_Q_RANK = 448
MLA_KV_RANK = 128
MLA_NOPE = 128
MLA_ROPE = 64
MLA_V = 128
MLA_COLS = MLA_Q_RANK + MLA_KV_RANK + MLA_ROPE

GQA_HEADS = 4
GQA_KV_HEADS = 2
GQA_HD = 128
GQA_COLS = (GQA_HEADS + 2 * GQA_KV_HEADS) * GQA_HD

HY_W = BRANCH_W
HY_EMB = 33
HY_BANDS = (HY_EMB - 1) // 2
HY_FFN = 64
HY_MIN_DECAY = math.log(1e-2) / 1.5
HY_MAX_DECAY = math.log(1e-2) / 0.3
HY_COLS = 3 * HY_W

MB_INNER = BRANCH_W
MB_HEADDIM = 64
MB_HEADS = 8
MB_GROUPS = 2
MB_STATE = 128
MB_CHUNK = 128
MB_CONV_CH = MB_INNER + 2 * MB_GROUPS * MB_STATE
MB_COLS = MB_INNER + MB_CONV_CH + 2 * MB_HEADS

FFN_HIDDEN = 5632

LANE = 128
VMEM_LIMIT = 56 * 1024 * 1024

P_MLA = 0
P_MLA_W = 768
P_DT = 896
P_GQA = 1024
P_Z = 2048
P_HY = 2560
P_XBC = 4096
P_COLS = 5120

LOG2_E = math.log2(math.e)
_HI = lax.Precision.HIGHEST


def _dot(a, b):
    return jnp.dot(a, b, preferred_element_type=F32)


def _dot_hi(a, b):
    return jnp.dot(a, b, precision=_HI, preferred_element_type=F32)


def _dot_nt(a, b):
    return lax.dot_general(a, b, (((1,), (1,)), ((), ())), preferred_element_type=F32)


def _sigmoid(x):
    return 1.0 / (1.0 + jnp.exp(-x))


def _silu(x):
    return x * _sigmoid(x)


def _params(*sem):
    return pltpu.CompilerParams(dimension_semantics=sem, vmem_limit_bytes=VMEM_LIMIT)


def _pick(n, prefs):
    for p in prefs:
        if n % p == 0:
            return p
    raise ValueError(f"no tile for {n} in {prefs}")


def _standardize(x):
    mu = jnp.mean(x, axis=-1, keepdims=True)
    xc = x - mu
    var = jnp.mean(xc * xc, axis=-1, keepdims=True)
    return xc * lax.rsqrt(var + LN_EPS)


def _is_ctx_rows(tile_idx, tm, n_lat):
    row = tile_idx * tm + lax.broadcasted_iota(jnp.int32, (tm, 1), 0)
    return row >= n_lat


def _ada_kernel(c_ref, w_ref, b_ref, o_ref):
    cs = _silu(c_ref[...])
    o_ref[...] = _dot(cs.astype(MXU_DT), w_ref[...].astype(MXU_DT)) + b_ref[...]


def ada_mod(c8, w_ada, b_ada3, layer):
    d = c8.shape[1]
    n = w_ada.shape[2]
    tn = 1024
    return pl.pallas_call(
        _ada_kernel,
        grid=(n // tn,),
        in_specs=[pl.BlockSpec((8, d), lambda j: (0, 0)),
                  pl.BlockSpec((None, d, tn), lambda j: (layer, 0, j)),
                  pl.BlockSpec((None, 1, tn), lambda j: (layer, 0, j))],
        out_specs=pl.BlockSpec((8, tn), lambda j: (0, j)),
        out_shape=jax.ShapeDtypeStruct((8, n), F32),
        compiler_params=_params("arbitrary"),
        name="ada_mod",
    )(c8, w_ada, b_ada3)


ROW_CHUNKS = 4
ROW_TILES = (1088, 1024, 544, 384, 128)


def _modulated(x, sh_ref, sc_ref, shc_ref, scc_ref, row0, n_lat):
    xn = _standardize(x)
    row = row0 + lax.broadcasted_iota(jnp.int32, (x.shape[0], 1), 0)
    is_ctx = row >= n_lat
    scale = jnp.where(is_ctx, scc_ref[...], sc_ref[...])
    shift = jnp.where(is_ctx, shc_ref[...], sh_ref[...])
    return (xn * (1.0 + scale) + shift).astype(MXU_DT)


def _inproj_kernel(x_ref, sh_ref, sc_ref, shc_ref, scc_ref, w_ref, p_ref, pdt_ref, h_ref, *, tm, n_lat):
    j = pl.program_id(2)
    rc = tm // ROW_CHUNKS

    @pl.when(j == 0)
    def _():
        for r in range(ROW_CHUNKS):
            rows = slice(r * rc, (r + 1) * rc)
            hb = _modulated(x_ref[rows, :], sh_ref, sc_ref, shc_ref, scc_ref,
                            pl.program_id(1) * tm + r * rc, n_lat)
            h_ref[rows, :] = hb
            acc = _dot(hb, w_ref[...])
            p_ref[rows, :] = acc.astype(p_ref.dtype)
            pdt_ref[rows, :] = acc[:, P_DT:P_DT + LANE]

    @pl.when(j > 0)
    def _():
        p_ref[...] = _dot(h_ref[...], w_ref[...]).astype(p_ref.dtype)


def _ffn_in_kernel(x_ref, sh_ref, sc_ref, shc_ref, scc_ref, wu_ref, wg_ref, a_ref, h_ref, *, tm, n_lat):
    j = pl.program_id(2)
    rc = tm // ROW_CHUNKS

    def swiglu(h):
        up = _dot(h, wu_ref[...])
        gate = _dot(h, wg_ref[...])
        return (_silu(gate) * up).astype(a_ref.dtype)

    @pl.when(j == 0)
    def _():
        for r in range(ROW_CHUNKS):
            rows = slice(r * rc, (r + 1) * rc)
            hb = _modulated(x_ref[rows, :], sh_ref, sc_ref, shc_ref, scc_ref,
                            pl.program_id(1) * tm + r * rc, n_lat)
            h_ref[rows, :] = hb
            a_ref[rows, :] = swiglu(hb)

    @pl.when(j > 0)
    def _():
        half = tm // 2
        for r in range(2):
            rows = slice(r * half, (r + 1) * half)
            a_ref[rows, :] = swiglu(h_ref[rows, :])


def _mod_specs(k_shift, k_scale, n_batch):
    d = D_MODEL
    return [pl.BlockSpec((None, 1, d), lambda b, i, j: (b, 0, k_shift)),
            pl.BlockSpec((None, 1, d), lambda b, i, j: (b, 0, k_scale)),
            pl.BlockSpec((None, 1, d), lambda b, i, j: (n_batch, 0, k_shift)),
            pl.BlockSpec((None, 1, d), lambda b, i, j: (n_batch, 0, k_scale))]


def in_projection(xz, mod, w_in_p, layer, n_lat):
    bsz, l_all, d = xz.shape
    n = w_in_p.shape[2]
    tm = _pick(l_all, (1088, 544, 384, 128))
    tn = 1280
    assert n % tn == 0 and P_DT + LANE <= tn
    kern = functools.partial(_inproj_kernel, tm=tm, n_lat=n_lat)
    return pl.pallas_call(
        kern,
        grid=(bsz, l_all // tm, n // tn),
        in_specs=[pl.BlockSpec((None, tm, d), lambda b, i, j: (b, i, 0))]
        + _mod_specs(0, 1, bsz)
        + [pl.BlockSpec((None, d, tn), lambda b, i, j: (layer, 0, j))],
        out_specs=[pl.BlockSpec((None, tm, tn), lambda b, i, j: (b, i, j)),
                   pl.BlockSpec((None, tm, LANE), lambda b, i, j: (b, i, 0)),
                   pl.BlockSpec((None, tm, d), lambda b, i, j: (b, i, 0))],
        out_shape=[jax.ShapeDtypeStruct((bsz, l_all, n), MXU_DT),
                   jax.ShapeDtypeStruct((bsz, l_all, LANE), F32),
                   jax.ShapeDtypeStruct((bsz, l_all, d), MXU_DT)],
        compiler_params=_params("arbitrary", "arbitrary", "arbitrary"),
        name="in_projection",
    )(xz, mod, mod, mod, mod, w_in_p)


def ffn_in(xz, mod, w_ffn_in, layer, n_lat):
    bsz, l_all, d = xz.shape
    hid = w_ffn_in.shape[2] // 2
    tm = _pick(l_all, ROW_TILES)
    tn = 512
    nj = hid // tn
    kern = functools.partial(_ffn_in_kernel, tm=tm, n_lat=n_lat)
    return pl.pallas_call(
        kern,
        grid=(bsz, l_all // tm, nj),
        in_specs=[pl.BlockSpec((None, tm, d), lambda b, i, j: (b, i, 0))]
        + _mod_specs(3, 4, bsz)
        + [pl.BlockSpec((None, d, tn), lambda b, i, j: (layer, 0, j)),
           pl.BlockSpec((None, d, tn), lambda b, i, j: (layer, 0, j + nj))],
        out_specs=pl.BlockSpec((None, tm, tn), lambda b, i, j: (b, i, j)),
        out_shape=jax.ShapeDtypeStruct((bsz, l_all, hid), MXU_DT),
        scratch_shapes=[pltpu.VMEM((tm, d), MXU_DT)],
        compiler_params=_params("arbitrary", "arbitrary", "arbitrary"),
        name="ffn_in",
    )(xz, mod, mod, mod, mod, w_ffn_in, w_ffn_in)


def _resln_kernel(a_ref, w_ref, res_ref, g_ref, gc_ref, lng_ref, lnb_ref, o_ref, *, tm, n_lat, nk, rc):
    k = pl.program_id(2)
    i = pl.program_id(1)

    if nk > 1:
        @pl.when(k == 0)
        def _():
            o_ref[...] = _dot(a_ref[...], w_ref[...])

        @pl.when((k > 0) & (k < nk - 1))
        def _():
            o_ref[...] += _dot(a_ref[...], w_ref[...])

    @pl.when(k == nk - 1)
    def _():
        for r in range(tm // rc):
            rows = slice(r * rc, (r + 1) * rc)
            acc = _dot(a_ref[rows, :], w_ref[...])
            if nk > 1:
                acc = acc + o_ref[rows, :]
            row = i * tm + r * rc + lax.broadcasted_iota(jnp.int32, (rc, 1), 0)
            gate = jnp.where(row >= n_lat, gc_ref[...], g_ref[...])
            y = DEEPNORM_ALPHA * res_ref[rows, :] + gate * acc
            o_ref[rows, :] = _standardize(y) * lng_ref[...] + lnb_ref[...]


def matmul_res_ln(a, w, res, mod, k_gate, ln_g, ln_b, layer, n_lat, out_rows, tm_prefs, tk):
    bsz, l_all, kdim = a.shape
    d = w.shape[2]
    tm = _pick(out_rows, tm_prefs)
    nk = kdim // tk
    rc = _pick(tm, (272, 256, 192, 128))
    kern = functools.partial(_resln_kernel, tm=tm, n_lat=n_lat, nk=nk, rc=rc)
    w_mode = dict(pipeline_mode=pl.Buffered(1)) if nk == 1 else {}
    return pl.pallas_call(
        kern,
        grid=(bsz, pl.cdiv(out_rows, tm), nk),
        in_specs=[pl.BlockSpec((None, tm, tk), lambda b, i, k: (b, i, k)),
                  pl.BlockSpec((None, tk, d), lambda b, i, k: (layer, k, 0), **w_mode),
                  pl.BlockSpec((None, tm, d), lambda b, i, k: (b, i, 0)),
                  pl.BlockSpec((None, 1, d), lambda b, i, k: (b, 0, k_gate)),
                  pl.BlockSpec((None, 1, d), lambda b, i, k: (bsz, 0, k_gate)),
                  pl.BlockSpec((None, 1, d), lambda b, i, k: (layer, 0, 0)),
                  pl.BlockSpec((None, 1, d), lambda b, i, k: (layer, 0, 0))],
        out_specs=pl.BlockSpec((None, tm, d), lambda b, i, k: (b, i, 0)),
        out_shape=jax.ShapeDtypeStruct((bsz, out_rows, d), F32),
        compiler_params=_params("arbitrary", "arbitrary", "arbitrary"),
        name="matmul_res_ln",
    )(a, w, res, mod, mod, ln_g, ln_b)


def _merge_kernel(h_ref, oa_ref, ob_ref, oc_ref, od_ref, wg_ref, bg_ref, wb_ref, o_ref):
    h = h_ref[...]
    acc = None
    pending = None
    for i, o_r in list(enumerate((oa_ref, ob_ref, oc_ref, od_ref))) + [(None, None)]:
        nxt = None
        if o_r is not None:
            nxt = (_dot(h, wg_ref[i]) + bg_ref[i], _dot(o_r[...], wb_ref[i]))
        if pending is not None:
            g, t = pending
            term = _sigmoid(g) * t
            acc = term if acc is None else acc + term
        pending = nxt
    o_ref[...] = acc.astype(o_ref.dtype)


def merge_branches(h, outs, w_gate, b_gate4, w_branch, layer, n_rows):
    bsz, _, d = h.shape
    bw = outs[0].shape[2]
    tm = _pick(n_rows, ROW_TILES)
    tn = 512
    o_spec = pl.BlockSpec((None, tm, bw), lambda b, i, j: (b, i, 0))
    return pl.pallas_call(
        _merge_kernel,
        grid=(bsz, n_rows // tm, d // tn),
        in_specs=[pl.BlockSpec((None, tm, d), lambda b, i, j: (b, i, 0)), o_spec, o_spec, o_spec, o_spec,
                  pl.BlockSpec((None, N_BRANCH, d, tn), lambda b, i, j: (layer, 0, 0, j)),
                  pl.BlockSpec((None, N_BRANCH, 1, tn), lambda b, i, j: (layer, 0, 0, j)),
                  pl.BlockSpec((None, N_BRANCH, bw, tn), lambda b, i, j: (layer, 0, 0, j))],
        out_specs=pl.BlockSpec((None, tm, tn), lambda b, i, j: (b, i, j)),
        out_shape=jax.ShapeDtypeStruct((bsz, n_rows, d), MXU_DT),
        compiler_params=_params("arbitrary", "arbitrary", "arbitrary"),
        name="merge_branches",
    )(h, *outs, w_gate, b_gate4, w_branch)


def _rope(x, cos, sin):
    lane = lax.broadcasted_iota(jnp.int32, x.shape, 1)
    partner = jnp.where(lane % 2 == 0, pltpu.roll(x, LANE - 1, 1), pltpu.roll(x, 1, 1))
    return x * cos + partner * sin


def _mla_prep_kernel(p_ref, gq_ref, gkv_ref, wuq_ref, wukv_ref, cos_ref, sin_ref, q_ref, k_ref, v_ref,
                     vrow_ref, *, rc):
    scale = (MLA_NOPE + MLA_ROPE) ** -0.5 * LOG2_E

    def body(t, carry):
        rows = pl.ds(pl.multiple_of(t * rc, 16), rc)
        p = p_ref[rows, :].astype(F32)
        cq = p[:, 0:512]
        ckv = p[:, 512:640]
        k_rot = p[:, 640:768]
        rq = lax.rsqrt(jnp.sum(cq * cq, axis=-1, keepdims=True) * (1.0 / MLA_Q_RANK) + RMS_EPS)
        cqn = (cq * rq * gq_ref[...]).astype(MXU_DT)
        rkv = lax.rsqrt(jnp.mean(ckv * ckv, axis=-1, keepdims=True) + RMS_EPS)
        ckvn = (ckv * rkv * gkv_ref[...]).astype(MXU_DT)
        qf = _dot(cqn, wuq_ref[...])
        kvf = _dot(ckvn, wukv_ref[...])
        cos = cos_ref[rows, :]
        sin = sin_ref[rows, :]
        k_rope = _rope(k_rot, cos, sin).astype(k_ref.dtype)
        for hd in range(MLA_HEADS):
            qb = hd * 256
            q_ref[hd, rows, 0:128] = (qf[:, qb:qb + 128] * scale).astype(q_ref.dtype)
            q_rope = _rope(qf[:, qb + 128:qb + 256], cos, sin)
            q_ref[hd, rows, 128:256] = (q_rope * scale).astype(q_ref.dtype)
            k_ref[hd, rows, 0:128] = kvf[:, hd * 256:hd * 256 + 128].astype(k_ref.dtype)
            k_ref[hd, rows, 128:256] = k_rope
            vrow_ref[hd, rows, :] = kvf[:, hd * 256 + 128:hd * 256 + 256]
        return carry

    lax.fori_loop(0, p_ref.shape[0] // rc, body, 0)
    for hd in range(MLA_HEADS):
        v_ref[hd] = vrow_ref[hd].T.astype(v_ref.dtype)


def mla_prep(p, gq, gkv, wuq, wukv, cos, sin, layer):
    bsz, l_all, _ = p.shape
    tr = _pick(l_all, (1088 * 2, 256, 128))
    rc = 272 if tr % 272 == 0 else tr
    h = MLA_HEADS
    return pl.pallas_call(
        functools.partial(_mla_prep_kernel, rc=rc),
        grid=(bsz, l_all // tr),
        in_specs=[pl.BlockSpec((None, tr, P_MLA_W), lambda b, i: (b, i, 0)),
                  pl.BlockSpec((None, 1, 512), lambda b, i: (layer, 0, 0)),
                  pl.BlockSpec((None, 1, 128), lambda b, i: (layer, 0, 0)),
                  pl.BlockSpec((None, 512, h * 256), lambda b, i: (layer, 0, 0)),
                  pl.BlockSpec((None, 128, h * 256), lambda b, i: (layer, 0, 0)),
                  pl.BlockSpec((tr, LANE), lambda b, i: (i, 0)),
                  pl.BlockSpec((tr, LANE), lambda b, i: (i, 0))],
        out_specs=[pl.BlockSpec((None, h, tr, 256), lambda b, i: (b, 0, i, 0)),
                   pl.BlockSpec((None, h, tr, 256), lambda b, i: (b, 0, i, 0)),
                   pl.BlockSpec((None, h, 128, tr), lambda b, i: (b, 0, 0, i))],
        out_shape=[jax.ShapeDtypeStruct((bsz, h, l_all, 256), MXU_DT),
                   jax.ShapeDtypeStruct((bsz, h, l_all, 256), MXU_DT),
                   jax.ShapeDtypeStruct((bsz, h, 128, l_all), MXU_DT)],
        scratch_shapes=[pltpu.VMEM((h, tr, 128), F32)],
        compiler_params=_params("arbitrary", "arbitrary"),
        name="mla_prep",
    )(p, gq, gkv, wuq, wukv, cos, sin)


def _gqa_prep_kernel(p_ref, gq_ref, gk_ref, cos_ref, sin_ref, q_ref, k_ref, v_ref):
    scale = GQA_HD ** -0.5 * LOG2_E
    cos = cos_ref[...]
    sin = sin_ref[...]

    def norm_rope(x, g):
        r = lax.rsqrt(jnp.mean(x * x, axis=-1, keepdims=True) + RMS_EPS)
        return _rope(x * r * g, cos, sin)

    for hd in range(GQA_HEADS):
        x = p_ref[:, hd * 128:(hd + 1) * 128].astype(F32)
        q_ref[hd] = (norm_rope(x, gq_ref[...]) * scale).astype(q_ref.dtype)
    for hd in range(GQA_KV_HEADS):
        x = p_ref[:, 512 + hd * 128:512 + (hd + 1) * 128].astype(F32)
        k_ref[hd] = norm_rope(x, gk_ref[...]).astype(k_ref.dtype)
        v = p_ref[:, 768 + hd * 128:768 + (hd + 1) * 128].astype(F32)
        v_ref[hd] = v.T.astype(v_ref.dtype)


def gqa_prep(p, gq, gk, cos, sin, layer):
    bsz, l_all, _ = p.shape
    tr = _pick(l_all, (1088 * 2, 256, 128))
    return pl.pallas_call(
        _gqa_prep_kernel,
        grid=(bsz, l_all // tr),
        in_specs=[pl.BlockSpec((None, tr, GQA_COLS), lambda b, i: (b, i, P_GQA // GQA_COLS)),
                  pl.BlockSpec((None, 1, 128), lambda b, i: (layer, 0, 0)),
                  pl.BlockSpec((None, 1, 128), lambda b, i: (layer, 0, 0)),
                  pl.BlockSpec((tr, LANE), lambda b, i: (i, 0)),
                  pl.BlockSpec((tr, LANE), lambda b, i: (i, 0))],
        out_specs=[pl.BlockSpec((None, GQA_HEADS, tr, 128), lambda b, i: (b, 0, i, 0)),
                   pl.BlockSpec((None, GQA_KV_HEADS, tr, 128), lambda b, i: (b, 0, i, 0)),
                   pl.BlockSpec((None, GQA_KV_HEADS, 128, tr), lambda b, i: (b, 0, 0, i))],
        out_shape=[jax.ShapeDtypeStruct((bsz, GQA_HEADS, l_all, 128), MXU_DT),
                   jax.ShapeDtypeStruct((bsz, GQA_KV_HEADS, l_all, 128), MXU_DT),
                   jax.ShapeDtypeStruct((bsz, GQA_KV_HEADS, 128, l_all), MXU_DT)],
        compiler_params=_params("arbitrary", "arbitrary"),
        name="gqa_prep",
    )(p, gq, gk, cos, sin)


def _attn_kernel(q_ref, k_ref, v_ref, o_ref, *, n_lat, n_lat_tiles, sub):
    i = pl.program_id(2)

    def attend_all(k, vt):
        groups = [slice(r, r + sub) for r in range(0, q_ref.shape[0], sub)]
        sts = [_dot_nt(k, q_ref[g, :]) for g in groups]
        es, ls = [], []
        for st in sts:
            m = jnp.max(st, axis=0, keepdims=True)
            e = jnp.exp2(st - m)
            ls.append(jnp.sum(e, axis=0, keepdims=True))
            es.append(e.astype(vt.dtype))
        for g, e, l in zip(groups, es, ls):
            ot = _dot(vt, e) / l
            o_ref[g, :] = ot.T.astype(o_ref.dtype)

    @pl.when(i < n_lat_tiles)
    def _():
        attend_all(k_ref[...], v_ref[...])

    @pl.when(i >= n_lat_tiles)
    def _():
        attend_all(k_ref[n_lat:, :], v_ref[:, n_lat:])


def attention(q, k, v, n_lat, with_ctx):
    bsz, h, l_all, dk = q.shape
    hkv = k.shape[1]
    grp = h // hkv
    dv = v.shape[2]
    tq = _pick(n_lat, (1024, 512, 256, 128))
    assert l_all - n_lat <= tq
    out_rows = l_all if with_ctx else n_lat
    kern = functools.partial(_attn_kernel, n_lat=n_lat, n_lat_tiles=n_lat // tq, sub=min(256, tq))
    return pl.pallas_call(
        kern,
        grid=(bsz, h, pl.cdiv(out_rows, tq)),
        in_specs=[pl.BlockSpec((None, None, tq, dk), lambda b, hh, i: (b, hh, i, 0)),
                  pl.BlockSpec((None, None, l_all, dk), lambda b, hh, i: (b, hh // grp, 0, 0)),
                  pl.BlockSpec((None, None, dv, l_all), lambda b, hh, i: (b, hh // grp, 0, 0))],
        out_specs=pl.BlockSpec((None, tq, dv), lambda b, hh, i: (b, i, hh)),
        out_shape=jax.ShapeDtypeStruct((bsz, out_rows, h * dv), MXU_DT),
        compiler_params=_params("arbitrary", "arbitrary", "arbitrary"),
        name="attention",
    )(q, k, v)


CONV_TC = 256


def _conv3(u, w, b, n_lat):
    u = u.astype(F32)
    n = u.shape[0]
    row = lax.broadcasted_iota(jnp.int32, (n, 1), 0)
    prev = jnp.where((row == 0) | (row == n_lat), 0.0, pltpu.roll(u, 1, 0))
    nxt = jnp.where((row == n_lat - 1) | (row == n - 1), 0.0, pltpu.roll(u, n - 1, 0))
    return w[0:1, :] * prev + w[1:2, :] * u + w[2:3, :] * nxt + b


def _hy_conv_kernel(p0_ref, p1_ref, pv_ref, w0_ref, w1_ref, wv_ref, b0_ref, b1_ref, bv_ref,
                    x0_ref, vxt_ref, vxc_ref, *, n_lat):
    x0_ref[...] = _conv3(p0_ref[...], w0_ref[...], b0_ref[...], n_lat)
    x1 = _conv3(p1_ref[...], w1_ref[...], b1_ref[...], n_lat)
    v = _conv3(pv_ref[...], wv_ref[...], bv_ref[...], n_lat)
    vx = v * x1
    vxt_ref[...] = vx[:n_lat, :].T
    vxc_ref[...] = vx[n_lat:, :]


def hyena_conv(p, conv_w, conv_b3, layer, n_lat):
    bsz, l_all, _ = p.shape
    tc = CONV_TC
    nb = HY_W // tc
    base = P_HY // tc

    def pspec(off):
        return pl.BlockSpec((None, l_all, tc), lambda b, c: (b, 0, base + off + c))

    def wspec(off):
        return pl.BlockSpec((None, 3, tc), lambda b, c: (layer, 0, off + c))

    def bspec(off):
        return pl.BlockSpec((None, 1, tc), lambda b, c: (layer, 0, off + c))

    o_spec = pl.BlockSpec((None, l_all, tc), lambda b, c: (b, 0, c))
    return pl.pallas_call(
        functools.partial(_hy_conv_kernel, n_lat=n_lat),
        grid=(bsz, nb),
        in_specs=[pspec(0), pspec(nb), pspec(2 * nb), wspec(0), wspec(nb), wspec(2 * nb),
                  bspec(0), bspec(nb), bspec(2 * nb)],
        out_specs=[o_spec,
                   pl.BlockSpec((None, tc, n_lat), lambda b, c: (b, c, 0)),
                   pl.BlockSpec((None, l_all - n_lat, tc), lambda b, c: (b, 0, c))],
        out_shape=[jax.ShapeDtypeStruct((bsz, l_all, HY_W), F32),
                   jax.ShapeDtypeStruct((bsz, HY_W, n_lat), F32),
                   jax.ShapeDtypeStruct((bsz, l_all - n_lat, HY_W), F32)],
        compiler_params=_params("arbitrary", "arbitrary"),
        name="hyena_conv",
    )(p, p, p, conv_w, conv_w, conv_w, conv_b3, conv_b3, conv_b3)


def _mb_conv_kernel(p_ref, w_ref, b_ref, o_ref, *, n_lat):
    o_ref[...] = _silu(_conv3(p_ref[...], w_ref[...], b_ref[...], n_lat))


def mamba_conv(p, conv_w, conv_b3, layer, n_lat):
    bsz, l_all, _ = p.shape
    tc = CONV_TC
    nb = MB_CONV_CH // tc
    base = P_XBC // tc
    return pl.pallas_call(
        functools.partial(_mb_conv_kernel, n_lat=n_lat),
        grid=(bsz, nb),
        in_specs=[pl.BlockSpec((None, l_all, tc), lambda b, c: (b, 0, base + c)),
                  pl.BlockSpec((None, 3, tc), lambda b, c: (layer, 0, c)),
                  pl.BlockSpec((None, 1, tc), lambda b, c: (layer, 0, c))],
        out_specs=pl.BlockSpec((None, l_all, tc), lambda b, c: (b, 0, c)),
        out_shape=jax.ShapeDtypeStruct((bsz, l_all, MB_CONV_CH), F32),
        compiler_params=_params("arbitrary", "arbitrary"),
        name="mamba_conv",
    )(p, conv_w, conv_b3)


def _softplus(x):
    return jnp.maximum(x, 0.0) + jnp.log(1.0 + jnp.exp(-jnp.abs(x)))


def _mb_dt_kernel(p_ref, bias_ref, o_ref):
    dt = _softplus(p_ref[...] + bias_ref[...])
    o_ref[0] = dt
    o_ref[1] = pltpu.roll(dt, LANE - MB_HEADS, 1)


def mamba_dt(p, dt_bias_row, layer):
    bsz, l_all, _ = p.shape
    return pl.pallas_call(
        _mb_dt_kernel,
        grid=(bsz,),
        in_specs=[pl.BlockSpec((None, l_all, LANE), lambda b: (b, 0, 0)),
                  pl.BlockSpec((None, 1, LANE), lambda b: (layer, 0, 0))],
        out_specs=pl.BlockSpec((2, None, l_all, LANE), lambda b: (0, b, 0, 0)),
        out_shape=jax.ShapeDtypeStruct((2, bsz, l_all, LANE), F32),
        compiler_params=_params("arbitrary"),
        name="mamba_dt",
    )(p, dt_bias_row)


def _hy_filter_kernel(f_ref, aux_ref, w1_ref, b1_ref, w2_ref, b2_ref, w3_ref, fr_ref, dl_ref, o_ref, *,
                      channel_major):
    fr = fr_ref[...]
    tr = f_ref.shape[0]
    n_ch = 4 if tr % 512 == 0 else 1
    rc = tr // n_ch
    chunks = [slice(r * rc, (r + 1) * rc) for r in range(n_ch)]
    hdn = [_dot_hi(f_ref[rows, :], w1_ref[...]) for rows in chunks]
    hdn = [jnp.sin(fr * (x + b1_ref[...])) for x in hdn]
    hdn = [_dot_hi(x, w2_ref[...]) for x in hdn]
    hdn = [jnp.sin(fr * (x + b2_ref[...])) for x in hdn]
    filts = [_dot_hi(x, w3_ref[...]) for x in hdn]
    for rows, filt in zip(chunks, filts):
        aux = aux_ref[rows, :]
        t = aux[:, 0:1]
        is_fwd = aux[:, 1:2] > 0.5
        valid = aux[:, 2:3]
        window = jnp.exp(-t * dl_ref[...]) * valid
        k2 = jnp.where(is_fwd, filt[:, :HY_W], filt[:, HY_W:]) * window
        if channel_major:
            o_ref[:, rows] = k2.T
        else:
            o_ref[rows, :] = k2


def hyena_filter(feats2, aux, w1p, b1p, w2p, b2p, w3p, frp, deltas, layer, channel_major):
    rows = feats2.shape[0]
    tr = _pick(rows, (512, 256))

    def lspec(shape):
        return pl.BlockSpec((None,) + shape, lambda i: (layer, 0, 0))

    if channel_major:
        out_spec = pl.BlockSpec((HY_W, tr), lambda i: (0, i))
        out_shape = jax.ShapeDtypeStruct((HY_W, rows), F32)
    else:
        out_spec = pl.BlockSpec((tr, HY_W), lambda i: (i, 0))
        out_shape = jax.ShapeDtypeStruct((rows, HY_W), F32)
    return pl.pallas_call(
        functools.partial(_hy_filter_kernel, channel_major=channel_major),
        grid=(rows // tr,),
        in_specs=[pl.BlockSpec((tr, LANE), lambda i: (i, 0)),
                  pl.BlockSpec((tr, LANE), lambda i: (i, 0)),
                  lspec((LANE, LANE)), lspec((1, LANE)), lspec((LANE, LANE)), lspec((1, LANE)),
                  lspec((LANE, 2 * HY_W)), lspec((1, LANE)),
                  pl.BlockSpec((1, HY_W), lambda i: (0, 0))],
        out_specs=out_spec,
        out_shape=out_shape,
        compiler_params=_params("arbitrary"),
        name="hyena_filter",
    )(feats2, aux, w1p, b1p, w2p, b2p, w3p, frp, deltas)


DFT_MINOR = 256


def _snap(c):
    for v in (0.0, 1.0, -1.0):
        if abs(c - v) < 1e-12:
            return v
    return float(c)


def _lincomb(terms):
    acc = None
    for cf, tile in terms:
        if cf == 0.0:
            continue
        v = tile()
        if acc is None:
            acc = v if cf == 1.0 else (-v if cf == -1.0 else cf * v)
        elif cf == 1.0:
            acc = acc + v
        elif cf == -1.0:
            acc = acc - v
        else:
            acc = acc + cf * v
    return acc


def _dft_consts(n_seq):
    n = 2 * n_seq
    n2 = DFT_MINOR
    n1 = n // n2
    nkj = n1 // 2 + 1
    ang1 = 2.0 * np.pi * np.outer(np.arange(nkj), np.arange(n1)) / n1
    cos1 = [[_snap(v) for v in r] for r in np.cos(ang1)]
    sin1 = [[_snap(v) for v in r] for r in np.sin(ang1)]
    ang_t = 2.0 * np.pi * np.outer(np.arange(nkj), np.arange(n2)) / n
    rows = ((nkj + 7) // 8) * 8
    tw_c = np.zeros((rows, n2))
    tw_s = np.zeros((rows, n2))
    tw_c[:nkj] = np.cos(ang_t)
    tw_s[:nkj] = np.sin(ang_t)
    ang2 = 2.0 * np.pi * np.outer(np.arange(n2), np.arange(n2)) / n2
    c2, s2 = np.cos(ang2), np.sin(ang2)
    m_fwd = np.block([[c2, -s2], [s2, c2]])
    m_inv = np.block([[c2, s2], [-s2, c2]])
    as32 = lambda a: jnp.asarray(a, dtype=F32)

    def split(m):
        m32 = as32(m)
        hi = m32.astype(MXU_DT)
        lo = (m32 - hi.astype(F32)).astype(MXU_DT)
        return jnp.stack([hi, lo])

    return dict(n=n, n1=n1, nkj=nkj, cos1=cos1, sin1=sin1, tw_c=as32(tw_c), tw_s=as32(tw_s),
                m_fwd=split(m_fwd), m_inv=split(m_inv))


def _dot_split(a, m_ref):
    a_hi = a.astype(MXU_DT)
    a_lo = (a - a_hi.astype(F32)).astype(MXU_DT)
    return _dot(a_hi, m_ref[0]) + _dot(a_lo, m_ref[0]) + _dot(a_hi, m_ref[1])


def _outer_fwd_block(src_ref, z_ref, twc_ref, tws_ref, cb, n_in, cst):
    n2 = DFT_MINOR
    nkj, cos1, sin1 = cst["nkj"], cst["cos1"], cst["sin1"]
    tile = lambda j: (lambda: src_ref[cb * 8:cb * 8 + 8, j * n2:(j + 1) * n2])
    for kj in range(nkj):
        re = _lincomb([(cos1[kj][j], tile(j)) for j in range(n_in)])
        im = _lincomb([(-sin1[kj][j], tile(j)) for j in range(n_in)])
        if kj > 0 and im is not None:
            c = twc_ref[kj:kj + 1, :]
            s = tws_ref[kj:kj + 1, :]
            re, im = re * c + im * s, im * c - re * s
        elif kj > 0:
            re, im = re * twc_ref[kj:kj + 1, :], -re * tws_ref[kj:kj + 1, :]
        r = (cb * nkj + kj) * 8
        z_ref[r:r + 8, 0:n2] = re
        z_ref[r:r + 8, n2:2 * n2] = jnp.zeros_like(re) if im is None else im


def _outer_inv_block(z_ref, yt_ref, twc_ref, tws_ref, cb, cst):
    n2 = DFT_MINOR
    nkj, n1, cos1, sin1 = cst["nkj"], cst["n1"], cst["cos1"], cst["sin1"]
    inv_n = 1.0 / cst["n"]
    for kj in range(1, nkj):
        r = (cb * nkj + kj) * 8
        re = z_ref[r:r + 8, 0:n2]
        im = z_ref[r:r + 8, n2:2 * n2]
        c = twc_ref[kj:kj + 1, :]
        s = tws_ref[kj:kj + 1, :]
        z_ref[r:r + 8, 0:n2] = re * c - im * s
        z_ref[r:r + 8, n2:2 * n2] = im * c + re * s
    row = lambda kj: (cb * nkj + kj) * 8
    g_re = lambda kj: (lambda: z_ref[row(kj):row(kj) + 8, 0:n2])
    g_im = lambda kj: (lambda: z_ref[row(kj):row(kj) + 8, n2:2 * n2])
    for j in range(n1 // 2):
        terms = []
        for kj in range(nkj):
            wgt = inv_n if kj in (0, nkj - 1) else 2.0 * inv_n
            terms.append((wgt * cos1[kj][j], g_re(kj)))
            terms.append((-wgt * sin1[kj][j], g_im(kj)))
        yt_ref[cb * 8:cb * 8 + 8, j * n2:(j + 1) * n2] = _lincomb(terms)


def _spectrum_kernel(k_ref, twc_ref, tws_ref, mf_ref, h_ref, z_ref, *, ct, cst, n_chunks):
    nkj = cst["nkj"]
    per = ct // 8 // n_chunks
    rc = per * nkj * 8
    for c in range(n_chunks + 1):
        if c < n_chunks:
            for cb in range(c * per, (c + 1) * per):
                _outer_fwd_block(k_ref, z_ref, twc_ref, tws_ref, cb, cst["n1"], cst)
        if c >= 1:
            rows = slice((c - 1) * rc, c * rc)
            h_ref[rows, :] = _dot_split(z_ref[rows, :], mf_ref)


def _long_conv_kernel(x_ref, h_ref, twc_ref, tws_ref, mf_ref, mi_ref, x0_ref, vxc_ref, yc_ref, skip_ref,
                      skip_row_ref, o_ref, z_ref, yt_ref, *, ct, cst, n_chunks, n_lat):
    n2 = DFT_MINOR
    nkj, n1 = cst["nkj"], cst["n1"]
    per = ct // 8 // n_chunks
    rc = per * nkj * 8
    for c in range(n_chunks + 2):
        if c < n_chunks:
            for cb in range(c * per, (c + 1) * per):
                _outer_fwd_block(x_ref, z_ref, twc_ref, tws_ref, cb, n1 // 2, cst)
        if 1 <= c <= n_chunks:
            rows = slice((c - 1) * rc, c * rc)
            x = _dot_split(z_ref[rows, :], mf_ref)
            xre, xim = x[:, :n2], x[:, n2:]
            hre, him = h_ref[rows, 0:n2], h_ref[rows, n2:2 * n2]
            y = jnp.concatenate([xre * hre - xim * him, xre * him + xim * hre], axis=1)
            z_ref[rows, :] = _dot_split(y, mi_ref)
        if c >= 2:
            for cb in range((c - 2) * per, (c - 1) * per):
                _outer_inv_block(z_ref, yt_ref, twc_ref, tws_ref, cb, cst)

    y = (yt_ref[...] + x_ref[...] * skip_ref[...]).T
    o_ref[0:n_lat, :] = (x0_ref[0:n_lat, :] * y).astype(o_ref.dtype)
    yc = yc_ref[...] + vxc_ref[...] * skip_row_ref[...]
    o_ref[n_lat:, :] = (x0_ref[n_lat:, :] * yc).astype(o_ref.dtype)


def filter_spectrum(k2t, cst):
    ch, n = k2t.shape
    ct = LANE
    nkj = cst["nkj"]
    n_chunks = _pick(nkj * ct // 8, (4, 5, 1))
    full = lambda a: pl.BlockSpec(a.shape, lambda c: (0,) * a.ndim)
    return pl.pallas_call(
        functools.partial(_spectrum_kernel, ct=ct, cst=cst, n_chunks=n_chunks),
        grid=(ch // ct,),
        in_specs=[pl.BlockSpec((ct, n), lambda c: (c, 0)),
                  full(cst["tw_c"]), full(cst["tw_s"]), full(cst["m_fwd"])],
        out_specs=pl.BlockSpec((None, nkj * ct, 2 * DFT_MINOR), lambda c: (c, 0, 0)),
        out_shape=jax.ShapeDtypeStruct((ch // ct, nkj * ct, 2 * DFT_MINOR), F32),
        scratch_shapes=[pltpu.VMEM((nkj * ct, 2 * DFT_MINOR), F32)],
        compiler_params=_params("arbitrary"),
        name="filter_spectrum",
    )(k2t, cst["tw_c"], cst["tw_s"], cst["m_fwd"])


def hyena_long_conv(vxt, h_spec, x0, vx_ctx, y_ctx, skip_col, skip_row, cst, layer, n_lat):
    bsz, ch, _ = vxt.shape
    l_all = x0.shape[1]
    n_ctx = l_all - n_lat
    ct = LANE
    nkj = cst["nkj"]
    n_chunks = _pick(nkj * ct // 8, (4, 5, 1))
    full = lambda a: pl.BlockSpec(a.shape, lambda b, c: (0,) * a.ndim)
    return pl.pallas_call(
        functools.partial(_long_conv_kernel, ct=ct, cst=cst, n_chunks=n_chunks, n_lat=n_lat),
        grid=(bsz, ch // ct),
        in_specs=[pl.BlockSpec((None, ct, n_lat), lambda b, c: (b, c, 0)),
                  pl.BlockSpec((None, nkj * ct, 2 * DFT_MINOR), lambda b, c: (c, 0, 0)),
                  full(cst["tw_c"]), full(cst["tw_s"]), full(cst["m_fwd"]), full(cst["m_inv"]),
                  pl.BlockSpec((None, l_all, ct), lambda b, c: (b, 0, c)),
                  pl.BlockSpec((None, n_ctx, ct), lambda b, c: (b, 0, c)),
                  pl.BlockSpec((None, n_ctx, ct), lambda b, c: (b, 0, c)),
                  pl.BlockSpec((None, ct, 1), lambda b, c: (layer, c, 0)),
                  pl.BlockSpec((None, 1, ct), lambda b, c: (layer, 0, c))],
        out_specs=pl.BlockSpec((None, l_all, ct), lambda b, c: (b, 0, c)),
        out_shape=jax.ShapeDtypeStruct((bsz, l_all, ch), MXU_DT),
        scratch_shapes=[pltpu.VMEM((nkj * ct, 2 * DFT_MINOR), F32), pltpu.VMEM((ct, n_lat), F32)],
        compiler_params=_params("arbitrary", "arbitrary"),
        name="hyena_long_conv",
    )(vxt, h_spec, cst["tw_c"], cst["tw_s"], cst["m_fwd"], cst["m_inv"], x0, vx_ctx, y_ctx, skip_col, skip_row)


def _ctx_conv_kernel(v_ref, k_ref, fd_ref, fk_ref, gi_ref, o_ref, *, kp):
    x = _dot_hi(fd_ref[...], v_ref[...])
    h = _dot_hi(fk_ref[...], k_ref[...])
    xre, xim = x[:kp], x[kp:]
    hre, him = h[:kp], h[kp:]
    y = jnp.concatenate([xre * hre - xim * him, xre * him + xim * hre], axis=0)
    o_ref[...] = _dot_hi(gi_ref[...], y)


def _ctx_tables(n_ctx):
    n = 2 * n_ctx
    kh = n_ctx + 1
    kp = ((kh + 7) // 8) * 8
    k = np.arange(kp)[:, None]
    live = (k < kh).astype(np.float64)
    ang = 2.0 * np.pi * k * np.arange(n)[None, :] / n
    ck, sk = np.cos(ang) * live, np.sin(ang) * live
    fk = np.concatenate([ck, -sk], axis=0)
    fd = fk[:, :n_ctx]
    wk = np.where((k == 0) | (k == n_ctx), 1.0, 2.0) * live / n
    gi = np.concatenate([(ck * wk).T, (-sk * wk).T], axis=1)[:n_ctx]
    as32 = lambda a: jnp.asarray(a, dtype=F32)
    return dict(kp=kp, fd=as32(fd), fk=as32(fk), gi=as32(gi))


def long_conv_ctx(vx_ctx, k2c, tb):
    bsz, n_ctx, ch = vx_ctx.shape
    kp = tb["kp"]
    full = lambda a: pl.BlockSpec(a.shape, lambda b, c: (0,) * a.ndim)
    return pl.pallas_call(
        functools.partial(_ctx_conv_kernel, kp=kp),
        grid=(bsz, ch // LANE),
        in_specs=[pl.BlockSpec((None, n_ctx, LANE), lambda b, c: (b, 0, c)),
                  pl.BlockSpec((2 * n_ctx, LANE), lambda b, c: (0, c)),
                  full(tb["fd"]), full(tb["fk"]), full(tb["gi"])],
        out_specs=pl.BlockSpec((None, n_ctx, LANE), lambda b, c: (b, 0, c)),
        out_shape=jax.ShapeDtypeStruct((bsz, n_ctx, ch), F32),
        compiler_params=_params("arbitrary", "arbitrary"),
        name="ctx_long_conv",
    )(vx_ctx, k2c, tb["fd"], tb["fk"], tb["gi"])


def _split3(a):
    a1 = a.astype(MXU_DT)
    r = a - a1.astype(F32)
    a2 = r.astype(MXU_DT)
    a3 = (r - a2.astype(F32)).astype(MXU_DT)
    return a1, a2, a3


def _ssd_chunk(backward, xs_ref, b_ref, c_ref, dt_ref, alog, ex, y_ref, state_ref):
    n = MB_CHUNK
    li = lax.broadcasted_iota(jnp.int32, (n, n), 0)
    si = lax.broadcasted_iota(jnp.int32, (n, n), 1)
    mask = (li <= si) if backward else (li >= si)
    tri = mask.astype(MXU_DT)
    lane = lax.broadcasted_iota(jnp.int32, (n, LANE), 1)

    dt = dt_ref[...]
    da = dt * (-jnp.exp(alog))
    cum = sum(_dot(tri, t) for t in _split3(da))
    cum_t = cum.T
    both = jnp.concatenate([cum, dt], axis=0)
    both_e = sum(_dot(t, ex) for t in _split3(both))
    cum_e = both_e[:n]
    dt_e = both_e[n:]
    total_e = cum_e[0:1] if backward else cum_e[n - 1:n]

    xdt = xs_ref[...] * dt_e
    xdt_b = xdt.astype(MXU_DT)
    w_b = (jnp.exp(total_e - cum_e) * xdt).astype(MXU_DT)
    st = state_ref[...]
    y_parts = []
    s_parts = []
    gw = MB_STATE
    hw = (MB_HEADS // MB_GROUPS) * MB_HEADDIM
    for g in range(MB_GROUPS):
        cg = c_ref[:, g * gw:(g + 1) * gw].astype(MXU_DT)
        bg = b_ref[:, g * gw:(g + 1) * gw]
        cb = _dot_nt(cg, bg.astype(MXU_DT))
        y_off = _dot(cg, st[:, g * hw:(g + 1) * hw].astype(MXU_DT))
        s_parts.append(_dot(bg.T.astype(MXU_DT), w_b[:, g * hw:(g + 1) * hw]))
        diag = []
        for j in range(hw // LANE):
            lo = g * hw + j * LANE
            xp = xdt_b[:, lo:lo + LANE]
            pair = []
            for e in range(2):
                hd = (lo // MB_HEADDIM) + e
                seg = cum[:, hd:hd + 1] - cum_t[hd:hd + 1, :]
                decay = jnp.exp(jnp.where(mask, seg, -jnp.inf))
                pair.append(_dot((cb * decay).astype(MXU_DT), xp))
            diag.append(jnp.where(lane < MB_HEADDIM, pair[0], pair[1]))
        y_parts.append(jnp.concatenate(diag, axis=1) + y_off * jnp.exp(cum_e[:, g * hw:(g + 1) * hw]))
    y_ref[...] = jnp.concatenate(y_parts, axis=1)
    state_ref[...] = st * jnp.exp(total_e) + jnp.concatenate(s_parts, axis=1)


def _ssd_kernel(xf_ref, bf_ref, cf_ref, dtf_ref, xb_ref, bb_ref, cb_ref, dtb_ref, alog_ref, ex_ref,
                yf_ref, yb_ref, state_ref):
    @pl.when(pl.program_id(1) == 0)
    def _():
        state_ref[...] = jnp.zeros_like(state_ref)

    ex = ex_ref[...]
    for e in range(xf_ref.shape[0]):
        _ssd_chunk(False, xf_ref.at[e], bf_ref.at[e], cf_ref.at[e], dtf_ref.at[e], alog_ref[0], ex,
                   yf_ref.at[e], state_ref.at[2 * e])
        _ssd_chunk(True, xb_ref.at[e], bb_ref.at[e], cb_ref.at[e], dtb_ref.at[e], alog_ref[1], ex,
                   yb_ref.at[e], state_ref.at[2 * e + 1])


def ssd_scan(xbc, dt2, a_log_rows, expand, layer, n_lat):
    bsz, l_all, _ = xbc.shape
    nc = l_all // MB_CHUNK
    ncl = n_lat // MB_CHUNK
    fwd = lambda c: (c + ncl) % nc
    bwd = lambda c: nc - 1 - c

    nb = 2 if bsz % 2 == 0 else 1

    def specs(blk, direction):
        return [pl.BlockSpec((nb, MB_CHUNK, MB_INNER), lambda b, c: (b, blk(c), 0)),
                pl.BlockSpec((nb, MB_CHUNK, 256), lambda b, c: (b, blk(c), 2)),
                pl.BlockSpec((nb, MB_CHUNK, 256), lambda b, c: (b, blk(c), 3)),
                pl.BlockSpec((None, nb, MB_CHUNK, LANE), lambda b, c: (direction, b, blk(c), 0))]

    y_shape = jax.ShapeDtypeStruct((bsz, l_all, MB_INNER), F32)
    return pl.pallas_call(
        _ssd_kernel,
        grid=(bsz // nb, nc),
        in_specs=specs(fwd, 0) + specs(bwd, 1)
        + [pl.BlockSpec((None, 2, 1, LANE), lambda b, c: (layer, 0, 0, 0)),
           pl.BlockSpec((LANE, MB_INNER), lambda b, c: (0, 0))],
        out_specs=[pl.BlockSpec((nb, MB_CHUNK, MB_INNER), lambda b, c: (b, fwd(c), 0)),
                   pl.BlockSpec((nb, MB_CHUNK, MB_INNER), lambda b, c: (b, bwd(c), 0))],
        out_shape=[y_shape, y_shape],
        scratch_shapes=[pltpu.VMEM((2 * nb, MB_STATE, MB_INNER), F32)],
        compiler_params=_params("arbitrary", "arbitrary"),
        name="ssd_scan",
    )(xbc, xbc, xbc, dt2, xbc, xbc, xbc, dt2, a_log_rows, expand)


def _mb_finish_kernel(yf_ref, yb_ref, xs_ref, z_ref, dsk_ref, g_ref, o_ref):
    y = yf_ref[...] + yb_ref[...] + xs_ref[...] * dsk_ref[...]
    y = y * _silu(z_ref[...].astype(F32))
    r = lax.rsqrt(jnp.mean(y * y, axis=-1, keepdims=True) + RMS_EPS)
    o_ref[...] = (y * r * g_ref[...]).astype(o_ref.dtype)


def mamba_finish(y_f, y_b, xbc, p, d_skip_e, norm_g3, layer):
    bsz, l_all, _ = xbc.shape
    tr = _pick(l_all, (544, 384, 128))
    w = MB_INNER
    return pl.pallas_call(
        _mb_finish_kernel,
        grid=(bsz, l_all // tr),
        in_specs=[pl.BlockSpec((None, tr, w), lambda b, i: (b, i, 0)),
                  pl.BlockSpec((None, tr, w), lambda b, i: (b, i, 0)),
                  pl.BlockSpec((None, tr, w), lambda b, i: (b, i, 0)),
                  pl.BlockSpec((None, tr, w), lambda b, i: (b, i, P_Z // w)),
                  pl.BlockSpec((None, 1, w), lambda b, i: (layer, 0, 0)),
                  pl.BlockSpec((None, 1, w), lambda b, i: (layer, 0, 0))],
        out_specs=pl.BlockSpec((None, tr, w), lambda b, i: (b, i, 0)),
        out_shape=jax.ShapeDtypeStruct((bsz, l_all, w), MXU_DT),
        compiler_params=_params("arbitrary", "arbitrary"),
        name="mamba_finish",
    )(y_f, y_b, xbc, p, d_skip_e, norm_g3)


_W_IN_MOVES = (
    (0, 0, MLA_Q_RANK),
    (512, MLA_Q_RANK, MLA_KV_RANK),
    (640, MLA_Q_RANK + MLA_KV_RANK, MLA_ROPE),
    (P_DT, MLA_COLS + GQA_COLS + HY_COLS + MB_INNER + MB_CONV_CH, 2 * MB_HEADS),
    (P_GQA, MLA_COLS, GQA_COLS),
    (P_Z, MLA_COLS + GQA_COLS + HY_COLS, MB_INNER),
    (P_HY, MLA_COLS + GQA_COLS, HY_COLS),
    (P_XBC, MLA_COLS + GQA_COLS + HY_COLS + MB_INNER, MB_CONV_CH),
)


def _w_in_prep_kernel(w_ref, o_ref):
    o_ref[...] = jnp.zeros_like(o_ref)
    for dst, src, n in _W_IN_MOVES:
        o_ref[:, dst:dst + n] = w_ref[:, src:src + n].astype(o_ref.dtype)


def _prep_w_in(w_in):
    nl, d, cols = w_in.shape
    tr = 256
    return pl.pallas_call(
        _w_in_prep_kernel,
        grid=(nl, d // tr),
        in_specs=[pl.BlockSpec((None, tr, cols), lambda l, i: (l, i, 0))],
        out_specs=pl.BlockSpec((None, tr, P_COLS), lambda l, i: (l, i, 0)),
        out_shape=jax.ShapeDtypeStruct((nl, d, P_COLS), MXU_DT),
        compiler_params=_params("arbitrary", "arbitrary"),
        name="w_in_layout",
    )(w_in)


def _prep_w_uq(w_uq):
    nl = w_uq.shape[0]
    w = w_uq.reshape(nl, MLA_Q_RANK, MLA_HEADS, MLA_NOPE + MLA_ROPE)
    w = _pad_to(w, (nl, 512, MLA_HEADS, 256))
    return w.reshape(nl, 512, MLA_HEADS * 256).astype(MXU_DT)


def _rope_tables(n_lat, n_ctx, rot_dim):
    rows = n_lat // GRID_W
    row = jnp.repeat(jnp.arange(rows, dtype=F32), GRID_W)
    col = jnp.tile(jnp.arange(GRID_W, dtype=F32), rows)
    n_freq = rot_dim // 4
    inv_freq = ROPE_THETA ** (-jnp.arange(n_freq, dtype=F32) / n_freq)
    ang = jnp.concatenate([row[:, None] * inv_freq, col[:, None] * inv_freq], axis=-1)
    half = rot_dim // 2
    cos = jnp.concatenate([jnp.cos(ang), jnp.ones((n_ctx, half), F32)], axis=0)
    sin = jnp.concatenate([jnp.sin(ang), jnp.zeros((n_ctx, half), F32)], axis=0)
    cos_t = jnp.repeat(cos, 2, axis=-1)
    sin_t = jnp.stack([-sin, sin], axis=-1).reshape(n_lat + n_ctx, rot_dim)
    shape = (n_lat + n_ctx, LANE)
    return _pad_to(cos_t, shape), _pad_to(sin_t, shape)


def _filter_inputs(n):
    t = jnp.linspace(0.0, 1.0, n, dtype=F32)[:, None]
    omega = 2.0 * math.pi * jnp.arange(n, dtype=F32) / n
    bands = jnp.linspace(1e-4, HY_BANDS - 1, HY_BANDS, dtype=F32)
    ang = omega[:, None] * bands[None, :]
    feats = jnp.concatenate([t, jnp.cos(ang), -jnp.sin(ang)], axis=-1)
    zero = jnp.zeros((1, HY_EMB), F32)
    feats2 = jnp.concatenate([feats, zero, feats[1:][::-1]], axis=0)
    feats2 = jnp.concatenate([feats2, jnp.zeros((2 * n, LANE - HY_EMB), F32)], axis=-1)
    t2 = jnp.concatenate([t, jnp.zeros((1, 1), F32), t[1:][::-1]], axis=0)
    idx = jnp.arange(2 * n)[:, None]
    aux = jnp.concatenate([t2, (idx < n).astype(F32), (idx != n).astype(F32),
                           jnp.zeros((2 * n, LANE - 3), F32)], axis=-1)
    return feats2, aux


def _pad_to(a, shape):
    pads = [(0, s - d) for d, s in zip(a.shape, shape)]
    return jnp.pad(a, pads)


def kernel(x, c, ctx, c_ctx, w_ada, b_ada, w_in, mla_q_norm, mla_kv_norm, mla_w_uq, mla_w_ukv, gqa_q_norm, gqa_k_norm, hy_conv_w, hy_conv_b, hy_w1, hy_b1, hy_w2, hy_b2, hy_w3, hy_freq, hy_skip, mb_conv_w, mb_conv_b, mb_a_log, mb_dt_bias, mb_d, mb_norm, w_mgate, b_mgate, w_branch, w_out, ln1_g, ln1_b, w_ffn_in, w_ffn_out, ln2_g, ln2_b):
    bsz, n_lat, d = x.shape
    n_ctx = ctx.shape[1]
    nl = w_in.shape[0]
    assert d == D_MODEL and bsz < 8 and n_lat % n_ctx == 0 and n_ctx % MB_CHUNK == 0

    w_in_p = _prep_w_in(w_in)
    w_uq_p = _prep_w_uq(mla_w_uq)
    w_ukv_p = mla_w_ukv.astype(MXU_DT)
    gq_mla = _pad_to(mla_q_norm, (nl, 512))[:, None, :]
    gkv_mla = mla_kv_norm[:, None, :]
    gq_gqa = gqa_q_norm[:, None, :]
    gk_gqa = gqa_k_norm[:, None, :]
    w_gate_b = w_mgate.astype(MXU_DT)
    w_branch_b = w_branch.astype(MXU_DT)
    w_out_b = w_out.astype(MXU_DT)
    w_ffn_in_b = w_ffn_in.astype(MXU_DT)
    w_ffn_out_b = w_ffn_out.astype(MXU_DT)
    b_gate4 = b_mgate[:, :, None, :]
    b_ada3 = b_ada[:, None, :]
    hy_conv_b3 = hy_conv_b[:, None, :]
    mb_conv_b3 = mb_conv_b[:, None, :]
    hy_skip3 = hy_skip[:, None, :]
    hy_skip_col = hy_skip[:, :, None]
    ln1_g3, ln1_b3, ln2_g3, ln2_b3 = (a[:, None, :] for a in (ln1_g, ln1_b, ln2_g, ln2_b))
    hy_w1p = _pad_to(hy_w1, (nl, LANE, LANE))
    hy_b1p = _pad_to(hy_b1, (nl, LANE))[:, None, :]
    hy_w2p = _pad_to(hy_w2, (nl, LANE, LANE))
    hy_b2p = _pad_to(hy_b2, (nl, LANE))[:, None, :]
    hy_w3p = _pad_to(hy_w3, (nl, LANE, 2 * HY_W))
    hy_frp = _pad_to(hy_freq, (nl, LANE))[:, None, :]
    deltas = jnp.abs(jnp.linspace(HY_MIN_DECAY, HY_MAX_DECAY, HY_W, dtype=F32))[None, :]
    dt_bias_row = _pad_to(mb_dt_bias.reshape(nl, 2 * MB_HEADS), (nl, LANE))[:, None, :]
    a_log_rows = _pad_to(mb_a_log, (nl, 2, LANE))[:, :, None, :]
    d_skip_e = jnp.repeat(mb_d, MB_HEADDIM, axis=-1)[:, None, :]
    mb_norm3 = mb_norm[:, None, :]
    expand = jnp.asarray(np.kron(np.eye(LANE, MB_HEADS), np.ones((1, MB_HEADDIM))), dtype=MXU_DT)

    cos_m, sin_m = _rope_tables(n_lat, n_ctx, MLA_ROPE)
    cos_g, sin_g = _rope_tables(n_lat, n_ctx, GQA_HD)
    feats_lat, aux_lat = _filter_inputs(n_lat)
    feats_ctx, aux_ctx = _filter_inputs(n_ctx)
    dft_lat = _dft_consts(n_lat)
    tb_ctx = _ctx_tables(n_ctx)

    c8 = jnp.concatenate([c, c_ctx[None, :], jnp.zeros((8 - bsz - 1, d), F32)], axis=0)
    xz = jnp.concatenate([x, ctx], axis=1)

    for l in range(nl):
        last = l == nl - 1
        mod = ada_mod(c8, w_ada, b_ada3, l).reshape(8, 1, 6 * d)
        p, p_dt, h = in_projection(xz, mod, w_in_p, l, n_lat)

        q, k, v = mla_prep(p, gq_mla, gkv_mla, w_uq_p, w_ukv_p, cos_m, sin_m, l)
        oa = attention(q, k, v, n_lat, not last)
        q, k, v = gqa_prep(p, gq_gqa, gk_gqa, cos_g, sin_g, l)
        ob = attention(q, k, v, n_lat, not last)
        x0, vxt, vx_ctx = hyena_conv(p, hy_conv_w, hy_conv_b3, l, n_lat)
        k2t = hyena_filter(feats_lat, aux_lat, hy_w1p, hy_b1p, hy_w2p, hy_b2p, hy_w3p, hy_frp, deltas, l, True)
        h_spec = filter_spectrum(k2t, dft_lat)
        if last:
            y_ctx = jnp.zeros((bsz, n_ctx, HY_W), F32)
        else:
            k2c = hyena_filter(feats_ctx, aux_ctx, hy_w1p, hy_b1p, hy_w2p, hy_b2p, hy_w3p, hy_frp, deltas, l,
                               False)
            y_ctx = long_conv_ctx(vx_ctx, k2c, tb_ctx)
        oc = hyena_long_conv(vxt, h_spec, x0, vx_ctx, y_ctx, hy_skip_col, hy_skip3, dft_lat, l, n_lat)
        xbc = mamba_conv(p, mb_conv_w, mb_conv_b3, l, n_lat)
        dt2 = mamba_dt(p_dt, dt_bias_row, l)
        y_f, y_b = ssd_scan(xbc, dt2, a_log_rows, expand, l, n_lat)
        od = mamba_finish(y_f, y_b, xbc, p, d_skip_e, mb_norm3, l)

        out_rows = n_lat if last else n_lat + n_ctx
        acc = merge_branches(h, (oa, ob, oc, od), w_gate_b, b_gate4, w_branch_b, l, out_rows)
        x1 = matmul_res_ln(acc, w_out_b, xz, mod, 2, ln1_g3, ln1_b3, l, n_lat, out_rows,
                           (544, 512, 384, 128), d)
        act = ffn_in(x1, mod, w_ffn_in_b, l, n_lat)
        xz = matmul_res_ln(act, w_ffn_out_b, x1, mod, 5, ln2_g3, ln2_b3, l, n_lat, out_rows,
                           (256, 128), act.shape[2])
    return xz
```

```python
import functools
import math

import jax
import jax.numpy as jnp
import numpy as np
from jax import lax
from jax.experimental import pallas as pl
from jax.experimental.pallas import tpu as pltpu

F32 = jnp.float32
MXU_DT = jnp.bfloat16

D_MODEL = 2048
DEPTH = 2
GRID_W = 64
N_BRANCH = 4
BRANCH_W = D_MODEL // N_BRANCH
ROPE_THETA = 10000.0
LN_EPS = 1e-6
RMS_EPS = 1e-6
DEEPNORM_ALPHA = (2 * DEPTH) ** 0.25

MLA_HEADS = 4
MLA_Q_RANK = 448
MLA_KV_RANK = 128
MLA_NOPE = 128
MLA_ROPE = 64
MLA_V = 128
MLA_COLS = MLA_Q_RANK + MLA_KV_RANK + MLA_ROPE

GQA_HEADS = 4
GQA_KV_HEADS = 2
GQA_HD = 128
GQA_COLS = (GQA_HEADS + 2 * GQA_KV_HEADS) * GQA_HD

HY_W = BRANCH_W
HY_EMB = 33
HY_BANDS = (HY_EMB - 1) // 2
HY_FFN = 64
HY_MIN_DECAY = math.log(1e-2) / 1.5
HY_MAX_DECAY = math.log(1e-2) / 0.3
HY_COLS = 3 * HY_W

MB_INNER = BRANCH_W
MB_HEADDIM = 64
MB_HEADS = 8
MB_GROUPS = 2
MB_STATE = 128
MB_CHUNK = 128
MB_CONV_CH = MB_INNER + 2 * MB_GROUPS * MB_STATE
MB_COLS = MB_INNER + MB_CONV_CH + 2 * MB_HEADS

FFN_HIDDEN = 5632

LANE = 128
VMEM_LIMIT = 56 * 1024 * 1024

P_MLA = 0
P_MLA_W = 768
P_DT = 896
P_GQA = 1024
P_Z = 2048
P_HY = 2560
P_XBC = 4096
P_COLS = 5120

LOG2_E = math.log2(math.e)
_HI = lax.Precision.HIGHEST


def _dot(a, b):
    return jnp.dot(a, b, preferred_element_type=F32)


def _dot_hi(a, b):
    return jnp.dot(a, b, precision=_HI, preferred_element_type=F32)


def _dot_nt(a, b):
    return lax.dot_general(a, b, (((1,), (1,)), ((), ())), preferred_element_type=F32)


def _sigmoid(x):
    return 1.0 / (1.0 + jnp.exp(-x))


def _silu(x):
    return x * _sigmoid(x)


def _params(*sem):
    return pltpu.CompilerParams(dimension_semantics=sem, vmem_limit_bytes=VMEM_LIMIT)


def _pick(n, prefs):
    for p in prefs:
        if n % p == 0:
            return p
    raise ValueError(f"no tile for {n} in {prefs}")


def _standardize(x):
    mu = jnp.mean(x, axis=-1, keepdims=True)
    xc = x - mu
    var = jnp.mean(xc * xc, axis=-1, keepdims=True)
    return xc * lax.rsqrt(var + LN_EPS)


def _is_ctx_rows(tile_idx, tm, n_lat):
    row = tile_idx * tm + lax.broadcasted_iota(jnp.int32, (tm, 1), 0)
    return row >= n_lat


def _ada_kernel(c_ref, w_ref, b_ref, o_ref):
    cs = _silu(c_ref[...])
    o_ref[...] = _dot(cs.astype(MXU_DT), w_ref[...].astype(MXU_DT)) + b_ref[...]


def ada_mod(c8, w_ada, b_ada3, layer):
    d = c8.shape[1]
    n = w_ada.shape[2]
    tn = 1024
    return pl.pallas_call(
        _ada_kernel,
        grid=(n // tn,),
        in_specs=[pl.BlockSpec((8, d), lambda j: (0, 0)),
                  pl.BlockSpec((None, d, tn), lambda j: (layer, 0, j)),
                  pl.BlockSpec((None, 1, tn), lambda j: (layer, 0, j))],
        out_specs=pl.BlockSpec((8, tn), lambda j: (0, j)),
        out_shape=jax.ShapeDtypeStruct((8, n), F32),
        compiler_params=_params("arbitrary"),
        name="ada_mod",
    )(c8, w_ada, b_ada3)


ROW_CHUNKS = 4
ROW_TILES = (1088, 1024, 544, 384, 128)


def _modulated(x, sh_ref, sc_ref, shc_ref, scc_ref, row0, n_lat):
    xn = _standardize(x)
    row = row0 + lax.broadcasted_iota(jnp.int32, (x.shape[0], 1), 0)
    is_ctx = row >= n_lat
    scale = jnp.where(is_ctx, scc_ref[...], sc_ref[...])
    shift = jnp.where(is_ctx, shc_ref[...], sh_ref[...])
    return (xn * (1.0 + scale) + shift).astype(MXU_DT)


def _inproj_kernel(x_ref, sh_ref, sc_ref, shc_ref, scc_ref, w_ref, p_ref, pdt_ref, h_ref, *, tm, n_lat):
    j = pl.program_id(2)
    rc = tm // ROW_CHUNKS

    @pl.when(j == 0)
    def _():
        for r in range(ROW_CHUNKS):
            rows = slice(r * rc, (r + 1) * rc)
            hb = _modulated(x_ref[rows, :], sh_ref, sc_ref, shc_ref, scc_ref,
                            pl.program_id(1) * tm + r * rc, n_lat)
            h_ref[rows, :] = hb
            acc = _dot(hb, w_ref[...])
            p_ref[rows, :] = acc.astype(p_ref.dtype)
            pdt_ref[rows, :] = acc[:, P_DT:P_DT + LANE]

    @pl.when(j > 0)
    def _():
        p_ref[...] = _dot(h_ref[...], w_ref[...]).astype(p_ref.dtype)


def _ffn_in_kernel(x_ref, sh_ref, sc_ref, shc_ref, scc_ref, wu_ref, wg_ref, a_ref, h_ref, *, tm, n_lat):
    j = pl.program_id(2)
    rc = tm // ROW_CHUNKS

    def swiglu(h):
        up = _dot(h, wu_ref[...])
        gate = _dot(h, wg_ref[...])
        return (_silu(gate) * up).astype(a_ref.dtype)

    @pl.when(j == 0)
    def _():
        for r in range(ROW_CHUNKS):
            rows = slice(r * rc, (r + 1) * rc)
            hb = _modulated(x_ref[rows, :], sh_ref, sc_ref, shc_ref, scc_ref,
                            pl.program_id(1) * tm + r * rc, n_lat)
            h_ref[rows, :] = hb
            a_ref[rows, :] = swiglu(hb)

    @pl.when(j > 0)
    def _():
        half = tm // 2
        for r in range(2):
            rows = slice(r * half, (r + 1) * half)
            a_ref[rows, :] = swiglu(h_ref[rows, :])


def _mod_specs(k_shift, k_scale, n_batch):
    d = D_MODEL
    return [pl.BlockSpec((None, 1, d), lambda b, i, j: (b, 0, k_shift)),
            pl.BlockSpec((None, 1, d), lambda b, i, j: (b, 0, k_scale)),
            pl.BlockSpec((None, 1, d), lambda b, i, j: (n_batch, 0, k_shift)),
            pl.BlockSpec((None, 1, d), lambda b, i, j: (n_batch, 0, k_scale))]


def in_projection(xz, mod, w_in_p, layer, n_lat):
    bsz, l_all, d = xz.shape
    n = w_in_p.shape[2]
    tm = _pick(l_all, (1088, 544, 384, 128))
    tn = 1280
    assert n % tn == 0 and P_DT + LANE <= tn
    kern = functools.partial(_inproj_kernel, tm=tm, n_lat=n_lat)
    return pl.pallas_call(
        kern,
        grid=(bsz, l_all // tm, n // tn),
        in_specs=[pl.BlockSpec((None, tm, d), lambda b, i, j: (b, i, 0))]
        + _mod_specs(0, 1, bsz)
        + [pl.BlockSpec((None, d, tn), lambda b, i, j: (layer, 0, j))],
        out_specs=[pl.BlockSpec((None, tm, tn), lambda b, i, j: (b, i, j)),
                   pl.BlockSpec((None, tm, LANE), lambda b, i, j: (b, i, 0)),
                   pl.BlockSpec((None, tm, d), lambda b, i, j: (b, i, 0))],
        out_shape=[jax.ShapeDtypeStruct((bsz, l_all, n), MXU_DT),
                   jax.ShapeDtypeStruct((bsz, l_all, LANE), F32),
                   jax.ShapeDtypeStruct((bsz, l_all, d), MXU_DT)],
        compiler_params=_params("arbitrary", "arbitrary", "arbitrary"),
        name="in_projection",
    )(xz, mod, mod, mod, mod, w_in_p)


def ffn_in(xz, mod, w_ffn_in, layer, n_lat):
    bsz, l_all, d = xz.shape
    hid = w_ffn_in.shape[2] // 2
    tm = _pick(l_all, ROW_TILES)
    tn = 512
    nj = hid // tn
    kern = functools.partial(_ffn_in_kernel, tm=tm, n_lat=n_lat)
    return pl.pallas_call(
        kern,
        grid=(bsz, l_all // tm, nj),
        in_specs=[pl.BlockSpec((None, tm, d), lambda b, i, j: (b, i, 0))]
        + _mod_specs(3, 4, bsz)
        + [pl.BlockSpec((None, d, tn), lambda b, i, j: (layer, 0, j)),
           pl.BlockSpec((None, d, tn), lambda b, i, j: (layer, 0, j + nj))],
        out_specs=pl.BlockSpec((None, tm, tn), lambda b, i, j: (b, i, j)),
        out_shape=jax.ShapeDtypeStruct((bsz, l_all, hid), MXU_DT),
        scratch_shapes=[pltpu.VMEM((tm, d), MXU_DT)],
        compiler_params=_params("arbitrary", "arbitrary", "arbitrary"),
        name="ffn_in",
    )(xz, mod, mod, mod, mod, w_ffn_in, w_ffn_in)


def _resln_kernel(a_ref, w_ref, res_ref, g_ref, gc_ref, lng_ref, lnb_ref, o_ref, *, tm, n_lat, nk, rc):
    k = pl.program_id(2)
    i = pl.program_id(1)

    if nk > 1:
        @pl.when(k == 0)
        def _():
            o_ref[...] = _dot(a_ref[...], w_ref[...])

        @pl.when((k > 0) & (k < nk - 1))
        def _():
            o_ref[...] += _dot(a_ref[...], w_ref[...])

    @pl.when(k == nk - 1)
    def _():
        for r in range(tm // rc):
            rows = slice(r * rc, (r + 1) * rc)
            acc = _dot(a_ref[rows, :], w_ref[...])
            if nk > 1:
                acc = acc + o_ref[rows, :]
            row = i * tm + r * rc + lax.broadcasted_iota(jnp.int32, (rc, 1), 0)
            gate = jnp.where(row >= n_lat, gc_ref[...], g_ref[...])
            y = DEEPNORM_ALPHA * res_ref[rows, :] + gate * acc
            o_ref[rows, :] = _standardize(y) * lng_ref[...] + lnb_ref[...]


def matmul_res_ln(a, w, res, mod, k_gate, ln_g, ln_b, layer, n_lat, out_rows, tm_prefs, tk):
    bsz, l_all, kdim = a.shape
    d = w.shape[2]
    tm = _pick(out_rows, tm_prefs)
    nk = kdim // tk
    rc = _pick(tm, (272, 256, 192, 128))
    kern = functools.partial(_resln_kernel, tm=tm, n_lat=n_lat, nk=nk, rc=rc)
    w_mode = dict(pipeline_mode=pl.Buffered(1)) if nk == 1 else {}
    return pl.pallas_call(
        kern,
        grid=(bsz, pl.cdiv(out_rows, tm), nk),
        in_specs=[pl.BlockSpec((None, tm, tk), lambda b, i, k: (b, i, k)),
                  pl.BlockSpec((None, tk, d), lambda b, i, k: (layer, k, 0), **w_mode),
                  pl.BlockSpec((None, tm, d), lambda b, i, k: (b, i, 0)),
                  pl.BlockSpec((None, 1, d), lambda b, i, k: (b, 0, k_gate)),
                  pl.BlockSpec((None, 1, d), lambda b, i, k: (bsz, 0, k_gate)),
                  pl.BlockSpec((None, 1, d), lambda b, i, k: (layer, 0, 0)),
                  pl.BlockSpec((None, 1, d), lambda b, i, k: (layer, 0, 0))],
        out_specs=pl.BlockSpec((None, tm, d), lambda b, i, k: (b, i, 0)),
        out_shape=jax.ShapeDtypeStruct((bsz, out_rows, d), F32),
        compiler_params=_params("arbitrary", "arbitrary", "arbitrary"),
        name="matmul_res_ln",
    )(a, w, res, mod, mod, ln_g, ln_b)


def _merge_kernel(h_ref, oa_ref, ob_ref, oc_ref, od_ref, wg_ref, bg_ref, wb_ref, o_ref):
    h = h_ref[...]
    acc = None
    pending = None
    for i, o_r in list(enumerate((oa_ref, ob_ref, oc_ref, od_ref))) + [(None, None)]:
        nxt = None
        if o_r is not None:
            nxt = (_dot(h, wg_ref[i]) + bg_ref[i], _dot(o_r[...], wb_ref[i]))
        if pending is not None:
            g, t = pending
            term = _sigmoid(g) * t
            acc = term if acc is None else acc + term
        pending = nxt
    o_ref[...] = acc.astype(o_ref.dtype)


def merge_branches(h, outs, w_gate, b_gate4, w_branch, layer, n_rows):
    bsz, _, d = h.shape
    bw = outs[0].shape[2]
    tm = _pick(n_rows, ROW_TILES)
    tn = 512
    o_spec = pl.BlockSpec((None, tm, bw), lambda b, i, j: (b, i, 0))
    return pl.pallas_call(
        _merge_kernel,
        grid=(bsz, n_rows // tm, d // tn),
        in_specs=[pl.BlockSpec((None, tm, d), lambda b, i, j: (b, i, 0)), o_spec, o_spec, o_spec, o_spec,
                  pl.BlockSpec((None, N_BRANCH, d, tn), lambda b, i, j: (layer, 0, 0, j)),
                  pl.BlockSpec((None, N_BRANCH, 1, tn), lambda b, i, j: (layer, 0, 0, j)),
                  pl.BlockSpec((None, N_BRANCH, bw, tn), lambda b, i, j: (layer, 0, 0, j))],
        out_specs=pl.BlockSpec((None, tm, tn), lambda b, i, j: (b, i, j)),
        out_shape=jax.ShapeDtypeStruct((bsz, n_rows, d), MXU_DT),
        compiler_params=_params("arbitrary", "arbitrary", "arbitrary"),
        name="merge_branches",
    )(h, *outs, w_gate, b_gate4, w_branch)


def _rope(x, cos, sin):
    lane = lax.broadcasted_iota(jnp.int32, x.shape, 1)
    partner = jnp.where(lane % 2 == 0, pltpu.roll(x, LANE - 1, 1), pltpu.roll(x, 1, 1))
    return x * cos + partner * sin


def _mla_prep_kernel(p_ref, gq_ref, gkv_ref, wuq_ref, wukv_ref, cos_ref, sin_ref, q_ref, k_ref, v_ref,
                     vrow_ref, *, rc):
    scale = (MLA_NOPE + MLA_ROPE) ** -0.5 * LOG2_E

    def body(t, carry):
        rows = pl.ds(pl.multiple_of(t * rc, 16), rc)
        p = p_ref[rows, :].astype(F32)
        cq = p[:, 0:512]
        ckv = p[:, 512:640]
        k_rot = p[:, 640:768]
        rq = lax.rsqrt(jnp.sum(cq * cq, axis=-1, keepdims=True) * (1.0 / MLA_Q_RANK) + RMS_EPS)
        cqn = (cq * rq * gq_ref[...]).astype(MXU_DT)
        rkv = lax.rsqrt(jnp.mean(ckv * ckv, axis=-1, keepdims=True) + RMS_EPS)
        ckvn = (ckv * rkv * gkv_ref[...]).astype(MXU_DT)
        qf = _dot(cqn, wuq_ref[...])
        kvf = _dot(ckvn, wukv_ref[...])
        cos = cos_ref[rows, :]
        sin = sin_ref[rows, :]
        k_rope = _rope(k_rot, cos, sin).astype(k_ref.dtype)
        for hd in range(MLA_HEADS):
            qb = hd * 256
            q_ref[hd, rows, 0:128] = (qf[:, qb:qb + 128] * scale).astype(q_ref.dtype)
            q_rope = _rope(qf[:, qb + 128:qb + 256], cos, sin)
            q_ref[hd, rows, 128:256] = (q_rope * scale).astype(q_ref.dtype)
            k_ref[hd, rows, 0:128] = kvf[:, hd * 256:hd * 256 + 128].astype(k_ref.dtype)
            k_ref[hd, rows, 128:256] = k_rope
            vrow_ref[hd, rows, :] = kvf[:, hd * 256 + 128:hd * 256 + 256]
        return carry

    lax.fori_loop(0, p_ref.shape[0] // rc, body, 0)
    for hd in range(MLA_HEADS):
        v_ref[hd] = vrow_ref[hd].T.astype(v_ref.dtype)


def mla_prep(p, gq, gkv, wuq, wukv, cos, sin, layer):
    bsz, l_all, _ = p.shape
    tr = _pick(l_all, (1088 * 2, 256, 128))
    rc = 272 if tr % 272 == 0 else tr
    h = MLA_HEADS
    return pl.pallas_call(
        functools.partial(_mla_prep_kernel, rc=rc),
        grid=(bsz, l_all // tr),
        in_specs=[pl.BlockSpec((None, tr, P_MLA_W), lambda b, i: (b, i, 0)),
                  pl.BlockSpec((None, 1, 512), lambda b, i: (layer, 0, 0)),
                  pl.BlockSpec((None, 1, 128), lambda b, i: (layer, 0, 0)),
                  pl.BlockSpec((None, 512, h * 256), lambda b, i: (layer, 0, 0)),
                  pl.BlockSpec((None, 128, h * 256), lambda b, i: (layer, 0, 0)),
                  pl.BlockSpec((tr, LANE), lambda b, i: (i, 0)),
                  pl.BlockSpec((tr, LANE), lambda b, i: (i, 0))],
        out_specs=[pl.BlockSpec((None, h, tr, 256), lambda b, i: (b, 0, i, 0)),
                   pl.BlockSpec((None, h, tr, 256), lambda b, i: (b, 0, i, 0)),
                   pl.BlockSpec((None, h, 128, tr), lambda b, i: (b, 0, 0, i))],
        out_shape=[jax.ShapeDtypeStruct((bsz, h, l_all, 256), MXU_DT),
                   jax.ShapeDtypeStruct((bsz, h, l_all, 256), MXU_DT),
                   jax.ShapeDtypeStruct((bsz, h, 128, l_all), MXU_DT)],
        scratch_shapes=[pltpu.VMEM((h, tr, 128), F32)],
        compiler_params=_params("arbitrary", "arbitrary"),
        name="mla_prep",
    )(p, gq, gkv, wuq, wukv, cos, sin)


def _gqa_prep_kernel(p_ref, gq_ref, gk_ref, cos_ref, sin_ref, q_ref, k_ref, v_ref):
    scale = GQA_HD ** -0.5 * LOG2_E
    cos = cos_ref[...]
    sin = sin_ref[...]

    def norm_rope(x, g):
        r = lax.rsqrt(jnp.mean(x * x, axis=-1, keepdims=True) + RMS_EPS)
        return _rope(x * r * g, cos, sin)

    for hd in range(GQA_HEADS):
        x = p_ref[:, hd * 128:(hd + 1) * 128].astype(F32)
        q_ref[hd] = (norm_rope(x, gq_ref[...]) * scale).astype(q_ref.dtype)
    for hd in range(GQA_KV_HEADS):
        x = p_ref[:, 512 + hd * 128:512 + (hd + 1) * 128].astype(F32)
        k_ref[hd] = norm_rope(x, gk_ref[...]).astype(k_ref.dtype)
        v = p_ref[:, 768 + hd * 128:768 + (hd + 1) * 128].astype(F32)
        v_ref[hd] = v.T.astype(v_ref.dtype)


def gqa_prep(p, gq, gk, cos, sin, layer):
    bsz, l_all, _ = p.shape
    tr = _pick(l_all, (1088 * 2, 256, 128))
    return pl.pallas_call(
        _gqa_prep_kernel,
        grid=(bsz, l_all // tr),
        in_specs=[pl.BlockSpec((None, tr, GQA_COLS), lambda b, i: (b, i, P_GQA // GQA_COLS)),
                  pl.BlockSpec((None, 1, 128), lambda b, i: (layer, 0, 0)),
                  pl.BlockSpec((None, 1, 128), lambda b, i: (layer, 0, 0)),
                  pl.BlockSpec((tr, LANE), lambda b, i: (i, 0)),
                  pl.BlockSpec((tr, LANE), lambda b, i: (i, 0))],
        out_specs=[pl.BlockSpec((None, GQA_HEADS, tr, 128), lambda b, i: (b, 0, i, 0)),
                   pl.BlockSpec((None, GQA_KV_HEADS, tr, 128), lambda b, i: (b, 0, i, 0)),
                   pl.BlockSpec((None, GQA_KV_HEADS, 128, tr), lambda b, i: (b, 0, 0, i))],
        out_shape=[jax.ShapeDtypeStruct((bsz, GQA_HEADS, l_all, 128), MXU_DT),
                   jax.ShapeDtypeStruct((bsz, GQA_KV_HEADS, l_all, 128), MXU_DT),
                   jax.ShapeDtypeStruct((bsz, GQA_KV_HEADS, 128, l_all), MXU_DT)],
        compiler_params=_params("arbitrary", "arbitrary"),
        name="gqa_prep",
    )(p, gq, gk, cos, sin)


def _attn_kernel(q_ref, k_ref, v_ref, o_ref, *, n_lat, n_lat_tiles, sub):
    i = pl.program_id(2)

    def attend_all(k, vt, n_rows):
        groups = [slice(r, r + sub) for r in range(0, n_rows, sub)]
        sts = [_dot_nt(k, q_ref[g, :]) for g in groups]
        es, ls = [], []
        for st in sts:
            m = jnp.max(st, axis=0, keepdims=True)
            e = jnp.exp2(st - m)
            ls.append(jnp.sum(e, axis=0, keepdims=True))
            es.append(e.astype(vt.dtype))
        for g, e, l in zip(groups, es, ls):
            ot = _dot(vt, e) / l
            o_ref[g, :] = ot.T.astype(o_ref.dtype)

    @pl.when(i < n_lat_tiles)
    def _():
        attend_all(k_ref[...], v_ref[...], q_ref.shape[0])

    @pl.when(i >= n_lat_tiles)
    def _():
        attend_all(k_ref[n_lat:, :], v_ref[:, n_lat:], k_ref.shape[0] - n_lat)


def attention(q, k, v, n_lat, with_ctx):
    bsz, h, l_all, dk = q.shape
    hkv = k.shape[1]
    grp = h // hkv
    dv = v.shape[2]
    tq = _pick(n_lat, (1024, 512, 256, 128))
    assert l_all - n_lat <= tq
    out_rows = l_all if with_ctx else n_lat
    kern = functools.partial(_attn_kernel, n_lat=n_lat, n_lat_tiles=n_lat // tq, sub=min(256, tq))
    return pl.pallas_call(
        kern,
        grid=(bsz, h, pl.cdiv(out_rows, tq)),
        in_specs=[pl.BlockSpec((None, None, tq, dk), lambda b, hh, i: (b, hh, i, 0)),
                  pl.BlockSpec((None, None, l_all, dk), lambda b, hh, i: (b, hh // grp, 0, 0)),
                  pl.BlockSpec((None, None, dv, l_all), lambda b, hh, i: (b, hh // grp, 0, 0))],
        out_specs=pl.BlockSpec((None, tq, dv), lambda b, hh, i: (b, i, hh)),
        out_shape=jax.ShapeDtypeStruct((bsz, out_rows, h * dv), MXU_DT),
        compiler_params=_params("arbitrary", "arbitrary", "arbitrary"),
        name="attention",
    )(q, k, v)


CONV_TC = 256


def _conv3(u, w, b, n_lat):
    u = u.astype(F32)
    n = u.shape[0]
    row = lax.broadcasted_iota(jnp.int32, (n, 1), 0)
    prev = jnp.where((row == 0) | (row == n_lat), 0.0, pltpu.roll(u, 1, 0))
    nxt = jnp.where((row == n_lat - 1) | (row == n - 1), 0.0, pltpu.roll(u, n - 1, 0))
    return w[0:1, :] * prev + w[1:2, :] * u + w[2:3, :] * nxt + b


def _hy_conv_kernel(p0_ref, p1_ref, pv_ref, w0_ref, w1_ref, wv_ref, b0_ref, b1_ref, bv_ref,
                    x0_ref, vxt_ref, vxc_ref, *, n_lat):
    x0_ref[...] = _conv3(p0_ref[...], w0_ref[...], b0_ref[...], n_lat)
    x1 = _conv3(p1_ref[...], w1_ref[...], b1_ref[...], n_lat)
    v = _conv3(pv_ref[...], wv_ref[...], bv_ref[...], n_lat)
    vx = v * x1
    vxt_ref[...] = vx[:n_lat, :].T
    vxc_ref[...] = vx[n_lat:, :]


def hyena_conv(p, conv_w, conv_b3, layer, n_lat):
    bsz, l_all, _ = p.shape
    tc = CONV_TC
    nb = HY_W // tc
    base = P_HY // tc

    def pspec(off):
        return pl.BlockSpec((None, l_all, tc), lambda b, c: (b, 0, base + off + c))

    def wspec(off):
        return pl.BlockSpec((None, 3, tc), lambda b, c: (layer, 0, off + c))

    def bspec(off):
        return pl.BlockSpec((None, 1, tc), lambda b, c: (layer, 0, off + c))

    o_spec = pl.BlockSpec((None, l_all, tc), lambda b, c: (b, 0, c))
    return pl.pallas_call(
        functools.partial(_hy_conv_kernel, n_lat=n_lat),
        grid=(bsz, nb),
        in_specs=[pspec(0), pspec(nb), pspec(2 * nb), wspec(0), wspec(nb), wspec(2 * nb),
                  bspec(0), bspec(nb), bspec(2 * nb)],
        out_specs=[o_spec,
                   pl.BlockSpec((None, tc, n_lat), lambda b, c: (b, c, 0)),
                   pl.BlockSpec((None, l_all - n_lat, tc), lambda b, c: (b, 0, c))],
        out_shape=[jax.ShapeDtypeStruct((bsz, l_all, HY_W), F32),
                   jax.ShapeDtypeStruct((bsz, HY_W, n_lat), F32),
                   jax.ShapeDtypeStruct((bsz, l_all - n_lat, HY_W), F32)],
        compiler_params=_params("arbitrary", "arbitrary"),
        name="hyena_conv",
    )(p, p, p, conv_w, conv_w, conv_w, conv_b3, conv_b3, conv_b3)


def _mb_conv_kernel(p_ref, w_ref, b_ref, o_ref, *, n_lat):
    o_ref[...] = _silu(_conv3(p_ref[...], w_ref[...], b_ref[...], n_lat))


def mamba_conv(p, conv_w, conv_b3, layer, n_lat):
    bsz, l_all, _ = p.shape
    tc = CONV_TC
    nb = MB_CONV_CH // tc
    base = P_XBC // tc
    return pl.pallas_call(
        functools.partial(_mb_conv_kernel, n_lat=n_lat),
        grid=(bsz, nb),
        in_specs=[pl.BlockSpec((None, l_all, tc), lambda b, c: (b, 0, base + c)),
                  pl.BlockSpec((None, 3, tc), lambda b, c: (layer, 0, c)),
                  pl.BlockSpec((None, 1, tc), lambda b, c: (layer, 0, c))],
        out_specs=pl.BlockSpec((None, l_all, tc), lambda b, c: (b, 0, c)),
        out_shape=jax.ShapeDtypeStruct((bsz, l_all, MB_CONV_CH), F32),
        compiler_params=_params("arbitrary", "arbitrary"),
        name="mamba_conv",
    )(p, conv_w, conv_b3)


def _softplus(x):
    return jnp.maximum(x, 0.0) + jnp.log(1.0 + jnp.exp(-jnp.abs(x)))


def _mb_dt_kernel(p_ref, bias_ref, o_ref):
    dt = _softplus(p_ref[...] + bias_ref[...])
    o_ref[0] = dt
    o_ref[1] = pltpu.roll(dt, LANE - MB_HEADS, 1)


def mamba_dt(p, dt_bias_row, layer):
    bsz, l_all, _ = p.shape
    return pl.pallas_call(
        _mb_dt_kernel,
        grid=(bsz,),
        in_specs=[pl.BlockSpec((None, l_all, LANE), lambda b: (b, 0, 0)),
                  pl.BlockSpec((None, 1, LANE), lambda b: (layer, 0, 0))],
        out_specs=pl.BlockSpec((2, None, l_all, LANE), lambda b: (0, b, 0, 0)),
        out_shape=jax.ShapeDtypeStruct((2, bsz, l_all, LANE), F32),
        compiler_params=_params("arbitrary"),
        name="mamba_dt",
    )(p, dt_bias_row)


def _hy_filter_kernel(f_ref, aux_ref, w1_ref, b1_ref, w2_ref, b2_ref, w3_ref, fr_ref, dl_ref, o_ref, *,
                      channel_major):
    fr = fr_ref[...]
    tr = f_ref.shape[0]
    n_ch = 4 if tr % 512 == 0 else 1
    rc = tr // n_ch
    chunks = [slice(r * rc, (r + 1) * rc) for r in range(n_ch)]
    hdn = [_dot_hi(f_ref[rows, :], w1_ref[...]) for rows in chunks]
    hdn = [jnp.sin(fr * (x + b1_ref[...])) for x in hdn]
    hdn = [_dot_hi(x, w2_ref[...]) for x in hdn]
    hdn = [jnp.sin(fr * (x + b2_ref[...])) for x in hdn]
    filts = [_dot_hi(x, w3_ref[...]) for x in hdn]
    for rows, filt in zip(chunks, filts):
        aux = aux_ref[rows, :]
        t = aux[:, 0:1]
        is_fwd = aux[:, 1:2] > 0.5
        valid = aux[:, 2:3]
        window = jnp.exp(-t * dl_ref[...]) * valid
        k2 = jnp.where(is_fwd, filt[:, :HY_W], filt[:, HY_W:]) * window
        if channel_major:
            o_ref[:, rows] = k2.T
        else:
            o_ref[rows, :] = k2


def hyena_filter(feats2, aux, w1p, b1p, w2p, b2p, w3p, frp, deltas, layer, channel_major):
    rows = feats2.shape[0]
    tr = _pick(rows, (512, 256))

    def lspec(shape):
        return pl.BlockSpec((None,) + shape, lambda i: (layer, 0, 0))

    if channel_major:
        out_spec = pl.BlockSpec((HY_W, tr), lambda i: (0, i))
        out_shape = jax.ShapeDtypeStruct((HY_W, rows), F32)
    else:
        out_spec = pl.BlockSpec((tr, HY_W), lambda i: (i, 0))
        out_shape = jax.ShapeDtypeStruct((rows, HY_W), F32)
    return pl.pallas_call(
        functools.partial(_hy_filter_kernel, channel_major=channel_major),
        grid=(rows // tr,),
        in_specs=[pl.BlockSpec((tr, LANE), lambda i: (i, 0)),
                  pl.BlockSpec((tr, LANE), lambda i: (i, 0)),
                  lspec((LANE, LANE)), lspec((1, LANE)), lspec((LANE, LANE)), lspec((1, LANE)),
                  lspec((LANE, 2 * HY_W)), lspec((1, LANE)),
                  pl.BlockSpec((1, HY_W), lambda i: (0, 0))],
        out_specs=out_spec,
        out_shape=out_shape,
        compiler_params=_params("arbitrary"),
        name="hyena_filter",
    )(feats2, aux, w1p, b1p, w2p, b2p, w3p, frp, deltas)


DFT_MINOR = 256


def _snap(c):
    for v in (0.0, 1.0, -1.0):
        if abs(c - v) < 1e-12:
            return v
    return float(c)


def _lincomb(terms):
    acc = None
    for cf, tile in terms:
        if cf == 0.0:
            continue
        v = tile()
        if acc is None:
            acc = v if cf == 1.0 else (-v if cf == -1.0 else cf * v)
        elif cf == 1.0:
            acc = acc + v
        elif cf == -1.0:
            acc = acc - v
        else:
            acc = acc + cf * v
    return acc


def _dft_consts(n_seq):
    n = 2 * n_seq
    n2 = DFT_MINOR
    n1 = n // n2
    nkj = n1 // 2 + 1
    ang1 = 2.0 * np.pi * np.outer(np.arange(nkj), np.arange(n1)) / n1
    cos1 = [[_snap(v) for v in r] for r in np.cos(ang1)]
    sin1 = [[_snap(v) for v in r] for r in np.sin(ang1)]
    ang_t = 2.0 * np.pi * np.outer(np.arange(nkj), np.arange(n2)) / n
    rows = ((nkj + 7) // 8) * 8
    tw_c = np.zeros((rows, n2))
    tw_s = np.zeros((rows, n2))
    tw_c[:nkj] = np.cos(ang_t)
    tw_s[:nkj] = np.sin(ang_t)
    ang2 = 2.0 * np.pi * np.outer(np.arange(n2), np.arange(n2)) / n2
    c2, s2 = np.cos(ang2), np.sin(ang2)
    m_fwd = np.block([[c2, -s2], [s2, c2]])
    m_inv = np.block([[c2, s2], [-s2, c2]])
    as32 = lambda a: jnp.asarray(a, dtype=F32)

    def split(m):
        m32 = as32(m)
        hi = m32.astype(MXU_DT)
        lo = (m32 - hi.astype(F32)).astype(MXU_DT)
        return jnp.stack([hi, lo])

    return dict(n=n, n1=n1, nkj=nkj, cos1=cos1, sin1=sin1, tw_c=as32(tw_c), tw_s=as32(tw_s),
                m_fwd=split(m_fwd), m_inv=split(m_inv))


def _dot_split(a, m_ref):
    a_hi = a.astype(MXU_DT)
    a_lo = (a - a_hi.astype(F32)).astype(MXU_DT)
    return _dot(a_hi, m_ref[0]) + _dot(a_lo, m_ref[0]) + _dot(a_hi, m_ref[1])


def _outer_fwd_block(src_ref, z_ref, twc_ref, tws_ref, cb, n_in, cst):
    n2 = DFT_MINOR
    nkj, cos1, sin1 = cst["nkj"], cst["cos1"], cst["sin1"]
    tile = lambda j: (lambda: src_ref[cb * 8:cb * 8 + 8, j * n2:(j + 1) * n2])
    for kj in range(nkj):
        re = _lincomb([(cos1[kj][j], tile(j)) for j in range(n_in)])
        im = _lincomb([(-sin1[kj][j], tile(j)) for j in range(n_in)])
        if kj > 0 and im is not None:
            c = twc_ref[kj:kj + 1, :]
            s = tws_ref[kj:kj + 1, :]
            re, im = re * c + im * s, im * c - re * s
        elif kj > 0:
            re, im = re * twc_ref[kj:kj + 1, :], -re * tws_ref[kj:kj + 1, :]
        r = (cb * nkj + kj) * 8
        z_ref[r:r + 8, 0:n2] = re
        z_ref[r:r + 8, n2:2 * n2] = jnp.zeros_like(re) if im is None else im


def _outer_inv_block(z_ref, yt_ref, twc_ref, tws_ref, cb, cst):
    n2 = DFT_MINOR
    nkj, n1, cos1, sin1 = cst["nkj"], cst["n1"], cst["cos1"], cst["sin1"]
    inv_n = 1.0 / cst["n"]
    for kj in range(1, nkj):
        r = (cb * nkj + kj) * 8
        re = z_ref[r:r + 8, 0:n2]
        im = z_ref[r:r + 8, n2:2 * n2]
        c = twc_ref[kj:kj + 1, :]
        s = tws_ref[kj:kj + 1, :]
        z_ref[r:r + 8, 0:n2] = re * c - im * s
        z_ref[r:r + 8, n2:2 * n2] = im * c + re * s
    row = lambda kj: (cb * nkj + kj) * 8
    g_re = lambda kj: (lambda: z_ref[row(kj):row(kj) + 8, 0:n2])
    g_im = lambda kj: (lambda: z_ref[row(kj):row(kj) + 8, n2:2 * n2])
    for j in range(n1 // 2):
        terms = []
        for kj in range(nkj):
            wgt = inv_n if kj in (0, nkj - 1) else 2.0 * inv_n
            terms.append((wgt * cos1[kj][j], g_re(kj)))
            terms.append((-wgt * sin1[kj][j], g_im(kj)))
        yt_ref[cb * 8:cb * 8 + 8, j * n2:(j + 1) * n2] = _lincomb(terms)


def _spectrum_kernel(k_ref, twc_ref, tws_ref, mf_ref, h_ref, z_ref, *, ct, cst, n_chunks):
    nkj = cst["nkj"]
    per = ct // 8 // n_chunks
    rc = per * nkj * 8
    for c in range(n_chunks + 1):
        if c < n_chunks:
            for cb in range(c * per, (c + 1) * per):
                _outer_fwd_block(k_ref, z_ref, twc_ref, tws_ref, cb, cst["n1"], cst)
        if c >= 1:
            rows = slice((c - 1) * rc, c * rc)
            h_ref[rows, :] = _dot_split(z_ref[rows, :], mf_ref)


def _long_conv_kernel(x_ref, h_ref, twc_ref, tws_ref, mf_ref, mi_ref, x0_ref, vxc_ref, yc_ref, skip_ref,
                      skip_row_ref, o_ref, z_ref, yt_ref, *, ct, cst, n_chunks, n_lat):
    n2 = DFT_MINOR
    nkj, n1 = cst["nkj"], cst["n1"]
    per = ct // 8 // n_chunks
    rc = per * nkj * 8
    for c in range(n_chunks + 2):
        if c < n_chunks:
            for cb in range(c * per, (c + 1) * per):
                _outer_fwd_block(x_ref, z_ref, twc_ref, tws_ref, cb, n1 // 2, cst)
        if 1 <= c <= n_chunks:
            rows = slice((c - 1) * rc, c * rc)
            x = _dot_split(z_ref[rows, :], mf_ref)
            xre, xim = x[:, :n2], x[:, n2:]
            hre, him = h_ref[rows, 0:n2], h_ref[rows, n2:2 * n2]
            y = jnp.concatenate([xre * hre - xim * him, xre * him + xim * hre], axis=1)
            z_ref[rows, :] = _dot_split(y, mi_ref)
        if c >= 2:
            for cb in range((c - 2) * per, (c - 1) * per):
                _outer_inv_block(z_ref, yt_ref, twc_ref, tws_ref, cb, cst)

    y = (yt_ref[...] + x_ref[...] * skip_ref[...]).T
    o_ref[0:n_lat, :] = (x0_ref[0:n_lat, :] * y).astype(o_ref.dtype)
    yc = yc_ref[...] + vxc_ref[...] * skip_row_ref[...]
    o_ref[n_lat:, :] = (x0_ref[n_lat:, :] * yc).astype(o_ref.dtype)


def filter_spectrum(k2t, cst):
    ch, n = k2t.shape
    ct = LANE
    nkj = cst["nkj"]
    n_chunks = _pick(nkj * ct // 8, (4, 5, 1))
    full = lambda a: pl.BlockSpec(a.shape, lambda c: (0,) * a.ndim)
    return pl.pallas_call(
        functools.partial(_spectrum_kernel, ct=ct, cst=cst, n_chunks=n_chunks),
        grid=(ch // ct,),
        in_specs=[pl.BlockSpec((ct, n), lambda c: (c, 0)),
                  full(cst["tw_c"]), full(cst["tw_s"]), full(cst["m_fwd"])],
        out_specs=pl.BlockSpec((None, nkj * ct, 2 * DFT_MINOR), lambda c: (c, 0, 0)),
        out_shape=jax.ShapeDtypeStruct((ch // ct, nkj * ct, 2 * DFT_MINOR), F32),
        scratch_shapes=[pltpu.VMEM((nkj * ct, 2 * DFT_MINOR), F32)],
        compiler_params=_params("arbitrary"),
        name="filter_spectrum",
    )(k2t, cst["tw_c"], cst["tw_s"], cst["m_fwd"])


def hyena_long_conv(vxt, h_spec, x0, vx_ctx, y_ctx, skip_col, skip_row, cst, layer, n_lat):
    bsz, ch, _ = vxt.shape
    l_all = x0.shape[1]
    n_ctx = l_all - n_lat
    ct = LANE
    nkj = cst["nkj"]
    n_chunks = _pick(nkj * ct // 8, (4, 5, 1))
    full = lambda a: pl.BlockSpec(a.shape, lambda b, c: (0,) * a.ndim)
    return pl.pallas_call(
        functools.partial(_long_conv_kernel, ct=ct, cst=cst, n_chunks=n_chunks, n_lat=n_lat),
        grid=(bsz, ch // ct),
        in_specs=[pl.BlockSpec((None, ct, n_lat), lambda b, c: (b, c, 0)),
                  pl.BlockSpec((None, nkj * ct, 2 * DFT_MINOR), lambda b, c: (c, 0, 0)),
                  full(cst["tw_c"]), full(cst["tw_s"]), full(cst["m_fwd"]), full(cst["m_inv"]),
                  pl.BlockSpec((None, l_all, ct), lambda b, c: (b, 0, c)),
                  pl.BlockSpec((None, n_ctx, ct), lambda b, c: (b, 0, c)),
                  pl.BlockSpec((None, n_ctx, ct), lambda b, c: (b, 0, c)),
                  pl.BlockSpec((None, ct, 1), lambda b, c: (layer, c, 0)),
                  pl.BlockSpec((None, 1, ct), lambda b, c: (layer, 0, c))],
        out_specs=pl.BlockSpec((None, l_all, ct), lambda b, c: (b, 0, c)),
        out_shape=jax.ShapeDtypeStruct((bsz, l_all, ch), MXU_DT),
        scratch_shapes=[pltpu.VMEM((nkj * ct, 2 * DFT_MINOR), F32), pltpu.VMEM((ct, n_lat), F32)],
        compiler_params=_params("arbitrary", "arbitrary"),
        name="hyena_long_conv",
    )(vxt, h_spec, cst["tw_c"], cst["tw_s"], cst["m_fwd"], cst["m_inv"], x0, vx_ctx, y_ctx, skip_col, skip_row)


def _ctx_conv_kernel(v_ref, k_ref, fd_ref, fk_ref, gi_ref, o_ref, *, kp):
    x = _dot_hi(fd_ref[...], v_ref[...])
    h = _dot_hi(fk_ref[...], k_ref[...])
    xre, xim = x[:kp], x[kp:]
    hre, him = h[:kp], h[kp:]
    y = jnp.concatenate([xre * hre - xim * him, xre * him + xim * hre], axis=0)
    o_ref[...] = _dot_hi(gi_ref[...], y)


def _ctx_tables(n_ctx):
    n = 2 * n_ctx
    kh = n_ctx + 1
    kp = ((kh + 7) // 8) * 8
    k = np.arange(kp)[:, None]
    live = (k < kh).astype(np.float64)
    ang = 2.0 * np.pi * k * np.arange(n)[None, :] / n
    ck, sk = np.cos(ang) * live, np.sin(ang) * live
    fk = np.concatenate([ck, -sk], axis=0)
    fd = fk[:, :n_ctx]
    wk = np.where((k == 0) | (k == n_ctx), 1.0, 2.0) * live / n
    gi = np.concatenate([(ck * wk).T, (-sk * wk).T], axis=1)[:n_ctx]
    as32 = lambda a: jnp.asarray(a, dtype=F32)
    return dict(kp=kp, fd=as32(fd), fk=as32(fk), gi=as32(gi))


def long_conv_ctx(vx_ctx, k2c, tb):
    bsz, n_ctx, ch = vx_ctx.shape
    kp = tb["kp"]
    full = lambda a: pl.BlockSpec(a.shape, lambda b, c: (0,) * a.ndim)
    return pl.pallas_call(
        functools.partial(_ctx_conv_kernel, kp=kp),
        grid=(bsz, ch // LANE),
        in_specs=[pl.BlockSpec((None, n_ctx, LANE), lambda b, c: (b, 0, c)),
                  pl.BlockSpec((2 * n_ctx, LANE), lambda b, c: (0, c)),
                  full(tb["fd"]), full(tb["fk"]), full(tb["gi"])],
        out_specs=pl.BlockSpec((None, n_ctx, LANE), lambda b, c: (b, 0, c)),
        out_shape=jax.ShapeDtypeStruct((bsz, n_ctx, ch), F32),
        compiler_params=_params("arbitrary", "arbitrary"),
        name="ctx_long_conv",
    )(vx_ctx, k2c, tb["fd"], tb["fk"], tb["gi"])


def _split3(a):
    a1 = a.astype(MXU_DT)
    r = a - a1.astype(F32)
    a2 = r.astype(MXU_DT)
    a3 = (r - a2.astype(F32)).astype(MXU_DT)
    return a1, a2, a3


def _ssd_chunk(backward, xs_ref, b_ref, c_ref, dt_ref, alog, ex, y_ref, state_ref):
    n = MB_CHUNK
    li = lax.broadcasted_iota(jnp.int32, (n, n), 0)
    si = lax.broadcasted_iota(jnp.int32, (n, n), 1)
    mask = (li <= si) if backward else (li >= si)
    tri = mask.astype(MXU_DT)
    lane = lax.broadcasted_iota(jnp.int32, (n, LANE), 1)

    dt = dt_ref[...]
    da = dt * (-jnp.exp(alog))
    cum = sum(_dot(tri, t) for t in _split3(da))
    cum_t = cum.T
    both = jnp.concatenate([cum, dt], axis=0)
    both_e = sum(_dot(t, ex) for t in _split3(both))
    cum_e = both_e[:n]
    dt_e = both_e[n:]
    total_e = cum_e[0:1] if backward else cum_e[n - 1:n]

    xdt = xs_ref[...] * dt_e
    xdt_b = xdt.astype(MXU_DT)
    w_b = (jnp.exp(total_e - cum_e) * xdt).astype(MXU_DT)
    st = state_ref[...]
    y_parts = []
    s_parts = []
    gw = MB_STATE
    hw = (MB_HEADS // MB_GROUPS) * MB_HEADDIM
    for g in range(MB_GROUPS):
        cg = c_ref[:, g * gw:(g + 1) * gw].astype(MXU_DT)
        bg = b_ref[:, g * gw:(g + 1) * gw]
        cb = _dot_nt(cg, bg.astype(MXU_DT))
        y_off = _dot(cg, st[:, g * hw:(g + 1) * hw].astype(MXU_DT))
        s_parts.append(_dot(bg.T.astype(MXU_DT), w_b[:, g * hw:(g + 1) * hw]))
        diag = []
        for j in range(hw // LANE):
            lo = g * hw + j * LANE
            xp = xdt_b[:, lo:lo + LANE]
            pair = []
            for e in range(2):
                hd = (lo // MB_HEADDIM) + e
                seg = cum[:, hd:hd + 1] - cum_t[hd:hd + 1, :]
                decay = jnp.exp(jnp.where(mask, seg, -jnp.inf))
                pair.append(_dot((cb * decay).astype(MXU_DT), xp))
            diag.append(jnp.where(lane < MB_HEADDIM, pair[0], pair[1]))
        y_parts.append(jnp.concatenate(diag, axis=1) + y_off * jnp.exp(cum_e[:, g * hw:(g + 1) * hw]))
    y_ref[...] = jnp.concatenate(y_parts, axis=1)
    state_ref[...] = st * jnp.exp(total_e) + jnp.concatenate(s_parts, axis=1)


def _ssd_kernel(xf_ref, bf_ref, cf_ref, dtf_ref, xb_ref, bb_ref, cb_ref, dtb_ref, alog_ref, ex_ref,
                yf_ref, yb_ref, state_ref):
    @pl.when(pl.program_id(1) == 0)
    def _():
        state_ref[...] = jnp.zeros_like(state_ref)

    ex = ex_ref[...]
    for e in range(xf_ref.shape[0]):
        _ssd_chunk(False, xf_ref.at[e], bf_ref.at[e], cf_ref.at[e], dtf_ref.at[e], alog_ref[0], ex,
                   yf_ref.at[e], state_ref.at[2 * e])
        _ssd_chunk(True, xb_ref.at[e], bb_ref.at[e], cb_ref.at[e], dtb_ref.at[e], alog_ref[1], ex,
                   yb_ref.at[e], state_ref.at[2 * e + 1])


def ssd_scan(xbc, dt2, a_log_rows, expand, layer, n_lat):
    bsz, l_all, _ = xbc.shape
    nc = l_all // MB_CHUNK
    ncl = n_lat // MB_CHUNK
    fwd = lambda c: (c + ncl) % nc
    bwd = lambda c: nc - 1 - c

    nb = 2 if bsz % 2 == 0 else 1

    def specs(blk, direction):
        return [pl.BlockSpec((nb, MB_CHUNK, MB_INNER), lambda b, c: (b, blk(c), 0)),
                pl.BlockSpec((nb, MB_CHUNK, 256), lambda b, c: (b, blk(c), 2)),
                pl.BlockSpec((nb, MB_CHUNK, 256), lambda b, c: (b, blk(c), 3)),
                pl.BlockSpec((None, nb, MB_CHUNK, LANE), lambda b, c: (direction, b, blk(c), 0))]

    y_shape = jax.ShapeDtypeStruct((bsz, l_all, MB_INNER), F32)
    return pl.pallas_call(
        _ssd_kernel,
        grid=(bsz // nb, nc),
        in_specs=specs(fwd, 0) + specs(bwd, 1)
        + [pl.BlockSpec((None, 2, 1, LANE), lambda b, c: (layer, 0, 0, 0)),
           pl.BlockSpec((LANE, MB_INNER), lambda b, c: (0, 0))],
        out_specs=[pl.BlockSpec((nb, MB_CHUNK, MB_INNER), lambda b, c: (b, fwd(c), 0)),
                   pl.BlockSpec((nb, MB_CHUNK, MB_INNER), lambda b, c: (b, bwd(c), 0))],
        out_shape=[y_shape, y_shape],
        scratch_shapes=[pltpu.VMEM((2 * nb, MB_STATE, MB_INNER), F32)],
        compiler_params=_params("arbitrary", "arbitrary"),
        name="ssd_scan",
    )(xbc, xbc, xbc, dt2, xbc, xbc, xbc, dt2, a_log_rows, expand)


def _mb_finish_kernel(yf_ref, yb_ref, xs_ref, z_ref, dsk_ref, g_ref, o_ref):
    y = yf_ref[...] + yb_ref[...] + xs_ref[...] * dsk_ref[...]
    y = y * _silu(z_ref[...].astype(F32))
    r = lax.rsqrt(jnp.mean(y * y, axis=-1, keepdims=True) + RMS_EPS)
    o_ref[...] = (y * r * g_ref[...]).astype(o_ref.dtype)


def mamba_finish(y_f, y_b, xbc, p, d_skip_e, norm_g3, layer):
    bsz, l_all, _ = xbc.shape
    tr = _pick(l_all, (544, 384, 128))
    w = MB_INNER
    return pl.pallas_call(
        _mb_finish_kernel,
        grid=(bsz, l_all // tr),
        in_specs=[pl.BlockSpec((None, tr, w), lambda b, i: (b, i, 0)),
                  pl.BlockSpec((None, tr, w), lambda b, i: (b, i, 0)),
                  pl.BlockSpec((None, tr, w), lambda b, i: (b, i, 0)),
                  pl.BlockSpec((None, tr, w), lambda b, i: (b, i, P_Z // w)),
                  pl.BlockSpec((None, 1, w), lambda b, i: (layer, 0, 0)),
                  pl.BlockSpec((None, 1, w), lambda b, i: (layer, 0, 0))],
        out_specs=pl.BlockSpec((None, tr, w), lambda b, i: (b, i, 0)),
        out_shape=jax.ShapeDtypeStruct((bsz, l_all, w), MXU_DT),
        compiler_params=_params("arbitrary", "arbitrary"),
        name="mamba_finish",
    )(y_f, y_b, xbc, p, d_skip_e, norm_g3)


_W_IN_MOVES = (
    (0, 0, MLA_Q_RANK),
    (512, MLA_Q_RANK, MLA_KV_RANK),
    (640, MLA_Q_RANK + MLA_KV_RANK, MLA_ROPE),
    (P_DT, MLA_COLS + GQA_COLS + HY_COLS + MB_INNER + MB_CONV_CH, 2 * MB_HEADS),
    (P_GQA, MLA_COLS, GQA_COLS),
    (P_Z, MLA_COLS + GQA_COLS + HY_COLS, MB_INNER),
    (P_HY, MLA_COLS + GQA_COLS, HY_COLS),
    (P_XBC, MLA_COLS + GQA_COLS + HY_COLS + MB_INNER, MB_CONV_CH),
)


def _w_in_prep_kernel(w_ref, o_ref):
    o_ref[...] = jnp.zeros_like(o_ref)
    for dst, src, n in _W_IN_MOVES:
        o_ref[:, dst:dst + n] = w_ref[:, src:src + n].astype(o_ref.dtype)


def _prep_w_in(w_in):
    nl, d, cols = w_in.shape
    tr = 256
    return pl.pallas_call(
        _w_in_prep_kernel,
        grid=(nl, d // tr),
        in_specs=[pl.BlockSpec((None, tr, cols), lambda l, i: (l, i, 0))],
        out_specs=pl.BlockSpec((None, tr, P_COLS), lambda l, i: (l, i, 0)),
        out_shape=jax.ShapeDtypeStruct((nl, d, P_COLS), MXU_DT),
        compiler_params=_params("arbitrary", "arbitrary"),
        name="w_in_layout",
    )(w_in)


def _prep_w_uq(w_uq):
    nl = w_uq.shape[0]
    w = w_uq.reshape(nl, MLA_Q_RANK, MLA_HEADS, MLA_NOPE + MLA_ROPE)
    w = _pad_to(w, (nl, 512, MLA_HEADS, 256))
    return w.reshape(nl, 512, MLA_HEADS * 256).astype(MXU_DT)


def _rope_tables(n_lat, n_ctx, rot_dim):
    rows = n_lat // GRID_W
    row = jnp.repeat(jnp.arange(rows, dtype=F32), GRID_W)
    col = jnp.tile(jnp.arange(GRID_W, dtype=F32), rows)
    n_freq = rot_dim // 4
    inv_freq = ROPE_THETA ** (-jnp.arange(n_freq, dtype=F32) / n_freq)
    ang = jnp.concatenate([row[:, None] * inv_freq, col[:, None] * inv_freq], axis=-1)
    half = rot_dim // 2
    cos = jnp.concatenate([jnp.cos(ang), jnp.ones((n_ctx, half), F32)], axis=0)
    sin = jnp.concatenate([jnp.sin(ang), jnp.zeros((n_ctx, half), F32)], axis=0)
    cos_t = jnp.repeat(cos, 2, axis=-1)
    sin_t = jnp.stack([-sin, sin], axis=-1).reshape(n_lat + n_ctx, rot_dim)
    shape = (n_lat + n_ctx, LANE)
    return _pad_to(cos_t, shape), _pad_to(sin_t, shape)


def _filter_inputs(n):
    t = jnp.linspace(0.0, 1.0, n, dtype=F32)[:, None]
    omega = 2.0 * math.pi * jnp.arange(n, dtype=F32) / n
    bands = jnp.linspace(1e-4, HY_BANDS - 1, HY_BANDS, dtype=F32)
    ang = omega[:, None] * bands[None, :]
    feats = jnp.concatenate([t, jnp.cos(ang), -jnp.sin(ang)], axis=-1)
    zero = jnp.zeros((1, HY_EMB), F32)
    feats2 = jnp.concatenate([feats, zero, feats[1:][::-1]], axis=0)
    feats2 = jnp.concatenate([feats2, jnp.zeros((2 * n, LANE - HY_EMB), F32)], axis=-1)
    t2 = jnp.concatenate([t, jnp.zeros((1, 1), F32), t[1:][::-1]], axis=0)
    idx = jnp.arange(2 * n)[:, None]
    aux = jnp.concatenate([t2, (idx < n).astype(F32), (idx != n).astype(F32),
                           jnp.zeros((2 * n, LANE - 3), F32)], axis=-1)
    return feats2, aux


def _pad_to(a, shape):
    pads = [(0, s - d) for d, s in zip(a.shape, shape)]
    return jnp.pad(a, pads)


def kernel(x, c, ctx, c_ctx, w_ada, b_ada, w_in, mla_q_norm, mla_kv_norm, mla_w_uq, mla_w_ukv, gqa_q_norm, gqa_k_norm, hy_conv_w, hy_conv_b, hy_w1, hy_b1, hy_w2, hy_b2, hy_w3, hy_freq, hy_skip, mb_conv_w, mb_conv_b, mb_a_log, mb_dt_bias, mb_d, mb_norm, w_mgate, b_mgate, w_branch, w_out, ln1_g, ln1_b, w_ffn_in, w_ffn_out, ln2_g, ln2_b):
    bsz, n_lat, d = x.shape
    n_ctx = ctx.shape[1]
    nl = w_in.shape[0]
    assert d == D_MODEL and bsz < 8 and n_lat % n_ctx == 0 and n_ctx % MB_CHUNK == 0

    w_in_p = _prep_w_in(w_in)
    w_uq_p = _prep_w_uq(mla_w_uq)
    w_ukv_p = mla_w_ukv.astype(MXU_DT)
    gq_mla = _pad_to(mla_q_norm, (nl, 512))[:, None, :]
    gkv_mla = mla_kv_norm[:, None, :]
    gq_gqa = gqa_q_norm[:, None, :]
    gk_gqa = gqa_k_norm[:, None, :]
    w_gate_b = w_mgate.astype(MXU_DT)
    w_branch_b = w_branch.astype(MXU_DT)
    w_out_b = w_out.astype(MXU_DT)
    w_ffn_in_b = w_ffn_in.astype(MXU_DT)
    w_ffn_out_b = w_ffn_out.astype(MXU_DT)
    b_gate4 = b_mgate[:, :, None, :]
    b_ada3 = b_ada[:, None, :]
    hy_conv_b3 = hy_conv_b[:, None, :]
    mb_conv_b3 = mb_conv_b[:, None, :]
    hy_skip3 = hy_skip[:, None, :]
    hy_skip_col = hy_skip[:, :, None]
    ln1_g3, ln1_b3, ln2_g3, ln2_b3 = (a[:, None, :] for a in (ln1_g, ln1_b, ln2_g, ln2_b))
    hy_w1p = _pad_to(hy_w1, (nl, LANE, LANE))
    hy_b1p = _pad_to(hy_b1, (nl, LANE))[:, None, :]
    hy_w2p = _pad_to(hy_w2, (nl, LANE, LANE))
    hy_b2p = _pad_to(hy_b2, (nl, LANE))[:, None, :]
    hy_w3p = _pad_to(hy_w3, (nl, LANE, 2 * HY_W))
    hy_frp = _pad_to(hy_freq, (nl, LANE))[:, None, :]
    deltas = jnp.abs(jnp.linspace(HY_MIN_DECAY, HY_MAX_DECAY, HY_W, dtype=F32))[None, :]
    dt_bias_row = _pad_to(mb_dt_bias.reshape(nl, 2 * MB_HEADS), (nl, LANE))[:, None, :]
    a_log_rows = _pad_to(mb_a_log, (nl, 2, LANE))[:, :, None, :]
    d_skip_e = jnp.repeat(mb_d, MB_HEADDIM, axis=-1)[:, None, :]
    mb_norm3 = mb_norm[:, None, :]
    expand = jnp.asarray(np.kron(np.eye(LANE, MB_HEADS), np.ones((1, MB_HEADDIM))), dtype=MXU_DT)

    cos_m, sin_m = _rope_tables(n_lat, n_ctx, MLA_ROPE)
    cos_g, sin_g = _rope_tables(n_lat, n_ctx, GQA_HD)
    feats_lat, aux_lat = _filter_inputs(n_lat)
    feats_ctx, aux_ctx = _filter_inputs(n_ctx)
    dft_lat = _dft_consts(n_lat)
    tb_ctx = _ctx_tables(n_ctx)

    c8 = jnp.concatenate([c, c_ctx[None, :], jnp.zeros((8 - bsz - 1, d), F32)], axis=0)
    xz = jnp.concatenate([x, ctx], axis=1)

    for l in range(nl):
        last = l == nl - 1
        mod = ada_mod(c8, w_ada, b_ada3, l).reshape(8, 1, 6 * d)
        p, p_dt, h = in_projection(xz, mod, w_in_p, l, n_lat)

        q, k, v = mla_prep(p, gq_mla, gkv_mla, w_uq_p, w_ukv_p, cos_m, sin_m, l)
        oa = attention(q, k, v, n_lat, not last)
        q, k, v = gqa_prep(p, gq_gqa, gk_gqa, cos_g, sin_g, l)
        ob = attention(q, k, v, n_lat, not last)
        x0, vxt, vx_ctx = hyena_conv(p, hy_conv_w, hy_conv_b3, l, n_lat)
        k2t = hyena_filter(feats_lat, aux_lat, hy_w1p, hy_b1p, hy_w2p, hy_b2p, hy_w3p, hy_frp, deltas, l, True)
        h_spec = filter_spectrum(k2t, dft_lat)
        if last:
            y_ctx = jnp.zeros((bsz, n_ctx, HY_W), F32)
        else:
            k2c = hyena_filter(feats_ctx, aux_ctx, hy_w1p, hy_b1p, hy_w2p, hy_b2p, hy_w3p, hy_frp, deltas, l,
                               False)
            y_ctx = long_conv_ctx(vx_ctx, k2c, tb_ctx)
        oc = hyena_long_conv(vxt, h_spec, x0, vx_ctx, y_ctx, hy_skip_col, hy_skip3, dft_lat, l, n_lat)
        xbc = mamba_conv(p, mb_conv_w, mb_conv_b3, l, n_lat)
        dt2 = mamba_dt(p_dt, dt_bias_row, l)
        y_f, y_b = ssd_scan(xbc, dt2, a_log_rows, expand, l, n_lat)
        od = mamba_finish(y_f, y_b, xbc, p, d_skip_e, mb_norm3, l)

        out_rows = n_lat if last else n_lat + n_ctx
        acc = merge_branches(h, (oa, ob, oc, od), w_gate_b, b_gate4, w_branch_b, l, out_rows)
        x1 = matmul_res_ln(acc, w_out_b, xz, mod, 2, ln1_g3, ln1_b3, l, n_lat, out_rows,
                           (544, 512, 384, 128), d)
        act = ffn_in(x1, mod, w_ffn_in_b, l, n_lat)
        xz = matmul_res_ln(act, w_ffn_out_b, x1, mod, 5, ln2_g3, ln2_b3, l, n_lat, out_rows,
                           (256, 128), act.shape[2])
    return xz
```

```python
import functools
import math

import jax
import jax.numpy as jnp
import numpy as np
from jax import lax
from jax.experimental import pallas as pl
from jax.experimental.pallas import tpu as pltpu

F32 = jnp.float32
MXU_DT = jnp.bfloat16

D_MODEL = 2048
DEPTH = 2
GRID_W = 64
N_BRANCH = 4
BRANCH_W = D_MODEL // N_BRANCH
ROPE_THETA = 10000.0
LN_EPS = 1e-6
RMS_EPS = 1e-6
DEEPNORM_ALPHA = (2 * DEPTH) ** 0.25

MLA_HEADS = 4
MLA_Q_RANK = 448
MLA_KV_RANK = 128
MLA_NOPE = 128
MLA_ROPE = 64
MLA_V = 128
MLA_COLS = MLA_Q_RANK + MLA_KV_RANK + MLA_ROPE

GQA_HEADS = 4
GQA_KV_HEADS = 2
GQA_HD = 128
GQA_COLS = (GQA_HEADS + 2 * GQA_KV_HEADS) * GQA_HD

HY_W = BRANCH_W
HY_EMB = 33
HY_BANDS = (HY_EMB - 1) // 2
HY_FFN = 64
HY_MIN_DECAY = math.log(1e-2) / 1.5
HY_MAX_DECAY = math.log(1e-2) / 0.3
HY_COLS = 3 * HY_W

MB_INNER = BRANCH_W
MB_HEADDIM = 64
MB_HEADS = 8
MB_GROUPS = 2
MB_STATE = 128
MB_CHUNK = 128
MB_CONV_CH = MB_INNER + 2 * MB_GROUPS * MB_STATE
MB_COLS = MB_INNER + MB_CONV_CH + 2 * MB_HEADS

FFN_HIDDEN = 5632

LANE = 128
VMEM_LIMIT = 56 * 1024 * 1024

P_MLA = 0
P_MLA_W = 768
P_DT = 896
P_GQA = 1024
P_Z = 2048
P_HY = 2560
P_XBC = 4096
P_COLS = 5120

LOG2_E = math.log2(math.e)
_HI = lax.Precision.HIGHEST


def _dot(a, b):
    return jnp.dot(a, b, preferred_element_type=F32)


def _dot_hi(a, b):
    return jnp.dot(a, b, precision=_HI, preferred_element_type=F32)


def _dot_nt(a, b):
    return lax.dot_general(a, b, (((1,), (1,)), ((), ())), preferred_element_type=F32)


def _sigmoid(x):
    return 1.0 / (1.0 + jnp.exp(-x))


def _silu(x):
    return x * _sigmoid(x)


def _params(*sem):
    return pltpu.CompilerParams(dimension_semantics=sem, vmem_limit_bytes=VMEM_LIMIT)


def _pick(n, prefs):
    for p in prefs:
        if n % p == 0:
            return p
    raise ValueError(f"no tile for {n} in {prefs}")


def _standardize(x):
    mu = jnp.mean(x, axis=-1, keepdims=True)
    xc = x - mu
    var = jnp.mean(xc * xc, axis=-1, keepdims=True)
    return xc * lax.rsqrt(var + LN_EPS)


def _ada_kernel(c_ref, w_ref, b_ref, o_ref):
    cs = _silu(c_ref[...])
    o_ref[...] = _dot(cs.astype(MXU_DT), w_ref[...].astype(MXU_DT)) + b_ref[...]


def ada_mod(c8, w_ada, b_ada3, layer):
    d = c8.shape[1]
    n = w_ada.shape[2]
    tn = 1024
    return pl.pallas_call(
        _ada_kernel,
        grid=(n // tn,),
        in_specs=[pl.BlockSpec((8, d), lambda j: (0, 0)),
                  pl.BlockSpec((None, d, tn), lambda j: (layer, 0, j)),
                  pl.BlockSpec((None, 1, tn), lambda j: (layer, 0, j))],
        out_specs=pl.BlockSpec((8, tn), lambda j: (0, j)),
        out_shape=jax.ShapeDtypeStruct((8, n), F32),
        compiler_params=_params("arbitrary"),
        name="ada_mod",
    )(c8, w_ada, b_ada3)


ROW_CHUNKS = 4
ROW_TILES = (1088, 1024, 544, 384, 128)


def _modulated(x, sh_ref, sc_ref, shc_ref, scc_ref, row0, n_lat):
    xn = _standardize(x)
    row = row0 + lax.broadcasted_iota(jnp.int32, (x.shape[0], 1), 0)
    is_ctx = row >= n_lat
    scale = jnp.where(is_ctx, scc_ref[...], sc_ref[...])
    shift = jnp.where(is_ctx, shc_ref[...], sh_ref[...])
    return (xn * (1.0 + scale) + shift).astype(MXU_DT)


def _inproj_kernel(x_ref, sh_ref, sc_ref, shc_ref, scc_ref, w_ref, p_ref, pdt_ref, h_ref, *, tm, n_lat):
    j = pl.program_id(2)
    rc = tm // ROW_CHUNKS

    @pl.when(j == 0)
    def _():
        for r in range(ROW_CHUNKS):
            rows = slice(r * rc, (r + 1) * rc)
            hb = _modulated(x_ref[rows, :], sh_ref, sc_ref, shc_ref, scc_ref,
                            pl.program_id(1) * tm + r * rc, n_lat)
            h_ref[rows, :] = hb
            acc = _dot(hb, w_ref[...])
            p_ref[rows, :] = acc.astype(p_ref.dtype)
            pdt_ref[rows, :] = acc[:, P_DT:P_DT + LANE]

    @pl.when(j > 0)
    def _():
        p_ref[...] = _dot(h_ref[...], w_ref[...]).astype(p_ref.dtype)


def _ffn_in_kernel(x_ref, sh_ref, sc_ref, shc_ref, scc_ref, wu_ref, wg_ref, a_ref, h_ref, *, tm, n_lat):
    j = pl.program_id(2)
    rc = tm // ROW_CHUNKS

    def swiglu(h):
        up = _dot(h, wu_ref[...])
        gate = _dot(h, wg_ref[...])
        return (_silu(gate) * up).astype(a_ref.dtype)

    @pl.when(j == 0)
    def _():
        for r in range(ROW_CHUNKS):
            rows = slice(r * rc, (r + 1) * rc)
            hb = _modulated(x_ref[rows, :], sh_ref, sc_ref, shc_ref, scc_ref,
                            pl.program_id(1) * tm + r * rc, n_lat)
            h_ref[rows, :] = hb
            a_ref[rows, :] = swiglu(hb)

    @pl.when(j > 0)
    def _():
        half = tm // 2
        for r in range(2):
            rows = slice(r * half, (r + 1) * half)
            a_ref[rows, :] = swiglu(h_ref[rows, :])


def _mod_specs(k_shift, k_scale, n_batch):
    d = D_MODEL
    return [pl.BlockSpec((None, 1, d), lambda b, i, j: (b, 0, k_shift)),
            pl.BlockSpec((None, 1, d), lambda b, i, j: (b, 0, k_scale)),
            pl.BlockSpec((None, 1, d), lambda b, i, j: (n_batch, 0, k_shift)),
            pl.BlockSpec((None, 1, d), lambda b, i, j: (n_batch, 0, k_scale))]


def in_projection(xz, mod, w_in_p, layer, n_lat):
    bsz, l_all, d = xz.shape
    n = w_in_p.shape[2]
    tm = _pick(l_all, (1088, 544, 384, 128))
    tn = 1280
    assert n % tn == 0 and P_DT + LANE <= tn
    kern = functools.partial(_inproj_kernel, tm=tm, n_lat=n_lat)
    return pl.pallas_call(
        kern,
        grid=(bsz, l_all // tm, n // tn),
        in_specs=[pl.BlockSpec((None, tm, d), lambda b, i, j: (b, i, 0))]
        + _mod_specs(0, 1, bsz)
        + [pl.BlockSpec((None, d, tn), lambda b, i, j: (layer, 0, j))],
        out_specs=[pl.BlockSpec((None, tm, tn), lambda b, i, j: (b, i, j)),
                   pl.BlockSpec((None, tm, LANE), lambda b, i, j: (b, i, 0)),
                   pl.BlockSpec((None, tm, d), lambda b, i, j: (b, i, 0))],
        out_shape=[jax.ShapeDtypeStruct((bsz, l_all, n), MXU_DT),
                   jax.ShapeDtypeStruct((bsz, l_all, LANE), F32),
                   jax.ShapeDtypeStruct((bsz, l_all, d), MXU_DT)],
        compiler_params=_params("arbitrary", "arbitrary", "arbitrary"),
        name="in_projection",
    )(xz, mod, mod, mod, mod, w_in_p)


def ffn_in(xz, mod, w_ffn_in, layer, n_lat):
    bsz, l_all, d = xz.shape
    hid = w_ffn_in.shape[2] // 2
    tm = _pick(l_all, ROW_TILES)
    tn = 512
    nj = hid // tn
    kern = functools.partial(_ffn_in_kernel, tm=tm, n_lat=n_lat)
    return pl.pallas_call(
        kern,
        grid=(bsz, l_all // tm, nj),
        in_specs=[pl.BlockSpec((None, tm, d), lambda b, i, j: (b, i, 0))]
        + _mod_specs(3, 4, bsz)
        + [pl.BlockSpec((None, d, tn), lambda b, i, j: (layer, 0, j)),
           pl.BlockSpec((None, d, tn), lambda b, i, j: (layer, 0, j + nj))],
        out_specs=pl.BlockSpec((None, tm, tn), lambda b, i, j: (b, i, j)),
        out_shape=jax.ShapeDtypeStruct((bsz, l_all, hid), MXU_DT),
        scratch_shapes=[pltpu.VMEM((tm, d), MXU_DT)],
        compiler_params=_params("arbitrary", "arbitrary", "arbitrary"),
        name="ffn_in",
    )(xz, mod, mod, mod, mod, w_ffn_in, w_ffn_in)


def _resln_kernel(a_ref, w_ref, res_ref, g_ref, gc_ref, lng_ref, lnb_ref, o_ref, *, tm, n_lat, nk, rc):
    k = pl.program_id(2)
    i = pl.program_id(1)

    if nk > 1:
        @pl.when(k == 0)
        def _():
            o_ref[...] = _dot(a_ref[...], w_ref[...])

        @pl.when((k > 0) & (k < nk - 1))
        def _():
            o_ref[...] += _dot(a_ref[...], w_ref[...])

    @pl.when(k == nk - 1)
    def _():
        for r in range(tm // rc):
            rows = slice(r * rc, (r + 1) * rc)
            acc = _dot(a_ref[rows, :], w_ref[...])
            if nk > 1:
                acc = acc + o_ref[rows, :]
            row = i * tm + r * rc + lax.broadcasted_iota(jnp.int32, (rc, 1), 0)
            gate = jnp.where(row >= n_lat, gc_ref[...], g_ref[...])
            y = DEEPNORM_ALPHA * res_ref[rows, :] + gate * acc
            o_ref[rows, :] = _standardize(y) * lng_ref[...] + lnb_ref[...]


def matmul_res_ln(a, w, res, mod, k_gate, ln_g, ln_b, layer, n_lat, out_rows, tm_prefs, tk):
    bsz, l_all, kdim = a.shape
    d = w.shape[2]
    tm = _pick(out_rows, tm_prefs)
    nk = kdim // tk
    rc = _pick(tm, (272, 256, 192, 128))
    kern = functools.partial(_resln_kernel, tm=tm, n_lat=n_lat, nk=nk, rc=rc)
    w_mode = dict(pipeline_mode=pl.Buffered(1)) if nk == 1 else {}
    return pl.pallas_call(
        kern,
        grid=(bsz, pl.cdiv(out_rows, tm), nk),
        in_specs=[pl.BlockSpec((None, tm, tk), lambda b, i, k: (b, i, k)),
                  pl.BlockSpec((None, tk, d), lambda b, i, k: (layer, k, 0), **w_mode),
                  pl.BlockSpec((None, tm, d), lambda b, i, k: (b, i, 0)),
                  pl.BlockSpec((None, 1, d), lambda b, i, k: (b, 0, k_gate)),
                  pl.BlockSpec((None, 1, d), lambda b, i, k: (bsz, 0, k_gate)),
                  pl.BlockSpec((None, 1, d), lambda b, i, k: (layer, 0, 0)),
                  pl.BlockSpec((None, 1, d), lambda b, i, k: (layer, 0, 0))],
        out_specs=pl.BlockSpec((None, tm, d), lambda b, i, k: (b, i, 0)),
        out_shape=jax.ShapeDtypeStruct((bsz, out_rows, d), F32),
        compiler_params=_params("arbitrary", "arbitrary", "arbitrary"),
        name="matmul_res_ln",
    )(a, w, res, mod, mod, ln_g, ln_b)


def _merge_kernel(h_ref, oa_ref, ob_ref, oc_ref, od_ref, wg_ref, bg_ref, wb_ref, o_ref):
    h = h_ref[...]
    acc = None
    pending = None
    for i, o_r in list(enumerate((oa_ref, ob_ref, oc_ref, od_ref))) + [(None, None)]:
        nxt = None
        if o_r is not None:
            nxt = (_dot(h, wg_ref[i]) + bg_ref[i], _dot(o_r[...], wb_ref[i]))
        if pending is not None:
            g, t = pending
            term = _sigmoid(g) * t
            acc = term if acc is None else acc + term
        pending = nxt
    o_ref[...] = acc.astype(o_ref.dtype)


def merge_branches(h, outs, w_gate, b_gate4, w_branch, layer, n_rows):
    bsz, _, d = h.shape
    bw = outs[0].shape[2]
    tm = _pick(n_rows, ROW_TILES)
    tn = 512
    o_spec = pl.BlockSpec((None, tm, bw), lambda b, i, j: (b, i, 0))
    return pl.pallas_call(
        _merge_kernel,
        grid=(bsz, n_rows // tm, d // tn),
        in_specs=[pl.BlockSpec((None, tm, d), lambda b, i, j: (b, i, 0)), o_spec, o_spec, o_spec, o_spec,
                  pl.BlockSpec((None, N_BRANCH, d, tn), lambda b, i, j: (layer, 0, 0, j)),
                  pl.BlockSpec((None, N_BRANCH, 1, tn), lambda b, i, j: (layer, 0, 0, j)),
                  pl.BlockSpec((None, N_BRANCH, bw, tn), lambda b, i, j: (layer, 0, 0, j))],
        out_specs=pl.BlockSpec((None, tm, tn), lambda b, i, j: (b, i, j)),
        out_shape=jax.ShapeDtypeStruct((bsz, n_rows, d), MXU_DT),
        compiler_params=_params("arbitrary", "arbitrary", "arbitrary"),
        name="merge_branches",
    )(h, *outs, w_gate, b_gate4, w_branch)


def _rope(x, cos, sin):
    lane = lax.broadcasted_iota(jnp.int32, x.shape, 1)
    partner = jnp.where(lane % 2 == 0, pltpu.roll(x, LANE - 1, 1), pltpu.roll(x, 1, 1))
    return x * cos + partner * sin


def _mla_prep_kernel(p_ref, gq_ref, gkv_ref, wuq_ref, wukv_ref, cos_ref, sin_ref, q_ref, k_ref, v_ref,
                     vrow_ref, *, rc):
    scale = (MLA_NOPE + MLA_ROPE) ** -0.5 * LOG2_E

    def body(t, carry):
        rows = pl.ds(pl.multiple_of(t * rc, 16), rc)
        p = p_ref[rows, :].astype(F32)
        cq = p[:, 0:512]
        ckv = p[:, 512:640]
        k_rot = p[:, 640:768]
        rq = lax.rsqrt(jnp.sum(cq * cq, axis=-1, keepdims=True) * (1.0 / MLA_Q_RANK) + RMS_EPS)
        cqn = (cq * rq * gq_ref[...]).astype(MXU_DT)
        rkv = lax.rsqrt(jnp.mean(ckv * ckv, axis=-1, keepdims=True) + RMS_EPS)
        ckvn = (ckv * rkv * gkv_ref[...]).astype(MXU_DT)
        qf = _dot(cqn, wuq_ref[...])
        kvf = _dot(ckvn, wukv_ref[...])
        cos = cos_ref[rows, :]
        sin = sin_ref[rows, :]
        k_rope = _rope(k_rot, cos, sin).astype(k_ref.dtype)
        for hd in range(MLA_HEADS):
            qb = hd * 256
            q_ref[hd, rows, 0:128] = (qf[:, qb:qb + 128] * scale).astype(q_ref.dtype)
            q_rope = _rope(qf[:, qb + 128:qb + 256], cos, sin)
            q_ref[hd, rows, 128:256] = (q_rope * scale).astype(q_ref.dtype)
            k_ref[hd, rows, 0:128] = kvf[:, hd * 256:hd * 256 + 128].astype(k_ref.dtype)
            k_ref[hd, rows, 128:256] = k_rope
            vrow_ref[hd, rows, :] = kvf[:, hd * 256 + 128:hd * 256 + 256]
        return carry

    lax.fori_loop(0, p_ref.shape[0] // rc, body, 0)
    for hd in range(MLA_HEADS):
        v_ref[hd] = vrow_ref[hd].T.astype(v_ref.dtype)


def mla_prep(p, gq, gkv, wuq, wukv, cos, sin, layer):
    bsz, l_all, _ = p.shape
    tr = _pick(l_all, (1088 * 2, 256, 128))
    rc = 272 if tr % 272 == 0 else tr
    h = MLA_HEADS
    return pl.pallas_call(
        functools.partial(_mla_prep_kernel, rc=rc),
        grid=(bsz, l_all // tr),
        in_specs=[pl.BlockSpec((None, tr, P_MLA_W), lambda b, i: (b, i, 0)),
                  pl.BlockSpec((None, 1, 512), lambda b, i: (layer, 0, 0)),
                  pl.BlockSpec((None, 1, 128), lambda b, i: (layer, 0, 0)),
                  pl.BlockSpec((None, 512, h * 256), lambda b, i: (layer, 0, 0)),
                  pl.BlockSpec((None, 128, h * 256), lambda b, i: (layer, 0, 0)),
                  pl.BlockSpec((tr, LANE), lambda b, i: (i, 0)),
                  pl.BlockSpec((tr, LANE), lambda b, i: (i, 0))],
        out_specs=[pl.BlockSpec((None, h, tr, 256), lambda b, i: (b, 0, i, 0)),
                   pl.BlockSpec((None, h, tr, 256), lambda b, i: (b, 0, i, 0)),
                   pl.BlockSpec((None, h, 128, tr), lambda b, i: (b, 0, 0, i))],
        out_shape=[jax.ShapeDtypeStruct((bsz, h, l_all, 256), MXU_DT),
                   jax.ShapeDtypeStruct((bsz, h, l_all, 256), MXU_DT),
                   jax.ShapeDtypeStruct((bsz, h, 128, l_all), MXU_DT)],
        scratch_shapes=[pltpu.VMEM((h, tr, 128), F32)],
        compiler_params=_params("arbitrary", "arbitrary"),
        name="mla_prep",
    )(p, gq, gkv, wuq, wukv, cos, sin)


def _gqa_prep_kernel(p_ref, gq_ref, gk_ref, cos_ref, sin_ref, q_ref, k_ref, v_ref):
    scale = GQA_HD ** -0.5 * LOG2_E
    cos = cos_ref[...]
    sin = sin_ref[...]

    def norm_rope(x, g):
        r = lax.rsqrt(jnp.mean(x * x, axis=-1, keepdims=True) + RMS_EPS)
        return _rope(x * r * g, cos, sin)

    for hd in range(GQA_HEADS):
        x = p_ref[:, hd * 128:(hd + 1) * 128].astype(F32)
        q_ref[hd] = (norm_rope(x, gq_ref[...]) * scale).astype(q_ref.dtype)
    for hd in range(GQA_KV_HEADS):
        x = p_ref[:, 512 + hd * 128:512 + (hd + 1) * 128].astype(F32)
        k_ref[hd] = norm_rope(x, gk_ref[...]).astype(k_ref.dtype)
        v = p_ref[:, 768 + hd * 128:768 + (hd + 1) * 128].astype(F32)
        v_ref[hd] = v.T.astype(v_ref.dtype)


def gqa_prep(p, gq, gk, cos, sin, layer):
    bsz, l_all, _ = p.shape
    tr = _pick(l_all, (1088 * 2, 256, 128))
    return pl.pallas_call(
        _gqa_prep_kernel,
        grid=(bsz, l_all // tr),
        in_specs=[pl.BlockSpec((None, tr, GQA_COLS), lambda b, i: (b, i, P_GQA // GQA_COLS)),
                  pl.BlockSpec((None, 1, 128), lambda b, i: (layer, 0, 0)),
                  pl.BlockSpec((None, 1, 128), lambda b, i: (layer, 0, 0)),
                  pl.BlockSpec((tr, LANE), lambda b, i: (i, 0)),
                  pl.BlockSpec((tr, LANE), lambda b, i: (i, 0))],
        out_specs=[pl.BlockSpec((None, GQA_HEADS, tr, 128), lambda b, i: (b, 0, i, 0)),
                   pl.BlockSpec((None, GQA_KV_HEADS, tr, 128), lambda b, i: (b, 0, i, 0)),
                   pl.BlockSpec((None, GQA_KV_HEADS, 128, tr), lambda b, i: (b, 0, 0, i))],
        out_shape=[jax.ShapeDtypeStruct((bsz, GQA_HEADS, l_all, 128), MXU_DT),
                   jax.ShapeDtypeStruct((bsz, GQA_KV_HEADS, l_all, 128), MXU_DT),
                   jax.ShapeDtypeStruct((bsz, GQA_KV_HEADS, 128, l_all), MXU_DT)],
        compiler_params=_params("arbitrary", "arbitrary"),
        name="gqa_prep",
    )(p, gq, gk, cos, sin)


def _attn_kernel(q_ref, k_ref, v_ref, o_ref, *, n_lat, n_lat_tiles, sub):
    i = pl.program_id(2)

    def attend_all(k, vt, n_rows):
        groups = [slice(r, r + sub) for r in range(0, n_rows, sub)]
        sts = [_dot_nt(k, q_ref[g, :]) for g in groups]
        es, ls = [], []
        for st in sts:
            m = jnp.max(st, axis=0, keepdims=True)
            e = jnp.exp2(st - m)
            ls.append(jnp.sum(e, axis=0, keepdims=True))
            es.append(e.astype(vt.dtype))
        for g, e, l in zip(groups, es, ls):
            ot = _dot(vt, e) / l
            o_ref[g, :] = ot.T.astype(o_ref.dtype)

    @pl.when(i < n_lat_tiles)
    def _():
        attend_all(k_ref[...], v_ref[...], q_ref.shape[0])

    @pl.when(i >= n_lat_tiles)
    def _():
        attend_all(k_ref[n_lat:, :], v_ref[:, n_lat:], k_ref.shape[0] - n_lat)


def attention(q, k, v, n_lat, with_ctx):
    bsz, h, l_all, dk = q.shape
    hkv = k.shape[1]
    grp = h // hkv
    dv = v.shape[2]
    tq = _pick(n_lat, (1024, 512, 256, 128))
    assert l_all - n_lat <= tq
    out_rows = l_all if with_ctx else n_lat
    kern = functools.partial(_attn_kernel, n_lat=n_lat, n_lat_tiles=n_lat // tq, sub=min(256, tq))
    return pl.pallas_call(
        kern,
        grid=(bsz, h, pl.cdiv(out_rows, tq)),
        in_specs=[pl.BlockSpec((None, None, tq, dk), lambda b, hh, i: (b, hh, i, 0)),
                  pl.BlockSpec((None, None, l_all, dk), lambda b, hh, i: (b, hh // grp, 0, 0)),
                  pl.BlockSpec((None, None, dv, l_all), lambda b, hh, i: (b, hh // grp, 0, 0))],
        out_specs=pl.BlockSpec((None, tq, dv), lambda b, hh, i: (b, i, hh)),
        out_shape=jax.ShapeDtypeStruct((bsz, out_rows, h * dv), MXU_DT),
        compiler_params=_params("arbitrary", "arbitrary", "arbitrary"),
        name="attention",
    )(q, k, v)


CONV_TC = 256


def _conv3(u, w, b, n_lat):
    u = u.astype(F32)
    n = u.shape[0]
    row = lax.broadcasted_iota(jnp.int32, (n, 1), 0)
    prev = jnp.where((row == 0) | (row == n_lat), 0.0, pltpu.roll(u, 1, 0))
    nxt = jnp.where((row == n_lat - 1) | (row == n - 1), 0.0, pltpu.roll(u, n - 1, 0))
    return w[0:1, :] * prev + w[1:2, :] * u + w[2:3, :] * nxt + b


def _hy_conv_kernel(p0_ref, p1_ref, pv_ref, w0_ref, w1_ref, wv_ref, b0_ref, b1_ref, bv_ref,
                    x0_ref, vxt_ref, vxc_ref, *, n_lat):
    x0_ref[...] = _conv3(p0_ref[...], w0_ref[...], b0_ref[...], n_lat)
    x1 = _conv3(p1_ref[...], w1_ref[...], b1_ref[...], n_lat)
    v = _conv3(pv_ref[...], wv_ref[...], bv_ref[...], n_lat)
    vx = v * x1
    vxt_ref[...] = vx[:n_lat, :].T
    vxc_ref[...] = vx[n_lat:, :]


def hyena_conv(p, conv_w, conv_b3, layer, n_lat):
    bsz, l_all, _ = p.shape
    tc = CONV_TC
    nb = HY_W // tc
    base = P_HY // tc

    def pspec(off):
        return pl.BlockSpec((None, l_all, tc), lambda b, c: (b, 0, base + off + c))

    def wspec(off):
        return pl.BlockSpec((None, 3, tc), lambda b, c: (layer, 0, off + c))

    def bspec(off):
        return pl.BlockSpec((None, 1, tc), lambda b, c: (layer, 0, off + c))

    o_spec = pl.BlockSpec((None, l_all, tc), lambda b, c: (b, 0, c))
    return pl.pallas_call(
        functools.partial(_hy_conv_kernel, n_lat=n_lat),
        grid=(bsz, nb),
        in_specs=[pspec(0), pspec(nb), pspec(2 * nb), wspec(0), wspec(nb), wspec(2 * nb),
                  bspec(0), bspec(nb), bspec(2 * nb)],
        out_specs=[o_spec,
                   pl.BlockSpec((None, tc, n_lat), lambda b, c: (b, c, 0)),
                   pl.BlockSpec((None, l_all - n_lat, tc), lambda b, c: (b, 0, c))],
        out_shape=[jax.ShapeDtypeStruct((bsz, l_all, HY_W), F32),
                   jax.ShapeDtypeStruct((bsz, HY_W, n_lat), F32),
                   jax.ShapeDtypeStruct((bsz, l_all - n_lat, HY_W), F32)],
        compiler_params=_params("arbitrary", "arbitrary"),
        name="hyena_conv",
    )(p, p, p, conv_w, conv_w, conv_w, conv_b3, conv_b3, conv_b3)


def _mb_conv_kernel(p_ref, w_ref, b_ref, o_ref, *, n_lat):
    o_ref[...] = _silu(_conv3(p_ref[...], w_ref[...], b_ref[...], n_lat))


def mamba_conv(p, conv_w, conv_b3, layer, n_lat):
    bsz, l_all, _ = p.shape
    tc = CONV_TC
    nb = MB_CONV_CH // tc
    base = P_XBC // tc
    return pl.pallas_call(
        functools.partial(_mb_conv_kernel, n_lat=n_lat),
        grid=(bsz, nb),
        in_specs=[pl.BlockSpec((None, l_all, tc), lambda b, c: (b, 0, base + c)),
                  pl.BlockSpec((None, 3, tc), lambda b, c: (layer, 0, c)),
                  pl.BlockSpec((None, 1, tc), lambda b, c: (layer, 0, c))],
        out_specs=pl.BlockSpec((None, l_all, tc), lambda b, c: (b, 0, c)),
        out_shape=jax.ShapeDtypeStruct((bsz, l_all, MB_CONV_CH), F32),
        compiler_params=_params("arbitrary", "arbitrary"),
        name="mamba_conv",
    )(p, conv_w, conv_b3)


def _softplus(x):
    return jnp.maximum(x, 0.0) + jnp.log(1.0 + jnp.exp(-jnp.abs(x)))


def _mb_dt_kernel(p_ref, bias_ref, o_ref):
    dt = _softplus(p_ref[...] + bias_ref[...])
    o_ref[0] = dt
    o_ref[1] = pltpu.roll(dt, LANE - MB_HEADS, 1)


def mamba_dt(p, dt_bias_row, layer):
    bsz, l_all, _ = p.shape
    return pl.pallas_call(
        _mb_dt_kernel,
        grid=(bsz,),
        in_specs=[pl.BlockSpec((None, l_all, LANE), lambda b: (b, 0, 0)),
                  pl.BlockSpec((None, 1, LANE), lambda b: (layer, 0, 0))],
        out_specs=pl.BlockSpec((2, None, l_all, LANE), lambda b: (0, b, 0, 0)),
        out_shape=jax.ShapeDtypeStruct((2, bsz, l_all, LANE), F32),
        compiler_params=_params("arbitrary"),
        name="mamba_dt",
    )(p, dt_bias_row)


def _hy_filter_kernel(f_ref, aux_ref, w1_ref, b1_ref, w2_ref, b2_ref, w3_ref, fr_ref, dl_ref, o_ref, *,
                      channel_major):
    fr = fr_ref[...]
    tr = f_ref.shape[0]
    n_ch = 4 if tr % 512 == 0 else 1
    rc = tr // n_ch
    chunks = [slice(r * rc, (r + 1) * rc) for r in range(n_ch)]
    hdn = [_dot_hi(f_ref[rows, :], w1_ref[...]) for rows in chunks]
    hdn = [jnp.sin(fr * (x + b1_ref[...])) for x in hdn]
    hdn = [_dot_hi(x, w2_ref[...]) for x in hdn]
    hdn = [jnp.sin(fr * (x + b2_ref[...])) for x in hdn]
    filts = [_dot_hi(x, w3_ref[...]) for x in hdn]
    for rows, filt in zip(chunks, filts):
        aux = aux_ref[rows, :]
        t = aux[:, 0:1]
        is_fwd = aux[:, 1:2] > 0.5
        valid = aux[:, 2:3]
        window = jnp.exp(-t * dl_ref[...]) * valid
        k2 = jnp.where(is_fwd, filt[:, :HY_W], filt[:, HY_W:]) * window
        if channel_major:
            o_ref[:, rows] = k2.T
        else:
            o_ref[rows, :] = k2


def hyena_filter(feats2, aux, w1p, b1p, w2p, b2p, w3p, frp, deltas, layer, channel_major):
    rows = feats2.shape[0]
    tr = _pick(rows, (512, 256))

    def lspec(shape):
        return pl.BlockSpec((None,) + shape, lambda i: (layer, 0, 0))

    if channel_major:
        out_spec = pl.BlockSpec((HY_W, tr), lambda i: (0, i))
        out_shape = jax.ShapeDtypeStruct((HY_W, rows), F32)
    else:
        out_spec = pl.BlockSpec((tr, HY_W), lambda i: (i, 0))
        out_shape = jax.ShapeDtypeStruct((rows, HY_W), F32)
    return pl.pallas_call(
        functools.partial(_hy_filter_kernel, channel_major=channel_major),
        grid=(rows // tr,),
        in_specs=[pl.BlockSpec((tr, LANE), lambda i: (i, 0)),
                  pl.BlockSpec((tr, LANE), lambda i: (i, 0)),
                  lspec((LANE, LANE)), lspec((1, LANE)), lspec((LANE, LANE)), lspec((1, LANE)),
                  lspec((LANE, 2 * HY_W)), lspec((1, LANE)),
                  pl.BlockSpec((1, HY_W), lambda i: (0, 0))],
        out_specs=out_spec,
        out_shape=out_shape,
        compiler_params=_params("arbitrary"),
        name="hyena_filter",
    )(feats2, aux, w1p, b1p, w2p, b2p, w3p, frp, deltas)


DFT_MINOR = 256


def _snap(c):
    for v in (0.0, 1.0, -1.0):
        if abs(c - v) < 1e-12:
            return v
    return float(c)


def _lincomb(terms):
    acc = None
    for cf, tile in terms:
        if cf == 0.0:
            continue
        v = tile()
        if acc is None:
            acc = v if cf == 1.0 else (-v if cf == -1.0 else cf * v)
        elif cf == 1.0:
            acc = acc + v
        elif cf == -1.0:
            acc = acc - v
        else:
            acc = acc + cf * v
    return acc


def _dft_consts(n_seq):
    n = 2 * n_seq
    n2 = DFT_MINOR
    n1 = n // n2
    nkj = n1 // 2 + 1
    ang1 = 2.0 * np.pi * np.outer(np.arange(nkj), np.arange(n1)) / n1
    cos1 = [[_snap(v) for v in r] for r in np.cos(ang1)]
    sin1 = [[_snap(v) for v in r] for r in np.sin(ang1)]
    ang_t = 2.0 * np.pi * np.outer(np.arange(nkj), np.arange(n2)) / n
    rows = ((nkj + 7) // 8) * 8
    tw_c = np.zeros((rows, n2))
    tw_s = np.zeros((rows, n2))
    tw_c[:nkj] = np.cos(ang_t)
    tw_s[:nkj] = np.sin(ang_t)
    ang2 = 2.0 * np.pi * np.outer(np.arange(n2), np.arange(n2)) / n2
    c2, s2 = np.cos(ang2), np.sin(ang2)
    m_fwd = np.block([[c2, -s2], [s2, c2]])
    m_inv = np.block([[c2, s2], [-s2, c2]])
    as32 = lambda a: jnp.asarray(a, dtype=F32)

    def split(m):
        m32 = as32(m)
        hi = m32.astype(MXU_DT)
        lo = (m32 - hi.astype(F32)).astype(MXU_DT)
        return jnp.stack([hi, lo])

    return dict(n=n, n1=n1, nkj=nkj, cos1=cos1, sin1=sin1, tw_c=as32(tw_c), tw_s=as32(tw_s),
                m_fwd=split(m_fwd), m_inv=split(m_inv))


def _dot_split(a, m_ref):
    a_hi = a.astype(MXU_DT)
    a_lo = (a - a_hi.astype(F32)).astype(MXU_DT)
    return _dot(a_hi, m_ref[0]) + _dot(a_lo, m_ref[0]) + _dot(a_hi, m_ref[1])


def _outer_fwd_block(src_ref, z_ref, twc_ref, tws_ref, cb, n_in, cst):
    n2 = DFT_MINOR
    nkj, cos1, sin1 = cst["nkj"], cst["cos1"], cst["sin1"]
    tile = lambda j: (lambda: src_ref[cb * 8:cb * 8 + 8, j * n2:(j + 1) * n2])
    for kj in range(nkj):
        re = _lincomb([(cos1[kj][j], tile(j)) for j in range(n_in)])
        im = _lincomb([(-sin1[kj][j], tile(j)) for j in range(n_in)])
        if kj > 0 and im is not None:
            c = twc_ref[kj:kj + 1, :]
            s = tws_ref[kj:kj + 1, :]
            re, im = re * c + im * s, im * c - re * s
        elif kj > 0:
            re, im = re * twc_ref[kj:kj + 1, :], -re * tws_ref[kj:kj + 1, :]
        r = (cb * nkj + kj) * 8
        z_ref[r:r + 8, 0:n2] = re
        z_ref[r:r + 8, n2:2 * n2] = jnp.zeros_like(re) if im is None else im


def _outer_inv_block(z_ref, yt_ref, twc_ref, tws_ref, cb, cst):
    n2 = DFT_MINOR
    nkj, n1, cos1, sin1 = cst["nkj"], cst["n1"], cst["cos1"], cst["sin1"]
    inv_n = 1.0 / cst["n"]
    for kj in range(1, nkj):
        r = (cb * nkj + kj) * 8
        re = z_ref[r:r + 8, 0:n2]
        im = z_ref[r:r + 8, n2:2 * n2]
        c = twc_ref[kj:kj + 1, :]
        s = tws_ref[kj:kj + 1, :]
        z_ref[r:r + 8, 0:n2] = re * c - im * s
        z_ref[r:r + 8, n2:2 * n2] = im * c + re * s
    row = lambda kj: (cb * nkj + kj) * 8
    g_re = lambda kj: (lambda: z_ref[row(kj):row(kj) + 8, 0:n2])
    g_im = lambda kj: (lambda: z_ref[row(kj):row(kj) + 8, n2:2 * n2])
    for j in range(n1 // 2):
        terms = []
        for kj in range(nkj):
            wgt = inv_n if kj in (0, nkj - 1) else 2.0 * inv_n
            terms.append((wgt * cos1[kj][j], g_re(kj)))
            terms.append((-wgt * sin1[kj][j], g_im(kj)))
        yt_ref[cb * 8:cb * 8 + 8, j * n2:(j + 1) * n2] = _lincomb(terms)


def _spectrum_kernel(k_ref, twc_ref, tws_ref, mf_ref, h_ref, z_ref, *, ct, cst, n_chunks):
    nkj = cst["nkj"]
    per = ct // 8 // n_chunks
    rc = per * nkj * 8
    for c in range(n_chunks + 1):
        if c < n_chunks:
            for cb in range(c * per, (c + 1) * per):
                _outer_fwd_block(k_ref, z_ref, twc_ref, tws_ref, cb, cst["n1"], cst)
        if c >= 1:
            rows = slice((c - 1) * rc, c * rc)
            h_ref[rows, :] = _dot_split(z_ref[rows, :], mf_ref)


def _long_conv_kernel(x_ref, h_ref, twc_ref, tws_ref, mf_ref, mi_ref, x0_ref, vxc_ref, yc_ref, skip_ref,
                      skip_row_ref, o_ref, z_ref, yt_ref, *, ct, cst, n_chunks, n_lat):
    n2 = DFT_MINOR
    nkj, n1 = cst["nkj"], cst["n1"]
    per = ct // 8 // n_chunks
    rc = per * nkj * 8
    for c in range(n_chunks + 2):
        if c < n_chunks:
            for cb in range(c * per, (c + 1) * per):
                _outer_fwd_block(x_ref, z_ref, twc_ref, tws_ref, cb, n1 // 2, cst)
        if 1 <= c <= n_chunks:
            rows = slice((c - 1) * rc, c * rc)
            x = _dot_split(z_ref[rows, :], mf_ref)
            xre, xim = x[:, :n2], x[:, n2:]
            hre, him = h_ref[rows, 0:n2], h_ref[rows, n2:2 * n2]
            y = jnp.concatenate([xre * hre - xim * him, xre * him + xim * hre], axis=1)
            z_ref[rows, :] = _dot_split(y, mi_ref)
        if c >= 2:
            for cb in range((c - 2) * per, (c - 1) * per):
                _outer_inv_block(z_ref, yt_ref, twc_ref, tws_ref, cb, cst)

    y = (yt_ref[...] + x_ref[...] * skip_ref[...]).T
    o_ref[0:n_lat, :] = (x0_ref[0:n_lat, :] * y).astype(o_ref.dtype)
    yc = yc_ref[...] + vxc_ref[...] * skip_row_ref[...]
    o_ref[n_lat:, :] = (x0_ref[n_lat:, :] * yc).astype(o_ref.dtype)


def filter_spectrum(k2t, cst):
    ch, n = k2t.shape
    ct = LANE
    nkj = cst["nkj"]
    n_chunks = _pick(nkj * ct // 8, (4, 5, 1))
    full = lambda a: pl.BlockSpec(a.shape, lambda c: (0,) * a.ndim)
    return pl.pallas_call(
        functools.partial(_spectrum_kernel, ct=ct, cst=cst, n_chunks=n_chunks),
        grid=(ch // ct,),
        in_specs=[pl.BlockSpec((ct, n), lambda c: (c, 0)),
                  full(cst["tw_c"]), full(cst["tw_s"]), full(cst["m_fwd"])],
        out_specs=pl.BlockSpec((None, nkj * ct, 2 * DFT_MINOR), lambda c: (c, 0, 0)),
        out_shape=jax.ShapeDtypeStruct((ch // ct, nkj * ct, 2 * DFT_MINOR), F32),
        scratch_shapes=[pltpu.VMEM((nkj * ct, 2 * DFT_MINOR), F32)],
        compiler_params=_params("arbitrary"),
        name="filter_spectrum",
    )(k2t, cst["tw_c"], cst["tw_s"], cst["m_fwd"])


def hyena_long_conv(vxt, h_spec, x0, vx_ctx, y_ctx, skip_col, skip_row, cst, layer, n_lat):
    bsz, ch, _ = vxt.shape
    l_all = x0.shape[1]
    n_ctx = l_all - n_lat
    ct = LANE
    nkj = cst["nkj"]
    n_chunks = _pick(nkj * ct // 8, (4, 5, 1))
    full = lambda a: pl.BlockSpec(a.shape, lambda b, c: (0,) * a.ndim)
    return pl.pallas_call(
        functools.partial(_long_conv_kernel, ct=ct, cst=cst, n_chunks=n_chunks, n_lat=n_lat),
        grid=(bsz, ch // ct),
        in_specs=[pl.BlockSpec((None, ct, n_lat), lambda b, c: (b, c, 0)),
                  pl.BlockSpec((None, nkj * ct, 2 * DFT_MINOR), lambda b, c: (c, 0, 0)),
                  full(cst["tw_c"]), full(cst["tw_s"]), full(cst["m_fwd"]), full(cst["m_inv"]),
                  pl.BlockSpec((None, l_all, ct), lambda b, c: (b, 0, c)),
                  pl.BlockSpec((None, n_ctx, ct), lambda b, c: (b, 0, c)),
                  pl.BlockSpec((None, n_ctx, ct), lambda b, c: (b, 0, c)),
                  pl.BlockSpec((None, ct, 1), lambda b, c: (layer, c, 0)),
                  pl.BlockSpec((None, 1, ct), lambda b, c: (layer, 0, c))],
        out_specs=pl.BlockSpec((None, l_all, ct), lambda b, c: (b, 0, c)),
        out_shape=jax.ShapeDtypeStruct((bsz, l_all, ch), MXU_DT),
        scratch_shapes=[pltpu.VMEM((nkj * ct, 2 * DFT_MINOR), F32), pltpu.VMEM((ct, n_lat), F32)],
        compiler_params=_params("arbitrary", "arbitrary"),
        name="hyena_long_conv",
    )(vxt, h_spec, cst["tw_c"], cst["tw_s"], cst["m_fwd"], cst["m_inv"], x0, vx_ctx, y_ctx, skip_col, skip_row)


def _ctx_conv_kernel(v_ref, k_ref, fd_ref, fk_ref, gi_ref, o_ref, *, kp):
    x = _dot_hi(fd_ref[...], v_ref[...])
    h = _dot_hi(fk_ref[...], k_ref[...])
    xre, xim = x[:kp], x[kp:]
    hre, him = h[:kp], h[kp:]
    y = jnp.concatenate([xre * hre - xim * him, xre * him + xim * hre], axis=0)
    o_ref[...] = _dot_hi(gi_ref[...], y)


def _ctx_tables(n_ctx):
    n = 2 * n_ctx
    kh = n_ctx + 1
    kp = ((kh + 7) // 8) * 8
    k = np.arange(kp)[:, None]
    live = (k < kh).astype(np.float64)
    ang = 2.0 * np.pi * k * np.arange(n)[None, :] / n
    ck, sk = np.cos(ang) * live, np.sin(ang) * live
    fk = np.concatenate([ck, -sk], axis=0)
    fd = fk[:, :n_ctx]
    wk = np.where((k == 0) | (k == n_ctx), 1.0, 2.0) * live / n
    gi = np.concatenate([(ck * wk).T, (-sk * wk).T], axis=1)[:n_ctx]
    as32 = lambda a: jnp.asarray(a, dtype=F32)
    return dict(kp=kp, fd=as32(fd), fk=as32(fk), gi=as32(gi))


def long_conv_ctx(vx_ctx, k2c, tb):
    bsz, n_ctx, ch = vx_ctx.shape
    kp = tb["kp"]
    full = lambda a: pl.BlockSpec(a.shape, lambda b, c: (0,) * a.ndim)
    return pl.pallas_call(
        functools.partial(_ctx_conv_kernel, kp=kp),
        grid=(bsz, ch // LANE),
        in_specs=[pl.BlockSpec((None, n_ctx, LANE), lambda b, c: (b, 0, c)),
                  pl.BlockSpec((2 * n_ctx, LANE), lambda b, c: (0, c)),
                  full(tb["fd"]), full(tb["fk"]), full(tb["gi"])],
        out_specs=pl.BlockSpec((None, n_ctx, LANE), lambda b, c: (b, 0, c)),
        out_shape=jax.ShapeDtypeStruct((bsz, n_ctx, ch), F32),
        compiler_params=_params("arbitrary", "arbitrary"),
        name="ctx_long_conv",
    )(vx_ctx, k2c, tb["fd"], tb["fk"], tb["gi"])


def _split3(a):
    a1 = a.astype(MXU_DT)
    r = a - a1.astype(F32)
    a2 = r.astype(MXU_DT)
    a3 = (r - a2.astype(F32)).astype(MXU_DT)
    return a1, a2, a3


def _ssd_chunk(backward, xs_ref, b_ref, c_ref, dt_ref, alog, ex, y_ref, state_ref):
    n = MB_CHUNK
    li = lax.broadcasted_iota(jnp.int32, (n, n), 0)
    si = lax.broadcasted_iota(jnp.int32, (n, n), 1)
    mask = (li <= si) if backward else (li >= si)
    tri = mask.astype(MXU_DT)
    lane = lax.broadcasted_iota(jnp.int32, (n, LANE), 1)

    dt = dt_ref[...]
    da = dt * (-jnp.exp(alog))
    cum = sum(_dot(tri, t) for t in _split3(da))
    cum_t = cum.T
    both = jnp.concatenate([cum, dt], axis=0)
    both_e = sum(_dot(t, ex) for t in _split3(both))
    cum_e = both_e[:n]
    dt_e = both_e[n:]
    total_e = cum_e[0:1] if backward else cum_e[n - 1:n]

    xdt = xs_ref[...] * dt_e
    xdt_b = xdt.astype(MXU_DT)
    w_b = (jnp.exp(total_e - cum_e) * xdt).astype(MXU_DT)
    st = state_ref[...]
    y_parts = []
    s_parts = []
    gw = MB_STATE
    hw = (MB_HEADS // MB_GROUPS) * MB_HEADDIM
    for g in range(MB_GROUPS):
        cg = c_ref[:, g * gw:(g + 1) * gw].astype(MXU_DT)
        bg = b_ref[:, g * gw:(g + 1) * gw]
        cb = _dot_nt(cg, bg.astype(MXU_DT))
        y_off = _dot(cg, st[:, g * hw:(g + 1) * hw].astype(MXU_DT))
        s_parts.append(_dot(bg.T.astype(MXU_DT), w_b[:, g * hw:(g + 1) * hw]))
        diag = []
        for j in range(hw // LANE):
            lo = g * hw + j * LANE
            xp = xdt_b[:, lo:lo + LANE]
            pair = []
            for e in range(2):
                hd = (lo // MB_HEADDIM) + e
                seg = cum[:, hd:hd + 1] - cum_t[hd:hd + 1, :]
                decay = jnp.exp(jnp.where(mask, seg, -jnp.inf))
                pair.append(_dot((cb * decay).astype(MXU_DT), xp))
            diag.append(jnp.where(lane < MB_HEADDIM, pair[0], pair[1]))
        y_parts.append(jnp.concatenate(diag, axis=1) + y_off * jnp.exp(cum_e[:, g * hw:(g + 1) * hw]))
    y_ref[...] = jnp.concatenate(y_parts, axis=1).astype(y_ref.dtype)
    state_ref[...] = st * jnp.exp(total_e) + jnp.concatenate(s_parts, axis=1)


def _ssd_kernel(xf_ref, bf_ref, cf_ref, dtf_ref, xb_ref, bb_ref, cb_ref, dtb_ref, alog_ref, ex_ref,
                yf_ref, yb_ref, state_ref):
    @pl.when(pl.program_id(1) == 0)
    def _():
        state_ref[...] = jnp.zeros_like(state_ref)

    ex = ex_ref[...]
    for e in range(xf_ref.shape[0]):
        _ssd_chunk(False, xf_ref.at[e], bf_ref.at[e], cf_ref.at[e], dtf_ref.at[e], alog_ref[0], ex,
                   yf_ref.at[e], state_ref.at[2 * e])
        _ssd_chunk(True, xb_ref.at[e], bb_ref.at[e], cb_ref.at[e], dtb_ref.at[e], alog_ref[1], ex,
                   yb_ref.at[e], state_ref.at[2 * e + 1])


def ssd_scan(xbc, dt2, a_log_rows, expand, layer, n_lat):
    bsz, l_all, _ = xbc.shape
    nc = l_all // MB_CHUNK
    ncl = n_lat // MB_CHUNK
    fwd = lambda c: (c + ncl) % nc
    bwd = lambda c: nc - 1 - c

    nb = 2 if bsz % 2 == 0 else 1

    def specs(blk, direction):
        return [pl.BlockSpec((nb, MB_CHUNK, MB_INNER), lambda b, c: (b, blk(c), 0)),
                pl.BlockSpec((nb, MB_CHUNK, 256), lambda b, c: (b, blk(c), 2)),
                pl.BlockSpec((nb, MB_CHUNK, 256), lambda b, c: (b, blk(c), 3)),
                pl.BlockSpec((None, nb, MB_CHUNK, LANE), lambda b, c: (direction, b, blk(c), 0))]

    y_shape = jax.ShapeDtypeStruct((bsz, l_all, MB_INNER), MXU_DT)
    return pl.pallas_call(
        _ssd_kernel,
        grid=(bsz // nb, nc),
        in_specs=specs(fwd, 0) + specs(bwd, 1)
        + [pl.BlockSpec((None, 2, 1, LANE), lambda b, c: (layer, 0, 0, 0)),
           pl.BlockSpec((LANE, MB_INNER), lambda b, c: (0, 0))],
        out_specs=[pl.BlockSpec((nb, MB_CHUNK, MB_INNER), lambda b, c: (b, fwd(c), 0)),
                   pl.BlockSpec((nb, MB_CHUNK, MB_INNER), lambda b, c: (b, bwd(c), 0))],
        out_shape=[y_shape, y_shape],
        scratch_shapes=[pltpu.VMEM((2 * nb, MB_STATE, MB_INNER), F32)],
        compiler_params=_params("arbitrary", "arbitrary"),
        name="ssd_scan",
    )(xbc, xbc, xbc, dt2, xbc, xbc, xbc, dt2, a_log_rows, expand)


def _mb_finish_kernel(yf_ref, yb_ref, xs_ref, z_ref, dsk_ref, g_ref, o_ref):
    y = yf_ref[...].astype(F32) + yb_ref[...].astype(F32) + xs_ref[...] * dsk_ref[...]
    y = y * _silu(z_ref[...].astype(F32))
    r = lax.rsqrt(jnp.mean(y * y, axis=-1, keepdims=True) + RMS_EPS)
    o_ref[...] = (y * r * g_ref[...]).astype(o_ref.dtype)


def mamba_finish(y_f, y_b, xbc, p, d_skip_e, norm_g3, layer):
    bsz, l_all, _ = xbc.shape
    tr = _pick(l_all, (544, 384, 128))
    w = MB_INNER
    return pl.pallas_call(
        _mb_finish_kernel,
        grid=(bsz, l_all // tr),
        in_specs=[pl.BlockSpec((None, tr, w), lambda b, i: (b, i, 0)),
                  pl.BlockSpec((None, tr, w), lambda b, i: (b, i, 0)),
                  pl.BlockSpec((None, tr, w), lambda b, i: (b, i, 0)),
                  pl.BlockSpec((None, tr, w), lambda b, i: (b, i, P_Z // w)),
                  pl.BlockSpec((None, 1, w), lambda b, i: (layer, 0, 0)),
                  pl.BlockSpec((None, 1, w), lambda b, i: (layer, 0, 0))],
        out_specs=pl.BlockSpec((None, tr, w), lambda b, i: (b, i, 0)),
        out_shape=jax.ShapeDtypeStruct((bsz, l_all, w), MXU_DT),
        compiler_params=_params("arbitrary", "arbitrary"),
        name="mamba_finish",
    )(y_f, y_b, xbc, p, d_skip_e, norm_g3)


_W_IN_MOVES = (
    (0, 0, MLA_Q_RANK),
    (512, MLA_Q_RANK, MLA_KV_RANK),
    (640, MLA_Q_RANK + MLA_KV_RANK, MLA_ROPE),
    (P_DT, MLA_COLS + GQA_COLS + HY_COLS + MB_INNER + MB_CONV_CH, 2 * MB_HEADS),
    (P_GQA, MLA_COLS, GQA_COLS),
    (P_Z, MLA_COLS + GQA_COLS + HY_COLS, MB_INNER),
    (P_HY, MLA_COLS + GQA_COLS, HY_COLS),
    (P_XBC, MLA_COLS + GQA_COLS + HY_COLS + MB_INNER, MB_CONV_CH),
)


def _w_in_prep_kernel(w_ref, o_ref):
    o_ref[...] = jnp.zeros_like(o_ref)
    for dst, src, n in _W_IN_MOVES:
        o_ref[:, dst:dst + n] = w_ref[:, src:src + n].astype(o_ref.dtype)


def _prep_w_in(w_in):
    nl, d, cols = w_in.shape
    tr = 256
    return pl.pallas_call(
        _w_in_prep_kernel,
        grid=(nl, d // tr),
        in_specs=[pl.BlockSpec((None, tr, cols), lambda l, i: (l, i, 0))],
        out_specs=pl.BlockSpec((None, tr, P_COLS), lambda l, i: (l, i, 0)),
        out_shape=jax.ShapeDtypeStruct((nl, d, P_COLS), MXU_DT),
        compiler_params=_params("arbitrary", "arbitrary"),
        name="w_in_layout",
    )(w_in)


def _prep_w_uq(w_uq):
    nl = w_uq.shape[0]
    w = w_uq.reshape(nl, MLA_Q_RANK, MLA_HEADS, MLA_NOPE + MLA_ROPE)
    w = _pad_to(w, (nl, 512, MLA_HEADS, 256))
    return w.reshape(nl, 512, MLA_HEADS * 256).astype(MXU_DT)


def _rope_tables(n_lat, n_ctx, rot_dim):
    rows = n_lat // GRID_W
    row = jnp.repeat(jnp.arange(rows, dtype=F32), GRID_W)
    col = jnp.tile(jnp.arange(GRID_W, dtype=F32), rows)
    n_freq = rot_dim // 4
    inv_freq = ROPE_THETA ** (-jnp.arange(n_freq, dtype=F32) / n_freq)
    ang = jnp.concatenate([row[:, None] * inv_freq, col[:, None] * inv_freq], axis=-1)
    half = rot_dim // 2
    cos = jnp.concatenate([jnp.cos(ang), jnp.ones((n_ctx, half), F32)], axis=0)
    sin = jnp.concatenate([jnp.sin(ang), jnp.zeros((n_ctx, half), F32)], axis=0)
    cos_t = jnp.repeat(cos, 2, axis=-1)
    sin_t = jnp.stack([-sin, sin], axis=-1).reshape(n_lat + n_ctx, rot_dim)
    shape = (n_lat + n_ctx, LANE)
    return _pad_to(cos_t, shape), _pad_to(sin_t, shape)


def _filter_inputs(n):
    t = jnp.linspace(0.0, 1.0, n, dtype=F32)[:, None]
    omega = 2.0 * math.pi * jnp.arange(n, dtype=F32) / n
    bands = jnp.linspace(1e-4, HY_BANDS - 1, HY_BANDS, dtype=F32)
    ang = omega[:, None] * bands[None, :]
    feats = jnp.concatenate([t, jnp.cos(ang), -jnp.sin(ang)], axis=-1)
    zero = jnp.zeros((1, HY_EMB), F32)
    feats2 = jnp.concatenate([feats, zero, feats[1:][::-1]], axis=0)
    feats2 = jnp.concatenate([feats2, jnp.zeros((2 * n, LANE - HY_EMB), F32)], axis=-1)
    t2 = jnp.concatenate([t, jnp.zeros((1, 1), F32), t[1:][::-1]], axis=0)
    idx = jnp.arange(2 * n)[:, None]
    aux = jnp.concatenate([t2, (idx < n).astype(F32), (idx != n).astype(F32),
                           jnp.zeros((2 * n, LANE - 3), F32)], axis=-1)
    return feats2, aux


def _pad_to(a, shape):
    pads = [(0, s - d) for d, s in zip(a.shape, shape)]
    return jnp.pad(a, pads)


def kernel(x, c, ctx, c_ctx, w_ada, b_ada, w_in, mla_q_norm, mla_kv_norm, mla_w_uq, mla_w_ukv, gqa_q_norm, gqa_k_norm, hy_conv_w, hy_conv_b, hy_w1, hy_b1, hy_w2, hy_b2, hy_w3, hy_freq, hy_skip, mb_conv_w, mb_conv_b, mb_a_log, mb_dt_bias, mb_d, mb_norm, w_mgate, b_mgate, w_branch, w_out, ln1_g, ln1_b, w_ffn_in, w_ffn_out, ln2_g, ln2_b):
    bsz, n_lat, d = x.shape
    n_ctx = ctx.shape[1]
    nl = w_in.shape[0]
    assert d == D_MODEL and bsz < 8 and n_lat % n_ctx == 0 and n_ctx % MB_CHUNK == 0

    w_in_p = _prep_w_in(w_in)
    w_uq_p = _prep_w_uq(mla_w_uq)
    w_ukv_p = mla_w_ukv.astype(MXU_DT)
    gq_mla = _pad_to(mla_q_norm, (nl, 512))[:, None, :]
    gkv_mla = mla_kv_norm[:, None, :]
    gq_gqa = gqa_q_norm[:, None, :]
    gk_gqa = gqa_k_norm[:, None, :]
    w_gate_b = w_mgate.astype(MXU_DT)
    w_branch_b = w_branch.astype(MXU_DT)
    w_out_b = w_out.astype(MXU_DT)
    w_ffn_in_b = w_ffn_in.astype(MXU_DT)
    w_ffn_out_b = w_ffn_out.astype(MXU_DT)
    b_gate4 = b_mgate[:, :, None, :]
    b_ada3 = b_ada[:, None, :]
    hy_conv_b3 = hy_conv_b[:, None, :]
    mb_conv_b3 = mb_conv_b[:, None, :]
    hy_skip3 = hy_skip[:, None, :]
    hy_skip_col = hy_skip[:, :, None]
    ln1_g3, ln1_b3, ln2_g3, ln2_b3 = (a[:, None, :] for a in (ln1_g, ln1_b, ln2_g, ln2_b))
    hy_w1p = _pad_to(hy_w1, (nl, LANE, LANE))
    hy_b1p = _pad_to(hy_b1, (nl, LANE))[:, None, :]
    hy_w2p = _pad_to(hy_w2, (nl, LANE, LANE))
    hy_b2p = _pad_to(hy_b2, (nl, LANE))[:, None, :]
    hy_w3p = _pad_to(hy_w3, (nl, LANE, 2 * HY_W))
    hy_frp = _pad_to(hy_freq, (nl, LANE))[:, None, :]
    deltas = jnp.abs(jnp.linspace(HY_MIN_DECAY, HY_MAX_DECAY, HY_W, dtype=F32))[None, :]
    dt_bias_row = _pad_to(mb_dt_bias.reshape(nl, 2 * MB_HEADS), (nl, LANE))[:, None, :]
    a_log_rows = _pad_to(mb_a_log, (nl, 2, LANE))[:, :, None, :]
    d_skip_e = jnp.repeat(mb_d, MB_HEADDIM, axis=-1)[:, None, :]
    mb_norm3 = mb_norm[:, None, :]
    expand = jnp.asarray(np.kron(np.eye(LANE, MB_HEADS), np.ones((1, MB_HEADDIM))), dtype=MXU_DT)

    cos_m, sin_m = _rope_tables(n_lat, n_ctx, MLA_ROPE)
    cos_g, sin_g = _rope_tables(n_lat, n_ctx, GQA_HD)
    feats_lat, aux_lat = _filter_inputs(n_lat)
    feats_ctx, aux_ctx = _filter_inputs(n_ctx)
    dft_lat = _dft_consts(n_lat)
    tb_ctx = _ctx_tables(n_ctx)

    c8 = jnp.concatenate([c, c_ctx[None, :], jnp.zeros((8 - bsz - 1, d), F32)], axis=0)
    xz = jnp.concatenate([x, ctx], axis=1)

    for l in range(nl):
        last = l == nl - 1
        mod = ada_mod(c8, w_ada, b_ada3, l).reshape(8, 1, 6 * d)
        p, p_dt, h = in_projection(xz, mod, w_in_p, l, n_lat)

        q, k, v = mla_prep(p, gq_mla, gkv_mla, w_uq_p, w_ukv_p, cos_m, sin_m, l)
        oa = attention(q, k, v, n_lat, not last)
        q, k, v = gqa_prep(p, gq_gqa, gk_gqa, cos_g, sin_g, l)
        ob = attention(q, k, v, n_lat, not last)
        x0, vxt, vx_ctx = hyena_conv(p, hy_conv_w, hy_conv_b3, l, n_lat)
        k2t = hyena_filter(feats_lat, aux_lat, hy_w1p, hy_b1p, hy_w2p, hy_b2p, hy_w3p, hy_frp, deltas, l, True)
        h_spec = filter_spectrum(k2t, dft_lat)
        if last:
            y_ctx = jnp.zeros((bsz, n_ctx, HY_W), F32)
        else:
            k2c = hyena_filter(feats_ctx, aux_ctx, hy_w1p, hy_b1p, hy_w2p, hy_b2p, hy_w3p, hy_frp, deltas, l,
                               False)
            y_ctx = long_conv_ctx(vx_ctx, k2c, tb_ctx)
        oc = hyena_long_conv(vxt, h_spec, x0, vx_ctx, y_ctx, hy_skip_col, hy_skip3, dft_lat, l, n_lat)
        xbc = mamba_conv(p, mb_conv_w, mb_conv_b3, l, n_lat)
        dt2 = mamba_dt(p_dt, dt_bias_row, l)
        y_f, y_b = ssd_scan(xbc, dt2, a_log_rows, expand, l, n_lat)
        od = mamba_finish(y_f, y_b, xbc, p, d_skip_e, mb_norm3, l)

        out_rows = n_lat if last else n_lat + n_ctx
        acc = merge_branches(h, (oa, ob, oc, od), w_gate_b, b_gate4, w_branch_b, l, out_rows)
        x1 = matmul_res_ln(acc, w_out_b, xz, mod, 2, ln1_g3, ln1_b3, l, n_lat, out_rows,
                           (544, 512, 384, 128), d)
        act = ffn_in(x1, mod, w_ffn_in_b, l, n_lat)
        xz = matmul_res_ln(act, w_ffn_out_b, x1, mod, 5, ln2_g3, ln2_b3, l, n_lat, out_rows,
                           (256, 128), act.shape[2])
    return xz
```

```python
import functools
import math

import jax
import jax.numpy as jnp
import numpy as np
from jax import lax
from jax.experimental import pallas as pl
from jax.experimental.pallas import tpu as pltpu

F32 = jnp.float32
MXU_DT = jnp.bfloat16

D_MODEL = 2048
DEPTH = 2
GRID_W = 64
N_BRANCH = 4
BRANCH_W = D_MODEL // N_BRANCH
ROPE_THETA = 10000.0
LN_EPS = 1e-6
RMS_EPS = 1e-6
DEEPNORM_ALPHA = (2 * DEPTH) ** 0.25

MLA_HEADS = 4
MLA_Q_RANK = 448
MLA_KV_RANK = 128
MLA_NOPE = 128
MLA_ROPE = 64
MLA_V = 128
MLA_COLS = MLA_Q_RANK + MLA_KV_RANK + MLA_ROPE

GQA_HEADS = 4
GQA_KV_HEADS = 2
GQA_HD = 128
GQA_COLS = (GQA_HEADS + 2 * GQA_KV_HEADS) * GQA_HD

HY_W = BRANCH_W
HY_EMB = 33
HY_BANDS = (HY_EMB - 1) // 2
HY_FFN = 64
HY_MIN_DECAY = math.log(1e-2) / 1.5
HY_MAX_DECAY = math.log(1e-2) / 0.3
HY_COLS = 3 * HY_W

MB_INNER = BRANCH_W
MB_HEADDIM = 64
MB_HEADS = 8
MB_GROUPS = 2
MB_STATE = 128
MB_CHUNK = 128
MB_CONV_CH = MB_INNER + 2 * MB_GROUPS * MB_STATE
MB_COLS = MB_INNER + MB_CONV_CH + 2 * MB_HEADS

FFN_HIDDEN = 5632

LANE = 128
VMEM_LIMIT = 56 * 1024 * 1024

P_MLA = 0
P_MLA_W = 768
P_DT = 896
P_GQA = 1024
P_Z = 2048
P_HY = 2560
P_XBC = 4096
P_COLS = 5120

LOG2_E = math.log2(math.e)
_HI = lax.Precision.HIGHEST


def _dot(a, b):
    return jnp.dot(a, b, preferred_element_type=F32)


def _dot_hi(a, b):
    return jnp.dot(a, b, precision=_HI, preferred_element_type=F32)


def _dot_nt(a, b):
    return lax.dot_general(a, b, (((1,), (1,)), ((), ())), preferred_element_type=F32)


def _sigmoid(x):
    return 1.0 / (1.0 + jnp.exp(-x))


def _silu(x):
    return x * _sigmoid(x)


def _params(*sem):
    return pltpu.CompilerParams(dimension_semantics=sem, vmem_limit_bytes=VMEM_LIMIT)


def _pick(n, prefs):
    for p in prefs:
        if n % p == 0:
            return p
    raise ValueError(f"no tile for {n} in {prefs}")


def _standardize(x):
    mu = jnp.mean(x, axis=-1, keepdims=True)
    xc = x - mu
    var = jnp.mean(xc * xc, axis=-1, keepdims=True)
    return xc * lax.rsqrt(var + LN_EPS)


def _ada_kernel(c_ref, w_ref, b_ref, o_ref):
    cs = _silu(c_ref[...])
    o_ref[...] = _dot(cs.astype(MXU_DT), w_ref[...].astype(MXU_DT)) + b_ref[...]


def ada_mod(c8, w_ada, b_ada3, layer):
    d = c8.shape[1]
    n = w_ada.shape[2]
    tn = 1024
    return pl.pallas_call(
        _ada_kernel,
        grid=(n // tn,),
        in_specs=[pl.BlockSpec((8, d), lambda j: (0, 0)),
                  pl.BlockSpec((None, d, tn), lambda j: (layer, 0, j)),
                  pl.BlockSpec((None, 1, tn), lambda j: (layer, 0, j))],
        out_specs=pl.BlockSpec((8, tn), lambda j: (0, j)),
        out_shape=jax.ShapeDtypeStruct((8, n), F32),
        compiler_params=_params("arbitrary"),
        name="ada_mod",
    )(c8, w_ada, b_ada3)


ROW_CHUNKS = 4
ROW_TILES = (1088, 1024, 544, 384, 128)


def _modulated(x, sh_ref, sc_ref, shc_ref, scc_ref, row0, n_lat):
    xn = _standardize(x)
    row = row0 + lax.broadcasted_iota(jnp.int32, (x.shape[0], 1), 0)
    is_ctx = row >= n_lat
    scale = jnp.where(is_ctx, scc_ref[...], sc_ref[...])
    shift = jnp.where(is_ctx, shc_ref[...], sh_ref[...])
    return (xn * (1.0 + scale) + shift).astype(MXU_DT)


def _inproj_kernel(x_ref, sh_ref, sc_ref, shc_ref, scc_ref, w_ref, p_ref, pdt_ref, h_ref, *, tm, n_lat):
    j = pl.program_id(2)
    rc = tm // ROW_CHUNKS

    @pl.when(j == 0)
    def _():
        for r in range(ROW_CHUNKS):
            rows = slice(r * rc, (r + 1) * rc)
            hb = _modulated(x_ref[rows, :], sh_ref, sc_ref, shc_ref, scc_ref,
                            pl.program_id(1) * tm + r * rc, n_lat)
            h_ref[rows, :] = hb
            acc = _dot(hb, w_ref[...])
            p_ref[rows, :] = acc.astype(p_ref.dtype)
            pdt_ref[rows, :] = acc[:, P_DT:P_DT + LANE]

    @pl.when(j > 0)
    def _():
        p_ref[...] = _dot(h_ref[...], w_ref[...]).astype(p_ref.dtype)


def _ffn_in_kernel(x_ref, sh_ref, sc_ref, shc_ref, scc_ref, wu_ref, wg_ref, a_ref, h_ref, *, tm, n_lat):
    j = pl.program_id(2)
    rc = tm // ROW_CHUNKS

    def swiglu(h):
        up = _dot(h, wu_ref[...])
        gate = _dot(h, wg_ref[...])
        return (_silu(gate) * up).astype(a_ref.dtype)

    @pl.when(j == 0)
    def _():
        for r in range(ROW_CHUNKS):
            rows = slice(r * rc, (r + 1) * rc)
            hb = _modulated(x_ref[rows, :], sh_ref, sc_ref, shc_ref, scc_ref,
                            pl.program_id(1) * tm + r * rc, n_lat)
            h_ref[rows, :] = hb
            a_ref[rows, :] = swiglu(hb)

    @pl.when(j > 0)
    def _():
        half = tm // 2
        for r in range(2):
            rows = slice(r * half, (r + 1) * half)
            a_ref[rows, :] = swiglu(h_ref[rows, :])


def _mod_specs(k_shift, k_scale, n_batch):
    d = D_MODEL
    return [pl.BlockSpec((None, 1, d), lambda b, i, j: (b, 0, k_shift)),
            pl.BlockSpec((None, 1, d), lambda b, i, j: (b, 0, k_scale)),
            pl.BlockSpec((None, 1, d), lambda b, i, j: (n_batch, 0, k_shift)),
            pl.BlockSpec((None, 1, d), lambda b, i, j: (n_batch, 0, k_scale))]


def in_projection(xz, mod, w_in_p, layer, n_lat):
    bsz, l_all, d = xz.shape
    n = w_in_p.shape[2]
    tm = _pick(l_all, (1088, 544, 384, 128))
    tn = 1280
    assert n % tn == 0 and P_DT + LANE <= tn
    kern = functools.partial(_inproj_kernel, tm=tm, n_lat=n_lat)
    return pl.pallas_call(
        kern,
        grid=(bsz, l_all // tm, n // tn),
        in_specs=[pl.BlockSpec((None, tm, d), lambda b, i, j: (b, i, 0))]
        + _mod_specs(0, 1, bsz)
        + [pl.BlockSpec((None, d, tn), lambda b, i, j: (layer, 0, j))],
        out_specs=[pl.BlockSpec((None, tm, tn), lambda b, i, j: (b, i, j)),
                   pl.BlockSpec((None, tm, LANE), lambda b, i, j: (b, i, 0)),
                   pl.BlockSpec((None, tm, d), lambda b, i, j: (b, i, 0))],
        out_shape=[jax.ShapeDtypeStruct((bsz, l_all, n), MXU_DT),
                   jax.ShapeDtypeStruct((bsz, l_all, LANE), F32),
                   jax.ShapeDtypeStruct((bsz, l_all, d), MXU_DT)],
        compiler_params=_params("arbitrary", "arbitrary", "arbitrary"),
        name="in_projection",
    )(xz, mod, mod, mod, mod, w_in_p)


def ffn_in(xz, mod, w_ffn_in, layer, n_lat):
    bsz, l_all, d = xz.shape
    hid = w_ffn_in.shape[2] // 2
    tm = _pick(l_all, ROW_TILES)
    tn = 512
    nj = hid // tn
    kern = functools.partial(_ffn_in_kernel, tm=tm, n_lat=n_lat)
    return pl.pallas_call(
        kern,
        grid=(bsz, l_all // tm, nj),
        in_specs=[pl.BlockSpec((None, tm, d), lambda b, i, j: (b, i, 0))]
        + _mod_specs(3, 4, bsz)
        + [pl.BlockSpec((None, d, tn), lambda b, i, j: (layer, 0, j)),
           pl.BlockSpec((None, d, tn), lambda b, i, j: (layer, 0, j + nj))],
        out_specs=pl.BlockSpec((None, tm, tn), lambda b, i, j: (b, i, j)),
        out_shape=jax.ShapeDtypeStruct((bsz, l_all, hid), MXU_DT),
        scratch_shapes=[pltpu.VMEM((tm, d), MXU_DT)],
        compiler_params=_params("arbitrary", "arbitrary", "arbitrary"),
        name="ffn_in",
    )(xz, mod, mod, mod, mod, w_ffn_in, w_ffn_in)


def _resln_kernel(a_ref, w_ref, res_ref, g_ref, gc_ref, lng_ref, lnb_ref, o_ref, *, tm, n_lat, nk, rc):
    k = pl.program_id(2)
    i = pl.program_id(1)

    if nk > 1:
        @pl.when(k == 0)
        def _():
            o_ref[...] = _dot(a_ref[...], w_ref[...])

        @pl.when((k > 0) & (k < nk - 1))
        def _():
            o_ref[...] += _dot(a_ref[...], w_ref[...])

    @pl.when(k == nk - 1)
    def _():
        for r in range(tm // rc):
            rows = slice(r * rc, (r + 1) * rc)
            acc = _dot(a_ref[rows, :], w_ref[...])
            if nk > 1:
                acc = acc + o_ref[rows, :]
            row = i * tm + r * rc + lax.broadcasted_iota(jnp.int32, (rc, 1), 0)
            gate = jnp.where(row >= n_lat, gc_ref[...], g_ref[...])
            y = DEEPNORM_ALPHA * res_ref[rows, :] + gate * acc
            o_ref[rows, :] = _standardize(y) * lng_ref[...] + lnb_ref[...]


def matmul_res_ln(a, w, res, mod, k_gate, ln_g, ln_b, layer, n_lat, out_rows, tm_prefs, tk):
    bsz, l_all, kdim = a.shape
    d = w.shape[2]
    tm = _pick(out_rows, tm_prefs)
    nk = kdim // tk
    rc = _pick(tm, (272, 256, 192, 128))
    kern = functools.partial(_resln_kernel, tm=tm, n_lat=n_lat, nk=nk, rc=rc)
    w_mode = dict(pipeline_mode=pl.Buffered(1)) if nk == 1 else {}
    return pl.pallas_call(
        kern,
        grid=(bsz, pl.cdiv(out_rows, tm), nk),
        in_specs=[pl.BlockSpec((None, tm, tk), lambda b, i, k: (b, i, k)),
                  pl.BlockSpec((None, tk, d), lambda b, i, k: (layer, k, 0), **w_mode),
                  pl.BlockSpec((None, tm, d), lambda b, i, k: (b, i, 0)),
                  pl.BlockSpec((None, 1, d), lambda b, i, k: (b, 0, k_gate)),
                  pl.BlockSpec((None, 1, d), lambda b, i, k: (bsz, 0, k_gate)),
                  pl.BlockSpec((None, 1, d), lambda b, i, k: (layer, 0, 0)),
                  pl.BlockSpec((None, 1, d), lambda b, i, k: (layer, 0, 0))],
        out_specs=pl.BlockSpec((None, tm, d), lambda b, i, k: (b, i, 0)),
        out_shape=jax.ShapeDtypeStruct((bsz, out_rows, d), F32),
        compiler_params=_params("arbitrary", "arbitrary", "arbitrary"),
        name="matmul_res_ln",
    )(a, w, res, mod, mod, ln_g, ln_b)


def _merge_kernel(h_ref, oa_ref, ob_ref, oc_ref, od_ref, wg_ref, bg_ref, wb_ref, o_ref):
    h = h_ref[...]
    acc = None
    pending = None
    for i, o_r in list(enumerate((oa_ref, ob_ref, oc_ref, od_ref))) + [(None, None)]:
        nxt = None
        if o_r is not None:
            nxt = (_dot(h, wg_ref[i]) + bg_ref[i], _dot(o_r[...], wb_ref[i]))
        if pending is not None:
            g, t = pending
            term = _sigmoid(g) * t
            acc = term if acc is None else acc + term
        pending = nxt
    o_ref[...] = acc.astype(o_ref.dtype)


def merge_branches(h, outs, w_gate, b_gate4, w_branch, layer, n_rows):
    bsz, _, d = h.shape
    bw = outs[0].shape[2]
    tm = _pick(n_rows, ROW_TILES)
    tn = 512
    o_spec = pl.BlockSpec((None, tm, bw), lambda b, i, j: (b, i, 0))
    return pl.pallas_call(
        _merge_kernel,
        grid=(bsz, n_rows // tm, d // tn),
        in_specs=[pl.BlockSpec((None, tm, d), lambda b, i, j: (b, i, 0)), o_spec, o_spec, o_spec, o_spec,
                  pl.BlockSpec((None, N_BRANCH, d, tn), lambda b, i, j: (layer, 0, 0, j)),
                  pl.BlockSpec((None, N_BRANCH, 1, tn), lambda b, i, j: (layer, 0, 0, j)),
                  pl.BlockSpec((None, N_BRANCH, bw, tn), lambda b, i, j: (layer, 0, 0, j))],
        out_specs=pl.BlockSpec((None, tm, tn), lambda b, i, j: (b, i, j)),
        out_shape=jax.ShapeDtypeStruct((bsz, n_rows, d), MXU_DT),
        compiler_params=_params("arbitrary", "arbitrary", "arbitrary"),
        name="merge_branches",
    )(h, *outs, w_gate, b_gate4, w_branch)


def _rope(x, cos, sin):
    lane = lax.broadcasted_iota(jnp.int32, x.shape, 1)
    partner = jnp.where(lane % 2 == 0, pltpu.roll(x, LANE - 1, 1), pltpu.roll(x, 1, 1))
    return x * cos + partner * sin


def _mla_prep_kernel(p_ref, gq_ref, gkv_ref, wuq_ref, wukv_ref, cos_ref, sin_ref, q_ref, k_ref, v_ref,
                     vrow_ref, *, rc):
    scale = (MLA_NOPE + MLA_ROPE) ** -0.5 * LOG2_E

    def body(t, carry):
        rows = pl.ds(pl.multiple_of(t * rc, 16), rc)
        p = p_ref[rows, :].astype(F32)
        cq = p[:, 0:512]
        ckv = p[:, 512:640]
        k_rot = p[:, 640:768]
        rq = lax.rsqrt(jnp.sum(cq * cq, axis=-1, keepdims=True) * (1.0 / MLA_Q_RANK) + RMS_EPS)
        cqn = (cq * rq * gq_ref[...]).astype(MXU_DT)
        rkv = lax.rsqrt(jnp.mean(ckv * ckv, axis=-1, keepdims=True) + RMS_EPS)
        ckvn = (ckv * rkv * gkv_ref[...]).astype(MXU_DT)
        qf = _dot(cqn, wuq_ref[...])
        kvf = _dot(ckvn, wukv_ref[...])
        cos = cos_ref[rows, :]
        sin = sin_ref[rows, :]
        k_rope = _rope(k_rot, cos, sin).astype(k_ref.dtype)
        for hd in range(MLA_HEADS):
            qb = hd * 256
            q_ref[hd, rows, 0:128] = (qf[:, qb:qb + 128] * scale).astype(q_ref.dtype)
            q_rope = _rope(qf[:, qb + 128:qb + 256], cos, sin)
            q_ref[hd, rows, 128:256] = (q_rope * scale).astype(q_ref.dtype)
            k_ref[hd, rows, 0:128] = kvf[:, hd * 256:hd * 256 + 128].astype(k_ref.dtype)
            k_ref[hd, rows, 128:256] = k_rope
            vrow_ref[hd, rows, :] = kvf[:, hd * 256 + 128:hd * 256 + 256]
        return carry

    lax.fori_loop(0, p_ref.shape[0] // rc, body, 0)
    for hd in range(MLA_HEADS):
        v_ref[hd] = vrow_ref[hd].T.astype(v_ref.dtype)


def mla_prep(p, gq, gkv, wuq, wukv, cos, sin, layer):
    bsz, l_all, _ = p.shape
    tr = _pick(l_all, (1088 * 2, 256, 128))
    rc = 272 if tr % 272 == 0 else tr
    h = MLA_HEADS
    return pl.pallas_call(
        functools.partial(_mla_prep_kernel, rc=rc),
        grid=(bsz, l_all // tr),
        in_specs=[pl.BlockSpec((None, tr, P_MLA_W), lambda b, i: (b, i, 0)),
                  pl.BlockSpec((None, 1, 512), lambda b, i: (layer, 0, 0)),
                  pl.BlockSpec((None, 1, 128), lambda b, i: (layer, 0, 0)),
                  pl.BlockSpec((None, 512, h * 256), lambda b, i: (layer, 0, 0)),
                  pl.BlockSpec((None, 128, h * 256), lambda b, i: (layer, 0, 0)),
                  pl.BlockSpec((tr, LANE), lambda b, i: (i, 0)),
                  pl.BlockSpec((tr, LANE), lambda b, i: (i, 0))],
        out_specs=[pl.BlockSpec((None, h, tr, 256), lambda b, i: (b, 0, i, 0)),
                   pl.BlockSpec((None, h, tr, 256), lambda b, i: (b, 0, i, 0)),
                   pl.BlockSpec((None, h, 128, tr), lambda b, i: (b, 0, 0, i))],
        out_shape=[jax.ShapeDtypeStruct((bsz, h, l_all, 256), MXU_DT),
                   jax.ShapeDtypeStruct((bsz, h, l_all, 256), MXU_DT),
                   jax.ShapeDtypeStruct((bsz, h, 128, l_all), MXU_DT)],
        scratch_shapes=[pltpu.VMEM((h, tr, 128), F32)],
        compiler_params=_params("arbitrary", "arbitrary"),
        name="mla_prep",
    )(p, gq, gkv, wuq, wukv, cos, sin)


def _gqa_prep_kernel(p_ref, gq_ref, gk_ref, cos_ref, sin_ref, q_ref, k_ref, v_ref):
    scale = GQA_HD ** -0.5 * LOG2_E
    cos = cos_ref[...]
    sin = sin_ref[...]

    def norm_rope(x, g):
        r = lax.rsqrt(jnp.mean(x * x, axis=-1, keepdims=True) + RMS_EPS)
        return _rope(x * r * g, cos, sin)

    for hd in range(GQA_HEADS):
        x = p_ref[:, hd * 128:(hd + 1) * 128].astype(F32)
        q_ref[hd] = (norm_rope(x, gq_ref[...]) * scale).astype(q_ref.dtype)
    for hd in range(GQA_KV_HEADS):
        x = p_ref[:, 512 + hd * 128:512 + (hd + 1) * 128].astype(F32)
        k_ref[hd] = norm_rope(x, gk_ref[...]).astype(k_ref.dtype)
        v = p_ref[:, 768 + hd * 128:768 + (hd + 1) * 128].astype(F32)
        v_ref[hd] = v.T.astype(v_ref.dtype)


def gqa_prep(p, gq, gk, cos, sin, layer):
    bsz, l_all, _ = p.shape
    tr = _pick(l_all, (1088 * 2, 256, 128))
    return pl.pallas_call(
        _gqa_prep_kernel,
        grid=(bsz, l_all // tr),
        in_specs=[pl.BlockSpec((None, tr, GQA_COLS), lambda b, i: (b, i, P_GQA // GQA_COLS)),
                  pl.BlockSpec((None, 1, 128), lambda b, i: (layer, 0, 0)),
                  pl.BlockSpec((None, 1, 128), lambda b, i: (layer, 0, 0)),
                  pl.BlockSpec((tr, LANE), lambda b, i: (i, 0)),
                  pl.BlockSpec((tr, LANE), lambda b, i: (i, 0))],
        out_specs=[pl.BlockSpec((None, GQA_HEADS, tr, 128), lambda b, i: (b, 0, i, 0)),
                   pl.BlockSpec((None, GQA_KV_HEADS, tr, 128), lambda b, i: (b, 0, i, 0)),
                   pl.BlockSpec((None, GQA_KV_HEADS, 128, tr), lambda b, i: (b, 0, 0, i))],
        out_shape=[jax.ShapeDtypeStruct((bsz, GQA_HEADS, l_all, 128), MXU_DT),
                   jax.ShapeDtypeStruct((bsz, GQA_KV_HEADS, l_all, 128), MXU_DT),
                   jax.ShapeDtypeStruct((bsz, GQA_KV_HEADS, 128, l_all), MXU_DT)],
        compiler_params=_params("arbitrary", "arbitrary"),
        name="gqa_prep",
    )(p, gq, gk, cos, sin)


def _attn_kernel(q_ref, k_ref, v_ref, o_ref, *, n_lat, n_lat_tiles, sub):
    i = pl.program_id(2)

    def attend_all(k, vt, n_rows):
        groups = [slice(r, r + sub) for r in range(0, n_rows, sub)]
        sts = [_dot_nt(k, q_ref[g, :]) for g in groups]
        es, ls = [], []
        for st in sts:
            m = jnp.max(st, axis=0, keepdims=True)
            e = jnp.exp2(st - m)
            ls.append(jnp.sum(e, axis=0, keepdims=True))
            es.append(e.astype(vt.dtype))
        for g, e, l in zip(groups, es, ls):
            ot = _dot(vt, e) / l
            o_ref[g, :] = ot.T.astype(o_ref.dtype)

    @pl.when(i < n_lat_tiles)
    def _():
        attend_all(k_ref[...], v_ref[...], q_ref.shape[0])

    @pl.when(i >= n_lat_tiles)
    def _():
        attend_all(k_ref[n_lat:, :], v_ref[:, n_lat:], k_ref.shape[0] - n_lat)


def attention(q, k, v, n_lat, with_ctx):
    bsz, h, l_all, dk = q.shape
    hkv = k.shape[1]
    grp = h // hkv
    dv = v.shape[2]
    tq = _pick(n_lat, (1024, 512, 256, 128))
    assert l_all - n_lat <= tq
    out_rows = l_all if with_ctx else n_lat
    kern = functools.partial(_attn_kernel, n_lat=n_lat, n_lat_tiles=n_lat // tq, sub=min(256, tq))
    return pl.pallas_call(
        kern,
        grid=(bsz, h, pl.cdiv(out_rows, tq)),
        in_specs=[pl.BlockSpec((None, None, tq, dk), lambda b, hh, i: (b, hh, i, 0)),
                  pl.BlockSpec((None, None, l_all, dk), lambda b, hh, i: (b, hh // grp, 0, 0)),
                  pl.BlockSpec((None, None, dv, l_all), lambda b, hh, i: (b, hh // grp, 0, 0))],
        out_specs=pl.BlockSpec((None, tq, dv), lambda b, hh, i: (b, i, hh)),
        out_shape=jax.ShapeDtypeStruct((bsz, out_rows, h * dv), MXU_DT),
        compiler_params=_params("arbitrary", "arbitrary", "arbitrary"),
        name="attention",
    )(q, k, v)


CONV_TC = 256


def _conv3(u, w, b, n_lat):
    u = u.astype(F32)
    n = u.shape[0]
    row = lax.broadcasted_iota(jnp.int32, (n, 1), 0)
    prev = jnp.where((row == 0) | (row == n_lat), 0.0, pltpu.roll(u, 1, 0))
    nxt = jnp.where((row == n_lat - 1) | (row == n - 1), 0.0, pltpu.roll(u, n - 1, 0))
    return w[0:1, :] * prev + w[1:2, :] * u + w[2:3, :] * nxt + b


def _hy_conv_kernel(p0_ref, p1_ref, pv_ref, w0_ref, w1_ref, wv_ref, b0_ref, b1_ref, bv_ref,
                    x0_ref, vxt_ref, vxc_ref, *, n_lat):
    x0_ref[...] = _conv3(p0_ref[...], w0_ref[...], b0_ref[...], n_lat)
    x1 = _conv3(p1_ref[...], w1_ref[...], b1_ref[...], n_lat)
    v = _conv3(pv_ref[...], wv_ref[...], bv_ref[...], n_lat)
    vx = v * x1
    vxt_ref[...] = vx[:n_lat, :].T
    vxc_ref[...] = vx[n_lat:, :]


def hyena_conv(p, conv_w, conv_b3, layer, n_lat):
    bsz, l_all, _ = p.shape
    tc = CONV_TC
    nb = HY_W // tc
    base = P_HY // tc

    def pspec(off):
        return pl.BlockSpec((None, l_all, tc), lambda b, c: (b, 0, base + off + c))

    def wspec(off):
        return pl.BlockSpec((None, 3, tc), lambda b, c: (layer, 0, off + c))

    def bspec(off):
        return pl.BlockSpec((None, 1, tc), lambda b, c: (layer, 0, off + c))

    o_spec = pl.BlockSpec((None, l_all, tc), lambda b, c: (b, 0, c))
    return pl.pallas_call(
        functools.partial(_hy_conv_kernel, n_lat=n_lat),
        grid=(bsz, nb),
        in_specs=[pspec(0), pspec(nb), pspec(2 * nb), wspec(0), wspec(nb), wspec(2 * nb),
                  bspec(0), bspec(nb), bspec(2 * nb)],
        out_specs=[o_spec,
                   pl.BlockSpec((None, tc, n_lat), lambda b, c: (b, c, 0)),
                   pl.BlockSpec((None, l_all - n_lat, tc), lambda b, c: (b, 0, c))],
        out_shape=[jax.ShapeDtypeStruct((bsz, l_all, HY_W), F32),
                   jax.ShapeDtypeStruct((bsz, HY_W, n_lat), F32),
                   jax.ShapeDtypeStruct((bsz, l_all - n_lat, HY_W), F32)],
        compiler_params=_params("arbitrary", "arbitrary"),
        name="hyena_conv",
    )(p, p, p, conv_w, conv_w, conv_w, conv_b3, conv_b3, conv_b3)


def _mb_conv_kernel(p_ref, w_ref, b_ref, o_ref, *, n_lat):
    o_ref[...] = _silu(_conv3(p_ref[...], w_ref[...], b_ref[...], n_lat))


def mamba_conv(p, conv_w, conv_b3, layer, n_lat):
    bsz, l_all, _ = p.shape
    tc = CONV_TC
    nb = MB_CONV_CH // tc
    base = P_XBC // tc
    return pl.pallas_call(
        functools.partial(_mb_conv_kernel, n_lat=n_lat),
        grid=(bsz, nb),
        in_specs=[pl.BlockSpec((None, l_all, tc), lambda b, c: (b, 0, base + c)),
                  pl.BlockSpec((None, 3, tc), lambda b, c: (layer, 0, c)),
                  pl.BlockSpec((None, 1, tc), lambda b, c: (layer, 0, c))],
        out_specs=pl.BlockSpec((None, l_all, tc), lambda b, c: (b, 0, c)),
        out_shape=jax.ShapeDtypeStruct((bsz, l_all, MB_CONV_CH), F32),
        compiler_params=_params("arbitrary", "arbitrary"),
        name="mamba_conv",
    )(p, conv_w, conv_b3)


def _softplus(x):
    return jnp.maximum(x, 0.0) + jnp.log(1.0 + jnp.exp(-jnp.abs(x)))


def _mb_dt_kernel(p_ref, bias_ref, o_ref):
    dt = _softplus(p_ref[...] + bias_ref[...])
    o_ref[0] = dt
    o_ref[1] = pltpu.roll(dt, LANE - MB_HEADS, 1)


def mamba_dt(p, dt_bias_row, layer):
    bsz, l_all, _ = p.shape
    return pl.pallas_call(
        _mb_dt_kernel,
        grid=(bsz,),
        in_specs=[pl.BlockSpec((None, l_all, LANE), lambda b: (b, 0, 0)),
                  pl.BlockSpec((None, 1, LANE), lambda b: (layer, 0, 0))],
        out_specs=pl.BlockSpec((2, None, l_all, LANE), lambda b: (0, b, 0, 0)),
        out_shape=jax.ShapeDtypeStruct((2, bsz, l_all, LANE), F32),
        compiler_params=_params("arbitrary"),
        name="mamba_dt",
    )(p, dt_bias_row)


def _hy_filter_kernel(f_ref, aux_ref, w1_ref, b1_ref, w2_ref, b2_ref, w3_ref, fr_ref, dl_ref, o_ref, *,
                      channel_major):
    fr = fr_ref[...]
    tr = f_ref.shape[0]
    n_ch = 4 if tr % 512 == 0 else 1
    rc = tr // n_ch
    chunks = [slice(r * rc, (r + 1) * rc) for r in range(n_ch)]
    hdn = [_dot_hi(f_ref[rows, :], w1_ref[...]) for rows in chunks]
    hdn = [jnp.sin(fr * (x + b1_ref[...])) for x in hdn]
    hdn = [_dot_hi(x, w2_ref[...]) for x in hdn]
    hdn = [jnp.sin(fr * (x + b2_ref[...])) for x in hdn]
    filts = [_dot_hi(x, w3_ref[...]) for x in hdn]
    for rows, filt in zip(chunks, filts):
        aux = aux_ref[rows, :]
        t = aux[:, 0:1]
        is_fwd = aux[:, 1:2] > 0.5
        valid = aux[:, 2:3]
        window = jnp.exp(-t * dl_ref[...]) * valid
        k2 = jnp.where(is_fwd, filt[:, :HY_W], filt[:, HY_W:]) * window
        if channel_major:
            o_ref[:, rows] = k2.T
        else:
            o_ref[rows, :] = k2


def hyena_filter(feats2, aux, w1p, b1p, w2p, b2p, w3p, frp, deltas, layer, channel_major):
    rows = feats2.shape[0]
    tr = _pick(rows, (512, 256))

    def lspec(shape):
        return pl.BlockSpec((None,) + shape, lambda i: (layer, 0, 0))

    if channel_major:
        out_spec = pl.BlockSpec((HY_W, tr), lambda i: (0, i))
        out_shape = jax.ShapeDtypeStruct((HY_W, rows), F32)
    else:
        out_spec = pl.BlockSpec((tr, HY_W), lambda i: (i, 0))
        out_shape = jax.ShapeDtypeStruct((rows, HY_W), F32)
    return pl.pallas_call(
        functools.partial(_hy_filter_kernel, channel_major=channel_major),
        grid=(rows // tr,),
        in_specs=[pl.BlockSpec((tr, LANE), lambda i: (i, 0)),
                  pl.BlockSpec((tr, LANE), lambda i: (i, 0)),
                  lspec((LANE, LANE)), lspec((1, LANE)), lspec((LANE, LANE)), lspec((1, LANE)),
                  lspec((LANE, 2 * HY_W)), lspec((1, LANE)),
                  pl.BlockSpec((1, HY_W), lambda i: (0, 0))],
        out_specs=out_spec,
        out_shape=out_shape,
        compiler_params=_params("arbitrary"),
        name="hyena_filter",
    )(feats2, aux, w1p, b1p, w2p, b2p, w3p, frp, deltas)


DFT_MINOR = 256


def _snap(c):
    for v in (0.0, 1.0, -1.0):
        if abs(c - v) < 1e-12:
            return v
    return float(c)


def _lincomb(terms):
    acc = None
    for cf, tile in terms:
        if cf == 0.0:
            continue
        v = tile()
        if acc is None:
            acc = v if cf == 1.0 else (-v if cf == -1.0 else cf * v)
        elif cf == 1.0:
            acc = acc + v
        elif cf == -1.0:
            acc = acc - v
        else:
            acc = acc + cf * v
    return acc


def _dft_consts(n_seq):
    n = 2 * n_seq
    n2 = DFT_MINOR
    n1 = n // n2
    nkj = n1 // 2 + 1
    ang1 = 2.0 * np.pi * np.outer(np.arange(nkj), np.arange(n1)) / n1
    cos1 = [[_snap(v) for v in r] for r in np.cos(ang1)]
    sin1 = [[_snap(v) for v in r] for r in np.sin(ang1)]
    ang_t = 2.0 * np.pi * np.outer(np.arange(nkj), np.arange(n2)) / n
    rows = ((nkj + 7) // 8) * 8
    tw_c = np.zeros((rows, n2))
    tw_s = np.zeros((rows, n2))
    tw_c[:nkj] = np.cos(ang_t)
    tw_s[:nkj] = np.sin(ang_t)
    ang2 = 2.0 * np.pi * np.outer(np.arange(n2), np.arange(n2)) / n2
    c2, s2 = np.cos(ang2), np.sin(ang2)
    m_fwd = np.block([[c2, -s2], [s2, c2]])
    m_inv = np.block([[c2, s2], [-s2, c2]])
    as32 = lambda a: jnp.asarray(a, dtype=F32)

    def split(m):
        m32 = as32(m)
        hi = m32.astype(MXU_DT)
        lo = (m32 - hi.astype(F32)).astype(MXU_DT)
        return jnp.stack([hi, lo])

    return dict(n=n, n1=n1, nkj=nkj, cos1=cos1, sin1=sin1, tw_c=as32(tw_c), tw_s=as32(tw_s),
                m_fwd=split(m_fwd), m_inv=split(m_inv))


def _dot_split(a, m_ref):
    a_hi = a.astype(MXU_DT)
    a_lo = (a - a_hi.astype(F32)).astype(MXU_DT)
    return _dot(a_hi, m_ref[0]) + _dot(a_lo, m_ref[0]) + _dot(a_hi, m_ref[1])


def _outer_fwd_block(src_ref, z_ref, twc_ref, tws_ref, cb, n_in, cst):
    n2 = DFT_MINOR
    nkj, cos1, sin1 = cst["nkj"], cst["cos1"], cst["sin1"]
    tile = lambda j: (lambda: src_ref[cb * 8:cb * 8 + 8, j * n2:(j + 1) * n2])
    for kj in range(nkj):
        re = _lincomb([(cos1[kj][j], tile(j)) for j in range(n_in)])
        im = _lincomb([(-sin1[kj][j], tile(j)) for j in range(n_in)])
        if kj > 0 and im is not None:
            c = twc_ref[kj:kj + 1, :]
            s = tws_ref[kj:kj + 1, :]
            re, im = re * c + im * s, im * c - re * s
        elif kj > 0:
            re, im = re * twc_ref[kj:kj + 1, :], -re * tws_ref[kj:kj + 1, :]
        r = (cb * nkj + kj) * 8
        z_ref[r:r + 8, 0:n2] = re
        z_ref[r:r + 8, n2:2 * n2] = jnp.zeros_like(re) if im is None else im


def _outer_inv_block(z_ref, yt_ref, twc_ref, tws_ref, cb, cst):
    n2 = DFT_MINOR
    nkj, n1, cos1, sin1 = cst["nkj"], cst["n1"], cst["cos1"], cst["sin1"]
    inv_n = 1.0 / cst["n"]
    for kj in range(1, nkj):
        r = (cb * nkj + kj) * 8
        re = z_ref[r:r + 8, 0:n2]
        im = z_ref[r:r + 8, n2:2 * n2]
        c = twc_ref[kj:kj + 1, :]
        s = tws_ref[kj:kj + 1, :]
        z_ref[r:r + 8, 0:n2] = re * c - im * s
        z_ref[r:r + 8, n2:2 * n2] = im * c + re * s
    row = lambda kj: (cb * nkj + kj) * 8
    g_re = lambda kj: (lambda: z_ref[row(kj):row(kj) + 8, 0:n2])
    g_im = lambda kj: (lambda: z_ref[row(kj):row(kj) + 8, n2:2 * n2])
    for j in range(n1 // 2):
        terms = []
        for kj in range(nkj):
            wgt = inv_n if kj in (0, nkj - 1) else 2.0 * inv_n
            terms.append((wgt * cos1[kj][j], g_re(kj)))
            terms.append((-wgt * sin1[kj][j], g_im(kj)))
        yt_ref[cb * 8:cb * 8 + 8, j * n2:(j + 1) * n2] = _lincomb(terms)


def _spectrum_kernel(k_ref, twc_ref, tws_ref, mf_ref, h_ref, z_ref, *, ct, cst, n_chunks):
    nkj = cst["nkj"]
    per = ct // 8 // n_chunks
    rc = per * nkj * 8
    for c in range(n_chunks + 1):
        if c < n_chunks:
            for cb in range(c * per, (c + 1) * per):
                _outer_fwd_block(k_ref, z_ref, twc_ref, tws_ref, cb, cst["n1"], cst)
        if c >= 1:
            rows = slice((c - 1) * rc, c * rc)
            h_ref[rows, :] = _dot_split(z_ref[rows, :], mf_ref)


def _long_conv_kernel(x_ref, h_ref, twc_ref, tws_ref, mf_ref, mi_ref, x0_ref, vxc_ref, yc_ref, skip_ref,
                      skip_row_ref, o_ref, z_ref, yt_ref, *, ct, cst, n_chunks, n_lat):
    n2 = DFT_MINOR
    nkj, n1 = cst["nkj"], cst["n1"]
    per = ct // 8 // n_chunks
    rc = per * nkj * 8
    for c in range(n_chunks + 2):
        if c < n_chunks:
            for cb in range(c * per, (c + 1) * per):
                _outer_fwd_block(x_ref, z_ref, twc_ref, tws_ref, cb, n1 // 2, cst)
        if 1 <= c <= n_chunks:
            rows = slice((c - 1) * rc, c * rc)
            x = _dot_split(z_ref[rows, :], mf_ref)
            xre, xim = x[:, :n2], x[:, n2:]
            hre, him = h_ref[rows, 0:n2], h_ref[rows, n2:2 * n2]
            y = jnp.concatenate([xre * hre - xim * him, xre * him + xim * hre], axis=1)
            z_ref[rows, :] = _dot_split(y, mi_ref)
        if c >= 2:
            for cb in range((c - 2) * per, (c - 1) * per):
                _outer_inv_block(z_ref, yt_ref, twc_ref, tws_ref, cb, cst)

    y = (yt_ref[...] + x_ref[...] * skip_ref[...]).T
    o_ref[0:n_lat, :] = (x0_ref[0:n_lat, :] * y).astype(o_ref.dtype)
    yc = yc_ref[...] + vxc_ref[...] * skip_row_ref[...]
    o_ref[n_lat:, :] = (x0_ref[n_lat:, :] * yc).astype(o_ref.dtype)


def filter_spectrum(k2t, cst):
    ch, n = k2t.shape
    ct = LANE
    nkj = cst["nkj"]
    n_chunks = _pick(nkj * ct // 8, (4, 5, 1))
    full = lambda a: pl.BlockSpec(a.shape, lambda c: (0,) * a.ndim)
    return pl.pallas_call(
        functools.partial(_spectrum_kernel, ct=ct, cst=cst, n_chunks=n_chunks),
        grid=(ch // ct,),
        in_specs=[pl.BlockSpec((ct, n), lambda c: (c, 0)),
                  full(cst["tw_c"]), full(cst["tw_s"]), full(cst["m_fwd"])],
        out_specs=pl.BlockSpec((None, nkj * ct, 2 * DFT_MINOR), lambda c: (c, 0, 0)),
        out_shape=jax.ShapeDtypeStruct((ch // ct, nkj * ct, 2 * DFT_MINOR), F32),
        scratch_shapes=[pltpu.VMEM((nkj * ct, 2 * DFT_MINOR), F32)],
        compiler_params=_params("arbitrary"),
        name="filter_spectrum",
    )(k2t, cst["tw_c"], cst["tw_s"], cst["m_fwd"])


def hyena_long_conv(vxt, h_spec, x0, vx_ctx, y_ctx, skip_col, skip_row, cst, layer, n_lat):
    bsz, ch, _ = vxt.shape
    l_all = x0.shape[1]
    n_ctx = l_all - n_lat
    ct = LANE
    nkj = cst["nkj"]
    n_chunks = _pick(nkj * ct // 8, (4, 5, 1))
    full = lambda a: pl.BlockSpec(a.shape, lambda b, c: (0,) * a.ndim)
    return pl.pallas_call(
        functools.partial(_long_conv_kernel, ct=ct, cst=cst, n_chunks=n_chunks, n_lat=n_lat),
        grid=(bsz, ch // ct),
        in_specs=[pl.BlockSpec((None, ct, n_lat), lambda b, c: (b, c, 0)),
                  pl.BlockSpec((None, nkj * ct, 2 * DFT_MINOR), lambda b, c: (c, 0, 0)),
                  full(cst["tw_c"]), full(cst["tw_s"]), full(cst["m_fwd"]), full(cst["m_inv"]),
                  pl.BlockSpec((None, l_all, ct), lambda b, c: (b, 0, c)),
                  pl.BlockSpec((None, n_ctx, ct), lambda b, c: (b, 0, c)),
                  pl.BlockSpec((None, n_ctx, ct), lambda b, c: (b, 0, c)),
                  pl.BlockSpec((None, ct, 1), lambda b, c: (layer, c, 0)),
                  pl.BlockSpec((None, 1, ct), lambda b, c: (layer, 0, c))],
        out_specs=pl.BlockSpec((None, l_all, ct), lambda b, c: (b, 0, c)),
        out_shape=jax.ShapeDtypeStruct((bsz, l_all, ch), MXU_DT),
        scratch_shapes=[pltpu.VMEM((nkj * ct, 2 * DFT_MINOR), F32), pltpu.VMEM((ct, n_lat), F32)],
        compiler_params=_params("arbitrary", "arbitrary"),
        name="hyena_long_conv",
    )(vxt, h_spec, cst["tw_c"], cst["tw_s"], cst["m_fwd"], cst["m_inv"], x0, vx_ctx, y_ctx, skip_col, skip_row)


def _ctx_conv_kernel(v_ref, k_ref, fd_ref, fk_ref, gi_ref, o_ref, *, kp):
    x = _dot_hi(fd_ref[...], v_ref[...])
    h = _dot_hi(fk_ref[...], k_ref[...])
    xre, xim = x[:kp], x[kp:]
    hre, him = h[:kp], h[kp:]
    y = jnp.concatenate([xre * hre - xim * him, xre * him + xim * hre], axis=0)
    o_ref[...] = _dot_hi(gi_ref[...], y)


def _ctx_tables(n_ctx):
    n = 2 * n_ctx
    kh = n_ctx + 1
    kp = ((kh + 7) // 8) * 8
    k = np.arange(kp)[:, None]
    live = (k < kh).astype(np.float64)
    ang = 2.0 * np.pi * k * np.arange(n)[None, :] / n
    ck, sk = np.cos(ang) * live, np.sin(ang) * live
    fk = np.concatenate([ck, -sk], axis=0)
    fd = fk[:, :n_ctx]
    wk = np.where((k == 0) | (k == n_ctx), 1.0, 2.0) * live / n
    gi = np.concatenate([(ck * wk).T, (-sk * wk).T], axis=1)[:n_ctx]
    as32 = lambda a: jnp.asarray(a, dtype=F32)
    return dict(kp=kp, fd=as32(fd), fk=as32(fk), gi=as32(gi))


def long_conv_ctx(vx_ctx, k2c, tb):
    bsz, n_ctx, ch = vx_ctx.shape
    kp = tb["kp"]
    full = lambda a: pl.BlockSpec(a.shape, lambda b, c: (0,) * a.ndim)
    return pl.pallas_call(
        functools.partial(_ctx_conv_kernel, kp=kp),
        grid=(bsz, ch // LANE),
        in_specs=[pl.BlockSpec((None, n_ctx, LANE), lambda b, c: (b, 0, c)),
                  pl.BlockSpec((2 * n_ctx, LANE), lambda b, c: (0, c)),
                  full(tb["fd"]), full(tb["fk"]), full(tb["gi"])],
        out_specs=pl.BlockSpec((None, n_ctx, LANE), lambda b, c: (b, 0, c)),
        out_shape=jax.ShapeDtypeStruct((bsz, n_ctx, ch), F32),
        compiler_params=_params("arbitrary", "arbitrary"),
        name="ctx_long_conv",
    )(vx_ctx, k2c, tb["fd"], tb["fk"], tb["gi"])


def _split3(a):
    a1 = a.astype(MXU_DT)
    r = a - a1.astype(F32)
    a2 = r.astype(MXU_DT)
    a3 = (r - a2.astype(F32)).astype(MXU_DT)
    return a1, a2, a3


def _ssd_chunk(backward, xs_ref, b_ref, c_ref, dt_ref, alog, ex, y_ref, state_ref):
    n = MB_CHUNK
    li = lax.broadcasted_iota(jnp.int32, (n, n), 0)
    si = lax.broadcasted_iota(jnp.int32, (n, n), 1)
    mask = (li <= si) if backward else (li >= si)
    tri = mask.astype(MXU_DT)
    lane = lax.broadcasted_iota(jnp.int32, (n, LANE), 1)

    dt = dt_ref[...]
    da = dt * (-jnp.exp(alog))
    cum = sum(_dot(tri, t) for t in _split3(da))
    cum_t = cum.T
    both = jnp.concatenate([cum, dt], axis=0)
    both_e = sum(_dot(t, ex) for t in _split3(both))
    cum_e = both_e[:n]
    dt_e = both_e[n:]
    total_e = cum_e[0:1] if backward else cum_e[n - 1:n]

    xdt = xs_ref[...] * dt_e
    xdt_b = xdt.astype(MXU_DT)
    w_b = (jnp.exp(total_e - cum_e) * xdt).astype(MXU_DT)
    st = state_ref[...]
    y_parts = []
    s_parts = []
    gw = MB_STATE
    hw = (MB_HEADS // MB_GROUPS) * MB_HEADDIM
    for g in range(MB_GROUPS):
        cg = c_ref[:, g * gw:(g + 1) * gw].astype(MXU_DT)
        bg = b_ref[:, g * gw:(g + 1) * gw]
        cb = _dot_nt(cg, bg.astype(MXU_DT))
        y_off = _dot(cg, st[:, g * hw:(g + 1) * hw].astype(MXU_DT))
        s_parts.append(_dot(bg.T.astype(MXU_DT), w_b[:, g * hw:(g + 1) * hw]))
        diag = []
        for j in range(hw // LANE):
            lo = g * hw + j * LANE
            xp = xdt_b[:, lo:lo + LANE]
            pair = []
            for e in range(2):
                hd = (lo // MB_HEADDIM) + e
                seg = cum[:, hd:hd + 1] - cum_t[hd:hd + 1, :]
                decay = jnp.exp(jnp.where(mask, seg, -jnp.inf))
                pair.append(_dot((cb * decay).astype(MXU_DT), xp))
            diag.append(jnp.where(lane < MB_HEADDIM, pair[0], pair[1]))
        y_parts.append(jnp.concatenate(diag, axis=1) + y_off * jnp.exp(cum_e[:, g * hw:(g + 1) * hw]))
    y_ref[...] = jnp.concatenate(y_parts, axis=1)
    state_ref[...] = st * jnp.exp(total_e) + jnp.concatenate(s_parts, axis=1)


def _ssd_kernel(xf_ref, bf_ref, cf_ref, dtf_ref, xb_ref, bb_ref, cb_ref, dtb_ref, alog_ref, ex_ref,
                yf_ref, yb_ref, state_ref):
    @pl.when(pl.program_id(1) == 0)
    def _():
        state_ref[...] = jnp.zeros_like(state_ref)

    ex = ex_ref[...]
    for e in range(xf_ref.shape[0]):
        _ssd_chunk(False, xf_ref.at[e], bf_ref.at[e], cf_ref.at[e], dtf_ref.at[e], alog_ref[0], ex,
                   yf_ref.at[e], state_ref.at[2 * e])
        _ssd_chunk(True, xb_ref.at[e], bb_ref.at[e], cb_ref.at[e], dtb_ref.at[e], alog_ref[1], ex,
                   yb_ref.at[e], state_ref.at[2 * e + 1])


def ssd_scan(xbc, dt2, a_log_rows, expand, layer, n_lat):
    bsz, l_all, _ = xbc.shape
    nc = l_all // MB_CHUNK
    ncl = n_lat // MB_CHUNK
    fwd = lambda c: (c + ncl) % nc
    bwd = lambda c: nc - 1 - c

    nb = 2 if bsz % 2 == 0 else 1

    def specs(blk, direction):
        return [pl.BlockSpec((nb, MB_CHUNK, MB_INNER), lambda b, c: (b, blk(c), 0)),
                pl.BlockSpec((nb, MB_CHUNK, 256), lambda b, c: (b, blk(c), 2)),
                pl.BlockSpec((nb, MB_CHUNK, 256), lambda b, c: (b, blk(c), 3)),
                pl.BlockSpec((None, nb, MB_CHUNK, LANE), lambda b, c: (direction, b, blk(c), 0))]

    y_shape = jax.ShapeDtypeStruct((bsz, l_all, MB_INNER), F32)
    return pl.pallas_call(
        _ssd_kernel,
        grid=(bsz // nb, nc),
        in_specs=specs(fwd, 0) + specs(bwd, 1)
        + [pl.BlockSpec((None, 2, 1, LANE), lambda b, c: (layer, 0, 0, 0)),
           pl.BlockSpec((LANE, MB_INNER), lambda b, c: (0, 0))],
        out_specs=[pl.BlockSpec((nb, MB_CHUNK, MB_INNER), lambda b, c: (b, fwd(c), 0)),
                   pl.BlockSpec((nb, MB_CHUNK, MB_INNER), lambda b, c: (b, bwd(c), 0))],
        out_shape=[y_shape, y_shape],
        scratch_shapes=[pltpu.VMEM((2 * nb, MB_STATE, MB_INNER), F32)],
        compiler_params=_params("arbitrary", "arbitrary"),
        name="ssd_scan",
    )(xbc, xbc, xbc, dt2, xbc, xbc, xbc, dt2, a_log_rows, expand)


def _mb_finish_kernel(yf_ref, yb_ref, xs_ref, z_ref, dsk_ref, g_ref, o_ref):
    y = yf_ref[...] + yb_ref[...] + xs_ref[...] * dsk_ref[...]
    y = y * _silu(z_ref[...].astype(F32))
    r = lax.rsqrt(jnp.mean(y * y, axis=-1, keepdims=True) + RMS_EPS)
    o_ref[...] = (y * r * g_ref[...]).astype(o_ref.dtype)


def mamba_finish(y_f, y_b, xbc, p, d_skip_e, norm_g3, layer):
    bsz, l_all, _ = xbc.shape
    tr = _pick(l_all, (544, 384, 128))
    w = MB_INNER
    return pl.pallas_call(
        _mb_finish_kernel,
        grid=(bsz, l_all // tr),
        in_specs=[pl.BlockSpec((None, tr, w), lambda b, i: (b, i, 0)),
                  pl.BlockSpec((None, tr, w), lambda b, i: (b, i, 0)),
                  pl.BlockSpec((None, tr, w), lambda b, i: (b, i, 0)),
                  pl.BlockSpec((None, tr, w), lambda b, i: (b, i, P_Z // w)),
                  pl.BlockSpec((None, 1, w), lambda b, i: (layer, 0, 0)),
                  pl.BlockSpec((None, 1, w), lambda b, i: (layer, 0, 0))],
        out_specs=pl.BlockSpec((None, tr, w), lambda b, i: (b, i, 0)),
        out_shape=jax.ShapeDtypeStruct((bsz, l_all, w), MXU_DT),
        compiler_params=_params("arbitrary", "arbitrary"),
        name="mamba_finish",
    )(y_f, y_b, xbc, p, d_skip_e, norm_g3)


_W_IN_MOVES = (
    (0, 0, MLA_Q_RANK),
    (512, MLA_Q_RANK, MLA_KV_RANK),
    (640, MLA_Q_RANK + MLA_KV_RANK, MLA_ROPE),
    (P_DT, MLA_COLS + GQA_COLS + HY_COLS + MB_INNER + MB_CONV_CH, 2 * MB_HEADS),
    (P_GQA, MLA_COLS, GQA_COLS),
    (P_Z, MLA_COLS + GQA_COLS + HY_COLS, MB_INNER),
    (P_HY, MLA_COLS + GQA_COLS, HY_COLS),
    (P_XBC, MLA_COLS + GQA_COLS + HY_COLS + MB_INNER, MB_CONV_CH),
)


def _w_in_prep_kernel(w_ref, o_ref):
    o_ref[...] = jnp.zeros_like(o_ref)
    for dst, src, n in _W_IN_MOVES:
        o_ref[:, dst:dst + n] = w_ref[:, src:src + n].astype(o_ref.dtype)


def _prep_w_in(w_in):
    nl, d, cols = w_in.shape
    tr = 256
    return pl.pallas_call(
        _w_in_prep_kernel,
        grid=(nl, d // tr),
        in_specs=[pl.BlockSpec((None, tr, cols), lambda l, i: (l, i, 0))],
        out_specs=pl.BlockSpec((None, tr, P_COLS), lambda l, i: (l, i, 0)),
        out_shape=jax.ShapeDtypeStruct((nl, d, P_COLS), MXU_DT),
        compiler_params=_params("arbitrary", "arbitrary"),
        name="w_in_layout",
    )(w_in)


def _prep_w_uq(w_uq):
    nl = w_uq.shape[0]
    w = w_uq.reshape(nl, MLA_Q_RANK, MLA_HEADS, MLA_NOPE + MLA_ROPE)
    w = _pad_to(w, (nl, 512, MLA_HEADS, 256))
    return w.reshape(nl, 512, MLA_HEADS * 256).astype(MXU_DT)


def _rope_tables(n_lat, n_ctx, rot_dim):
    rows = n_lat // GRID_W
    row = jnp.repeat(jnp.arange(rows, dtype=F32), GRID_W)
    col = jnp.tile(jnp.arange(GRID_W, dtype=F32), rows)
    n_freq = rot_dim // 4
    inv_freq = ROPE_THETA ** (-jnp.arange(n_freq, dtype=F32) / n_freq)
    ang = jnp.concatenate([row[:, None] * inv_freq, col[:, None] * inv_freq], axis=-1)
    half = rot_dim // 2
    cos = jnp.concatenate([jnp.cos(ang), jnp.ones((n_ctx, half), F32)], axis=0)
    sin = jnp.concatenate([jnp.sin(ang), jnp.zeros((n_ctx, half), F32)], axis=0)
    cos_t = jnp.repeat(cos, 2, axis=-1)
    sin_t = jnp.stack([-sin, sin], axis=-1).reshape(n_lat + n_ctx, rot_dim)
    shape = (n_lat + n_ctx, LANE)
    return _pad_to(cos_t, shape), _pad_to(sin_t, shape)


def _filter_inputs(n):
    t = jnp.linspace(0.0, 1.0, n, dtype=F32)[:, None]
    omega = 2.0 * math.pi * jnp.arange(n, dtype=F32) / n
    bands = jnp.linspace(1e-4, HY_BANDS - 1, HY_BANDS, dtype=F32)
    ang = omega[:, None] * bands[None, :]
    feats = jnp.concatenate([t, jnp.cos(ang), -jnp.sin(ang)], axis=-1)
    zero = jnp.zeros((1, HY_EMB), F32)
    feats2 = jnp.concatenate([feats, zero, feats[1:][::-1]], axis=0)
    feats2 = jnp.concatenate([feats2, jnp.zeros((2 * n, LANE - HY_EMB), F32)], axis=-1)
    t2 = jnp.concatenate([t, jnp.zeros((1, 1), F32), t[1:][::-1]], axis=0)
    idx = jnp.arange(2 * n)[:, None]
    aux = jnp.concatenate([t2, (idx < n).astype(F32), (idx != n).astype(F32),
                           jnp.zeros((2 * n, LANE - 3), F32)], axis=-1)
    return feats2, aux


def _pad_to(a, shape):
    pads = [(0, s - d) for d, s in zip(a.shape, shape)]
    return jnp.pad(a, pads)


def kernel(x, c, ctx, c_ctx, w_ada, b_ada, w_in, mla_q_norm, mla_kv_norm, mla_w_uq, mla_w_ukv, gqa_q_norm, gqa_k_norm, hy_conv_w, hy_conv_b, hy_w1, hy_b1, hy_w2, hy_b2, hy_w3, hy_freq, hy_skip, mb_conv_w, mb_conv_b, mb_a_log, mb_dt_bias, mb_d, mb_norm, w_mgate, b_mgate, w_branch, w_out, ln1_g, ln1_b, w_ffn_in, w_ffn_out, ln2_g, ln2_b):
    bsz, n_lat, d = x.shape
    n_ctx = ctx.shape[1]
    nl = w_in.shape[0]
    assert d == D_MODEL and bsz < 8 and n_lat % n_ctx == 0 and n_ctx % MB_CHUNK == 0

    w_in_p = _prep_w_in(w_in)
    w_uq_p = _prep_w_uq(mla_w_uq)
    w_ukv_p = mla_w_ukv.astype(MXU_DT)
    gq_mla = _pad_to(mla_q_norm, (nl, 512))[:, None, :]
    gkv_mla = mla_kv_norm[:, None, :]
    gq_gqa = gqa_q_norm[:, None, :]
    gk_gqa = gqa_k_norm[:, None, :]
    w_gate_b = w_mgate.astype(MXU_DT)
    w_branch_b = w_branch.astype(MXU_DT)
    w_out_b = w_out.astype(MXU_DT)
    w_ffn_in_b = w_ffn_in.astype(MXU_DT)
    w_ffn_out_b = w_ffn_out.astype(MXU_DT)
    b_gate4 = b_mgate[:, :, None, :]
    b_ada3 = b_ada[:, None, :]
    hy_conv_b3 = hy_conv_b[:, None, :]
    mb_conv_b3 = mb_conv_b[:, None, :]
    hy_skip3 = hy_skip[:, None, :]
    hy_skip_col = hy_skip[:, :, None]
    ln1_g3, ln1_b3, ln2_g3, ln2_b3 = (a[:, None, :] for a in (ln1_g, ln1_b, ln2_g, ln2_b))
    hy_w1p = _pad_to(hy_w1, (nl, LANE, LANE))
    hy_b1p = _pad_to(hy_b1, (nl, LANE))[:, None, :]
    hy_w2p = _pad_to(hy_w2, (nl, LANE, LANE))
    hy_b2p = _pad_to(hy_b2, (nl, LANE))[:, None, :]
    hy_w3p = _pad_to(hy_w3, (nl, LANE, 2 * HY_W))
    hy_frp = _pad_to(hy_freq, (nl, LANE))[:, None, :]
    deltas = jnp.abs(jnp.linspace(HY_MIN_DECAY, HY_MAX_DECAY, HY_W, dtype=F32))[None, :]
    dt_bias_row = _pad_to(mb_dt_bias.reshape(nl, 2 * MB_HEADS), (nl, LANE))[:, None, :]
    a_log_rows = _pad_to(mb_a_log, (nl, 2, LANE))[:, :, None, :]
    d_skip_e = jnp.repeat(mb_d, MB_HEADDIM, axis=-1)[:, None, :]
    mb_norm3 = mb_norm[:, None, :]
    expand = jnp.asarray(np.kron(np.eye(LANE, MB_HEADS), np.ones((1, MB_HEADDIM))), dtype=MXU_DT)

    cos_m, sin_m = _rope_tables(n_lat, n_ctx, MLA_ROPE)
    cos_g, sin_g = _rope_tables(n_lat, n_ctx, GQA_HD)
    feats_lat, aux_lat = _filter_inputs(n_lat)
    feats_ctx, aux_ctx = _filter_inputs(n_ctx)
    dft_lat = _dft_consts(n_lat)
    tb_ctx = _ctx_tables(n_ctx)

    c8 = jnp.concatenate([c, c_ctx[None, :], jnp.zeros((8 - bsz - 1, d), F32)], axis=0)
    xz = jnp.concatenate([x, ctx], axis=1)

    for l in range(nl):
        last = l == nl - 1
        mod = ada_mod(c8, w_ada, b_ada3, l).reshape(8, 1, 6 * d)
        p, p_dt, h = in_projection(xz, mod, w_in_p, l, n_lat)

        q, k, v = mla_prep(p, gq_mla, gkv_mla, w_uq_p, w_ukv_p, cos_m, sin_m, l)
        oa = attention(q, k, v, n_lat, not last)
        q, k, v = gqa_prep(p, gq_gqa, gk_gqa, cos_g, sin_g, l)
        ob = attention(q, k, v, n_lat, not last)
        x0, vxt, vx_ctx = hyena_conv(p, hy_conv_w, hy_conv_b3, l, n_lat)
        k2t = hyena_filter(feats_lat, aux_lat, hy_w1p, hy_b1p, hy_w2p, hy_b2p, hy_w3p, hy_frp, deltas, l, True)
        h_spec = filter_spectrum(k2t, dft_lat)
        if last:
            y_ctx = jnp.zeros((bsz, n_ctx, HY_W), F32)
        else:
            k2c = hyena_filter(feats_ctx, aux_ctx, hy_w1p, hy_b1p, hy_w2p, hy_b2p, hy_w3p, hy_frp, deltas, l,
                               False)
            y_ctx = long_conv_ctx(vx_ctx, k2c, tb_ctx)
        oc = hyena_long_conv(vxt, h_spec, x0, vx_ctx, y_ctx, hy_skip_col, hy_skip3, dft_lat, l, n_lat)
        xbc = mamba_conv(p, mb_conv_w, mb_conv_b3, l, n_lat)
        dt2 = mamba_dt(p_dt, dt_bias_row, l)
        y_f, y_b = ssd_scan(xbc, dt2, a_log_rows, expand, l, n_lat)
        od = mamba_finish(y_f, y_b, xbc, p, d_skip_e, mb_norm3, l)

        out_rows = n_lat if last else n_lat + n_ctx
        acc = merge_branches(h, (oa, ob, oc, od), w_gate_b, b_gate4, w_branch_b, l, out_rows)
        x1 = matmul_res_ln(acc, w_out_b, xz, mod, 2, ln1_g3, ln1_b3, l, n_lat, out_rows,
                           (544, 512, 384, 128), d)
        act = ffn_in(x1, mod, w_ffn_in_b, l, n_lat)
        xz = matmul_res_ln(act, w_ffn_out_b, x1, mod, 5, ln2_g3, ln2_b3, l, n_lat, out_rows,
                           (256, 128), act.shape[2])
    return xz
```
